```python
import math
import jax, jax.numpy as jnp
from jax import lax
import numpy as np

D_MODEL = 1024
BATCH = 2
SEQ = 16384
DEPTH = 2

CHUNK = 64
RMS_EPS = 1e-6
POOL_WINDOWS = (2, 4, 8, 16)
N_POOL_GROUPS = len(POOL_WINDOWS)
POOL_WIDTH = D_MODEL // 2
POOL_GROUP = POOL_WIDTH // N_POOL_GROUPS
CONV_CH = D_MODEL // 2
CONV_K = 3
MIX_IN0 = POOL_WIDTH + 3 * CONV_CH
HEAD_DIM = 64
N_Q_HEADS = D_MODEL // HEAD_DIM
N_KV_HEADS = 2
Q_PER_KV = N_Q_HEADS // N_KV_HEADS
WINDOW = 128
WINDOW_CHUNKS = -(-WINDOW // CHUNK)
KV_SPAN = (WINDOW_CHUNKS + 1) * CHUNK
QKV_WIDTH = (N_Q_HEADS + 2 * N_KV_HEADS) * HEAD_DIM
N_BUCKETS = 32
MAX_DISTANCE = 128
D_FF = 2816
N_EXPERTS = 8
TOP_K = 2
D_FF_EXPERT = 3584
N_EVEN = (DEPTH + 1) // 2
N_ODD = DEPTH // 2
NEG_INF = -1e30

kernel_name = 'hybrid_pool_conv_swa_moe_encoder'


def rmsnorm(x, g):
    xf = x.astype(jnp.float32)
    y = xf * lax.rsqrt(jnp.mean(xf * xf, axis=-1, keepdims=True) + RMS_EPS)
    return (y * g.astype(jnp.float32)).astype(x.dtype)


def swiglu(h, w_gate, w_up, w_down):
    return (jax.nn.silu(h @ w_gate) * (h @ w_up)) @ w_down


def multiscale_pool(u, pool_w, pool_scale):
    s = u.shape[1]
    uf = u.astype(jnp.float32)
    cs = jnp.pad(jnp.cumsum(uf, axis=1), ((0, 0), (1, 0), (0, 0)))
    t = jnp.arange(s)
    groups = []
    for g, w in enumerate(POOL_WINDOWS):
        c = cs[:, :, g * POOL_GROUP:(g + 1) * POOL_GROUP]
        hi = c[:, 1:]
        lo = jnp.pad(c[:, :s + 1 - w], ((0, 0), (w - 1, 0), (0, 0)))
        count = jnp.minimum(t + 1, w).astype(jnp.float32)[None, :, None]
        groups.append((hi - lo) / count - uf[:, :, g * POOL_GROUP:(g + 1) * POOL_GROUP])
    pooled = jnp.stack(groups, axis=2).astype(u.dtype)
    mixed = jnp.einsum('bsgc,gcd->bsgd', pooled, pool_w)
    return mixed.reshape(u.shape) * pool_scale


def causal_depthwise_conv(z, conv_w):
    c = z.shape[-1]
    return lax.conv_general_dilated(
        z, conv_w[:, None, :].astype(z.dtype), window_strides=(1,),
        padding=[(CONV_K - 1, 0)], dimension_numbers=('NWC', 'WIO', 'NWC'),
        feature_group_count=c)


def pool_conv_mixer(h, w_in, pool_w, pool_scale, conv_w, w_out):
    proj = h @ w_in
    u, gate_post, gate_pre, v = jnp.split(
        proj, [POOL_WIDTH, POOL_WIDTH + CONV_CH, POOL_WIDTH + 2 * CONV_CH], axis=-1)
    y_a = multiscale_pool(u, pool_w, pool_scale)
    y_b = gate_post * causal_depthwise_conv(gate_pre * v, conv_w)
    return jnp.concatenate([y_a, y_b], axis=-1) @ w_out


def t5_bucket(rel):
    nb = N_BUCKETS // 2
    max_exact = nb // 2
    ret = jnp.where(rel > 0, nb, 0)
    n = jnp.abs(rel)
    nf = jnp.maximum(n, 1).astype(jnp.float32)
    large = max_exact + (jnp.log(nf / max_exact) / math.log(MAX_DISTANCE / max_exact)
                         * (nb - max_exact)).astype(jnp.int32)
    large = jnp.minimum(large, nb - 1)
    return ret + jnp.where(n < max_exact, n, large)


def sliding_window_attention(h, w_qkv, b_qkv, sinks, w_o, b_o, rel_bias):
    b, s, _ = h.shape
    nc = s // CHUNK
    pad = WINDOW_CHUNKS * CHUNK
    qkv = h @ w_qkv + b_qkv
    q, k, v = jnp.split(qkv, [N_Q_HEADS * HEAD_DIM, (N_Q_HEADS + N_KV_HEADS) * HEAD_DIM], axis=-1)
    q = q.reshape(b, nc, CHUNK, N_KV_HEADS, Q_PER_KV, HEAD_DIM)

    def band(t):
        tp = jnp.pad(t.reshape(b, s, N_KV_HEADS, HEAD_DIM), ((0, 0), (pad, 0), (0, 0), (0, 0)))
        tp = tp.reshape(b, nc + WINDOW_CHUNKS, CHUNK, N_KV_HEADS, HEAD_DIM)
        return jnp.concatenate([tp[:, i:i + nc] for i in range(WINDOW_CHUNKS + 1)], axis=2)

    kb, vb = band(k), band(v)
    scores = jnp.einsum('bnqkgd,bnskd->bnkgqs', q, kb,
                        preferred_element_type=jnp.float32) * (HEAD_DIM ** -0.5)
    rel = (jnp.arange(KV_SPAN) - pad)[None, :] - jnp.arange(CHUNK)[:, None]
    bias = jnp.transpose(rel_bias[t5_bucket(rel)].astype(jnp.float32), (2, 0, 1))
    scores = scores + bias.reshape(N_KV_HEADS, Q_PER_KV, CHUNK, KV_SPAN)
    key_pos = jnp.arange(nc)[:, None] * CHUNK - pad + jnp.arange(KV_SPAN)[None, :]
    valid = (key_pos >= 0)[None, :, None, None, None, :]
    scores = jnp.where(valid, scores, NEG_INF)
    sink = sinks.astype(jnp.float32).reshape(N_KV_HEADS, Q_PER_KV)[None, None, :, :, None, None]
    m = jnp.maximum(jnp.max(scores, axis=-1, keepdims=True), sink)
    p = jnp.exp(scores - m)
    p = p / (jnp.sum(p, axis=-1, keepdims=True) + jnp.exp(sink - m))
    out = jnp.einsum('bnkgqs,bnskd->bnqkgd', p.astype(vb.dtype), vb)
    return out.reshape(b, s, N_Q_HEADS * HEAD_DIM) @ w_o + b_o


def moe_swiglu(h, w_router, w_gate, w_up, w_down):
    logits = (h @ w_router).astype(jnp.float32)
    top_vals, top_idx = lax.top_k(logits, TOP_K)
    top_w = jax.nn.softmax(top_vals, axis=-1)
    combine = jnp.einsum('bsk,bske->bse', top_w,
                         jax.nn.one_hot(top_idx, N_EXPERTS, dtype=jnp.float32)).astype(h.dtype)
    out = jnp.zeros_like(h)
    for e in range(N_EXPERTS):
        out = out + combine[..., e:e + 1] * swiglu(h, w_gate[e], w_up[e], w_down[e])
    return out


def setup_inputs(seed: int = 0) -> dict:
    key = jax.random.key(seed)
    ks = jax.random.split(key, 32)
    f32 = jnp.float32

    def nrm(k, shape, scale):
        return jax.random.normal(k, shape, f32) * scale

    d = D_MODEL
    return {
        'x': nrm(ks[0], (BATCH, SEQ, d), 1.0),
        'rel_bias': nrm(ks[1], (N_BUCKETS, N_Q_HEADS), 0.5),
        'ev_norm_mix': 1.0 + nrm(ks[2], (N_EVEN, d), 0.02),
        'ev_w_in': nrm(ks[3], (N_EVEN, d, MIX_IN0), d ** -0.5),
        'ev_pool_w': nrm(ks[4], (N_EVEN, N_POOL_GROUPS, POOL_GROUP, POOL_GROUP), POOL_GROUP ** -0.5),
        'ev_pool_scale': 1.0 + nrm(ks[5], (N_EVEN, POOL_WIDTH), 0.1),
        'ev_conv_w': nrm(ks[6], (N_EVEN, CONV_K, CONV_CH), CONV_K ** -0.5),
        'ev_w_out': nrm(ks[7], (N_EVEN, d, d), d ** -0.5),
        'ev_norm_ffn': 1.0 + nrm(ks[8], (N_EVEN, d), 0.02),
        'ev_ffn_gate': nrm(ks[9], (N_EVEN, d, D_FF), d ** -0.5),
        'ev_ffn_up': nrm(ks[10], (N_EVEN, d, D_FF), d ** -0.5),
        'ev_ffn_down': nrm(ks[11], (N_EVEN, D_FF, d), D_FF ** -0.5),
        'od_norm_mix': 1.0 + nrm(ks[12], (N_ODD, d), 0.02),
        'od_w_qkv': nrm(ks[13], (N_ODD, d, QKV_WIDTH), d ** -0.5),
        'od_b_qkv': nrm(ks[14], (N_ODD, QKV_WIDTH), 0.02),
        'od_sinks': nrm(ks[15], (N_ODD, N_Q_HEADS), 0.5),
        'od_w_o': nrm(ks[16], (N_ODD, N_Q_HEADS * HEAD_DIM, d), (N_Q_HEADS * HEAD_DIM) ** -0.5),
        'od_b_o': nrm(ks[17], (N_ODD, d), 0.02),
        'od_norm_ffn': 1.0 + nrm(ks[18], (N_ODD, d), 0.02),
        'od_router': nrm(ks[19], (N_ODD, d, N_EXPERTS), d ** -0.5),
        'od_exp_gate': nrm(ks[20], (N_ODD, N_EXPERTS, d, D_FF_EXPERT), d ** -0.5),
        'od_exp_up': nrm(ks[21], (N_ODD, N_EXPERTS, d, D_FF_EXPERT), d ** -0.5),
        'od_exp_down': nrm(ks[22], (N_ODD, N_EXPERTS, D_FF_EXPERT, d), D_FF_EXPERT ** -0.5),
        'final_norm': 1.0 + nrm(ks[23], (d,), 0.02),
    }


def reference(x, rel_bias, ev_norm_mix, ev_w_in, ev_pool_w, ev_pool_scale, ev_conv_w, ev_w_out,
              ev_norm_ffn, ev_ffn_gate, ev_ffn_up, ev_ffn_down, od_norm_mix, od_w_qkv, od_b_qkv,
              od_sinks, od_w_o, od_b_o, od_norm_ffn, od_router, od_exp_gate, od_exp_up,
              od_exp_down, final_norm):
    for layer in range(DEPTH):
        i = layer // 2
        if layer % 2 == 0:
            h = rmsnorm(x, ev_norm_mix[i])
            x = x + pool_conv_mixer(h, ev_w_in[i], ev_pool_w[i], ev_pool_scale[i],
                                    ev_conv_w[i], ev_w_out[i])
            h = rmsnorm(x, ev_norm_ffn[i])
            x = x + swiglu(h, ev_ffn_gate[i], ev_ffn_up[i], ev_ffn_down[i])
        else:
            h = rmsnorm(x, od_norm_mix[i])
            x = x + sliding_window_attention(h, od_w_qkv[i], od_b_qkv[i], od_sinks[i],
                                             od_w_o[i], od_b_o[i], rel_bias)
            h = rmsnorm(x, od_norm_ffn[i])
            x = x + moe_swiglu(h, od_router[i], od_exp_gate[i], od_exp_up[i], od_exp_down[i])
    return rmsnorm(x, final_norm)
```

```python
import functools
import math

import jax
import jax.numpy as jnp
from jax import lax
from jax.experimental import pallas as pl
from jax.experimental.pallas import tpu as pltpu

F32 = jnp.float32
BF16 = jnp.bfloat16

RMS_EPS = 1e-6
CHUNK = 64
POOL_WINDOWS = (2, 4, 8, 16)
CONV_K = 3
HEAD_DIM = 64
N_KV_HEADS = 2
WINDOW_CHUNKS = 2
KV_SPAN = (WINDOW_CHUNKS + 1) * CHUNK
N_BUCKETS = 32
MAX_DISTANCE = 128
TOP_K = 2
NEG_INF = -1e30

LANES = 128
SUBLANES = 8
VMEM_LIMIT = 56 * 1024 * 1024

POOL_HALO = 16
CONV_HALO = 8


def _rms(x, g):
    return x * lax.rsqrt(jnp.mean(x * x, axis=-1, keepdims=True) + RMS_EPS) * g


def _params(n_axes, vmem=VMEM_LIMIT):
    return pltpu.CompilerParams(dimension_semantics=("arbitrary",) * n_axes,
                                vmem_limit_bytes=vmem)


def _const_spec(shape):
    nd = len(shape)
    return pl.BlockSpec(shape, lambda *_: (0,) * nd, pipeline_mode=pl.Buffered(1))


def _mixer_kernel(x_ref, g_ref, win_ref, pw_ref, ps_ref, cw_ref, wout_ref, o_ref, ubuf, zbuf):
    s = pl.program_id(1)
    tm = x_ref.shape[0]
    pool_w = ubuf.shape[1]
    grp = pool_w // len(POOL_WINDOWS)
    conv_c = zbuf.shape[1]

    @pl.when(s == 0)
    def _():
        ubuf[0:POOL_HALO, :] = jnp.zeros((POOL_HALO, pool_w), F32)
        zbuf[0:CONV_HALO, :] = jnp.zeros((CONV_HALO, conv_c), F32)

    @pl.when(s > 0)
    def _():
        ubuf[0:POOL_HALO, :] = ubuf[tm:tm + POOL_HALO, :]
        zbuf[0:CONV_HALO, :] = zbuf[tm:tm + CONV_HALO, :]

    x = x_ref[...]
    h = _rms(x, g_ref[...]).astype(BF16)
    proj = jnp.dot(h, win_ref[...], preferred_element_type=F32)
    u = proj[:, :pool_w]
    gate_post = proj[:, pool_w:pool_w + conv_c]
    gate_pre = proj[:, pool_w + conv_c:pool_w + 2 * conv_c]
    v = proj[:, pool_w + 2 * conv_c:]
    ubuf[POOL_HALO:POOL_HALO + tm, :] = u
    z = gate_pre * v
    zbuf[CONV_HALO:CONV_HALO + tm, :] = z

    t = s * tm + lax.broadcasted_iota(jnp.int32, (tm, 1), 0)
    parts = []
    for g, w in enumerate(POOL_WINDOWS):
        lo, hi = g * grp, (g + 1) * grp
        ug = u[:, lo:hi]
        acc = ug
        for k in range(1, w):
            acc = acc + ubuf[POOL_HALO - k:POOL_HALO - k + tm, lo:hi]
        count = jnp.minimum(t + 1, w).astype(F32)
        pooled = acc / count - ug
        mixed = jnp.dot(pooled.astype(BF16), pw_ref[g], preferred_element_type=F32)
        parts.append(mixed * ps_ref[:, lo:hi])
    conv = cw_ref[CONV_K - 1:CONV_K, :] * z
    for k in range(CONV_K - 1):
        off = CONV_HALO - (CONV_K - 1) + k
        conv = conv + cw_ref[k:k + 1, :] * zbuf[off:off + tm, :]
    parts.append(gate_post * conv)
    y = jnp.concatenate(parts, axis=1).astype(BF16)
    o_ref[...] = x + jnp.dot(y, wout_ref[...], preferred_element_type=F32)


def _mixer(x, g, w_in, pool_w, pool_scale, conv_w, w_out, tm):
    b, s, d = x.shape
    pool_width = pool_scale.shape[-1]
    conv_c = conv_w.shape[-1]
    return pl.pallas_call(
        _mixer_kernel,
        out_shape=jax.ShapeDtypeStruct((b, s, d), F32),
        grid=(b, s // tm),
        in_specs=[
            pl.BlockSpec((None, tm, d), lambda i, j: (i, j, 0)),
            _const_spec((1, d)),
            _const_spec(w_in.shape),
            _const_spec(pool_w.shape),
            _const_spec((1, pool_width)),
            _const_spec(conv_w.shape),
            _const_spec(w_out.shape),
        ],
        out_specs=pl.BlockSpec((None, tm, d), lambda i, j: (i, j, 0)),
        scratch_shapes=[pltpu.VMEM((POOL_HALO + tm, pool_width), F32),
                        pltpu.VMEM((CONV_HALO + tm, conv_c), F32)],
        compiler_params=_params(2),
        name="l0_mixer",
    )(x, g.reshape(1, d), w_in, pool_w, pool_scale.reshape(1, pool_width), conv_w, w_out)


def _ffn_kernel(x_ref, g_ref, wg_ref, wu_ref, wd_ref, o_ref, h_scr, acc_scr, *, fc):
    x = x_ref[...]
    h_scr[...] = _rms(x, g_ref[...]).astype(BF16)
    acc_scr[...] = jnp.zeros_like(acc_scr)
    n_chunks = wg_ref.shape[1] // fc

    def body(c, carry):
        off = pl.multiple_of(c * fc, fc)
        h = h_scr[...]
        a = jnp.dot(h, wg_ref[:, pl.ds(off, fc)], preferred_element_type=F32)
        b = jnp.dot(h, wu_ref[:, pl.ds(off, fc)], preferred_element_type=F32)
        t = (a * jax.nn.sigmoid(a) * b).astype(BF16)
        acc_scr[...] += jnp.dot(t, wd_ref[pl.ds(off, fc), :], preferred_element_type=F32)
        return carry

    lax.fori_loop(0, n_chunks, body, 0)
    o_ref[...] = x + acc_scr[...]


def _ffn(x, g, w_gate, w_up, w_down, tm, fc):
    n, d = x.shape
    return pl.pallas_call(
        functools.partial(_ffn_kernel, fc=fc),
        out_shape=jax.ShapeDtypeStruct((n, d), F32),
        grid=(n // tm,),
        in_specs=[
            pl.BlockSpec((tm, d), lambda i: (i, 0)),
            _const_spec((1, d)),
            _const_spec(w_gate.shape),
            _const_spec(w_up.shape),
            _const_spec(w_down.shape),
        ],
        out_specs=pl.BlockSpec((tm, d), lambda i: (i, 0)),
        scratch_shapes=[pltpu.VMEM((tm, d), BF16), pltpu.VMEM((tm, d), F32)],
        compiler_params=_params(1),
        name="l0_ffn",
    )(x, g.reshape(1, d), w_gate, w_up, w_down)


def _qkv_kernel(x_ref, g_ref, w_ref, b_ref, q_ref, k_ref, v_ref):
    h = _rms(x_ref[...], g_ref[...]).astype(BF16)
    qkv = jnp.dot(h, w_ref[...], preferred_element_type=F32) + b_ref[...]
    dq = q_ref.shape[1]
    dk = k_ref.shape[1]
    q_ref[...] = (qkv[:, :dq] * (HEAD_DIM ** -0.5)).astype(BF16)
    k_ref[...] = qkv[:, dq:dq + dk].astype(BF16)
    v_ref[...] = qkv[:, dq + dk:].astype(BF16)


def _qkv(x, g, w_qkv, b_qkv, dq, dk, tm):
    n, d = x.shape
    width = w_qkv.shape[1]
    return pl.pallas_call(
        _qkv_kernel,
        out_shape=(jax.ShapeDtypeStruct((n, dq), BF16),
                   jax.ShapeDtypeStruct((n, dk), BF16),
                   jax.ShapeDtypeStruct((n, dk), BF16)),
        grid=(n // tm,),
        in_specs=[
            pl.BlockSpec((tm, d), lambda i: (i, 0)),
            _const_spec((1, d)),
            _const_spec(w_qkv.shape),
            _const_spec((1, width)),
        ],
        out_specs=(pl.BlockSpec((tm, dq), lambda i: (i, 0)),
                   pl.BlockSpec((tm, dk), lambda i: (i, 0)),
                   pl.BlockSpec((tm, dk), lambda i: (i, 0))),
        compiler_params=_params(1),
        name="l1_qkv",
    )(x, g.reshape(1, d), w_qkv, b_qkv.reshape(1, width))


def _bias_kernel(onehot_ref, rb_ref, o_ref):
    o_ref[...] = jnp.dot(onehot_ref[...], rb_ref[...], preferred_element_type=F32,
                         precision=lax.Precision.HIGHEST)


def _t5_bucket(rel):
    nb = N_BUCKETS // 2
    max_exact = nb // 2
    ret = jnp.where(rel > 0, nb, 0)
    n = jnp.abs(rel)
    nf = jnp.maximum(n, 1).astype(jnp.float32)
    large = max_exact + (jnp.log(nf / max_exact) / math.log(MAX_DISTANCE / max_exact)
                         * (nb - max_exact)).astype(jnp.int32)
    large = jnp.minimum(large, nb - 1)
    return ret + jnp.where(n < max_exact, n, large)


def _rel_bias_table(rel_bias):
    n_heads = rel_bias.shape[1]
    pad = WINDOW_CHUNKS * CHUNK
    rel = (jnp.arange(KV_SPAN) - pad)[None, :] - jnp.arange(CHUNK)[:, None]
    bucket = _t5_bucket(rel).reshape(-1)
    onehot = (bucket[:, None] == jnp.arange(N_BUCKETS)[None, :]).astype(F32)
    rows = CHUNK * KV_SPAN
    rb = jnp.pad(rel_bias.astype(F32), ((0, 0), (0, LANES - n_heads)))
    out = pl.pallas_call(
        _bias_kernel,
        out_shape=jax.ShapeDtypeStruct((rows, LANES), F32),
        name="l1_rel_bias",
    )(onehot, rb)
    return jnp.transpose(out[:, :n_heads].reshape(CHUNK, KV_SPAN, n_heads), (2, 0, 1))


def _attn_kernel(q_ref, kp_ref, km_ref, vp_ref, vm_ref, bias_ref, sink_ref, o_ref,
                 kbuf, vbuf, *, n_slabs):
    i = pl.program_id(1)
    tq = q_ref.shape[0]
    pad = WINDOW_CHUNKS * CHUNK
    kbuf[0:pad, :] = kp_ref[...]
    kbuf[pad:pad + tq, :] = km_ref[...]
    vbuf[0:pad, :] = vp_ref[...]
    vbuf[pad:pad + tq, :] = vm_ref[...]
    lane = lax.broadcasted_iota(jnp.int32, (CHUNK, LANES), 1)
    low = lane < HEAD_DIM
    sink = sink_ref[...]
    key_off = lax.broadcasted_iota(jnp.int32, (1, KV_SPAN), 1)

    def body(j, carry):
        r0 = pl.multiple_of(j * CHUNK, CHUNK)
        kc = kbuf[pl.ds(r0, KV_SPAN), :]
        vc = vbuf[pl.ds(r0, KV_SPAN), :]
        q = q_ref[pl.ds(r0, CHUNK), :]
        zero = jnp.zeros((CHUNK, LANES), BF16)
        blocks = []
        for m in range(n_slabs):
            blocks.append(jnp.where(low, q[:, m * LANES:(m + 1) * LANES], zero))
        for m in range(n_slabs):
            blocks.append(jnp.where(low, zero, q[:, m * LANES:(m + 1) * LANES]))
        lhs = jnp.concatenate(blocks, axis=0)
        s = lax.dot_general(lhs, kc, (((1,), (1,)), ((), ())), preferred_element_type=F32)
        s = s + bias_ref[...]
        key_pos = i * tq + r0 - pad + key_off
        s = jnp.where(key_pos >= 0, s, NEG_INF)
        m_ = jnp.maximum(jnp.max(s, axis=-1, keepdims=True), sink)
        p = jnp.exp(s - m_)
        denom = jnp.sum(p, axis=-1, keepdims=True) + jnp.exp(sink - m_)
        p = (p / denom).astype(BF16)
        r = jnp.dot(p, vc, preferred_element_type=F32)
        outs = []
        for m in range(n_slabs):
            a = r[m * CHUNK:(m + 1) * CHUNK, :]
            b = r[(n_slabs + m) * CHUNK:(n_slabs + m + 1) * CHUNK, :]
            outs.append(jnp.where(low, a, b))
        o_ref[pl.ds(r0, CHUNK), :] = jnp.concatenate(outs, axis=1).astype(BF16)
        return carry

    lax.fori_loop(0, tq // CHUNK, body, 0)


def _attention(q, k, v, bias, sinks, tq):
    b, s, dq = q.shape
    dk = k.shape[-1]
    n_heads = dq // HEAD_DIM
    n_slabs = dq // LANES
    pad = WINDOW_CHUNKS * CHUNK
    ratio = tq // pad
    bias2 = bias.reshape(n_heads * CHUNK, KV_SPAN)
    sink2 = jnp.repeat(sinks.astype(F32), CHUNK).reshape(n_heads * CHUNK, 1)
    prev = lambda bi, i: (bi, jnp.maximum(i * ratio - 1, 0), 0)
    main = lambda bi, i: (bi, i, 0)
    return pl.pallas_call(
        functools.partial(_attn_kernel, n_slabs=n_slabs),
        out_shape=jax.ShapeDtypeStruct((b, s, dq), BF16),
        grid=(b, s // tq),
        in_specs=[
            pl.BlockSpec((None, tq, dq), main),
            pl.BlockSpec((None, pad, dk), prev),
            pl.BlockSpec((None, tq, dk), main),
            pl.BlockSpec((None, pad, dk), prev),
            pl.BlockSpec((None, tq, dk), main),
            _const_spec(bias2.shape),
            _const_spec(sink2.shape),
        ],
        out_specs=pl.BlockSpec((None, tq, dq), main),
        scratch_shapes=[pltpu.VMEM((pad + tq, dk), BF16), pltpu.VMEM((pad + tq, dk), BF16)],
        compiler_params=_params(2),
        name="l1_attention",
    )(q, k, k, v, v, bias2, sink2)


INFO_E0, INFO_E1, INFO_R0, INFO_R1, INFO_W0, INFO_W1 = range(6)


def _oproj_router_kernel(o_ref, x_ref, wo_ref, bo_ref, g_ref, wr_ref, x3_ref, info_ref, cnt_ref,
                         carry, *, n_experts):
    i = pl.program_id(0)
    tm = x_ref.shape[0]

    @pl.when(i == 0)
    def _():
        carry[...] = jnp.zeros_like(carry)

    x3 = x_ref[...] + jnp.dot(o_ref[...], wo_ref[...], preferred_element_type=F32) + bo_ref[...]
    x3_ref[...] = x3
    h = _rms(x3, g_ref[...])
    logits = jnp.dot(h, wr_ref[...], preferred_element_type=F32, precision=lax.Precision.HIGHEST)
    lane_i = lax.broadcasted_iota(jnp.int32, (tm, LANES), 1)
    lane = lane_i.astype(F32)
    neg = jnp.float32(-jnp.inf)
    logits = jnp.where(lane_i < n_experts, logits, neg)
    m0 = jnp.max(logits, axis=-1, keepdims=True)
    e0 = jnp.min(jnp.where(logits == m0, lane, float(LANES)), axis=-1, keepdims=True)
    rest = jnp.where(lane == e0, neg, logits)
    m1 = jnp.max(rest, axis=-1, keepdims=True)
    e1 = jnp.min(jnp.where(rest == m1, lane, float(LANES)), axis=-1, keepdims=True)
    t = jnp.exp(m1 - m0)
    w0 = 1.0 / (1.0 + t)
    w1 = t / (1.0 + t)
    hot0 = lane == e0
    hot1 = lane == e1
    both = jnp.where(hot0 | hot1, 1.0, 0.0).astype(BF16)
    row = lax.broadcasted_iota(jnp.int32, (tm, tm), 0)
    col = lax.broadcasted_iota(jnp.int32, (tm, tm), 1)
    tri = (col < row).astype(BF16)
    before = jnp.dot(tri, both, preferred_element_type=F32) + carry[...]
    r0 = jnp.sum(jnp.where(hot0, before, 0.0), axis=-1, keepdims=True)
    r1 = jnp.sum(jnp.where(hot1, before, 0.0), axis=-1, keepdims=True)
    carry[...] += jnp.sum(both.astype(F32), axis=0, keepdims=True)
    cnt_ref[...] = carry[...]
    info = jnp.zeros((tm, LANES), F32)
    for idx, val in ((INFO_E0, e0), (INFO_E1, e1), (INFO_R0, r0),
                     (INFO_R1, r1), (INFO_W0, w0), (INFO_W1, w1)):
        info = jnp.where(lane_i == idx, val, info)
    info_ref[...] = info


def _oproj_router(o, x, w_o, b_o, g, w_router, tm):
    n, d = x.shape
    n_experts = w_router.shape[1]
    wr = jnp.pad(w_router.astype(F32), ((0, 0), (0, LANES - n_experts)))
    return pl.pallas_call(
        functools.partial(_oproj_router_kernel, n_experts=n_experts),
        out_shape=(jax.ShapeDtypeStruct((n, d), F32),
                   jax.ShapeDtypeStruct((n, LANES), F32),
                   jax.ShapeDtypeStruct((1, LANES), F32)),
        grid=(n // tm,),
        in_specs=[
            pl.BlockSpec((tm, o.shape[1]), lambda i: (i, 0)),
            pl.BlockSpec((tm, d), lambda i: (i, 0)),
            _const_spec(w_o.shape),
            _const_spec((1, d)),
            _const_spec((1, d)),
            _const_spec(wr.shape),
        ],
        out_specs=(pl.BlockSpec((tm, d), lambda i: (i, 0)),
                   pl.BlockSpec((tm, LANES), lambda i: (i, 0)),
                   pl.BlockSpec((1, LANES), lambda i: (0, 0))),
        scratch_shapes=[pltpu.VMEM((1, LANES), F32)],
        compiler_params=_params(1),
        name="l1_oproj_router",
    )(o, x, w_o, b_o.reshape(1, d), g.reshape(1, d), wr)


def _row_copy(src_ref, src_row, dst_ref, dst_row, sem):
    s0 = pl.multiple_of(src_row * SUBLANES, SUBLANES)
    d0 = pl.multiple_of(dst_row * SUBLANES, SUBLANES)
    return pltpu.make_async_copy(src_ref.at[pl.ds(s0, SUBLANES)], dst_ref.at[pl.ds(d0, SUBLANES)], sem)


def _dispatch_kernel(pos_ref, x_ref, g_ref, zeros_hbm, xs_hbm, hbuf, sem):
    del zeros_hbm
    tm = x_ref.shape[0]
    h = _rms(x_ref[...], g_ref[...])
    for c in range(x_ref.shape[1] // LANES):
        hbuf[pl.ds(c, tm, stride=SUBLANES), :] = h[:, c * LANES:(c + 1) * LANES]

    def copies(r):
        return [_row_copy(hbuf, r, xs_hbm, pos_ref[TOP_K * r + k], sem) for k in range(TOP_K)]

    def start(r, carry):
        for cp in copies(r):
            cp.start()
        return carry

    def wait(r, carry):
        for cp in copies(r):
            cp.wait()
        return carry

    lax.fori_loop(0, tm, start, 0)
    lax.fori_loop(0, tm, wait, 0)


def _dispatch(x, g, pos_flat, n_rows, tm):
    n, d = x.shape
    zeros = jnp.zeros((n_rows * SUBLANES, LANES), F32)
    return pl.pallas_call(
        _dispatch_kernel,
        out_shape=jax.ShapeDtypeStruct((n_rows * SUBLANES, LANES), F32),
        grid=(n // tm,),
        in_specs=[
            pl.BlockSpec((TOP_K * tm,), lambda i: (i,), memory_space=pltpu.SMEM),
            pl.BlockSpec((tm, d), lambda i: (i, 0)),
            _const_spec((1, d)),
            pl.BlockSpec(memory_space=pl.ANY),
        ],
        out_specs=pl.BlockSpec(memory_space=pl.ANY),
        scratch_shapes=[pltpu.VMEM((tm * SUBLANES, LANES), F32), pltpu.SemaphoreType.DMA(())],
        input_output_aliases={3: 0},
        compiler_params=_params(1),
        name="l1_moe_dispatch",
    )(pos_flat, x, g.reshape(1, d), zeros)


def _experts_kernel(te_ref, tv_ref, xs_ref, wg_ref, wu_ref, wd_ref, ys_ref, xb, acc):
    j = pl.program_id(0)
    c = pl.program_id(1)
    tm = xb.shape[0]
    n_slabs = xb.shape[1] // LANES
    valid = tv_ref[j] == 1

    @pl.when(jnp.logical_and(valid, c == 0))
    def _():
        slabs = [xs_ref[pl.ds(s, tm, stride=SUBLANES), :] for s in range(n_slabs)]
        xb[...] = jnp.concatenate(slabs, axis=1).astype(BF16)
        acc[...] = jnp.zeros_like(acc)

    @pl.when(valid)
    def _():
        x = xb[...]
        a = jnp.dot(x, wg_ref[...], preferred_element_type=F32)
        b = jnp.dot(x, wu_ref[...], preferred_element_type=F32)
        t = (a * jax.nn.sigmoid(a) * b).astype(BF16)
        acc[...] += jnp.dot(t, wd_ref[...], preferred_element_type=F32)

    is_last = c == pl.num_programs(1) - 1

    @pl.when(jnp.logical_and(valid, is_last))
    def _():
        y = acc[...]
        for s in range(n_slabs):
            ys_ref[pl.ds(s, tm, stride=SUBLANES), :] = y[:, s * LANES:(s + 1) * LANES]

    @pl.when(jnp.logical_and(jnp.logical_not(valid), is_last))
    def _():
        ys_ref[...] = jnp.zeros_like(ys_ref)


def _experts(xs, tile_expert, tile_valid, w_gate, w_up, w_down, d, tm, fc):
    n_tiles = tile_expert.shape[0]
    n_fc = w_gate.shape[2] // fc
    last = n_fc - 1

    def wsel(c, tv, j):
        return jnp.where(tv[j] == 1, c, last)

    grid_spec = pltpu.PrefetchScalarGridSpec(
        num_scalar_prefetch=2,
        grid=(n_tiles, n_fc),
        in_specs=[
            pl.BlockSpec((tm * SUBLANES, LANES), lambda j, c, te, tv: (j, 0)),
            pl.BlockSpec((None, d, fc), lambda j, c, te, tv: (te[j], 0, wsel(c, tv, j))),
            pl.BlockSpec((None, d, fc), lambda j, c, te, tv: (te[j], 0, wsel(c, tv, j))),
            pl.BlockSpec((None, fc, d), lambda j, c, te, tv: (te[j], wsel(c, tv, j), 0)),
        ],
        out_specs=pl.BlockSpec((tm * SUBLANES, LANES), lambda j, c, te, tv: (j, 0)),
        scratch_shapes=[pltpu.VMEM((tm, d), BF16), pltpu.VMEM((tm, d), F32)],
    )
    return pl.pallas_call(
        _experts_kernel,
        out_shape=jax.ShapeDtypeStruct(xs.shape, F32),
        grid_spec=grid_spec,
        compiler_params=_params(2),
        name="l1_moe_experts",
    )(tile_expert, tile_valid, xs, w_gate, w_up, w_down)


def _combine_kernel(pos_ref, x_ref, info_ref, g_ref, ys_hbm, o_ref, ybuf, sem):
    tm = x_ref.shape[0]
    n_slabs = x_ref.shape[1] // LANES

    def copies(r):
        return [_row_copy(ys_hbm, pos_ref[TOP_K * r + k], ybuf, k * tm + r, sem) for k in range(TOP_K)]

    def start(r, carry):
        for cp in copies(r):
            cp.start()
        return carry

    def wait(r, carry):
        for cp in copies(r):
            cp.wait()
        return carry

    lax.fori_loop(0, tm, start, 0)
    lax.fori_loop(0, tm, wait, 0)
    info = info_ref[...]
    x = x_ref[...]
    for k, col in enumerate((INFO_W0, INFO_W1)):
        wk = info[:, col:col + 1]
        slabs = [ybuf[pl.ds(k * tm * SUBLANES + s, tm, stride=SUBLANES), :] for s in range(n_slabs)]
        x = x + wk * jnp.concatenate(slabs, axis=1)
    o_ref[...] = _rms(x, g_ref[...])


def _combine(x, info, pos_flat, ys, g, tm):
    n, d = x.shape
    return pl.pallas_call(
        _combine_kernel,
        out_shape=jax.ShapeDtypeStruct((n, d), F32),
        grid=(n // tm,),
        in_specs=[
            pl.BlockSpec((TOP_K * tm,), lambda i: (i,), memory_space=pltpu.SMEM),
            pl.BlockSpec((tm, d), lambda i: (i, 0)),
            pl.BlockSpec((tm, LANES), lambda i: (i, 0)),
            _const_spec((1, d)),
            pl.BlockSpec(memory_space=pl.ANY),
        ],
        out_specs=pl.BlockSpec((tm, d), lambda i: (i, 0)),
        scratch_shapes=[pltpu.VMEM((TOP_K * tm * SUBLANES, LANES), F32), pltpu.SemaphoreType.DMA(())],
        compiler_params=_params(1),
        name="l1_moe_combine",
    )(pos_flat, x, info, g.reshape(1, d), ys)


def _slab_order(n_heads):
    per_kv = n_heads // N_KV_HEADS
    order = []
    for m in range(per_kv):
        for kv in range(N_KV_HEADS):
            order.append(kv * per_kv + m)
    return order


def _forward(x, rel_bias, ev_norm_mix, ev_w_in, ev_pool_w, ev_pool_scale, ev_conv_w, ev_w_out,
             ev_norm_ffn, ev_ffn_gate, ev_ffn_up, ev_ffn_down, od_norm_mix, od_w_qkv, od_b_qkv,
             od_sinks, od_w_o, od_b_o, od_norm_ffn, od_router, od_exp_gate, od_exp_up,
             od_exp_down, final_norm, *, tm, tq, tme, fc_dense, fc_expert):
    b, s, d = x.shape
    n = b * s
    n_heads = od_sinks.shape[-1]
    dq = n_heads * HEAD_DIM
    dk = N_KV_HEADS * HEAD_DIM
    n_experts = od_router.shape[-1]

    x = _mixer(x, ev_norm_mix[0], ev_w_in[0].astype(BF16), ev_pool_w[0].astype(BF16),
               ev_pool_scale[0], ev_conv_w[0], ev_w_out[0].astype(BF16), tm)
    x = x.reshape(n, d)
    x = _ffn(x, ev_norm_ffn[0], ev_ffn_gate[0].astype(BF16), ev_ffn_up[0].astype(BF16),
             ev_ffn_down[0].astype(BF16), tm, fc_dense)

    order = _slab_order(n_heads)
    cols = jnp.asarray([h * HEAD_DIM + i for h in order for i in range(HEAD_DIM)], jnp.int32)
    w_qkv = od_w_qkv[0]
    b_qkv = od_b_qkv[0]
    w_qkv = jnp.concatenate([w_qkv[:, :dq][:, cols], w_qkv[:, dq:]], axis=1).astype(BF16)
    b_qkv = jnp.concatenate([b_qkv[:dq][cols], b_qkv[dq:]])
    w_o = od_w_o[0][cols, :].astype(BF16)
    q, k, v = _qkv(x, od_norm_mix[0], w_qkv, b_qkv, dq, dk, tm)
    bias = _rel_bias_table(rel_bias)
    attn = _attention(q.reshape(b, s, dq), k.reshape(b, s, dk), v.reshape(b, s, dk),
                      bias, od_sinks[0], tq)
    x, info, counts = _oproj_router(attn.reshape(n, dq), x, w_o, od_b_o[0], od_norm_ffn[0],
                                    od_router[0], tm)

    counts = counts[0, :n_experts].astype(jnp.int32)
    tiles_per = (counts + tme - 1) // tme
    tile_end = jnp.cumsum(tiles_per)
    group_start = (tile_end - tiles_per) * tme
    n_tiles = (TOP_K * n) // tme + n_experts
    tile_ids = jnp.arange(n_tiles, dtype=jnp.int32)
    tile_valid = (tile_ids < tile_end[-1]).astype(jnp.int32)
    tile_expert = jnp.searchsorted(tile_end, jnp.minimum(tile_ids, tile_end[-1] - 1),
                                   side="right").astype(jnp.int32)
    expert = info[:, INFO_E0:INFO_E1 + 1].astype(jnp.int32)
    rank = info[:, INFO_R0:INFO_R1 + 1].astype(jnp.int32)
    pos_flat = (group_start[expert] + rank).reshape(-1)

    xs = _dispatch(x, od_norm_ffn[0], pos_flat, n_tiles * tme, tm)
    ys = _experts(xs, tile_expert, tile_valid, od_exp_gate[0].astype(BF16),
                  od_exp_up[0].astype(BF16), od_exp_down[0].astype(BF16), d, tme, fc_expert)
    out = _combine(x, info, pos_flat, ys, final_norm, tm)
    return out.reshape(b, s, d)


def kernel(x, rel_bias, ev_norm_mix, ev_w_in, ev_pool_w, ev_pool_scale, ev_conv_w, ev_w_out, ev_norm_ffn, ev_ffn_gate, ev_ffn_up, ev_ffn_down, od_norm_mix, od_w_qkv, od_b_qkv, od_sinks, od_w_o, od_b_o, od_norm_ffn, od_router, od_exp_gate, od_exp_up, od_exp_down, final_norm):
    return _forward(x, rel_bias, ev_norm_mix, ev_w_in, ev_pool_w, ev_pool_scale, ev_conv_w, ev_w_out,
                    ev_norm_ffn, ev_ffn_gate, ev_ffn_up, ev_ffn_down, od_norm_mix, od_w_qkv, od_b_qkv,
                    od_sinks, od_w_o, od_b_o, od_norm_ffn, od_router, od_exp_gate, od_exp_up,
                    od_exp_down, final_norm, tm=512, tq=512, tme=512, fc_dense=256, fc_expert=512)
```

```python
import functools
import math

import jax
import jax.numpy as jnp
from jax import lax
from jax.experimental import pallas as pl
from jax.experimental.pallas import tpu as pltpu

F32 = jnp.float32
BF16 = jnp.bfloat16

RMS_EPS = 1e-6
CHUNK = 64
POOL_WINDOWS = (2, 4, 8, 16)
CONV_K = 3
HEAD_DIM = 64
N_KV_HEADS = 2
WINDOW_CHUNKS = 2
KV_SPAN = (WINDOW_CHUNKS + 1) * CHUNK
N_BUCKETS = 32
MAX_DISTANCE = 128
TOP_K = 2
NEG_INF = -1e30

LANES = 128
SUBLANES = 8
VMEM_LIMIT = 56 * 1024 * 1024

POOL_HALO = 16
CONV_HALO = 8
ATTN_CHUNKS_PER_ITER = 1


def _rms(x, g):
    return x * lax.rsqrt(jnp.mean(x * x, axis=-1, keepdims=True) + RMS_EPS) * g


def _params(n_axes, vmem=VMEM_LIMIT):
    return pltpu.CompilerParams(dimension_semantics=("arbitrary",) * n_axes,
                                vmem_limit_bytes=vmem)


def _const_spec(shape):
    nd = len(shape)
    return pl.BlockSpec(shape, lambda *_: (0,) * nd, pipeline_mode=pl.Buffered(1))


def _mixer_kernel(x_ref, g_ref, win_ref, pw_ref, ps_ref, cw_ref, wout_ref, o_ref, ubuf, zbuf):
    s = pl.program_id(1)
    tm = x_ref.shape[0]
    pool_w = ubuf.shape[1]
    grp = pool_w // len(POOL_WINDOWS)
    conv_c = zbuf.shape[1]

    @pl.when(s == 0)
    def _():
        ubuf[0:POOL_HALO, :] = jnp.zeros((POOL_HALO, pool_w), F32)
        zbuf[0:CONV_HALO, :] = jnp.zeros((CONV_HALO, conv_c), F32)

    @pl.when(s > 0)
    def _():
        ubuf[0:POOL_HALO, :] = ubuf[tm:tm + POOL_HALO, :]
        zbuf[0:CONV_HALO, :] = zbuf[tm:tm + CONV_HALO, :]

    x = x_ref[...]
    h = _rms(x, g_ref[...]).astype(BF16)
    proj = jnp.dot(h, win_ref[...], preferred_element_type=F32)
    u = proj[:, :pool_w]
    gate_post = proj[:, pool_w:pool_w + conv_c]
    gate_pre = proj[:, pool_w + conv_c:pool_w + 2 * conv_c]
    v = proj[:, pool_w + 2 * conv_c:]
    ubuf[POOL_HALO:POOL_HALO + tm, :] = u
    z = gate_pre * v
    zbuf[CONV_HALO:CONV_HALO + tm, :] = z

    t = s * tm + lax.broadcasted_iota(jnp.int32, (tm, 1), 0)
    parts = []
    for g, w in enumerate(POOL_WINDOWS):
        lo, hi = g * grp, (g + 1) * grp
        ug = u[:, lo:hi]
        acc = ug
        for k in range(1, w):
            acc = acc + ubuf[POOL_HALO - k:POOL_HALO - k + tm, lo:hi]
        count = jnp.minimum(t + 1, w).astype(F32)
        pooled = acc / count - ug
        mixed = jnp.dot(pooled.astype(BF16), pw_ref[g], preferred_element_type=F32)
        parts.append(mixed * ps_ref[:, lo:hi])
    conv = cw_ref[CONV_K - 1:CONV_K, :] * z
    for k in range(CONV_K - 1):
        off = CONV_HALO - (CONV_K - 1) + k
        conv = conv + cw_ref[k:k + 1, :] * zbuf[off:off + tm, :]
    parts.append(gate_post * conv)
    y = jnp.concatenate(parts, axis=1).astype(BF16)
    o_ref[...] = x + jnp.dot(y, wout_ref[...], preferred_element_type=F32)


def _mixer(x, g, w_in, pool_w, pool_scale, conv_w, w_out, tm):
    b, s, d = x.shape
    pool_width = pool_scale.shape[-1]
    conv_c = conv_w.shape[-1]
    return pl.pallas_call(
        _mixer_kernel,
        out_shape=jax.ShapeDtypeStruct((b, s, d), F32),
        grid=(b, s // tm),
        in_specs=[
            pl.BlockSpec((None, tm, d), lambda i, j: (i, j, 0)),
            _const_spec((1, d)),
            _const_spec(w_in.shape),
            _const_spec(pool_w.shape),
            _const_spec((1, pool_width)),
            _const_spec(conv_w.shape),
            _const_spec(w_out.shape),
        ],
        out_specs=pl.BlockSpec((None, tm, d), lambda i, j: (i, j, 0)),
        scratch_shapes=[pltpu.VMEM((POOL_HALO + tm, pool_width), F32),
                        pltpu.VMEM((CONV_HALO + tm, conv_c), F32)],
        compiler_params=_params(2),
        name="l0_mixer",
    )(x, g.reshape(1, d), w_in, pool_w, pool_scale.reshape(1, pool_width), conv_w, w_out)


def _ffn_kernel(x_ref, g_ref, wg_ref, wu_ref, wd_ref, o_ref, h_scr, acc_scr, *, fc):
    x = x_ref[...]
    h_scr[...] = _rms(x, g_ref[...]).astype(BF16)
    acc_scr[...] = jnp.zeros_like(acc_scr)
    n_chunks = wg_ref.shape[1] // fc

    def body(c, carry):
        off = pl.multiple_of(c * fc, fc)
        h = h_scr[...]
        a = jnp.dot(h, wg_ref[:, pl.ds(off, fc)], preferred_element_type=F32)
        b = jnp.dot(h, wu_ref[:, pl.ds(off, fc)], preferred_element_type=F32)
        t = (a * jax.nn.sigmoid(a) * b).astype(BF16)
        acc_scr[...] += jnp.dot(t, wd_ref[pl.ds(off, fc), :], preferred_element_type=F32)
        return carry

    lax.fori_loop(0, n_chunks, body, 0)
    o_ref[...] = x + acc_scr[...]


def _ffn(x, g, w_gate, w_up, w_down, tm, fc):
    n, d = x.shape
    return pl.pallas_call(
        functools.partial(_ffn_kernel, fc=fc),
        out_shape=jax.ShapeDtypeStruct((n, d), F32),
        grid=(n // tm,),
        in_specs=[
            pl.BlockSpec((tm, d), lambda i: (i, 0)),
            _const_spec((1, d)),
            _const_spec(w_gate.shape),
            _const_spec(w_up.shape),
            _const_spec(w_down.shape),
        ],
        out_specs=pl.BlockSpec((tm, d), lambda i: (i, 0)),
        scratch_shapes=[pltpu.VMEM((tm, d), BF16), pltpu.VMEM((tm, d), F32)],
        compiler_params=_params(1),
        name="l0_ffn",
    )(x, g.reshape(1, d), w_gate, w_up, w_down)


def _qkv_kernel(x_ref, g_ref, w_ref, b_ref, q_ref, k_ref, v_ref):
    h = _rms(x_ref[...], g_ref[...]).astype(BF16)
    qkv = jnp.dot(h, w_ref[...], preferred_element_type=F32) + b_ref[...]
    dq = q_ref.shape[1]
    dk = k_ref.shape[1]
    q_ref[...] = (qkv[:, :dq] * (HEAD_DIM ** -0.5)).astype(BF16)
    k_ref[...] = qkv[:, dq:dq + dk].astype(BF16)
    v_ref[...] = qkv[:, dq + dk:].astype(BF16)


def _qkv(x, g, w_qkv, b_qkv, dq, dk, tm):
    n, d = x.shape
    width = w_qkv.shape[1]
    return pl.pallas_call(
        _qkv_kernel,
        out_shape=(jax.ShapeDtypeStruct((n, dq), BF16),
                   jax.ShapeDtypeStruct((n, dk), BF16),
                   jax.ShapeDtypeStruct((n, dk), BF16)),
        grid=(n // tm,),
        in_specs=[
            pl.BlockSpec((tm, d), lambda i: (i, 0)),
            _const_spec((1, d)),
            _const_spec(w_qkv.shape),
            _const_spec((1, width)),
        ],
        out_specs=(pl.BlockSpec((tm, dq), lambda i: (i, 0)),
                   pl.BlockSpec((tm, dk), lambda i: (i, 0)),
                   pl.BlockSpec((tm, dk), lambda i: (i, 0))),
        compiler_params=_params(1),
        name="l1_qkv",
    )(x, g.reshape(1, d), w_qkv, b_qkv.reshape(1, width))


def _bias_kernel(onehot_ref, rb_ref, o_ref):
    o_ref[...] = jnp.dot(onehot_ref[...], rb_ref[...], preferred_element_type=F32,
                         precision=lax.Precision.HIGHEST)


def _t5_bucket(rel):
    nb = N_BUCKETS // 2
    max_exact = nb // 2
    ret = jnp.where(rel > 0, nb, 0)
    n = jnp.abs(rel)
    nf = jnp.maximum(n, 1).astype(jnp.float32)
    large = max_exact + (jnp.log(nf / max_exact) / math.log(MAX_DISTANCE / max_exact)
                         * (nb - max_exact)).astype(jnp.int32)
    large = jnp.minimum(large, nb - 1)
    return ret + jnp.where(n < max_exact, n, large)


def _rel_bias_table(rel_bias):
    n_heads = rel_bias.shape[1]
    pad = WINDOW_CHUNKS * CHUNK
    rel = (jnp.arange(KV_SPAN) - pad)[None, :] - jnp.arange(CHUNK)[:, None]
    bucket = _t5_bucket(rel).reshape(-1)
    onehot = (bucket[:, None] == jnp.arange(N_BUCKETS)[None, :]).astype(F32)
    rows = CHUNK * KV_SPAN
    rb = jnp.pad(rel_bias.astype(F32), ((0, 0), (0, LANES - n_heads)))
    out = pl.pallas_call(
        _bias_kernel,
        out_shape=jax.ShapeDtypeStruct((rows, LANES), F32),
        name="l1_rel_bias",
    )(onehot, rb)
    return jnp.transpose(out[:, :n_heads].reshape(CHUNK, KV_SPAN, n_heads), (2, 0, 1))


def _attn_kernel(q_ref, kp_ref, km_ref, vp_ref, vm_ref, bias_ref, sink_ref, o_ref,
                 kbuf, vbuf, *, n_slabs):
    i = pl.program_id(1)
    tq = q_ref.shape[0]
    pad = WINDOW_CHUNKS * CHUNK
    kbuf[0:pad, :] = kp_ref[...]
    kbuf[pad:pad + tq, :] = km_ref[...]
    vbuf[0:pad, :] = vp_ref[...]
    vbuf[pad:pad + tq, :] = vm_ref[...]
    lane = lax.broadcasted_iota(jnp.int32, (CHUNK, LANES), 1)
    low = lane < HEAD_DIM
    sink = sink_ref[...]
    chunks_per_tile = tq // CHUNK

    def one_chunk(j, u):
        r0 = pl.multiple_of(j * CHUNK, CHUNK)
        kc = kbuf[pl.ds(r0, KV_SPAN), :]
        vc = vbuf[pl.ds(r0, KV_SPAN), :]
        q = q_ref[pl.ds(r0, CHUNK), :]
        zero = jnp.zeros((CHUNK, LANES), BF16)
        blocks = []
        for m in range(n_slabs):
            blocks.append(jnp.where(low, q[:, m * LANES:(m + 1) * LANES], zero))
        for m in range(n_slabs):
            blocks.append(jnp.where(low, zero, q[:, m * LANES:(m + 1) * LANES]))
        lhs = jnp.concatenate(blocks, axis=0)
        s = lax.dot_general(lhs, kc, (((1,), (1,)), ((), ())), preferred_element_type=F32)
        seq_chunk = i * chunks_per_tile + j
        variant = jnp.where(seq_chunk < WINDOW_CHUNKS, seq_chunk + 1, 0)
        s = s + bias_ref[variant]
        m_ = jnp.maximum(jnp.max(s, axis=-1, keepdims=True), sink)
        p = jnp.exp(s - m_)
        denom = jnp.sum(p, axis=-1, keepdims=True) + jnp.exp(sink - m_)
        p = (p / denom).astype(BF16)
        r = jnp.dot(p, vc, preferred_element_type=F32)
        outs = []
        for m in range(n_slabs):
            a = r[m * CHUNK:(m + 1) * CHUNK, :]
            b = r[(n_slabs + m) * CHUNK:(n_slabs + m + 1) * CHUNK, :]
            outs.append(jnp.where(low, a, b))
        o_ref[pl.ds(r0, CHUNK), :] = jnp.concatenate(outs, axis=1).astype(BF16)

    def body(jj, carry):
        for u in range(ATTN_CHUNKS_PER_ITER):
            one_chunk(jj * ATTN_CHUNKS_PER_ITER + u, u)
        return carry

    lax.fori_loop(0, chunks_per_tile // ATTN_CHUNKS_PER_ITER, body, 0)


def _attention(q, k, v, bias, sinks, tq):
    b, s, dq = q.shape
    dk = k.shape[-1]
    n_heads = dq // HEAD_DIM
    n_slabs = dq // LANES
    pad = WINDOW_CHUNKS * CHUNK
    ratio = tq // pad
    bias2 = bias.reshape(1, n_heads * CHUNK, KV_SPAN)
    key = jnp.arange(KV_SPAN)[None, None, :]
    first_valid = (WINDOW_CHUNKS - jnp.arange(WINDOW_CHUNKS + 1) + 1)[:, None, None] * CHUNK
    first_valid = first_valid.at[0].set(0)
    bias2 = bias2 + jnp.where(key >= first_valid, 0.0, NEG_INF).astype(F32)
    sink2 =jnp.repeat(sinks.astype(F32), CHUNK).reshape(n_heads * CHUNK, 1)
    prev = lambda bi, i: (bi, jnp.maximum(i * ratio - 1, 0), 0)
    main = lambda bi, i: (bi, i, 0)
    return pl.pallas_call(
        functools.partial(_attn_kernel, n_slabs=n_slabs),
        out_shape=jax.ShapeDtypeStruct((b, s, dq), BF16),
        grid=(b, s // tq),
        in_specs=[
            pl.BlockSpec((None, tq, dq), main),
            pl.BlockSpec((None, pad, dk), prev),
            pl.BlockSpec((None, tq, dk), main),
            pl.BlockSpec((None, pad, dk), prev),
            pl.BlockSpec((None, tq, dk), main),
            _const_spec(bias2.shape),
            _const_spec(sink2.shape),
        ],
        out_specs=pl.BlockSpec((None, tq, dq), main),
        scratch_shapes=[pltpu.VMEM((pad + tq, dk), BF16), pltpu.VMEM((pad + tq, dk), BF16)],
        compiler_params=_params(2),
        name="l1_attention",
    )(q, k, k, v, v, bias2, sink2)


INFO_E0, INFO_E1, INFO_R0, INFO_R1, INFO_W0, INFO_W1 = range(6)


def _oproj_router_kernel(o_ref, x_ref, wo_ref, bo_ref, g_ref, wr_ref, x3_ref, info_ref, cnt_ref,
                         carry, *, n_experts):
    i = pl.program_id(0)
    tm = x_ref.shape[0]

    @pl.when(i == 0)
    def _():
        carry[...] = jnp.zeros_like(carry)

    x3 = x_ref[...] + jnp.dot(o_ref[...], wo_ref[...], preferred_element_type=F32) + bo_ref[...]
    x3_ref[...] = x3
    h = _rms(x3, g_ref[...])
    h_hi = h.astype(BF16)
    h_lo = (h - h_hi.astype(F32)).astype(BF16)
    wr = wr_ref[...]
    hw = jnp.dot(h_hi, wr, preferred_element_type=F32)
    logits = (hw[:, :LANES] + hw[:, LANES:]) + jnp.dot(h_lo, wr[:, :LANES], preferred_element_type=F32)
    lane_i = lax.broadcasted_iota(jnp.int32, (tm, LANES), 1)
    lane = lane_i.astype(F32)
    neg = jnp.float32(-jnp.inf)
    logits = jnp.where(lane_i < n_experts, logits, neg)
    m0 = jnp.max(logits, axis=-1, keepdims=True)
    e0 = jnp.min(jnp.where(logits == m0, lane, float(LANES)), axis=-1, keepdims=True)
    rest = jnp.where(lane == e0, neg, logits)
    m1 = jnp.max(rest, axis=-1, keepdims=True)
    e1 = jnp.min(jnp.where(rest == m1, lane, float(LANES)), axis=-1, keepdims=True)
    t = jnp.exp(m1 - m0)
    w0 = 1.0 / (1.0 + t)
    w1 = t / (1.0 + t)
    hot0 = lane == e0
    hot1 = lane == e1
    both = jnp.where(hot0 | hot1, 1.0, 0.0).astype(BF16)
    row = lax.broadcasted_iota(jnp.int32, (tm, tm), 0)
    col = lax.broadcasted_iota(jnp.int32, (tm, tm), 1)
    tri = (col < row).astype(BF16)
    before = jnp.dot(tri, both, preferred_element_type=F32) + carry[...]
    r0 = jnp.sum(jnp.where(hot0, before, 0.0), axis=-1, keepdims=True)
    r1 = jnp.sum(jnp.where(hot1, before, 0.0), axis=-1, keepdims=True)
    carry[...] += jnp.sum(both.astype(F32), axis=0, keepdims=True)
    cnt_ref[...] = carry[...]
    info = jnp.zeros((tm, LANES), F32)
    for idx, val in ((INFO_E0, e0), (INFO_E1, e1), (INFO_R0, r0),
                     (INFO_R1, r1), (INFO_W0, w0), (INFO_W1, w1)):
        info = jnp.where(lane_i == idx, val, info)
    info_ref[...] = info


def _oproj_router(o, x, w_o, b_o, g, w_router, tm):
    n, d = x.shape
    n_experts = w_router.shape[1]
    wr = jnp.pad(w_router.astype(F32), ((0, 0), (0, LANES - n_experts)))
    wr_hi = wr.astype(BF16)
    wr_lo = (wr - wr_hi.astype(F32)).astype(BF16)
    wr = jnp.concatenate([wr_hi, wr_lo], axis=1)
    return pl.pallas_call(
        functools.partial(_oproj_router_kernel, n_experts=n_experts),
        out_shape=(jax.ShapeDtypeStruct((n, d), F32),
                   jax.ShapeDtypeStruct((n, LANES), F32),
                   jax.ShapeDtypeStruct((1, LANES), F32)),
        grid=(n // tm,),
        in_specs=[
            pl.BlockSpec((tm, o.shape[1]), lambda i: (i, 0)),
            pl.BlockSpec((tm, d), lambda i: (i, 0)),
            _const_spec(w_o.shape),
            _const_spec((1, d)),
            _const_spec((1, d)),
            _const_spec(wr.shape),
        ],
        out_specs=(pl.BlockSpec((tm, d), lambda i: (i, 0)),
                   pl.BlockSpec((tm, LANES), lambda i: (i, 0)),
                   pl.BlockSpec((1, LANES), lambda i: (0, 0))),
        scratch_shapes=[pltpu.VMEM((1, LANES), F32)],
        compiler_params=_params(1),
        name="l1_oproj_router",
    )(o, x, w_o, b_o.reshape(1, d), g.reshape(1, d), wr)


def _row_copy(src_ref, src_row, dst_ref, dst_row, sem):
    s0 = pl.multiple_of(src_row * SUBLANES, SUBLANES)
    d0 = pl.multiple_of(dst_row * SUBLANES, SUBLANES)
    return pltpu.make_async_copy(src_ref.at[pl.ds(s0, SUBLANES)], dst_ref.at[pl.ds(d0, SUBLANES)], sem)


def _dispatch_kernel(pos_ref, pos_prev_ref, x_ref, g_ref, zeros_hbm, xs_hbm, hbuf, sems):
    del zeros_hbm
    i = pl.program_id(0)
    slot = i % 2
    tm = x_ref.shape[0]
    h = _rms(x_ref[...], g_ref[...])
    for c in range(x_ref.shape[1] // LANES):
        hbuf[slot, pl.ds(c, tm, stride=SUBLANES), :] = h[:, c * LANES:(c + 1) * LANES]

    def copies(p_ref, s, r):
        return [_row_copy(hbuf.at[s], r, xs_hbm, p_ref[TOP_K * r + k], sems.at[s]) for k in range(TOP_K)]

    def start(r, carry):
        for cp in copies(pos_ref, slot, r):
            cp.start()
        return carry

    lax.fori_loop(0, tm, start, 0)

    @pl.when(i > 0)
    def _():
        def wait_prev(r, carry):
            for cp in copies(pos_prev_ref, 1 - slot, r):
                cp.wait()
            return carry
        lax.fori_loop(0, tm, wait_prev, 0)

    @pl.when(i == pl.num_programs(0) - 1)
    def _():
        def wait_own(r, carry):
            for cp in copies(pos_ref, slot, r):
                cp.wait()
            return carry
        lax.fori_loop(0, tm, wait_own, 0)


def _dispatch(x, g, pos_flat, n_rows, tm):
    n, d = x.shape
    zeros = jnp.zeros((n_rows * SUBLANES, LANES), F32)
    return pl.pallas_call(
        _dispatch_kernel,
        out_shape=jax.ShapeDtypeStruct((n_rows * SUBLANES, LANES), F32),
        grid=(n // tm,),
        in_specs=[
            pl.BlockSpec((TOP_K * tm,), lambda i: (i,), memory_space=pltpu.SMEM),
            pl.BlockSpec((TOP_K * tm,), lambda i: (jnp.maximum(i - 1, 0),), memory_space=pltpu.SMEM),
            pl.BlockSpec((tm, d), lambda i: (i, 0)),
            _const_spec((1, d)),
            pl.BlockSpec(memory_space=pl.ANY),
        ],
        out_specs=pl.BlockSpec(memory_space=pl.ANY),
        scratch_shapes=[pltpu.VMEM((2, tm * SUBLANES, LANES), F32), pltpu.SemaphoreType.DMA((2,))],
        input_output_aliases={4: 0},
        compiler_params=_params(1),
        name="l1_moe_dispatch",
    )(pos_flat, pos_flat, x, g.reshape(1, d), zeros)


def _experts_kernel(te_ref, tv_ref, xs_ref, wg_ref, wu_ref, wd_ref, ys_ref, xb, acc, *, sub):
    j = pl.program_id(0)
    c = pl.program_id(1)
    tm = xb.shape[0]
    n_slabs = xb.shape[1] // LANES
    valid = tv_ref[j] == 1

    @pl.when(jnp.logical_and(valid, c == 0))
    def _():
        slabs = [xs_ref[pl.ds(s, tm, stride=SUBLANES), :] for s in range(n_slabs)]
        xb[...] = jnp.concatenate(slabs, axis=1).astype(BF16)
        acc[...] = jnp.zeros_like(acc)

    @pl.when(valid)
    def _():
        def body(q, carry):
            off = pl.multiple_of(q * sub, sub)
            x = xb[...]
            a = jnp.dot(x, wg_ref[:, pl.ds(off, sub)], preferred_element_type=F32)
            b = jnp.dot(x, wu_ref[:, pl.ds(off, sub)], preferred_element_type=F32)
            t = (a * jax.nn.sigmoid(a) * b).astype(BF16)
            acc[...] += jnp.dot(t, wd_ref[pl.ds(off, sub), :], preferred_element_type=F32)
            return carry
        lax.fori_loop(0, wg_ref.shape[1] // sub, body, 0)

    is_last = c == pl.num_programs(1) - 1

    @pl.when(jnp.logical_and(valid, is_last))
    def _():
        y = acc[...]
        for s in range(n_slabs):
            ys_ref[pl.ds(s, tm, stride=SUBLANES), :] = y[:, s * LANES:(s + 1) * LANES]

    @pl.when(jnp.logical_and(jnp.logical_not(valid), is_last))
    def _():
        ys_ref[...] = jnp.zeros_like(ys_ref)


def _experts(xs, tile_expert, tile_valid, w_gate, w_up, w_down, d, tm, fc, sub):
    n_tiles = tile_expert.shape[0]
    n_fc = w_gate.shape[2] // fc
    last = n_fc - 1

    def wsel(c, tv, j):
        return jnp.where(tv[j] == 1, c, last)

    grid_spec = pltpu.PrefetchScalarGridSpec(
        num_scalar_prefetch=2,
        grid=(n_tiles, n_fc),
        in_specs=[
            pl.BlockSpec((tm * SUBLANES, LANES), lambda j, c, te, tv: (j, 0)),
            pl.BlockSpec((None, d, fc), lambda j, c, te, tv: (te[j], 0, wsel(c, tv, j))),
            pl.BlockSpec((None, d, fc), lambda j, c, te, tv: (te[j], 0, wsel(c, tv, j))),
            pl.BlockSpec((None, fc, d), lambda j, c, te, tv: (te[j], wsel(c, tv, j), 0)),
        ],
        out_specs=pl.BlockSpec((tm * SUBLANES, LANES), lambda j, c, te, tv: (j, 0)),
        scratch_shapes=[pltpu.VMEM((tm, d), BF16), pltpu.VMEM((tm, d), F32)],
    )
    return pl.pallas_call(
        functools.partial(_experts_kernel, sub=sub),
        out_shape=jax.ShapeDtypeStruct(xs.shape, F32),
        grid_spec=grid_spec,
        compiler_params=_params(2),
        name="l1_moe_experts",
    )(tile_expert, tile_valid, xs, w_gate, w_up, w_down)


def _combine_kernel(pos_ref, pos_next_ref, x_ref, info_ref, g_ref, ys_hbm, o_ref, ybuf, sems):
    i = pl.program_id(0)
    slot = i % 2
    tm = x_ref.shape[0]
    n_slabs = x_ref.shape[1] // LANES

    def copies(p_ref, s, r):
        return [_row_copy(ys_hbm, p_ref[TOP_K * r + k], ybuf.at[s], k * tm + r, sems.at[s])
                for k in range(TOP_K)]

    def start_all(p_ref, s):
        def start(r, carry):
            for cp in copies(p_ref, s, r):
                cp.start()
            return carry
        lax.fori_loop(0, tm, start, 0)

    @pl.when(i == 0)
    def _():
        start_all(pos_ref, slot)

    @pl.when(i + 1 < pl.num_programs(0))
    def _():
        start_all(pos_next_ref, 1 - slot)

    def wait(r, carry):
        for cp in copies(pos_ref, slot, r):
            cp.wait()
        return carry

    lax.fori_loop(0, tm, wait, 0)
    info = info_ref[...]
    x = x_ref[...]
    for k, col in enumerate((INFO_W0, INFO_W1)):
        wk = info[:, col:col + 1]
        slabs = [ybuf[slot, pl.ds(k * tm * SUBLANES + s, tm, stride=SUBLANES), :] for s in range(n_slabs)]
        x = x + wk * jnp.concatenate(slabs, axis=1)
    o_ref[...] = _rms(x, g_ref[...])


def _combine(x, info, pos_flat, ys, g, tm):
    n, d = x.shape
    n_steps = n // tm
    return pl.pallas_call(
        _combine_kernel,
        out_shape=jax.ShapeDtypeStruct((n, d), F32),
        grid=(n // tm,),
        in_specs=[
            pl.BlockSpec((TOP_K * tm,), lambda i: (i,), memory_space=pltpu.SMEM),
            pl.BlockSpec((TOP_K * tm,), lambda i: (jnp.minimum(i + 1, n_steps - 1),),
                         memory_space=pltpu.SMEM),
            pl.BlockSpec((tm, d), lambda i: (i, 0)),
            pl.BlockSpec((tm, LANES), lambda i: (i, 0)),
            _const_spec((1, d)),
            pl.BlockSpec(memory_space=pl.ANY),
        ],
        out_specs=pl.BlockSpec((tm, d), lambda i: (i, 0)),
        scratch_shapes=[pltpu.VMEM((2, TOP_K * tm * SUBLANES, LANES), F32),
                        pltpu.SemaphoreType.DMA((2,))],
        compiler_params=_params(1),
        name="l1_moe_combine",
    )(pos_flat, pos_flat, x, info, g.reshape(1, d), ys)


def _slab_order(n_heads):
    per_kv = n_heads // N_KV_HEADS
    order = []
    for m in range(per_kv):
        for kv in range(N_KV_HEADS):
            order.append(kv * per_kv + m)
    return order


def _forward(x, rel_bias, ev_norm_mix, ev_w_in, ev_pool_w, ev_pool_scale, ev_conv_w, ev_w_out,
             ev_norm_ffn, ev_ffn_gate, ev_ffn_up, ev_ffn_down, od_norm_mix, od_w_qkv, od_b_qkv,
             od_sinks, od_w_o, od_b_o, od_norm_ffn, od_router, od_exp_gate, od_exp_up,
             od_exp_down, final_norm, *, tm, tq, tme, fc_dense, fc_expert):
    b, s, d = x.shape
    n = b * s
    n_heads = od_sinks.shape[-1]
    dq = n_heads * HEAD_DIM
    dk = N_KV_HEADS * HEAD_DIM
    n_experts = od_router.shape[-1]

    x = _mixer(x, ev_norm_mix[0], ev_w_in[0].astype(BF16), ev_pool_w[0].astype(BF16),
               ev_pool_scale[0], ev_conv_w[0], ev_w_out[0].astype(BF16), tm)
    x = x.reshape(n, d)
    x = _ffn(x, ev_norm_ffn[0], ev_ffn_gate[0].astype(BF16), ev_ffn_up[0].astype(BF16),
             ev_ffn_down[0].astype(BF16), tm, fc_dense)

    order = _slab_order(n_heads)
    cols = jnp.asarray([h * HEAD_DIM + i for h in order for i in range(HEAD_DIM)], jnp.int32)
    w_qkv = od_w_qkv[0]
    b_qkv = od_b_qkv[0]
    w_qkv = jnp.concatenate([w_qkv[:, :dq][:, cols], w_qkv[:, dq:]], axis=1).astype(BF16)
    b_qkv = jnp.concatenate([b_qkv[:dq][cols], b_qkv[dq:]])
    w_o = od_w_o[0][cols, :].astype(BF16)
    q, k, v = _qkv(x, od_norm_mix[0], w_qkv, b_qkv, dq, dk, tm)
    bias = _rel_bias_table(rel_bias)
    attn = _attention(q.reshape(b, s, dq), k.reshape(b, s, dk), v.reshape(b, s, dk),
                      bias, od_sinks[0], tq)
    x, info, counts = _oproj_router(attn.reshape(n, dq), x, w_o, od_b_o[0], od_norm_ffn[0],
                                    od_router[0], tm)

    counts = counts[0, :n_experts].astype(jnp.int32)
    tiles_per = (counts + tme - 1) // tme
    tile_end = jnp.cumsum(tiles_per)
    group_start = (tile_end - tiles_per) * tme
    n_tiles = (TOP_K * n) // tme + n_experts
    tile_ids = jnp.arange(n_tiles, dtype=jnp.int32)
    tile_valid = (tile_ids < tile_end[-1]).astype(jnp.int32)
    tile_expert = jnp.searchsorted(tile_end, jnp.minimum(tile_ids, tile_end[-1] - 1),
                                   side="right").astype(jnp.int32)
    expert = info[:, INFO_E0:INFO_E1 + 1].astype(jnp.int32)
    rank = info[:, INFO_R0:INFO_R1 + 1].astype(jnp.int32)
    pos_flat = (group_start[expert] + rank).reshape(-1)

    xs = _dispatch(x, od_norm_ffn[0], pos_flat, n_tiles * tme, tm)
    ys = _experts(xs, tile_expert, tile_valid, od_exp_gate[0].astype(BF16),
                  od_exp_up[0].astype(BF16), od_exp_down[0].astype(BF16), d, tme, fc_expert,
                  fc_dense)
    out = _combine(x, info, pos_flat, ys, final_norm, tm)
    return out.reshape(b, s, d)


def kernel(x, rel_bias, ev_norm_mix, ev_w_in, ev_pool_w, ev_pool_scale, ev_conv_w, ev_w_out, ev_norm_ffn, ev_ffn_gate, ev_ffn_up, ev_ffn_down, od_norm_mix, od_w_qkv, od_b_qkv, od_sinks, od_w_o, od_b_o, od_norm_ffn, od_router, od_exp_gate, od_exp_up, od_exp_down, final_norm):
    return _forward(x, rel_bias, ev_norm_mix, ev_w_in, ev_pool_w, ev_pool_scale, ev_conv_w, ev_w_out,
                    ev_norm_ffn, ev_ffn_gate, ev_ffn_up, ev_ffn_down, od_norm_mix, od_w_qkv, od_b_qkv,
                    od_sinks, od_w_o, od_b_o, od_norm_ffn, od_router, od_exp_gate, od_exp_up,
                    od_exp_down, final_norm, tm=512, tq=512, tme=512, fc_dense=256, fc_expert=1792)
```

```python
import functools
import math

import jax
import jax.numpy as jnp
from jax import lax
from jax.experimental import pallas as pl
from jax.experimental.pallas import tpu as pltpu

F32 = jnp.float32
BF16 = jnp.bfloat16

RMS_EPS = 1e-6
CHUNK = 64
POOL_WINDOWS = (2, 4, 8, 16)
CONV_K = 3
HEAD_DIM = 64
N_KV_HEADS = 2
WINDOW_CHUNKS = 2
KV_SPAN = (WINDOW_CHUNKS + 1) * CHUNK
N_BUCKETS = 32
MAX_DISTANCE = 128
TOP_K = 2
NEG_INF = -1e30

LANES = 128
SUBLANES = 8
VMEM_LIMIT = 56 * 1024 * 1024

POOL_HALO = 16
CONV_HALO = 8

def _rms(x, g):
    return x * lax.rsqrt(jnp.mean(x * x, axis=-1, keepdims=True) + RMS_EPS) * g


def _params(n_axes, vmem=VMEM_LIMIT):
    return pltpu.CompilerParams(dimension_semantics=("arbitrary",) * n_axes,
                                vmem_limit_bytes=vmem)


def _const_spec(shape):
    nd = len(shape)
    return pl.BlockSpec(shape, lambda *_: (0,) * nd, pipeline_mode=pl.Buffered(1))


def _mixer_kernel(x_ref, g_ref, win_ref, pw_ref, ps_ref, cw_ref, wout_ref, o_ref, ubuf, zbuf):
    s = pl.program_id(1)
    tm = x_ref.shape[0]
    pool_w = ubuf.shape[1]
    grp = pool_w // len(POOL_WINDOWS)
    conv_c = zbuf.shape[1]

    @pl.when(s == 0)
    def _():
        ubuf[0:POOL_HALO, :] = jnp.zeros((POOL_HALO, pool_w), F32)
        zbuf[0:CONV_HALO, :] = jnp.zeros((CONV_HALO, conv_c), F32)

    @pl.when(s > 0)
    def _():
        ubuf[0:POOL_HALO, :] = ubuf[tm:tm + POOL_HALO, :]
        zbuf[0:CONV_HALO, :] = zbuf[tm:tm + CONV_HALO, :]

    x = x_ref[...]
    h = _rms(x, g_ref[...]).astype(BF16)
    proj = jnp.dot(h, win_ref[...], preferred_element_type=F32)
    u = proj[:, :pool_w]
    gate_post = proj[:, pool_w:pool_w + conv_c]
    gate_pre = proj[:, pool_w + conv_c:pool_w + 2 * conv_c]
    v = proj[:, pool_w + 2 * conv_c:]
    ubuf[POOL_HALO:POOL_HALO + tm, :] = u
    z = gate_pre * v
    zbuf[CONV_HALO:CONV_HALO + tm, :] = z

    t = s * tm + lax.broadcasted_iota(jnp.int32, (tm, 1), 0)
    parts = []
    for g, w in enumerate(POOL_WINDOWS):
        lo, hi = g * grp, (g + 1) * grp
        ug = u[:, lo:hi]
        acc = ug
        for k in range(1, w):
            acc = acc + ubuf[POOL_HALO - k:POOL_HALO - k + tm, lo:hi]
        count = jnp.minimum(t + 1, w).astype(F32)
        pooled = acc / count - ug
        mixed = jnp.dot(pooled.astype(BF16), pw_ref[g], preferred_element_type=F32)
        parts.append(mixed * ps_ref[:, lo:hi])
    conv = cw_ref[CONV_K - 1:CONV_K, :] * z
    for k in range(CONV_K - 1):
        off = CONV_HALO - (CONV_K - 1) + k
        conv = conv + cw_ref[k:k + 1, :] * zbuf[off:off + tm, :]
    parts.append(gate_post * conv)
    y = jnp.concatenate(parts, axis=1).astype(BF16)
    o_ref[...] = x + jnp.dot(y, wout_ref[...], preferred_element_type=F32)


def _mixer(x, g, w_in, pool_w, pool_scale, conv_w, w_out, tm):
    b, s, d = x.shape
    pool_width = pool_scale.shape[-1]
    conv_c = conv_w.shape[-1]
    return pl.pallas_call(
        _mixer_kernel,
        out_shape=jax.ShapeDtypeStruct((b, s, d), F32),
        grid=(b, s // tm),
        in_specs=[
            pl.BlockSpec((None, tm, d), lambda i, j: (i, j, 0)),
            _const_spec((1, d)),
            _const_spec(w_in.shape),
            _const_spec(pool_w.shape),
            _const_spec((1, pool_width)),
            _const_spec(conv_w.shape),
            _const_spec(w_out.shape),
        ],
        out_specs=pl.BlockSpec((None, tm, d), lambda i, j: (i, j, 0)),
        scratch_shapes=[pltpu.VMEM((POOL_HALO + tm, pool_width), F32),
                        pltpu.VMEM((CONV_HALO + tm, conv_c), F32)],
        compiler_params=_params(2),
        name="l0_mixer",
    )(x, g.reshape(1, d), w_in, pool_w, pool_scale.reshape(1, pool_width), conv_w, w_out)


def _ffn_kernel(x_ref, g_ref, wg_ref, wu_ref, wd_ref, o_ref, h_scr, t_scr, *, fc):
    x = x_ref[...]
    h_scr[...] = _rms(x, g_ref[...]).astype(BF16)
    n_chunks = wg_ref.shape[1] // fc

    def body(c, carry):
        off = pl.multiple_of(c * fc, fc)
        h = h_scr[...]
        a = jnp.dot(h, wg_ref[:, pl.ds(off, fc)], preferred_element_type=F32)
        b = jnp.dot(h, wu_ref[:, pl.ds(off, fc)], preferred_element_type=F32)
        t_scr[:, pl.ds(off, fc)] = (a * jax.nn.sigmoid(a) * b).astype(BF16)
        return carry

    lax.fori_loop(0, n_chunks, body, 0, unroll=True)
    o_ref[...] = x + jnp.dot(t_scr[...], wd_ref[...], preferred_element_type=F32)


def _ffn(x, g, w_gate, w_up, w_down, tm, fc):
    n, d = x.shape
    return pl.pallas_call(
        functools.partial(_ffn_kernel, fc=fc),
        out_shape=jax.ShapeDtypeStruct((n, d), F32),
        grid=(n // tm,),
        in_specs=[
            pl.BlockSpec((tm, d), lambda i: (i, 0)),
            _const_spec((1, d)),
            _const_spec(w_gate.shape),
            _const_spec(w_up.shape),
            _const_spec(w_down.shape),
        ],
        out_specs=pl.BlockSpec((tm, d), lambda i: (i, 0)),
        scratch_shapes=[pltpu.VMEM((tm, d), BF16), pltpu.VMEM((tm, w_gate.shape[1]), BF16)],
        compiler_params=_params(1),
        name="l0_ffn",
    )(x, g.reshape(1, d), w_gate, w_up, w_down)


_NT = (((1,), (1,)), ((), ()))
_TN = (((0,), (0,)), ((), ()))


def _qkv_kernel(x_ref, g_ref, wqt_ref, bq_ref, wk_ref, bk_ref, wvt_ref, bv_ref, qt_ref, k_ref, vt_ref):
    h = _rms(x_ref[...], g_ref[...]).astype(BF16)
    qt = lax.dot_general(wqt_ref[...], h, _NT, preferred_element_type=F32) + bq_ref[...]
    qt_ref[...] = (qt * (HEAD_DIM ** -0.5)).astype(BF16)
    k_ref[...] = (jnp.dot(h, wk_ref[...], preferred_element_type=F32) + bk_ref[...]).astype(BF16)
    vt = lax.dot_general(wvt_ref[...], h, _NT, preferred_element_type=F32) + bv_ref[...]
    vt_ref[...] = vt.astype(BF16)


def _qkv(x, g, w_qkv, b_qkv, dq, dk, tm):
    b, s, d = x.shape
    wqt = w_qkv[:, :dq].T.astype(BF16)
    wk = w_qkv[:, dq:dq + dk].astype(BF16)
    wvt = w_qkv[:, dq + dk:].T.astype(BF16)
    bq = b_qkv[:dq].reshape(dq, 1)
    bk = b_qkv[dq:dq + dk].reshape(1, dk)
    bv = b_qkv[dq + dk:].reshape(dk, 1)
    return pl.pallas_call(
        _qkv_kernel,
        out_shape=(jax.ShapeDtypeStruct((b, dq, s), BF16),
                   jax.ShapeDtypeStruct((b, s, dk), BF16),
                   jax.ShapeDtypeStruct((b, dk, s), BF16)),
        grid=(b, s // tm),
        in_specs=[
            pl.BlockSpec((None, tm, d), lambda bi, i: (bi, i, 0)),
            _const_spec((1, d)),
            _const_spec(wqt.shape), _const_spec(bq.shape),
            _const_spec(wk.shape), _const_spec(bk.shape),
            _const_spec(wvt.shape), _const_spec(bv.shape),
        ],
        out_specs=(pl.BlockSpec((None, dq, tm), lambda bi, i: (bi, 0, i)),
                   pl.BlockSpec((None, tm, dk), lambda bi, i: (bi, i, 0)),
                   pl.BlockSpec((None, dk, tm), lambda bi, i: (bi, 0, i))),
        compiler_params=_params(2),
        name="l1_qkv",
    )(x, g.reshape(1, d), wqt, bq, wk, bk, wvt, bv)


def _bias_kernel(onehot_ref, rb_ref, o_ref):
    o_ref[...] = jnp.dot(onehot_ref[...], rb_ref[...], preferred_element_type=F32,
                         precision=lax.Precision.HIGHEST)


def _t5_bucket(rel):
    nb = N_BUCKETS // 2
    max_exact = nb // 2
    ret = jnp.where(rel > 0, nb, 0)
    n = jnp.abs(rel)
    nf = jnp.maximum(n, 1).astype(jnp.float32)
    large = max_exact + (jnp.log(nf / max_exact) / math.log(MAX_DISTANCE / max_exact)
                         * (nb - max_exact)).astype(jnp.int32)
    large = jnp.minimum(large, nb - 1)
    return ret + jnp.where(n < max_exact, n, large)


def _rel_bias_table(rel_bias):
    n_heads = rel_bias.shape[1]
    pad = WINDOW_CHUNKS * CHUNK
    rel = (jnp.arange(KV_SPAN) - pad)[None, :] - jnp.arange(CHUNK)[:, None]
    bucket = _t5_bucket(rel).reshape(-1)
    onehot = (bucket[:, None] == jnp.arange(N_BUCKETS)[None, :]).astype(F32)
    rows = CHUNK * KV_SPAN
    rb = jnp.pad(rel_bias.astype(F32), ((0, 0), (0, LANES - n_heads)))
    out = pl.pallas_call(
        _bias_kernel,
        out_shape=jax.ShapeDtypeStruct((rows, LANES), F32),
        name="l1_rel_bias",
    )(onehot, rb)
    return jnp.transpose(out[:, :n_heads].reshape(CHUNK, KV_SPAN, n_heads), (2, 0, 1))


PAIR = 2 * CHUNK
PAIR_KEYS = KV_SPAN + CHUNK


def _attn_kernel(qt_ref, kp_ref, km_ref, vtp_ref, vtm_ref, bias_ref, sink_ref, ot_ref, kbuf, vtbuf,
                 s_a, s_b, *, n_heads):
    i = pl.program_id(1)
    tq = qt_ref.shape[1]
    pad = WINDOW_CHUNKS * CHUNK
    per_kv = n_heads // N_KV_HEADS
    kbuf[0:pad, :] = kp_ref[...]
    kbuf[pad:pad + tq, :] = km_ref[...]
    vtbuf[:, 0:pad] = vtp_ref[...]
    vtbuf[:, pad:pad + tq] = vtm_ref[...]
    zeros = jnp.zeros((HEAD_DIM, PAIR), BF16)
    n_pairs = tq // PAIR

    def scores(jj, s_scr):
        c0 = jj * PAIR
        kc = kbuf[c0:c0 + PAIR_KEYS, :]
        for h in range(n_heads):
            pieces = [zeros] * N_KV_HEADS
            pieces[h // per_kv] = qt_ref[h * HEAD_DIM:(h + 1) * HEAD_DIM, c0:c0 + PAIR]
            rhs = jnp.concatenate(pieces, axis=0)
            s_scr[h] = jnp.dot(kc, rhs, preferred_element_type=F32)

    def softmax_values(jj, s_scr):
        c0 = jj * PAIR
        variant = jnp.where(i == 0, 1, 0) if jj == 0 else 0
        for h in range(n_heads):
            kv = h // per_kv
            s = s_scr[h] + bias_ref[variant, h]
            sk = sink_ref[h]
            m_ = jnp.maximum(jnp.max(s, axis=0, keepdims=True), sk)
            p = jnp.exp(s - m_)
            denom = jnp.sum(p, axis=0, keepdims=True) + jnp.exp(sk - m_)
            p = (p / denom).astype(BF16)
            vt = vtbuf[kv * HEAD_DIM:(kv + 1) * HEAD_DIM, c0:c0 + PAIR_KEYS]
            o = jnp.dot(vt, p, preferred_element_type=F32)
            ot_ref[h * HEAD_DIM:(h + 1) * HEAD_DIM, c0:c0 + PAIR] = o.astype(BF16)

    bufs = (s_a, s_b)
    scores(0, bufs[0])
    for jj in range(n_pairs):
        if jj + 1 < n_pairs:
            scores(jj + 1, bufs[(jj + 1) % 2])
        softmax_values(jj, bufs[jj % 2])


def _pair_bias(bias):
    bt = jnp.transpose(bias, (0, 2, 1)).astype(F32)
    neg = jnp.full((bias.shape[0], CHUNK, CHUNK), NEG_INF, F32)
    first = jnp.concatenate([bt, neg], axis=1)
    second = jnp.concatenate([neg, bt], axis=1)
    table = jnp.concatenate([first, second], axis=2)
    key = jnp.arange(PAIR_KEYS)[None, :, None]
    masked = jnp.where(key < WINDOW_CHUNKS * CHUNK, NEG_INF, table)
    return jnp.stack([table, masked])


def _attention(qt, k, vt, bias, sinks, tq):
    b, dq, s = qt.shape
    dk = k.shape[-1]
    n_heads = dq // HEAD_DIM
    pad = WINDOW_CHUNKS * CHUNK
    ratio = tq // pad
    table = _pair_bias(bias)
    prev = lambda i: jnp.maximum(i * ratio - 1, 0)
    return pl.pallas_call(
        functools.partial(_attn_kernel, n_heads=n_heads),
        out_shape=jax.ShapeDtypeStruct((b, dq, s), BF16),
        grid=(b, s // tq),
        in_specs=[
            pl.BlockSpec((None, dq, tq), lambda bi, i: (bi, 0, i)),
            pl.BlockSpec((None, pad, dk), lambda bi, i: (bi, prev(i), 0)),
            pl.BlockSpec((None, tq, dk), lambda bi, i: (bi, i, 0)),
            pl.BlockSpec((None, dk, pad), lambda bi, i: (bi, 0, prev(i))),
            pl.BlockSpec((None, dk, tq), lambda bi, i: (bi, 0, i)),
            _const_spec(table.shape),
            pl.BlockSpec(memory_space=pltpu.SMEM),
        ],
        out_specs=pl.BlockSpec((None, dq, tq), lambda bi, i: (bi, 0, i)),
        scratch_shapes=[pltpu.VMEM((pad + tq, dk), BF16), pltpu.VMEM((dk, pad + tq), BF16),
                        pltpu.VMEM((n_heads, PAIR_KEYS, PAIR), F32),
                        pltpu.VMEM((n_heads, PAIR_KEYS, PAIR), F32)],
        compiler_params=_params(2),
        name="l1_attention",
    )(qt, k, k, vt, vt, table, sinks.astype(F32))


INFO_E0, INFO_E1, INFO_R0, INFO_R1, INFO_W0, INFO_W1 = range(6)


def _oproj_router_kernel(ot_ref, x_ref, wo_ref, bo_ref, g_ref, wr_ref, x3_ref, info_ref, cnt_ref,
                         carry, *, n_experts):
    i = pl.program_id(0)
    tm = x_ref.shape[0]

    @pl.when(i == 0)
    def _():
        carry[...] = jnp.zeros_like(carry)

    attn = lax.dot_general(ot_ref[...], wo_ref[...], _TN, preferred_element_type=F32)
    x3 = x_ref[...] + attn + bo_ref[...]
    x3_ref[...] = x3
    h = _rms(x3, g_ref[...])
    h_hi = h.astype(BF16)
    h_lo = (h - h_hi.astype(F32)).astype(BF16)
    wr = wr_ref[...]
    hw = jnp.dot(h_hi, wr, preferred_element_type=F32)
    logits = (hw[:, :LANES] + hw[:, LANES:]) + jnp.dot(h_lo, wr[:, :LANES], preferred_element_type=F32)
    lane_i = lax.broadcasted_iota(jnp.int32, (tm, LANES), 1)
    lane = lane_i.astype(F32)
    neg = jnp.float32(-jnp.inf)
    logits = jnp.where(lane_i < n_experts, logits, neg)
    m0 = jnp.max(logits, axis=-1, keepdims=True)
    e0 = jnp.min(jnp.where(logits == m0, lane, float(LANES)), axis=-1, keepdims=True)
    rest = jnp.where(lane == e0, neg, logits)
    m1 = jnp.max(rest, axis=-1, keepdims=True)
    e1 = jnp.min(jnp.where(rest == m1, lane, float(LANES)), axis=-1, keepdims=True)
    t = jnp.exp(m1 - m0)
    w0 = 1.0 / (1.0 + t)
    w1 = t / (1.0 + t)
    hot0 = lane == e0
    hot1 = lane == e1
    both = jnp.where(hot0 | hot1, 1.0, 0.0).astype(BF16)
    row = lax.broadcasted_iota(jnp.int32, (tm, tm), 0)
    col = lax.broadcasted_iota(jnp.int32, (tm, tm), 1)
    tri = (col < row).astype(BF16)
    before = jnp.dot(tri, both, preferred_element_type=F32) + carry[...]
    r0 = jnp.sum(jnp.where(hot0, before, 0.0), axis=-1, keepdims=True)
    r1 = jnp.sum(jnp.where(hot1, before, 0.0), axis=-1, keepdims=True)
    carry[...] += jnp.sum(both.astype(F32), axis=0, keepdims=True)
    cnt_ref[...] = carry[...]
    info = jnp.zeros((tm, LANES), F32)
    for idx, val in ((INFO_E0, e0), (INFO_E1, e1), (INFO_R0, r0),
                     (INFO_R1, r1), (INFO_W0, w0), (INFO_W1, w1)):
        info = jnp.where(lane_i == idx, val, info)
    info_ref[...] = info


def _oproj_router(ot, x, w_o, b_o, g, w_router, tm):
    n, d = x.shape
    dq, seq = ot.shape[1:]
    per_seq = seq // tm
    n_experts = w_router.shape[1]
    wr = jnp.pad(w_router.astype(F32), ((0, 0), (0, LANES - n_experts)))
    wr_hi = wr.astype(BF16)
    wr_lo = (wr - wr_hi.astype(F32)).astype(BF16)
    wr = jnp.concatenate([wr_hi, wr_lo], axis=1)
    return pl.pallas_call(
        functools.partial(_oproj_router_kernel, n_experts=n_experts),
        out_shape=(jax.ShapeDtypeStruct((n, d), F32),
                   jax.ShapeDtypeStruct((n, LANES), F32),
                   jax.ShapeDtypeStruct((1, LANES), F32)),
        grid=(n // tm,),
        in_specs=[
            pl.BlockSpec((None, dq, tm), lambda i: (i // per_seq, 0, i % per_seq)),
            pl.BlockSpec((tm, d), lambda i: (i, 0)),
            _const_spec(w_o.shape),
            _const_spec((1, d)),
            _const_spec((1, d)),
            _const_spec(wr.shape),
        ],
        out_specs=(pl.BlockSpec((tm, d), lambda i: (i, 0)),
                   pl.BlockSpec((tm, LANES), lambda i: (i, 0)),
                   pl.BlockSpec((1, LANES), lambda i: (0, 0))),
        scratch_shapes=[pltpu.VMEM((1, LANES), F32)],
        compiler_params=_params(1),
        name="l1_oproj_router",
    )(ot, x, w_o, b_o.reshape(1, d), g.reshape(1, d), wr)


def _row_copy(src_ref, src_row, dst_ref, dst_row, sem):
    s0 = pl.multiple_of(src_row * SUBLANES, SUBLANES)
    d0 = pl.multiple_of(dst_row * SUBLANES, SUBLANES)
    return pltpu.make_async_copy(src_ref.at[pl.ds(s0, SUBLANES)], dst_ref.at[pl.ds(d0, SUBLANES)], sem)


def _dispatch_kernel(pos_ref, x_ref, g_ref, zeros_hbm, xs_hbm, hbuf, sem):
    del zeros_hbm
    tm = x_ref.shape[0]
    h = _rms(x_ref[...], g_ref[...])
    for c in range(x_ref.shape[1] // LANES):
        hbuf[pl.ds(c, tm, stride=SUBLANES), :] = h[:, c * LANES:(c + 1) * LANES]

    def copies(r):
        return [_row_copy(hbuf, r, xs_hbm, pos_ref[TOP_K * r + k], sem) for k in range(TOP_K)]

    def start(r, carry):
        for cp in copies(r):
            cp.start()
        return carry

    def wait(r, carry):
        for cp in copies(r):
            cp.wait()
        return carry

    lax.fori_loop(0, tm, start, 0)
    lax.fori_loop(0, tm, wait, 0)


def _dispatch(x, g, pos_flat, n_rows, tm):
    n, d = x.shape
    zeros = jnp.zeros((n_rows * SUBLANES, LANES), F32)
    return pl.pallas_call(
        _dispatch_kernel,
        out_shape=jax.ShapeDtypeStruct((n_rows * SUBLANES, LANES), F32),
        grid=(n // tm,),
        in_specs=[
            pl.BlockSpec((TOP_K * tm,), lambda i: (i,), memory_space=pltpu.SMEM),
            pl.BlockSpec((tm, d), lambda i: (i, 0)),
            _const_spec((1, d)),
            pl.BlockSpec(memory_space=pl.ANY),
        ],
        out_specs=pl.BlockSpec(memory_space=pl.ANY),
        scratch_shapes=[pltpu.VMEM((tm * SUBLANES, LANES), F32), pltpu.SemaphoreType.DMA(())],
        input_output_aliases={3: 0},
        compiler_params=_params(1),
        name="l1_moe_dispatch",
    )(pos_flat, x, g.reshape(1, d), zeros)


def _experts_kernel(te_ref, tv_ref, xs_ref, wg_ref, wu_ref, wd_ref, ys_ref, xb, acc, t_scr, *, sub):
    j = pl.program_id(0)
    c = pl.program_id(1)
    tm = xb.shape[0]
    n_slabs = xb.shape[1] // LANES
    valid = tv_ref[j] == 1

    @pl.when(jnp.logical_and(valid, c == 0))
    def _():
        slabs = [xs_ref[pl.ds(s, tm, stride=SUBLANES), :] for s in range(n_slabs)]
        xb[...] = jnp.concatenate(slabs, axis=1).astype(BF16)
        acc[...] = jnp.zeros_like(acc)

    @pl.when(valid)
    def _():
        def body(q, carry):
            off = pl.multiple_of(q * sub, sub)
            x = xb[...]
            a = jnp.dot(x, wg_ref[:, pl.ds(off, sub)], preferred_element_type=F32)
            b = jnp.dot(x, wu_ref[:, pl.ds(off, sub)], preferred_element_type=F32)
            t_scr[:, pl.ds(off, sub)] = (a * jax.nn.sigmoid(a) * b).astype(BF16)
            return carry
        lax.fori_loop(0, wg_ref.shape[1] // sub, body, 0, unroll=True)
        acc[...] += jnp.dot(t_scr[...], wd_ref[...], preferred_element_type=F32)

    is_last = c == pl.num_programs(1) - 1

    @pl.when(jnp.logical_and(valid, is_last))
    def _():
        y = acc[...]
        for s in range(n_slabs):
            ys_ref[pl.ds(s, tm, stride=SUBLANES), :] = y[:, s * LANES:(s + 1) * LANES]

    @pl.when(jnp.logical_and(jnp.logical_not(valid), is_last))
    def _():
        ys_ref[...] = jnp.zeros_like(ys_ref)


def _experts(xs, tile_expert, tile_valid, w_gate, w_up, w_down, d, tm, fc, sub):
    n_tiles = tile_expert.shape[0]
    n_fc = w_gate.shape[2] // fc
    last = n_fc - 1

    def wsel(c, tv, j):
        return jnp.where(tv[j] == 1, c, last)

    grid_spec = pltpu.PrefetchScalarGridSpec(
        num_scalar_prefetch=2,
        grid=(n_tiles, n_fc),
        in_specs=[
            pl.BlockSpec((tm * SUBLANES, LANES), lambda j, c, te, tv: (j, 0)),
            pl.BlockSpec((None, d, fc), lambda j, c, te, tv: (te[j], 0, wsel(c, tv, j))),
            pl.BlockSpec((None, d, fc), lambda j, c, te, tv: (te[j], 0, wsel(c, tv, j))),
            pl.BlockSpec((None, fc, d), lambda j, c, te, tv: (te[j], wsel(c, tv, j), 0)),
        ],
        out_specs=pl.BlockSpec((tm * SUBLANES, LANES), lambda j, c, te, tv: (j, 0)),
        scratch_shapes=[pltpu.VMEM((tm, d), BF16), pltpu.VMEM((tm, d), F32),
                        pltpu.VMEM((tm, fc), BF16)],
    )
    return pl.pallas_call(
        functools.partial(_experts_kernel, sub=sub),
        out_shape=jax.ShapeDtypeStruct(xs.shape, F32),
        grid_spec=grid_spec,
        compiler_params=_params(2),
        name="l1_moe_experts",
    )(tile_expert, tile_valid, xs, w_gate, w_up, w_down)


def _combine_kernel(pos_ref, x_ref, info_ref, g_ref, ys_hbm, o_ref, ybuf, sem):
    tm = x_ref.shape[0]
    n_slabs = x_ref.shape[1] // LANES

    def copies(r):
        return [_row_copy(ys_hbm, pos_ref[TOP_K * r + k], ybuf, k * tm + r, sem) for k in range(TOP_K)]

    def start(r, carry):
        for cp in copies(r):
            cp.start()
        return carry

    def wait(r, carry):
        for cp in copies(r):
            cp.wait()
        return carry

    lax.fori_loop(0, tm, start, 0)
    lax.fori_loop(0, tm, wait, 0)
    info = info_ref[...]
    x = x_ref[...]
    for k, col in enumerate((INFO_W0, INFO_W1)):
        wk = info[:, col:col + 1]
        slabs = [ybuf[pl.ds(k * tm * SUBLANES + s, tm, stride=SUBLANES), :] for s in range(n_slabs)]
        x = x + wk * jnp.concatenate(slabs, axis=1)
    o_ref[...] = _rms(x, g_ref[...])


def _combine(x, info, pos_flat, ys, g, tm):
    n, d = x.shape
    return pl.pallas_call(
        _combine_kernel,
        out_shape=jax.ShapeDtypeStruct((n, d), F32),
        grid=(n // tm,),
        in_specs=[
            pl.BlockSpec((TOP_K * tm,), lambda i: (i,), memory_space=pltpu.SMEM),
            pl.BlockSpec((tm, d), lambda i: (i, 0)),
            pl.BlockSpec((tm, LANES), lambda i: (i, 0)),
            _const_spec((1, d)),
            pl.BlockSpec(memory_space=pl.ANY),
        ],
        out_specs=pl.BlockSpec((tm, d), lambda i: (i, 0)),
        scratch_shapes=[pltpu.VMEM((TOP_K * tm * SUBLANES, LANES), F32), pltpu.SemaphoreType.DMA(())],
        compiler_params=_params(1),
        name="l1_moe_combine",
    )(pos_flat, x, info, g.reshape(1, d), ys)


def _forward(x, rel_bias, ev_norm_mix, ev_w_in, ev_pool_w, ev_pool_scale, ev_conv_w, ev_w_out,
             ev_norm_ffn, ev_ffn_gate, ev_ffn_up, ev_ffn_down, od_norm_mix, od_w_qkv, od_b_qkv,
             od_sinks, od_w_o, od_b_o, od_norm_ffn, od_router, od_exp_gate, od_exp_up,
             od_exp_down, final_norm, *, tm, tq, tme, fc_dense, fc_expert):
    b, s, d = x.shape
    n = b * s
    n_heads = od_sinks.shape[-1]
    dq = n_heads * HEAD_DIM
    dk = N_KV_HEADS * HEAD_DIM
    n_experts = od_router.shape[-1]

    x = _mixer(x, ev_norm_mix[0], ev_w_in[0].astype(BF16), ev_pool_w[0].astype(BF16),
               ev_pool_scale[0], ev_conv_w[0], ev_w_out[0].astype(BF16), tm)
    x = x.reshape(n, d)
    x = _ffn(x, ev_norm_ffn[0], ev_ffn_gate[0].astype(BF16), ev_ffn_up[0].astype(BF16),
             ev_ffn_down[0].astype(BF16), tm, fc_dense)

    qt, k, vt = _qkv(x.reshape(b, s, d), od_norm_mix[0], od_w_qkv[0], od_b_qkv[0], dq, dk, tm)
    bias = _rel_bias_table(rel_bias)
    attn_t = _attention(qt, k, vt, bias, od_sinks[0], tq)
    x, info, counts = _oproj_router(attn_t, x, od_w_o[0].astype(BF16), od_b_o[0], od_norm_ffn[0],
                                    od_router[0], tm)

    counts = counts[0, :n_experts].astype(jnp.int32)
    tiles_per = (counts + tme - 1) // tme
    tile_end = jnp.cumsum(tiles_per)
    group_start = (tile_end - tiles_per) * tme
    n_tiles = (TOP_K * n) // tme + n_experts
    tile_ids = jnp.arange(n_tiles, dtype=jnp.int32)
    tile_valid = (tile_ids < tile_end[-1]).astype(jnp.int32)
    tile_expert = jnp.searchsorted(tile_end, jnp.minimum(tile_ids, tile_end[-1] - 1),
                                   side="right").astype(jnp.int32)
    expert = info[:, INFO_E0:INFO_E1 + 1].astype(jnp.int32)
    rank = info[:, INFO_R0:INFO_R1 + 1].astype(jnp.int32)
    pos_flat = (group_start[expert] + rank).reshape(-1)

    xs = _dispatch(x, od_norm_ffn[0], pos_flat, n_tiles * tme, tm)
    ys = _experts(xs, tile_expert, tile_valid, od_exp_gate[0].astype(BF16),
                  od_exp_up[0].astype(BF16), od_exp_down[0].astype(BF16), d, tme, fc_expert,
                  fc_dense)
    out = _combine(x, info, pos_flat, ys, final_norm, tm)
    return out.reshape(b, s, d)


def kernel(x, rel_bias, ev_norm_mix, ev_w_in, ev_pool_w, ev_pool_scale, ev_conv_w, ev_w_out, ev_norm_ffn, ev_ffn_gate, ev_ffn_up, ev_ffn_down, od_norm_mix, od_w_qkv, od_b_qkv, od_sinks, od_w_o, od_b_o, od_norm_ffn, od_router, od_exp_gate, od_exp_up, od_exp_down, final_norm):
    return _forward(x, rel_bias, ev_norm_mix, ev_w_in, ev_pool_w, ev_pool_scale, ev_conv_w, ev_w_out,
                    ev_norm_ffn, ev_ffn_gate, ev_ffn_up, ev_ffn_down, od_norm_mix, od_w_qkv, od_b_qkv,
                    od_sinks, od_w_o, od_b_o, od_norm_ffn, od_router, od_exp_gate, od_exp_up,
                    od_exp_down, final_norm, tm=512, tq=512, tme=512, fc_dense=256, fc_expert=1792)
```

```python
import functools
import math

import jax
import jax.numpy as jnp
from jax import lax
from jax.experimental import pallas as pl
from jax.experimental.pallas import tpu as pltpu

F32 = jnp.float32
BF16 = jnp.bfloat16

RMS_EPS = 1e-6
CHUNK = 64
POOL_WINDOWS = (2, 4, 8, 16)
CONV_K = 3
HEAD_DIM = 64
N_KV_HEADS = 2
WINDOW_CHUNKS = 2
KV_SPAN = (WINDOW_CHUNKS + 1) * CHUNK
N_BUCKETS = 32
MAX_DISTANCE = 128
TOP_K = 2
NEG_INF = -1e30

LANES = 128
SUBLANES = 8
VMEM_LIMIT = 56 * 1024 * 1024

POOL_HALO = 16
CONV_HALO = 8

def _rms(x, g):
    return x * lax.rsqrt(jnp.mean(x * x, axis=-1, keepdims=True) + RMS_EPS) * g


def _params(n_axes, vmem=VMEM_LIMIT):
    return pltpu.CompilerParams(dimension_semantics=("arbitrary",) * n_axes,
                                vmem_limit_bytes=vmem)


def _const_spec(shape):
    nd = len(shape)
    return pl.BlockSpec(shape, lambda *_: (0,) * nd, pipeline_mode=pl.Buffered(1))


def _mixer_kernel(x_ref, g_ref, win_ref, pw_ref, ps_ref, cw_ref, wout_ref, o_ref, ubuf, zbuf):
    s = pl.program_id(1)
    tm = x_ref.shape[0]
    pool_w = ubuf.shape[1]
    grp = pool_w // len(POOL_WINDOWS)
    conv_c = zbuf.shape[1]

    @pl.when(s == 0)
    def _():
        ubuf[0:POOL_HALO, :] = jnp.zeros((POOL_HALO, pool_w), F32)
        zbuf[0:CONV_HALO, :] = jnp.zeros((CONV_HALO, conv_c), F32)

    @pl.when(s > 0)
    def _():
        ubuf[0:POOL_HALO, :] = ubuf[tm:tm + POOL_HALO, :]
        zbuf[0:CONV_HALO, :] = zbuf[tm:tm + CONV_HALO, :]

    x = x_ref[...]
    h = _rms(x, g_ref[...]).astype(BF16)
    proj = jnp.dot(h, win_ref[...], preferred_element_type=F32)
    u = proj[:, :pool_w]
    gate_post = proj[:, pool_w:pool_w + conv_c]
    gate_pre = proj[:, pool_w + conv_c:pool_w + 2 * conv_c]
    v = proj[:, pool_w + 2 * conv_c:]
    ubuf[POOL_HALO:POOL_HALO + tm, :] = u
    z = gate_pre * v
    zbuf[CONV_HALO:CONV_HALO + tm, :] = z

    t = s * tm + lax.broadcasted_iota(jnp.int32, (tm, 1), 0)
    parts = []
    for g, w in enumerate(POOL_WINDOWS):
        lo, hi = g * grp, (g + 1) * grp
        ug = u[:, lo:hi]
        acc = ug
        for k in range(1, w):
            acc = acc + ubuf[POOL_HALO - k:POOL_HALO - k + tm, lo:hi]
        count = jnp.minimum(t + 1, w).astype(F32)
        pooled = acc / count - ug
        mixed = jnp.dot(pooled.astype(BF16), pw_ref[g], preferred_element_type=F32)
        parts.append(mixed * ps_ref[:, lo:hi])
    conv = cw_ref[CONV_K - 1:CONV_K, :] * z
    for k in range(CONV_K - 1):
        off = CONV_HALO - (CONV_K - 1) + k
        conv = conv + cw_ref[k:k + 1, :] * zbuf[off:off + tm, :]
    parts.append(gate_post * conv)
    y = jnp.concatenate(parts, axis=1).astype(BF16)
    o_ref[...] = x + jnp.dot(y, wout_ref[...], preferred_element_type=F32)


def _mixer(x, g, w_in, pool_w, pool_scale, conv_w, w_out, tm):
    b, s, d = x.shape
    pool_width = pool_scale.shape[-1]
    conv_c = conv_w.shape[-1]
    return pl.pallas_call(
        _mixer_kernel,
        out_shape=jax.ShapeDtypeStruct((b, s, d), F32),
        grid=(b, s // tm),
        in_specs=[
            pl.BlockSpec((None, tm, d), lambda i, j: (i, j, 0)),
            _const_spec((1, d)),
            _const_spec(w_in.shape),
            _const_spec(pool_w.shape),
            _const_spec((1, pool_width)),
            _const_spec(conv_w.shape),
            _const_spec(w_out.shape),
        ],
        out_specs=pl.BlockSpec((None, tm, d), lambda i, j: (i, j, 0)),
        scratch_shapes=[pltpu.VMEM((POOL_HALO + tm, pool_width), F32),
                        pltpu.VMEM((CONV_HALO + tm, conv_c), F32)],
        compiler_params=_params(2),
        name="l0_mixer",
    )(x, g.reshape(1, d), w_in, pool_w, pool_scale.reshape(1, pool_width), conv_w, w_out)


def _ffn_kernel(x_ref, g_ref, wg_ref, wu_ref, wd_ref, o_ref, h_scr, t_scr, *, fc):
    x = x_ref[...]
    h_scr[...] = _rms(x, g_ref[...]).astype(BF16)
    n_chunks = wg_ref.shape[1] // fc

    def body(c, carry):
        off = pl.multiple_of(c * fc, fc)
        h = h_scr[...]
        a = jnp.dot(h, wg_ref[:, pl.ds(off, fc)], preferred_element_type=F32)
        b = jnp.dot(h, wu_ref[:, pl.ds(off, fc)], preferred_element_type=F32)
        t_scr[:, pl.ds(off, fc)] = (a * jax.nn.sigmoid(a) * b).astype(BF16)
        return carry

    lax.fori_loop(0, n_chunks, body, 0, unroll=True)
    o_ref[...] = x + jnp.dot(t_scr[...], wd_ref[...], preferred_element_type=F32)


def _ffn(x, g, w_gate, w_up, w_down, tm, fc):
    n, d = x.shape
    return pl.pallas_call(
        functools.partial(_ffn_kernel, fc=fc),
        out_shape=jax.ShapeDtypeStruct((n, d), F32),
        grid=(n // tm,),
        in_specs=[
            pl.BlockSpec((tm, d), lambda i: (i, 0)),
            _const_spec((1, d)),
            _const_spec(w_gate.shape),
            _const_spec(w_up.shape),
            _const_spec(w_down.shape),
        ],
        out_specs=pl.BlockSpec((tm, d), lambda i: (i, 0)),
        scratch_shapes=[pltpu.VMEM((tm, d), BF16), pltpu.VMEM((tm, w_gate.shape[1]), BF16)],
        compiler_params=_params(1),
        name="l0_ffn",
    )(x, g.reshape(1, d), w_gate, w_up, w_down)


_NT = (((1,), (1,)), ((), ()))
_TN = (((0,), (0,)), ((), ()))


def _qkv_kernel(x_ref, g_ref, wqt_ref, bq_ref, wk_ref, bk_ref, wvt_ref, bv_ref, qt_ref, k_ref, vt_ref):
    h = _rms(x_ref[...], g_ref[...]).astype(BF16)
    qt = lax.dot_general(wqt_ref[...], h, _NT, preferred_element_type=F32) + bq_ref[...]
    qt_ref[...] = (qt * (HEAD_DIM ** -0.5)).astype(BF16)
    k_ref[...] = (jnp.dot(h, wk_ref[...], preferred_element_type=F32) + bk_ref[...]).astype(BF16)
    vt = lax.dot_general(wvt_ref[...], h, _NT, preferred_element_type=F32) + bv_ref[...]
    vt_ref[...] = vt.astype(BF16)


def _qkv(x, g, w_qkv, b_qkv, dq, dk, tm):
    b, s, d = x.shape
    wqt = w_qkv[:, :dq].T.astype(BF16)
    wk = w_qkv[:, dq:dq + dk].astype(BF16)
    wvt = w_qkv[:, dq + dk:].T.astype(BF16)
    bq = b_qkv[:dq].reshape(dq, 1)
    bk = b_qkv[dq:dq + dk].reshape(1, dk)
    bv = b_qkv[dq + dk:].reshape(dk, 1)
    return pl.pallas_call(
        _qkv_kernel,
        out_shape=(jax.ShapeDtypeStruct((b, dq, s), BF16),
                   jax.ShapeDtypeStruct((b, s, dk), BF16),
                   jax.ShapeDtypeStruct((b, dk, s), BF16)),
        grid=(b, s // tm),
        in_specs=[
            pl.BlockSpec((None, tm, d), lambda bi, i: (bi, i, 0)),
            _const_spec((1, d)),
            _const_spec(wqt.shape), _const_spec(bq.shape),
            _const_spec(wk.shape), _const_spec(bk.shape),
            _const_spec(wvt.shape), _const_spec(bv.shape),
        ],
        out_specs=(pl.BlockSpec((None, dq, tm), lambda bi, i: (bi, 0, i)),
                   pl.BlockSpec((None, tm, dk), lambda bi, i: (bi, i, 0)),
                   pl.BlockSpec((None, dk, tm), lambda bi, i: (bi, 0, i))),
        compiler_params=_params(2),
        name="l1_qkv",
    )(x, g.reshape(1, d), wqt, bq, wk, bk, wvt, bv)


def _bias_kernel(onehot_ref, rb_ref, o_ref):
    o_ref[...] = jnp.dot(onehot_ref[...], rb_ref[...], preferred_element_type=F32,
                         precision=lax.Precision.HIGHEST)


def _t5_bucket(rel):
    nb = N_BUCKETS // 2
    max_exact = nb // 2
    ret = jnp.where(rel > 0, nb, 0)
    n = jnp.abs(rel)
    nf = jnp.maximum(n, 1).astype(jnp.float32)
    large = max_exact + (jnp.log(nf / max_exact) / math.log(MAX_DISTANCE / max_exact)
                         * (nb - max_exact)).astype(jnp.int32)
    large = jnp.minimum(large, nb - 1)
    return ret + jnp.where(n < max_exact, n, large)


def _rel_bias_table(rel_bias):
    n_heads = rel_bias.shape[1]
    pad = WINDOW_CHUNKS * CHUNK
    rel = (jnp.arange(KV_SPAN) - pad)[None, :] - jnp.arange(CHUNK)[:, None]
    bucket = _t5_bucket(rel).reshape(-1)
    onehot = (bucket[:, None] == jnp.arange(N_BUCKETS)[None, :]).astype(F32)
    rows = CHUNK * KV_SPAN
    rb = jnp.pad(rel_bias.astype(F32), ((0, 0), (0, LANES - n_heads)))
    out = pl.pallas_call(
        _bias_kernel,
        out_shape=jax.ShapeDtypeStruct((rows, LANES), F32),
        name="l1_rel_bias",
    )(onehot, rb)
    return jnp.transpose(out[:, :n_heads].reshape(CHUNK, KV_SPAN, n_heads), (2, 0, 1))


PAIR = 2 * CHUNK
PAIR_KEYS = KV_SPAN + CHUNK


def _attn_kernel(qt_ref, kp_ref, km_ref, vtp_ref, vtm_ref, bias_ref, sink_ref, ot_ref, kbuf, vtbuf,
                 s_a, s_b, *, n_heads):
    i = pl.program_id(1)
    tq = qt_ref.shape[1]
    pad = WINDOW_CHUNKS * CHUNK
    per_kv = n_heads // N_KV_HEADS
    kbuf[0:pad, :] = kp_ref[...]
    kbuf[pad:pad + tq, :] = km_ref[...]
    vtbuf[:, 0:pad] = vtp_ref[...]
    vtbuf[:, pad:pad + tq] = vtm_ref[...]
    zeros = jnp.zeros((HEAD_DIM, PAIR), BF16)
    n_pairs = tq // PAIR

    def scores(jj, s_scr):
        c0 = jj * PAIR
        kc = kbuf[c0:c0 + PAIR_KEYS, :]
        for h in range(n_heads):
            pieces = [zeros] * N_KV_HEADS
            pieces[h // per_kv] = qt_ref[h * HEAD_DIM:(h + 1) * HEAD_DIM, c0:c0 + PAIR]
            rhs = jnp.concatenate(pieces, axis=0)
            s_scr[h] = jnp.dot(kc, rhs, preferred_element_type=F32)

    def softmax_values(jj, s_scr):
        c0 = jj * PAIR
        variant = jnp.where(i == 0, 1, 0) if jj == 0 else 0
        for h in range(n_heads):
            kv = h // per_kv
            s = s_scr[h] + bias_ref[variant, h]
            sk = sink_ref[h]
            m_ = jnp.maximum(jnp.max(s, axis=0, keepdims=True), sk)
            p = jnp.exp(s - m_)
            denom = jnp.sum(p, axis=0, keepdims=True) + jnp.exp(sk - m_)
            p = (p / denom).astype(BF16)
            vt = vtbuf[kv * HEAD_DIM:(kv + 1) * HEAD_DIM, c0:c0 + PAIR_KEYS]
            o = jnp.dot(vt, p, preferred_element_type=F32)
            ot_ref[h * HEAD_DIM:(h + 1) * HEAD_DIM, c0:c0 + PAIR] = o.astype(BF16)

    bufs = (s_a, s_b)
    scores(0, bufs[0])
    for jj in range(n_pairs):
        if jj + 1 < n_pairs:
            scores(jj + 1, bufs[(jj + 1) % 2])
        softmax_values(jj, bufs[jj % 2])


def _pair_bias(bias):
    bt = jnp.transpose(bias, (0, 2, 1)).astype(F32)
    neg = jnp.full((bias.shape[0], CHUNK, CHUNK), NEG_INF, F32)
    first = jnp.concatenate([bt, neg], axis=1)
    second = jnp.concatenate([neg, bt], axis=1)
    table = jnp.concatenate([first, second], axis=2)
    key = jnp.arange(PAIR_KEYS)[None, :, None]
    masked = jnp.where(key < WINDOW_CHUNKS * CHUNK, NEG_INF, table)
    return jnp.stack([table, masked])


def _attention(qt, k, vt, bias, sinks, tq):
    b, dq, s = qt.shape
    dk = k.shape[-1]
    n_heads = dq // HEAD_DIM
    pad = WINDOW_CHUNKS * CHUNK
    ratio = tq // pad
    table = _pair_bias(bias)
    prev = lambda i: jnp.maximum(i * ratio - 1, 0)
    return pl.pallas_call(
        functools.partial(_attn_kernel, n_heads=n_heads),
        out_shape=jax.ShapeDtypeStruct((b, dq, s), BF16),
        grid=(b, s // tq),
        in_specs=[
            pl.BlockSpec((None, dq, tq), lambda bi, i: (bi, 0, i)),
            pl.BlockSpec((None, pad, dk), lambda bi, i: (bi, prev(i), 0)),
            pl.BlockSpec((None, tq, dk), lambda bi, i: (bi, i, 0)),
            pl.BlockSpec((None, dk, pad), lambda bi, i: (bi, 0, prev(i))),
            pl.BlockSpec((None, dk, tq), lambda bi, i: (bi, 0, i)),
            _const_spec(table.shape),
            pl.BlockSpec(memory_space=pltpu.SMEM),
        ],
        out_specs=pl.BlockSpec((None, dq, tq), lambda bi, i: (bi, 0, i)),
        scratch_shapes=[pltpu.VMEM((pad + tq, dk), BF16), pltpu.VMEM((dk, pad + tq), BF16),
                        pltpu.VMEM((n_heads, PAIR_KEYS, PAIR), F32),
                        pltpu.VMEM((n_heads, PAIR_KEYS, PAIR), F32)],
        compiler_params=_params(2),
        name="l1_attention",
    )(qt, k, k, vt, vt, table, sinks.astype(F32))


INFO_E0, INFO_E1, INFO_R0, INFO_R1, INFO_W0, INFO_W1 = range(6)


def _oproj_router_kernel(ot_ref, x_ref, wo_ref, bo_ref, g_ref, wr_ref, x3_ref, info_ref, cnt_ref,
                         *, n_experts):
    tm = x_ref.shape[0]
    attn = lax.dot_general(ot_ref[...], wo_ref[...], _TN, preferred_element_type=F32)
    x3 = x_ref[...] + attn + bo_ref[...]
    x3_ref[...] = x3
    h = _rms(x3, g_ref[...])
    h_hi = h.astype(BF16)
    h_lo = (h - h_hi.astype(F32)).astype(BF16)
    wr = wr_ref[...]
    hw = jnp.dot(h_hi, wr, preferred_element_type=F32)
    logits = (hw[:, :LANES] + hw[:, LANES:]) + jnp.dot(h_lo, wr[:, :LANES], preferred_element_type=F32)
    lane_i = lax.broadcasted_iota(jnp.int32, (tm, LANES), 1)
    lane = lane_i.astype(F32)
    neg = jnp.float32(-jnp.inf)
    logits = jnp.where(lane_i < n_experts, logits, neg)
    m0 = jnp.max(logits, axis=-1, keepdims=True)
    e0 = jnp.min(jnp.where(logits == m0, lane, float(LANES)), axis=-1, keepdims=True)
    rest = jnp.where(lane == e0, neg, logits)
    m1 = jnp.max(rest, axis=-1, keepdims=True)
    e1 = jnp.min(jnp.where(rest == m1, lane, float(LANES)), axis=-1, keepdims=True)
    t = jnp.exp(m1 - m0)
    w0 = 1.0 / (1.0 + t)
    w1 = t / (1.0 + t)
    hot0 = lane == e0
    hot1 = lane == e1
    both = jnp.where(hot0 | hot1, 1.0, 0.0).astype(BF16)
    row = lax.broadcasted_iota(jnp.int32, (tm, tm), 0)
    col = lax.broadcasted_iota(jnp.int32, (tm, tm), 1)
    tri = (col < row).astype(BF16)
    before = jnp.dot(tri, both, preferred_element_type=F32)
    r0 = jnp.sum(jnp.where(hot0, before, 0.0), axis=-1, keepdims=True)
    r1 = jnp.sum(jnp.where(hot1, before, 0.0), axis=-1, keepdims=True)
    cnt_ref[...] = jnp.sum(both.astype(F32), axis=0, keepdims=True)
    info = jnp.zeros((tm, LANES), F32)
    for idx, val in ((INFO_E0, e0), (INFO_E1, e1), (INFO_R0, r0),
                     (INFO_R1, r1), (INFO_W0, w0), (INFO_W1, w1)):
        info = jnp.where(lane_i == idx, val, info)
    info_ref[...] = info


def _oproj_router(ot, x, w_o, b_o, g, w_router, tm):
    n, d = x.shape
    dq, seq = ot.shape[1:]
    per_seq = seq // tm
    n_experts = w_router.shape[1]
    wr = jnp.pad(w_router.astype(F32), ((0, 0), (0, LANES - n_experts)))
    wr_hi = wr.astype(BF16)
    wr_lo = (wr - wr_hi.astype(F32)).astype(BF16)
    wr = jnp.concatenate([wr_hi, wr_lo], axis=1)
    return pl.pallas_call(
        functools.partial(_oproj_router_kernel, n_experts=n_experts),
        out_shape=(jax.ShapeDtypeStruct((n, d), F32),
                   jax.ShapeDtypeStruct((n, LANES), F32),
                   jax.ShapeDtypeStruct((n // tm, 1, LANES), F32)),
        grid=(n // tm,),
        in_specs=[
            pl.BlockSpec((None, dq, tm), lambda i: (i // per_seq, 0, i % per_seq)),
            pl.BlockSpec((tm, d), lambda i: (i, 0)),
            _const_spec(w_o.shape),
            _const_spec((1, d)),
            _const_spec((1, d)),
            _const_spec(wr.shape),
        ],
        out_specs=(pl.BlockSpec((tm, d), lambda i: (i, 0)),
                   pl.BlockSpec((tm, LANES), lambda i: (i, 0)),
                   pl.BlockSpec((None, 1, LANES), lambda i: (i, 0, 0))),
        compiler_params=_params(1),
        name="l1_oproj_router",
    )(ot, x, w_o, b_o.reshape(1, d), g.reshape(1, d), wr)


def _segment_copies(tabs, i, n_experts, lbuf, hbm, sem, to_hbm, act):
    n_tab, o_tab, d_tab = tabs
    n_bits = (lbuf.shape[0] // SUBLANES // TOP_K).bit_length()
    for e in range(n_experts):
        n = n_tab[i * n_experts + e]
        local0 = o_tab[i * n_experts + e]
        sorted0 = d_tab[i * n_experts + e]
        for bit in reversed(range(n_bits)):
            size = 1 << bit
            done = n & ~((size << 1) - 1)

            @pl.when((n & size) != 0)
            def _():
                lo = pl.multiple_of((local0 + done) * SUBLANES, SUBLANES)
                so = pl.multiple_of((sorted0 + done) * SUBLANES, SUBLANES)
                local = lbuf.at[pl.ds(lo, size * SUBLANES)]
                remote = hbm.at[pl.ds(so, size * SUBLANES)]
                act(pltpu.make_async_copy(local, remote, sem) if to_hbm
                    else pltpu.make_async_copy(remote, local, sem))


def _local_slots(info, o_tab, i, n_experts):
    slots = []
    for e_col, r_col in ((INFO_E0, INFO_R0), (INFO_E1, INFO_R1)):
        expert = info[:, e_col:e_col + 1]
        slot = info[:, r_col:r_col + 1]
        for e in range(n_experts):
            slot = slot + jnp.where(expert == float(e), o_tab[i * n_experts + e].astype(F32), 0.0)
        slots.append(slot.astype(jnp.int32))
    return slots


def _dispatch_kernel(n_tab, o_tab, d_tab, x_ref, g_ref, info_ref, zeros_hbm, xs_hbm, lbuf, sem,
                     *, n_experts):
    del zeros_hbm
    i = pl.program_id(0)
    tm = x_ref.shape[0]
    n_slots = TOP_K * tm
    h = _rms(x_ref[...], g_ref[...]).astype(BF16)
    slot0, slot1 = _local_slots(info_ref[...], o_tab, i, n_experts)
    lane = lax.broadcasted_iota(jnp.int32, (tm, n_slots), 1)
    onehot = jnp.where((lane == slot0) | (lane == slot1), 1.0, 0.0).astype(BF16)
    rows = lax.dot_general(onehot, h, _TN, preferred_element_type=F32)
    for c in range(x_ref.shape[1] // LANES):
        lbuf[pl.ds(c, n_slots, stride=SUBLANES), :] = rows[:, c * LANES:(c + 1) * LANES]
    tabs = (n_tab, o_tab, d_tab)
    _segment_copies(tabs, i, n_experts, lbuf, xs_hbm, sem, True, lambda cp: cp.start())
    _segment_copies(tabs, i, n_experts, lbuf, xs_hbm, sem, True, lambda cp: cp.wait())


def _dispatch(x, g, info, tabs, n_experts, n_rows, tm):
    n, d = x.shape
    zeros = jnp.zeros((n_rows * SUBLANES, LANES), F32)
    grid_spec = pltpu.PrefetchScalarGridSpec(
        num_scalar_prefetch=3,
        grid=(n // tm,),
        in_specs=[
            pl.BlockSpec((tm, d), lambda i, *_: (i, 0)),
            pl.BlockSpec((1, d), lambda i, *_: (0, 0)),
            pl.BlockSpec((tm, LANES), lambda i, *_: (i, 0)),
            pl.BlockSpec(memory_space=pl.ANY),
        ],
        out_specs=pl.BlockSpec(memory_space=pl.ANY),
        scratch_shapes=[pltpu.VMEM((TOP_K * tm * SUBLANES, LANES), F32), pltpu.SemaphoreType.DMA(())],
    )
    return pl.pallas_call(
        functools.partial(_dispatch_kernel, n_experts=n_experts),
        out_shape=jax.ShapeDtypeStruct((n_rows * SUBLANES, LANES), F32),
        grid_spec=grid_spec,
        input_output_aliases={6: 0},
        compiler_params=_params(1),
        name="l1_moe_dispatch",
    )(*tabs, x, g.reshape(1, d), info, zeros)


def _experts_kernel(te_ref, tv_ref, xs_ref, wg_ref, wu_ref, wd_ref, ys_ref, xb, acc, t_scr, *, sub):
    j = pl.program_id(0)
    c = pl.program_id(1)
    tm = xb.shape[0]
    n_slabs = xb.shape[1] // LANES
    valid = tv_ref[j] == 1

    @pl.when(jnp.logical_and(valid, c == 0))
    def _():
        slabs = [xs_ref[pl.ds(s, tm, stride=SUBLANES), :] for s in range(n_slabs)]
        xb[...] = jnp.concatenate(slabs, axis=1).astype(BF16)
        acc[...] = jnp.zeros_like(acc)

    @pl.when(valid)
    def _():
        def body(q, carry):
            off = pl.multiple_of(q * sub, sub)
            x = xb[...]
            a = jnp.dot(x, wg_ref[:, pl.ds(off, sub)], preferred_element_type=F32)
            b = jnp.dot(x, wu_ref[:, pl.ds(off, sub)], preferred_element_type=F32)
            t_scr[:, pl.ds(off, sub)] = (a * jax.nn.sigmoid(a) * b).astype(BF16)
            return carry
        lax.fori_loop(0, wg_ref.shape[1] // sub, body, 0, unroll=True)
        acc[...] += jnp.dot(t_scr[...], wd_ref[...], preferred_element_type=F32)

    is_last = c == pl.num_programs(1) - 1

    @pl.when(jnp.logical_and(valid, is_last))
    def _():
        y = acc[...]
        for s in range(n_slabs):
            ys_ref[pl.ds(s, tm, stride=SUBLANES), :] = y[:, s * LANES:(s + 1) * LANES]

    @pl.when(jnp.logical_and(jnp.logical_not(valid), is_last))
    def _():
        ys_ref[...] = jnp.zeros_like(ys_ref)


def _experts(xs, tile_expert, tile_valid, w_gate, w_up, w_down, d, tm, fc, sub):
    n_tiles = tile_expert.shape[0]
    n_fc = w_gate.shape[2] // fc
    last = n_fc - 1

    def wsel(c, tv, j):
        return jnp.where(tv[j] == 1, c, last)

    grid_spec = pltpu.PrefetchScalarGridSpec(
        num_scalar_prefetch=2,
        grid=(n_tiles, n_fc),
        in_specs=[
            pl.BlockSpec((tm * SUBLANES, LANES), lambda j, c, te, tv: (j, 0)),
            pl.BlockSpec((None, d, fc), lambda j, c, te, tv: (te[j], 0, wsel(c, tv, j))),
            pl.BlockSpec((None, d, fc), lambda j, c, te, tv: (te[j], 0, wsel(c, tv, j))),
            pl.BlockSpec((None, fc, d), lambda j, c, te, tv: (te[j], wsel(c, tv, j), 0)),
        ],
        out_specs=pl.BlockSpec((tm * SUBLANES, LANES), lambda j, c, te, tv: (j, 0)),
        scratch_shapes=[pltpu.VMEM((tm, d), BF16), pltpu.VMEM((tm, d), F32),
                        pltpu.VMEM((tm, fc), BF16)],
    )
    return pl.pallas_call(
        functools.partial(_experts_kernel, sub=sub),
        out_shape=jax.ShapeDtypeStruct(xs.shape, F32),
        grid_spec=grid_spec,
        compiler_params=_params(2),
        name="l1_moe_experts",
    )(tile_expert, tile_valid, xs, w_gate, w_up, w_down)


def _combine_kernel(n_tab, o_tab, d_tab, x_ref, info_ref, g_ref, ys_hbm, o_ref, lbuf, sem,
                    *, n_experts):
    i = pl.program_id(0)
    tm = x_ref.shape[0]
    n_slots = TOP_K * tm
    tabs = (n_tab, o_tab, d_tab)
    _segment_copies(tabs, i, n_experts, lbuf, ys_hbm, sem, False, lambda cp: cp.start())
    _segment_copies(tabs, i, n_experts, lbuf, ys_hbm, sem, False, lambda cp: cp.wait())
    info = info_ref[...]
    slot0, slot1 = _local_slots(info, o_tab, i, n_experts)
    lane = lax.broadcasted_iota(jnp.int32, (tm, n_slots), 1)
    w = (jnp.where(lane == slot0, info[:, INFO_W0:INFO_W0 + 1], 0.0)
         + jnp.where(lane == slot1, info[:, INFO_W1:INFO_W1 + 1], 0.0))
    slabs = [lbuf[pl.ds(c, n_slots, stride=SUBLANES), :] for c in range(x_ref.shape[1] // LANES)]
    y = jnp.concatenate(slabs, axis=1)
    w_hi = w.astype(BF16)
    w_lo = (w - w_hi.astype(F32)).astype(BF16)
    y_hi = y.astype(BF16)
    y_lo = (y - y_hi.astype(F32)).astype(BF16)
    moe = (jnp.dot(w_hi, y_hi, preferred_element_type=F32)
           + jnp.dot(w_hi, y_lo, preferred_element_type=F32)
           + jnp.dot(w_lo, y_hi, preferred_element_type=F32))
    o_ref[...] = _rms(x_ref[...] + moe, g_ref[...])


def _combine(x, info, tabs, n_experts, ys, g, tm):
    n, d = x.shape
    grid_spec = pltpu.PrefetchScalarGridSpec(
        num_scalar_prefetch=3,
        grid=(n // tm,),
        in_specs=[
            pl.BlockSpec((tm, d), lambda i, *_: (i, 0)),
            pl.BlockSpec((tm, LANES), lambda i, *_: (i, 0)),
            pl.BlockSpec((1, d), lambda i, *_: (0, 0)),
            pl.BlockSpec(memory_space=pl.ANY),
        ],
        out_specs=pl.BlockSpec((tm, d), lambda i, *_: (i, 0)),
        scratch_shapes=[pltpu.VMEM((TOP_K * tm * SUBLANES, LANES), F32), pltpu.SemaphoreType.DMA(())],
    )
    return pl.pallas_call(
        functools.partial(_combine_kernel, n_experts=n_experts),
        out_shape=jax.ShapeDtypeStruct((n, d), F32),
        grid_spec=grid_spec,
        compiler_params=_params(1),
        name="l1_moe_combine",
    )(*tabs, x, info, g.reshape(1, d), ys)


def _forward(x, rel_bias, ev_norm_mix, ev_w_in, ev_pool_w, ev_pool_scale, ev_conv_w, ev_w_out,
             ev_norm_ffn, ev_ffn_gate, ev_ffn_up, ev_ffn_down, od_norm_mix, od_w_qkv, od_b_qkv,
             od_sinks, od_w_o, od_b_o, od_norm_ffn, od_router, od_exp_gate, od_exp_up,
             od_exp_down, final_norm, *, tm, tq, tme, fc_dense, fc_expert):
    b, s, d = x.shape
    n = b * s
    n_heads = od_sinks.shape[-1]
    dq = n_heads * HEAD_DIM
    dk = N_KV_HEADS * HEAD_DIM
    n_experts = od_router.shape[-1]

    x = _mixer(x, ev_norm_mix[0], ev_w_in[0].astype(BF16), ev_pool_w[0].astype(BF16),
               ev_pool_scale[0], ev_conv_w[0], ev_w_out[0].astype(BF16), tm)
    x = x.reshape(n, d)
    x = _ffn(x, ev_norm_ffn[0], ev_ffn_gate[0].astype(BF16), ev_ffn_up[0].astype(BF16),
             ev_ffn_down[0].astype(BF16), tm, fc_dense)

    qt, k, vt = _qkv(x.reshape(b, s, d), od_norm_mix[0], od_w_qkv[0], od_b_qkv[0], dq, dk, tm)
    bias = _rel_bias_table(rel_bias)
    attn_t = _attention(qt, k, vt, bias, od_sinks[0], tq)
    x, info, counts = _oproj_router(attn_t, x, od_w_o[0].astype(BF16), od_b_o[0], od_norm_ffn[0],
                                    od_router[0], tm)

    per_tile = counts[:, 0, :n_experts].astype(jnp.int32)
    local_off = jnp.cumsum(per_tile, axis=1) - per_tile
    earlier = jnp.cumsum(per_tile, axis=0) - per_tile
    totals = jnp.sum(per_tile, axis=0)
    tiles_per = (totals + tme - 1) // tme
    tile_end = jnp.cumsum(tiles_per)
    group_start = (tile_end - tiles_per) * tme
    n_tiles = (TOP_K * n) // tme + n_experts
    tile_ids = jnp.arange(n_tiles, dtype=jnp.int32)
    tile_valid = (tile_ids < tile_end[-1]).astype(jnp.int32)
    tile_expert = jnp.searchsorted(tile_end, jnp.minimum(tile_ids, tile_end[-1] - 1),
                                   side="right").astype(jnp.int32)
    tabs = (per_tile.reshape(-1), local_off.reshape(-1), (group_start[None, :] + earlier).reshape(-1))

    xs = _dispatch(x, od_norm_ffn[0], info, tabs, n_experts, n_tiles * tme, tm)
    ys = _experts(xs, tile_expert, tile_valid, od_exp_gate[0].astype(BF16),
                  od_exp_up[0].astype(BF16), od_exp_down[0].astype(BF16), d, tme, fc_expert,
                  fc_dense)
    out = _combine(x, info, tabs, n_experts, ys, final_norm, tm)
    return out.reshape(b, s, d)


def kernel(x, rel_bias, ev_norm_mix, ev_w_in, ev_pool_w, ev_pool_scale, ev_conv_w, ev_w_out, ev_norm_ffn, ev_ffn_gate, ev_ffn_up, ev_ffn_down, od_norm_mix, od_w_qkv, od_b_qkv, od_sinks, od_w_o, od_b_o, od_norm_ffn, od_router, od_exp_gate, od_exp_up, od_exp_down, final_norm):
    return _forward(x, rel_bias, ev_norm_mix, ev_w_in, ev_pool_w, ev_pool_scale, ev_conv_w, ev_w_out,
                    ev_norm_ffn, ev_ffn_gate, ev_ffn_up, ev_ffn_down, od_norm_mix, od_w_qkv, od_b_qkv,
                    od_sinks, od_w_o, od_b_o, od_norm_ffn, od_router, od_exp_gate, od_exp_up,
                    od_exp_down, final_norm, tm=512, tq=512, tme=512, fc_dense=256, fc_expert=1792)
```

```python
import functools
import math

import jax
import jax.numpy as jnp
from jax import lax
from jax.experimental import pallas as pl
from jax.experimental.pallas import tpu as pltpu

F32 = jnp.float32
BF16 = jnp.bfloat16

RMS_EPS = 1e-6
CHUNK = 64
POOL_WINDOWS = (2, 4, 8, 16)
CONV_K = 3
HEAD_DIM = 64
N_KV_HEADS = 2
WINDOW_CHUNKS = 2
KV_SPAN = (WINDOW_CHUNKS + 1) * CHUNK
N_BUCKETS = 32
MAX_DISTANCE = 128
TOP_K = 2
NEG_INF = -1e30

LANES = 128
SUBLANES = 8
VMEM_LIMIT = 56 * 1024 * 1024

POOL_HALO = 16
CONV_HALO = 8

def _rms(x, g):
    return x * lax.rsqrt(jnp.mean(x * x, axis=-1, keepdims=True) + RMS_EPS) * g


def _params(n_axes, vmem=VMEM_LIMIT):
    return pltpu.CompilerParams(dimension_semantics=("arbitrary",) * n_axes,
                                vmem_limit_bytes=vmem)


def _const_spec(shape):
    nd = len(shape)
    return pl.BlockSpec(shape, lambda *_: (0,) * nd, pipeline_mode=pl.Buffered(1))


def _mixer_kernel(x_ref, g_ref, win_ref, pw_ref, ps_ref, cw_ref, wout_ref, o_ref, ubuf, zbuf):
    s = pl.program_id(1)
    tm = x_ref.shape[0]
    pool_w = ubuf.shape[1]
    grp = pool_w // len(POOL_WINDOWS)
    conv_c = zbuf.shape[1]

    @pl.when(s == 0)
    def _():
        ubuf[0:POOL_HALO, :] = jnp.zeros((POOL_HALO, pool_w), F32)
        zbuf[0:CONV_HALO, :] = jnp.zeros((CONV_HALO, conv_c), F32)

    @pl.when(s > 0)
    def _():
        ubuf[0:POOL_HALO, :] = ubuf[tm:tm + POOL_HALO, :]
        zbuf[0:CONV_HALO, :] = zbuf[tm:tm + CONV_HALO, :]

    x = x_ref[...]
    h = _rms(x, g_ref[...]).astype(BF16)
    proj = jnp.dot(h, win_ref[...], preferred_element_type=F32)
    u = proj[:, :pool_w]
    gate_post = proj[:, pool_w:pool_w + conv_c]
    gate_pre = proj[:, pool_w + conv_c:pool_w + 2 * conv_c]
    v = proj[:, pool_w + 2 * conv_c:]
    ubuf[POOL_HALO:POOL_HALO + tm, :] = u
    z = gate_pre * v
    zbuf[CONV_HALO:CONV_HALO + tm, :] = z

    t = s * tm + lax.broadcasted_iota(jnp.int32, (tm, 1), 0)
    parts = []
    for g, w in enumerate(POOL_WINDOWS):
        lo, hi = g * grp, (g + 1) * grp
        ug = u[:, lo:hi]
        acc = ug
        for k in range(1, w):
            acc = acc + ubuf[POOL_HALO - k:POOL_HALO - k + tm, lo:hi]
        count = jnp.minimum(t + 1, w).astype(F32)
        pooled = acc / count - ug
        mixed = jnp.dot(pooled.astype(BF16), pw_ref[g], preferred_element_type=F32)
        parts.append(mixed * ps_ref[:, lo:hi])
    conv = cw_ref[CONV_K - 1:CONV_K, :] * z
    for k in range(CONV_K - 1):
        off = CONV_HALO - (CONV_K - 1) + k
        conv = conv + cw_ref[k:k + 1, :] * zbuf[off:off + tm, :]
    parts.append(gate_post * conv)
    y = jnp.concatenate(parts, axis=1).astype(BF16)
    o_ref[...] = x + jnp.dot(y, wout_ref[...], preferred_element_type=F32)


def _mixer(x, g, w_in, pool_w, pool_scale, conv_w, w_out, tm):
    b, s, d = x.shape
    pool_width = pool_scale.shape[-1]
    conv_c = conv_w.shape[-1]
    return pl.pallas_call(
        _mixer_kernel,
        out_shape=jax.ShapeDtypeStruct((b, s, d), F32),
        grid=(b, s // tm),
        in_specs=[
            pl.BlockSpec((None, tm, d), lambda i, j: (i, j, 0)),
            _const_spec((1, d)),
            _const_spec(w_in.shape),
            _const_spec(pool_w.shape),
            _const_spec((1, pool_width)),
            _const_spec(conv_w.shape),
            _const_spec(w_out.shape),
        ],
        out_specs=pl.BlockSpec((None, tm, d), lambda i, j: (i, j, 0)),
        scratch_shapes=[pltpu.VMEM((POOL_HALO + tm, pool_width), F32),
                        pltpu.VMEM((CONV_HALO + tm, conv_c), F32)],
        compiler_params=_params(2),
        name="l0_mixer",
    )(x, g.reshape(1, d), w_in, pool_w, pool_scale.reshape(1, pool_width), conv_w, w_out)


def _ffn_kernel(x_ref, g_ref, wg_ref, wu_ref, wd_ref, o_ref, h_scr, t_scr, *, fc):
    x = x_ref[...]
    h_scr[...] = _rms(x, g_ref[...]).astype(BF16)
    n_chunks = wg_ref.shape[1] // fc

    def body(c, carry):
        off = pl.multiple_of(c * fc, fc)
        h = h_scr[...]
        a = jnp.dot(h, wg_ref[:, pl.ds(off, fc)], preferred_element_type=F32)
        b = jnp.dot(h, wu_ref[:, pl.ds(off, fc)], preferred_element_type=F32)
        t_scr[:, pl.ds(off, fc)] = (a * jax.nn.sigmoid(a) * b).astype(BF16)
        return carry

    lax.fori_loop(0, n_chunks, body, 0, unroll=True)
    o_ref[...] = x + jnp.dot(t_scr[...], wd_ref[...], preferred_element_type=F32)


def _ffn(x, g, w_gate, w_up, w_down, tm, fc):
    n, d = x.shape
    return pl.pallas_call(
        functools.partial(_ffn_kernel, fc=fc),
        out_shape=jax.ShapeDtypeStruct((n, d), F32),
        grid=(n // tm,),
        in_specs=[
            pl.BlockSpec((tm, d), lambda i: (i, 0)),
            _const_spec((1, d)),
            _const_spec(w_gate.shape),
            _const_spec(w_up.shape),
            _const_spec(w_down.shape),
        ],
        out_specs=pl.BlockSpec((tm, d), lambda i: (i, 0)),
        scratch_shapes=[pltpu.VMEM((tm, d), BF16), pltpu.VMEM((tm, w_gate.shape[1]), BF16)],
        compiler_params=_params(1),
        name="l0_ffn",
    )(x, g.reshape(1, d), w_gate, w_up, w_down)


_NT = (((1,), (1,)), ((), ()))
_TN = (((0,), (0,)), ((), ()))


def _qkv_kernel(x_ref, g_ref, wqt_ref, bq_ref, wk_ref, bk_ref, wvt_ref, bv_ref, qt_ref, k_ref, vt_ref):
    h = _rms(x_ref[...], g_ref[...]).astype(BF16)
    qt = lax.dot_general(wqt_ref[...], h, _NT, preferred_element_type=F32) + bq_ref[...]
    qt_ref[...] = (qt * (HEAD_DIM ** -0.5)).astype(BF16)
    k_ref[...] = (jnp.dot(h, wk_ref[...], preferred_element_type=F32) + bk_ref[...]).astype(BF16)
    vt = lax.dot_general(wvt_ref[...], h, _NT, preferred_element_type=F32) + bv_ref[...]
    vt_ref[...] = vt.astype(BF16)


def _qkv(x, g, w_qkv, b_qkv, dq, dk, tm):
    b, s, d = x.shape
    wqt = w_qkv[:, :dq].T.astype(BF16)
    wk = w_qkv[:, dq:dq + dk].astype(BF16)
    wvt = w_qkv[:, dq + dk:].T.astype(BF16)
    bq = b_qkv[:dq].reshape(dq, 1)
    bk = b_qkv[dq:dq + dk].reshape(1, dk)
    bv = b_qkv[dq + dk:].reshape(dk, 1)
    return pl.pallas_call(
        _qkv_kernel,
        out_shape=(jax.ShapeDtypeStruct((b, dq, s), BF16),
                   jax.ShapeDtypeStruct((b, s, dk), BF16),
                   jax.ShapeDtypeStruct((b, dk, s), BF16)),
        grid=(b, s // tm),
        in_specs=[
            pl.BlockSpec((None, tm, d), lambda bi, i: (bi, i, 0)),
            _const_spec((1, d)),
            _const_spec(wqt.shape), _const_spec(bq.shape),
            _const_spec(wk.shape), _const_spec(bk.shape),
            _const_spec(wvt.shape), _const_spec(bv.shape),
        ],
        out_specs=(pl.BlockSpec((None, dq, tm), lambda bi, i: (bi, 0, i)),
                   pl.BlockSpec((None, tm, dk), lambda bi, i: (bi, i, 0)),
                   pl.BlockSpec((None, dk, tm), lambda bi, i: (bi, 0, i))),
        compiler_params=_params(2),
        name="l1_qkv",
    )(x, g.reshape(1, d), wqt, bq, wk, bk, wvt, bv)


def _bias_kernel(onehot_ref, rb_ref, o_ref):
    o_ref[...] = jnp.dot(onehot_ref[...], rb_ref[...], preferred_element_type=F32,
                         precision=lax.Precision.HIGHEST)


def _t5_bucket(rel):
    nb = N_BUCKETS // 2
    max_exact = nb // 2
    ret = jnp.where(rel > 0, nb, 0)
    n = jnp.abs(rel)
    nf = jnp.maximum(n, 1).astype(jnp.float32)
    large = max_exact + (jnp.log(nf / max_exact) / math.log(MAX_DISTANCE / max_exact)
                         * (nb - max_exact)).astype(jnp.int32)
    large = jnp.minimum(large, nb - 1)
    return ret + jnp.where(n < max_exact, n, large)


def _rel_bias_table(rel_bias):
    n_heads = rel_bias.shape[1]
    pad = WINDOW_CHUNKS * CHUNK
    rel = (jnp.arange(KV_SPAN) - pad)[None, :] - jnp.arange(CHUNK)[:, None]
    bucket = _t5_bucket(rel).reshape(-1)
    onehot = (bucket[:, None] == jnp.arange(N_BUCKETS)[None, :]).astype(F32)
    rows = CHUNK * KV_SPAN
    rb = jnp.pad(rel_bias.astype(F32), ((0, 0), (0, LANES - n_heads)))
    out = pl.pallas_call(
        _bias_kernel,
        out_shape=jax.ShapeDtypeStruct((rows, LANES), F32),
        name="l1_rel_bias",
    )(onehot, rb)
    return jnp.transpose(out[:, :n_heads].reshape(CHUNK, KV_SPAN, n_heads), (2, 0, 1))


PAIR = 2 * CHUNK
PAIR_KEYS = KV_SPAN + CHUNK


def _attn_kernel(qt_ref, kp_ref, km_ref, vtp_ref, vtm_ref, bias_ref, sink_ref, ot_ref, kbuf, vtbuf,
                 s_a, s_b, *, n_heads):
    i = pl.program_id(1)
    tq = qt_ref.shape[1]
    pad = WINDOW_CHUNKS * CHUNK
    per_kv = n_heads // N_KV_HEADS
    kbuf[0:pad, :] = kp_ref[...]
    kbuf[pad:pad + tq, :] = km_ref[...]
    vtbuf[:, 0:pad] = vtp_ref[...]
    vtbuf[:, pad:pad + tq] = vtm_ref[...]
    zeros = jnp.zeros((HEAD_DIM, PAIR), BF16)
    n_pairs = tq // PAIR

    def scores(jj, s_scr):
        c0 = jj * PAIR
        kc = kbuf[c0:c0 + PAIR_KEYS, :]
        for h in range(n_heads):
            pieces = [zeros] * N_KV_HEADS
            pieces[h // per_kv] = qt_ref[h * HEAD_DIM:(h + 1) * HEAD_DIM, c0:c0 + PAIR]
            rhs = jnp.concatenate(pieces, axis=0)
            s_scr[h] = jnp.dot(kc, rhs, preferred_element_type=F32)

    def softmax_values(jj, s_scr):
        c0 = jj * PAIR
        variant = jnp.where(i == 0, 1, 0) if jj == 0 else 0
        for h in range(n_heads):
            kv = h // per_kv
            s = s_scr[h] + bias_ref[variant, h]
            sk = sink_ref[h]
            m_ = jnp.maximum(jnp.max(s, axis=0, keepdims=True), sk)
            p = jnp.exp(s - m_)
            denom = jnp.sum(p, axis=0, keepdims=True) + jnp.exp(sk - m_)
            p = (p / denom).astype(BF16)
            vt = vtbuf[kv * HEAD_DIM:(kv + 1) * HEAD_DIM, c0:c0 + PAIR_KEYS]
            o = jnp.dot(vt, p, preferred_element_type=F32)
            ot_ref[h * HEAD_DIM:(h + 1) * HEAD_DIM, c0:c0 + PAIR] = o.astype(BF16)

    bufs = (s_a, s_b)
    scores(0, bufs[0])
    for jj in range(n_pairs):
        if jj + 1 < n_pairs:
            scores(jj + 1, bufs[(jj + 1) % 2])
        softmax_values(jj, bufs[jj % 2])


def _pair_bias(bias):
    bt = jnp.transpose(bias, (0, 2, 1)).astype(F32)
    neg = jnp.full((bias.shape[0], CHUNK, CHUNK), NEG_INF, F32)
    first = jnp.concatenate([bt, neg], axis=1)
    second = jnp.concatenate([neg, bt], axis=1)
    table = jnp.concatenate([first, second], axis=2)
    key = jnp.arange(PAIR_KEYS)[None, :, None]
    masked = jnp.where(key < WINDOW_CHUNKS * CHUNK, NEG_INF, table)
    return jnp.stack([table, masked])


def _attention(qt, k, vt, bias, sinks, tq):
    b, dq, s = qt.shape
    dk = k.shape[-1]
    n_heads = dq // HEAD_DIM
    pad = WINDOW_CHUNKS * CHUNK
    ratio = tq // pad
    table = _pair_bias(bias)
    prev = lambda i: jnp.maximum(i * ratio - 1, 0)
    return pl.pallas_call(
        functools.partial(_attn_kernel, n_heads=n_heads),
        out_shape=jax.ShapeDtypeStruct((b, dq, s), BF16),
        grid=(b, s // tq),
        in_specs=[
            pl.BlockSpec((None, dq, tq), lambda bi, i: (bi, 0, i)),
            pl.BlockSpec((None, pad, dk), lambda bi, i: (bi, prev(i), 0)),
            pl.BlockSpec((None, tq, dk), lambda bi, i: (bi, i, 0)),
            pl.BlockSpec((None, dk, pad), lambda bi, i: (bi, 0, prev(i))),
            pl.BlockSpec((None, dk, tq), lambda bi, i: (bi, 0, i)),
            _const_spec(table.shape),
            pl.BlockSpec(memory_space=pltpu.SMEM),
        ],
        out_specs=pl.BlockSpec((None, dq, tq), lambda bi, i: (bi, 0, i)),
        scratch_shapes=[pltpu.VMEM((pad + tq, dk), BF16), pltpu.VMEM((dk, pad + tq), BF16),
                        pltpu.VMEM((n_heads, PAIR_KEYS, PAIR), F32),
                        pltpu.VMEM((n_heads, PAIR_KEYS, PAIR), F32)],
        compiler_params=_params(2),
        name="l1_attention",
    )(qt, k, k, vt, vt, table, sinks.astype(F32))


INFO_E0, INFO_E1, INFO_R0, INFO_R1, INFO_W0, INFO_W1 = range(6)


def _oproj_router_kernel(ot_ref, x_ref, wo_ref, bo_ref, g_ref, wr_ref, x3_ref, info_ref, cnt_ref,
                         *, n_experts):
    tm = x_ref.shape[0]
    attn = lax.dot_general(ot_ref[...], wo_ref[...], _TN, preferred_element_type=F32)
    x3 = x_ref[...] + attn + bo_ref[...]
    x3_ref[...] = x3
    h = _rms(x3, g_ref[...])
    h_hi = h.astype(BF16)
    h_lo = (h - h_hi.astype(F32)).astype(BF16)
    wr = wr_ref[...]
    hw = jnp.dot(h_hi, wr, preferred_element_type=F32)
    logits = (hw[:, :LANES] + hw[:, LANES:]) + jnp.dot(h_lo, wr[:, :LANES], preferred_element_type=F32)
    lane_i = lax.broadcasted_iota(jnp.int32, (tm, LANES), 1)
    lane = lane_i.astype(F32)
    neg = jnp.float32(-jnp.inf)
    logits = jnp.where(lane_i < n_experts, logits, neg)
    m0 = jnp.max(logits, axis=-1, keepdims=True)
    e0 = jnp.min(jnp.where(logits == m0, lane, float(LANES)), axis=-1, keepdims=True)
    rest = jnp.where(lane == e0, neg, logits)
    m1 = jnp.max(rest, axis=-1, keepdims=True)
    e1 = jnp.min(jnp.where(rest == m1, lane, float(LANES)), axis=-1, keepdims=True)
    t = jnp.exp(m1 - m0)
    w0 = 1.0 / (1.0 + t)
    w1 = t / (1.0 + t)
    hot0 = lane == e0
    hot1 = lane == e1
    both = jnp.where(hot0 | hot1, 1.0, 0.0).astype(BF16)
    row = lax.broadcasted_iota(jnp.int32, (tm, tm), 0)
    col = lax.broadcasted_iota(jnp.int32, (tm, tm), 1)
    tri = (col < row).astype(BF16)
    before = jnp.dot(tri, both, preferred_element_type=F32)
    r0 = jnp.sum(jnp.where(hot0, before, 0.0), axis=-1, keepdims=True)
    r1 = jnp.sum(jnp.where(hot1, before, 0.0), axis=-1, keepdims=True)
    cnt_ref[...] = jnp.sum(both.astype(F32), axis=0, keepdims=True)
    info = jnp.zeros((tm, LANES), F32)
    for idx, val in ((INFO_E0, e0), (INFO_E1, e1), (INFO_R0, r0),
                     (INFO_R1, r1), (INFO_W0, w0), (INFO_W1, w1)):
        info = jnp.where(lane_i == idx, val, info)
    info_ref[...] = info


def _oproj_router(ot, x, w_o, b_o, g, w_router, tm):
    n, d = x.shape
    dq, seq = ot.shape[1:]
    per_seq = seq // tm
    n_experts = w_router.shape[1]
    wr = jnp.pad(w_router.astype(F32), ((0, 0), (0, LANES - n_experts)))
    wr_hi = wr.astype(BF16)
    wr_lo = (wr - wr_hi.astype(F32)).astype(BF16)
    wr = jnp.concatenate([wr_hi, wr_lo], axis=1)
    return pl.pallas_call(
        functools.partial(_oproj_router_kernel, n_experts=n_experts),
        out_shape=(jax.ShapeDtypeStruct((n, d), F32),
                   jax.ShapeDtypeStruct((n, LANES), F32),
                   jax.ShapeDtypeStruct((n // tm, 1, LANES), F32)),
        grid=(n // tm,),
        in_specs=[
            pl.BlockSpec((None, dq, tm), lambda i: (i // per_seq, 0, i % per_seq)),
            pl.BlockSpec((tm, d), lambda i: (i, 0)),
            _const_spec(w_o.shape),
            _const_spec((1, d)),
            _const_spec((1, d)),
            _const_spec(wr.shape),
        ],
        out_specs=(pl.BlockSpec((tm, d), lambda i: (i, 0)),
                   pl.BlockSpec((tm, LANES), lambda i: (i, 0)),
                   pl.BlockSpec((None, 1, LANES), lambda i: (i, 0, 0))),
        compiler_params=_params(1),
        name="l1_oproj_router",
    )(ot, x, w_o, b_o.reshape(1, d), g.reshape(1, d), wr)


def _segment_copies(tabs, i, n_experts, lbuf, hbm, sem, to_hbm, act):
    n_tab, o_tab, d_tab = tabs
    n_bits = (lbuf.shape[0] // SUBLANES // TOP_K).bit_length()
    for e in range(n_experts):
        n = n_tab[i * n_experts + e]
        local0 = o_tab[i * n_experts + e]
        sorted0 = d_tab[i * n_experts + e]
        for bit in reversed(range(n_bits)):
            size = 1 << bit
            done = n & ~((size << 1) - 1)

            @pl.when((n & size) != 0)
            def _():
                lo = pl.multiple_of((local0 + done) * SUBLANES, SUBLANES)
                so = pl.multiple_of((sorted0 + done) * SUBLANES, SUBLANES)
                local = lbuf.at[pl.ds(lo, size * SUBLANES)]
                remote = hbm.at[pl.ds(so, size * SUBLANES)]
                act(pltpu.make_async_copy(local, remote, sem) if to_hbm
                    else pltpu.make_async_copy(remote, local, sem))


def _local_slots(info, o_tab, i, n_experts):
    slots = []
    for e_col, r_col in ((INFO_E0, INFO_R0), (INFO_E1, INFO_R1)):
        expert = info[:, e_col:e_col + 1]
        slot = info[:, r_col:r_col + 1]
        for e in range(n_experts):
            slot = slot + jnp.where(expert == float(e), o_tab[i * n_experts + e].astype(F32), 0.0)
        slots.append(slot.astype(jnp.int32))
    return slots


def _dispatch_kernel(n_tab, o_tab, d_tab, x_ref, g_ref, info_ref, zeros_hbm, xs_hbm, lbuf, sems,
                     *, n_experts):
    del zeros_hbm
    i = pl.program_id(0)
    tm = x_ref.shape[0]
    n_slots = TOP_K * tm
    h = _rms(x_ref[...], g_ref[...]).astype(BF16)
    slot0, slot1 = _local_slots(info_ref[...], o_tab, i, n_experts)
    lane = lax.broadcasted_iota(jnp.int32, (tm, n_slots), 1)
    onehot = jnp.where((lane == slot0) | (lane == slot1), 1.0, 0.0).astype(BF16)
    rows = lax.dot_general(onehot, h, _TN, preferred_element_type=F32)
    slot = i % 2
    for c in range(x_ref.shape[1] // LANES):
        lbuf[slot, pl.ds(c, n_slots, stride=SUBLANES), :] = rows[:, c * LANES:(c + 1) * LANES]
    tabs = (n_tab, o_tab, d_tab)

    def copies(tile, s, act):
        _segment_copies(tabs, tile, n_experts, lbuf.at[s], xs_hbm, sems.at[s], True, act)

    copies(i, slot, lambda cp: cp.start())

    @pl.when(i > 0)
    def _():
        copies(i - 1, 1 - slot, lambda cp: cp.wait())

    @pl.when(i == pl.num_programs(0) - 1)
    def _():
        copies(i, slot, lambda cp: cp.wait())


def _dispatch(x, g, info, tabs, n_experts, n_rows, tm):
    n, d = x.shape
    zeros = jnp.zeros((n_rows * SUBLANES, LANES), F32)
    grid_spec = pltpu.PrefetchScalarGridSpec(
        num_scalar_prefetch=3,
        grid=(n // tm,),
        in_specs=[
            pl.BlockSpec((tm, d), lambda i, *_: (i, 0)),
            pl.BlockSpec((1, d), lambda i, *_: (0, 0)),
            pl.BlockSpec((tm, LANES), lambda i, *_: (i, 0)),
            pl.BlockSpec(memory_space=pl.ANY),
        ],
        out_specs=pl.BlockSpec(memory_space=pl.ANY),
        scratch_shapes=[pltpu.VMEM((2, TOP_K * tm * SUBLANES, LANES), F32),
                        pltpu.SemaphoreType.DMA((2,))],
    )
    return pl.pallas_call(
        functools.partial(_dispatch_kernel, n_experts=n_experts),
        out_shape=jax.ShapeDtypeStruct((n_rows * SUBLANES, LANES), F32),
        grid_spec=grid_spec,
        input_output_aliases={6: 0},
        compiler_params=_params(1),
        name="l1_moe_dispatch",
    )(*tabs, x, g.reshape(1, d), info, zeros)


def _experts_kernel(te_ref, tv_ref, xs_ref, wg_ref, wu_ref, wd_ref, ys_ref, t_scr, *, sub):
    del te_ref
    j = pl.program_id(0)
    tm = t_scr.shape[0]
    n_slabs = wg_ref.shape[0] // LANES

    @pl.when(tv_ref[j] == 1)
    def _():
        slabs = [xs_ref[pl.ds(s, tm, stride=SUBLANES), :] for s in range(n_slabs)]
        x = jnp.concatenate(slabs, axis=1).astype(BF16)

        def body(q, carry):
            off = pl.multiple_of(q * sub, sub)
            a = jnp.dot(x, wg_ref[:, pl.ds(off, sub)], preferred_element_type=F32)
            b = jnp.dot(x, wu_ref[:, pl.ds(off, sub)], preferred_element_type=F32)
            t_scr[:, pl.ds(off, sub)] = (a * jax.nn.sigmoid(a) * b).astype(BF16)
            return carry
        lax.fori_loop(0, wg_ref.shape[1] // sub, body, 0, unroll=True)
        y = jnp.dot(t_scr[...], wd_ref[...], preferred_element_type=F32)
        for s in range(n_slabs):
            ys_ref[pl.ds(s, tm, stride=SUBLANES), :] = y[:, s * LANES:(s + 1) * LANES]

    @pl.when(tv_ref[j] != 1)
    def _():
        ys_ref[...] = jnp.zeros_like(ys_ref)


def _experts(xs, tile_expert, tile_valid, w_gate, w_up, w_down, d, tm, sub):
    n_tiles = tile_expert.shape[0]
    ff = w_gate.shape[2]
    once = pl.Buffered(1)
    grid_spec = pltpu.PrefetchScalarGridSpec(
        num_scalar_prefetch=2,
        grid=(n_tiles,),
        in_specs=[
            pl.BlockSpec((tm * SUBLANES, LANES), lambda j, te, tv: (j, 0)),
            pl.BlockSpec((None, d, ff), lambda j, te, tv: (te[j], 0, 0), pipeline_mode=once),
            pl.BlockSpec((None, d, ff), lambda j, te, tv: (te[j], 0, 0), pipeline_mode=once),
            pl.BlockSpec((None, ff, d), lambda j, te, tv: (te[j], 0, 0), pipeline_mode=once),
        ],
        out_specs=pl.BlockSpec((tm * SUBLANES, LANES), lambda j, te, tv: (j, 0)),
        scratch_shapes=[pltpu.VMEM((tm, ff), BF16)],
    )
    return pl.pallas_call(
        functools.partial(_experts_kernel, sub=sub),
        out_shape=jax.ShapeDtypeStruct(xs.shape, F32),
        grid_spec=grid_spec,
        compiler_params=_params(1),
        name="l1_moe_experts",
    )(tile_expert, tile_valid, xs, w_gate, w_up, w_down)


def _combine_kernel(n_tab, o_tab, d_tab, x_ref, info_ref, g_ref, ys_hbm, o_ref, lbuf, sems,
                    *, n_experts):
    i = pl.program_id(0)
    tm = x_ref.shape[0]
    n_slots = TOP_K * tm
    tabs = (n_tab, o_tab, d_tab)
    slot = i % 2

    def copies(tile, s, act):
        _segment_copies(tabs, tile, n_experts, lbuf.at[s], ys_hbm, sems.at[s], False, act)

    @pl.when(i == 0)
    def _():
        copies(i, slot, lambda cp: cp.start())

    @pl.when(i + 1 < pl.num_programs(0))
    def _():
        copies(i + 1, 1 - slot, lambda cp: cp.start())

    copies(i, slot, lambda cp: cp.wait())
    info = info_ref[...]
    slot0, slot1 = _local_slots(info, o_tab, i, n_experts)
    lane = lax.broadcasted_iota(jnp.int32, (tm, n_slots), 1)
    w = (jnp.where(lane == slot0, info[:, INFO_W0:INFO_W0 + 1], 0.0)
         + jnp.where(lane == slot1, info[:, INFO_W1:INFO_W1 + 1], 0.0))
    slabs = [lbuf[slot, pl.ds(c, n_slots, stride=SUBLANES), :] for c in range(x_ref.shape[1] // LANES)]
    y = jnp.concatenate(slabs, axis=1)
    w_hi = w.astype(BF16)
    w_lo = (w - w_hi.astype(F32)).astype(BF16)
    y_hi = y.astype(BF16)
    y_lo = (y - y_hi.astype(F32)).astype(BF16)
    moe = (jnp.dot(w_hi, y_hi, preferred_element_type=F32)
           + jnp.dot(w_hi, y_lo, preferred_element_type=F32)
           + jnp.dot(w_lo, y_hi, preferred_element_type=F32))
    o_ref[...] = _rms(x_ref[...] + moe, g_ref[...])


def _combine(x, info, tabs, n_experts, ys, g, tm):
    n, d = x.shape
    grid_spec = pltpu.PrefetchScalarGridSpec(
        num_scalar_prefetch=3,
        grid=(n // tm,),
        in_specs=[
            pl.BlockSpec((tm, d), lambda i, *_: (i, 0)),
            pl.BlockSpec((tm, LANES), lambda i, *_: (i, 0)),
            pl.BlockSpec((1, d), lambda i, *_: (0, 0)),
            pl.BlockSpec(memory_space=pl.ANY),
        ],
        out_specs=pl.BlockSpec((tm, d), lambda i, *_: (i, 0)),
        scratch_shapes=[pltpu.VMEM((2, TOP_K * tm * SUBLANES, LANES), F32),
                        pltpu.SemaphoreType.DMA((2,))],
    )
    return pl.pallas_call(
        functools.partial(_combine_kernel, n_experts=n_experts),
        out_shape=jax.ShapeDtypeStruct((n, d), F32),
        grid_spec=grid_spec,
        compiler_params=_params(1),
        name="l1_moe_combine",
    )(*tabs, x, info, g.reshape(1, d), ys)


def _forward(x, rel_bias, ev_norm_mix, ev_w_in, ev_pool_w, ev_pool_scale, ev_conv_w, ev_w_out,
             ev_norm_ffn, ev_ffn_gate, ev_ffn_up, ev_ffn_down, od_norm_mix, od_w_qkv, od_b_qkv,
             od_sinks, od_w_o, od_b_o, od_norm_ffn, od_router, od_exp_gate, od_exp_up,
             od_exp_down, final_norm, *, tm, tq, tme, fc_dense):
    b, s, d = x.shape
    n = b * s
    n_heads = od_sinks.shape[-1]
    dq = n_heads * HEAD_DIM
    dk = N_KV_HEADS * HEAD_DIM
    n_experts = od_router.shape[-1]

    x = _mixer(x, ev_norm_mix[0], ev_w_in[0].astype(BF16), ev_pool_w[0].astype(BF16),
               ev_pool_scale[0], ev_conv_w[0], ev_w_out[0].astype(BF16), tm)
    x = x.reshape(n, d)
    x = _ffn(x, ev_norm_ffn[0], ev_ffn_gate[0].astype(BF16), ev_ffn_up[0].astype(BF16),
             ev_ffn_down[0].astype(BF16), tm, fc_dense)

    qt, k, vt = _qkv(x.reshape(b, s, d), od_norm_mix[0], od_w_qkv[0], od_b_qkv[0], dq, dk, tm)
    bias = _rel_bias_table(rel_bias)
    attn_t = _attention(qt, k, vt, bias, od_sinks[0], tq)
    x, info, counts = _oproj_router(attn_t, x, od_w_o[0].astype(BF16), od_b_o[0], od_norm_ffn[0],
                                    od_router[0], tm)

    per_tile = counts[:, 0, :n_experts].astype(jnp.int32)
    local_off = jnp.cumsum(per_tile, axis=1) - per_tile
    earlier = jnp.cumsum(per_tile, axis=0) - per_tile
    totals = jnp.sum(per_tile, axis=0)
    tiles_per = (totals + tme - 1) // tme
    tile_end = jnp.cumsum(tiles_per)
    group_start = (tile_end - tiles_per) * tme
    n_tiles = (TOP_K * n) // tme + n_experts
    tile_ids = jnp.arange(n_tiles, dtype=jnp.int32)
    tile_valid = (tile_ids < tile_end[-1]).astype(jnp.int32)
    tile_expert = jnp.searchsorted(tile_end, jnp.minimum(tile_ids, tile_end[-1] - 1),
                                   side="right").astype(jnp.int32)
    tabs = (per_tile.reshape(-1), local_off.reshape(-1), (group_start[None, :] + earlier).reshape(-1))

    xs = _dispatch(x, od_norm_ffn[0], info, tabs, n_experts, n_tiles * tme, tm)
    ys = _experts(xs, tile_expert, tile_valid, od_exp_gate[0].astype(BF16),
                  od_exp_up[0].astype(BF16), od_exp_down[0].astype(BF16), d, tme, fc_dense)
    out = _combine(x, info, tabs, n_experts, ys, final_norm, tm)
    return out.reshape(b, s, d)


def kernel(x, rel_bias, ev_norm_mix, ev_w_in, ev_pool_w, ev_pool_scale, ev_conv_w, ev_w_out, ev_norm_ffn, ev_ffn_gate, ev_ffn_up, ev_ffn_down, od_norm_mix, od_w_qkv, od_b_qkv, od_sinks, od_w_o, od_b_o, od_norm_ffn, od_router, od_exp_gate, od_exp_up, od_exp_down, final_norm):
    return _forward(x, rel_bias, ev_norm_mix, ev_w_in, ev_pool_w, ev_pool_scale, ev_conv_w, ev_w_out,
                    ev_norm_ffn, ev_ffn_gate, ev_ffn_up, ev_ffn_down, od_norm_mix, od_w_qkv, od_b_qkv,
                    od_sinks, od_w_o, od_b_o, od_norm_ffn, od_router, od_exp_gate, od_exp_up,
                    od_exp_down, final_norm, tm=512, tq=512, tme=512, fc_dense=256)
```

```python
import functools
import math

import jax
import jax.numpy as jnp
from jax import lax
from jax.experimental import pallas as pl
from jax.experimental.pallas import tpu as pltpu

F32 = jnp.float32
BF16 = jnp.bfloat16

RMS_EPS = 1e-6
CHUNK = 64
POOL_WINDOWS = (2, 4, 8, 16)
CONV_K = 3
HEAD_DIM = 64
N_KV_HEADS = 2
WINDOW_CHUNKS = 2
KV_SPAN = (WINDOW_CHUNKS + 1) * CHUNK
N_BUCKETS = 32
MAX_DISTANCE = 128
TOP_K = 2
NEG_INF = -1e30

LANES = 128
SUBLANES = 8
VMEM_LIMIT = 56 * 1024 * 1024

POOL_HALO = 16
CONV_HALO = 8

def _rms(x, g):
    return x * lax.rsqrt(jnp.mean(x * x, axis=-1, keepdims=True) + RMS_EPS) * g


def _params(n_axes, vmem=VMEM_LIMIT):
    return pltpu.CompilerParams(dimension_semantics=("arbitrary",) * n_axes,
                                vmem_limit_bytes=vmem)


def _const_spec(shape):
    nd = len(shape)
    return pl.BlockSpec(shape, lambda *_: (0,) * nd, pipeline_mode=pl.Buffered(1))


def _mixer_kernel(x_ref, g_ref, win_ref, pw_ref, ps_ref, cw_ref, wout_ref, o_ref, ubuf, zbuf):
    s = pl.program_id(1)
    tm = x_ref.shape[0]
    pool_w = ubuf.shape[1]
    grp = pool_w // len(POOL_WINDOWS)
    conv_c = zbuf.shape[1]

    @pl.when(s == 0)
    def _():
        ubuf[0:POOL_HALO, :] = jnp.zeros((POOL_HALO, pool_w), F32)
        zbuf[0:CONV_HALO, :] = jnp.zeros((CONV_HALO, conv_c), F32)

    @pl.when(s > 0)
    def _():
        ubuf[0:POOL_HALO, :] = ubuf[tm:tm + POOL_HALO, :]
        zbuf[0:CONV_HALO, :] = zbuf[tm:tm + CONV_HALO, :]

    x = x_ref[...]
    h = _rms(x, g_ref[...]).astype(BF16)
    proj = jnp.dot(h, win_ref[...], preferred_element_type=F32)
    u = proj[:, :pool_w]
    gate_post = proj[:, pool_w:pool_w + conv_c]
    gate_pre = proj[:, pool_w + conv_c:pool_w + 2 * conv_c]
    v = proj[:, pool_w + 2 * conv_c:]
    ubuf[POOL_HALO:POOL_HALO + tm, :] = u
    z = gate_pre * v
    zbuf[CONV_HALO:CONV_HALO + tm, :] = z

    t = s * tm + lax.broadcasted_iota(jnp.int32, (tm, 1), 0)
    parts = []
    for g, w in enumerate(POOL_WINDOWS):
        lo, hi = g * grp, (g + 1) * grp
        ug = u[:, lo:hi]
        acc = ug
        for k in range(1, w):
            acc = acc + ubuf[POOL_HALO - k:POOL_HALO - k + tm, lo:hi]
        count = jnp.minimum(t + 1, w).astype(F32)
        pooled = acc / count - ug
        mixed = jnp.dot(pooled.astype(BF16), pw_ref[g], preferred_element_type=F32)
        parts.append(mixed * ps_ref[:, lo:hi])
    conv = cw_ref[CONV_K - 1:CONV_K, :] * z
    for k in range(CONV_K - 1):
        off = CONV_HALO - (CONV_K - 1) + k
        conv = conv + cw_ref[k:k + 1, :] * zbuf[off:off + tm, :]
    parts.append(gate_post * conv)
    y = jnp.concatenate(parts, axis=1).astype(BF16)
    o_ref[...] = x + jnp.dot(y, wout_ref[...], preferred_element_type=F32)


def _mixer(x, g, w_in, pool_w, pool_scale, conv_w, w_out, tm):
    b, s, d = x.shape
    pool_width = pool_scale.shape[-1]
    conv_c = conv_w.shape[-1]
    return pl.pallas_call(
        _mixer_kernel,
        out_shape=jax.ShapeDtypeStruct((b, s, d), F32),
        grid=(b, s // tm),
        in_specs=[
            pl.BlockSpec((None, tm, d), lambda i, j: (i, j, 0)),
            _const_spec((1, d)),
            _const_spec(w_in.shape),
            _const_spec(pool_w.shape),
            _const_spec((1, pool_width)),
            _const_spec(conv_w.shape),
            _const_spec(w_out.shape),
        ],
        out_specs=pl.BlockSpec((None, tm, d), lambda i, j: (i, j, 0)),
        scratch_shapes=[pltpu.VMEM((POOL_HALO + tm, pool_width), F32),
                        pltpu.VMEM((CONV_HALO + tm, conv_c), F32)],
        compiler_params=_params(2),
        name="l0_mixer",
    )(x, g.reshape(1, d), w_in, pool_w, pool_scale.reshape(1, pool_width), conv_w, w_out)


def _ffn_kernel(x_ref, g_ref, wg_ref, wu_ref, wd_ref, *rest, fc, n_cast):
    cast_in, (o_ref, *cast_out), (h_scr, t_scr) = rest[:n_cast], rest[n_cast:2 * n_cast + 1], rest[2 * n_cast + 1:]
    x = x_ref[...]
    h_scr[...] = _rms(x, g_ref[...]).astype(BF16)
    n_chunks = wg_ref.shape[1] // fc

    def body(c, carry):
        off = pl.multiple_of(c * fc, fc)
        h = h_scr[...]
        a = jnp.dot(h, wg_ref[:, pl.ds(off, fc)], preferred_element_type=F32)
        b = jnp.dot(h, wu_ref[:, pl.ds(off, fc)], preferred_element_type=F32)
        t_scr[:, pl.ds(off, fc)] = (a * jax.nn.sigmoid(a) * b).astype(BF16)
        return carry

    lax.fori_loop(0, n_chunks, body, 0, unroll=True)
    o_ref[...] = x + jnp.dot(t_scr[...], wd_ref[...], preferred_element_type=F32)
    for src, dst in zip(cast_in, cast_out):
        dst[...] = src[...].astype(BF16)


def _ffn(x, g, w_gate, w_up, w_down, to_cast, tm, fc):
    n, d = x.shape
    steps = n // tm
    slabs = []
    for w in to_cast:
        rows = w.shape[0] // steps
        assert rows * steps == w.shape[0] and rows % (2 * SUBLANES) == 0, w.shape
        slabs.append(pl.BlockSpec((rows, w.shape[1]), lambda i: (i, 0)))
    outs = pl.pallas_call(
        functools.partial(_ffn_kernel, fc=fc, n_cast=len(to_cast)),
        out_shape=(jax.ShapeDtypeStruct((n, d), F32),
                   *[jax.ShapeDtypeStruct(w.shape, BF16) for w in to_cast]),
        grid=(steps,),
        in_specs=[
            pl.BlockSpec((tm, d), lambda i: (i, 0)),
            _const_spec((1, d)),
            _const_spec(w_gate.shape),
            _const_spec(w_up.shape),
            _const_spec(w_down.shape),
            *slabs,
        ],
        out_specs=(pl.BlockSpec((tm, d), lambda i: (i, 0)), *slabs),
        scratch_shapes=[pltpu.VMEM((tm, d), BF16), pltpu.VMEM((tm, w_gate.shape[1]), BF16)],
        compiler_params=_params(1),
        name="l0_ffn",
    )(x, g.reshape(1, d), w_gate, w_up, w_down, *to_cast)
    return outs[0], outs[1:]


_NT = (((1,), (1,)), ((), ()))
_TN = (((0,), (0,)), ((), ()))


def _qkv_kernel(x_ref, g_ref, wqt_ref, bq_ref, wk_ref, bk_ref, wvt_ref, bv_ref, qt_ref, k_ref, vt_ref):
    h = _rms(x_ref[...], g_ref[...]).astype(BF16)
    qt = lax.dot_general(wqt_ref[...], h, _NT, preferred_element_type=F32) + bq_ref[...]
    qt_ref[...] = (qt * (HEAD_DIM ** -0.5)).astype(BF16)
    k_ref[...] = (jnp.dot(h, wk_ref[...], preferred_element_type=F32) + bk_ref[...]).astype(BF16)
    vt = lax.dot_general(wvt_ref[...], h, _NT, preferred_element_type=F32) + bv_ref[...]
    vt_ref[...] = vt.astype(BF16)


def _qkv(x, g, w_qkv, b_qkv, dq, dk, tm):
    b, s, d = x.shape
    wqt = w_qkv[:, :dq].T.astype(BF16)
    wk = w_qkv[:, dq:dq + dk].astype(BF16)
    wvt = w_qkv[:, dq + dk:].T.astype(BF16)
    bq = b_qkv[:dq].reshape(dq, 1)
    bk = b_qkv[dq:dq + dk].reshape(1, dk)
    bv = b_qkv[dq + dk:].reshape(dk, 1)
    return pl.pallas_call(
        _qkv_kernel,
        out_shape=(jax.ShapeDtypeStruct((b, dq, s), BF16),
                   jax.ShapeDtypeStruct((b, s, dk), BF16),
                   jax.ShapeDtypeStruct((b, dk, s), BF16)),
        grid=(b, s // tm),
        in_specs=[
            pl.BlockSpec((None, tm, d), lambda bi, i: (bi, i, 0)),
            _const_spec((1, d)),
            _const_spec(wqt.shape), _const_spec(bq.shape),
            _const_spec(wk.shape), _const_spec(bk.shape),
            _const_spec(wvt.shape), _const_spec(bv.shape),
        ],
        out_specs=(pl.BlockSpec((None, dq, tm), lambda bi, i: (bi, 0, i)),
                   pl.BlockSpec((None, tm, dk), lambda bi, i: (bi, i, 0)),
                   pl.BlockSpec((None, dk, tm), lambda bi, i: (bi, 0, i))),
        compiler_params=_params(2),
        name="l1_qkv",
    )(x, g.reshape(1, d), wqt, bq, wk, bk, wvt, bv)


def _bias_kernel(onehot_ref, rb_ref, o_ref):
    o_ref[...] = jnp.dot(onehot_ref[...], rb_ref[...], preferred_element_type=F32,
                         precision=lax.Precision.HIGHEST)


def _t5_bucket(rel):
    nb = N_BUCKETS // 2
    max_exact = nb // 2
    ret = jnp.where(rel > 0, nb, 0)
    n = jnp.abs(rel)
    nf = jnp.maximum(n, 1).astype(jnp.float32)
    large = max_exact + (jnp.log(nf / max_exact) / math.log(MAX_DISTANCE / max_exact)
                         * (nb - max_exact)).astype(jnp.int32)
    large = jnp.minimum(large, nb - 1)
    return ret + jnp.where(n < max_exact, n, large)


def _rel_bias_table(rel_bias):
    n_heads = rel_bias.shape[1]
    pad = WINDOW_CHUNKS * CHUNK
    rel = (jnp.arange(KV_SPAN) - pad)[None, :] - jnp.arange(CHUNK)[:, None]
    bucket = _t5_bucket(rel).reshape(-1)
    onehot = (bucket[:, None] == jnp.arange(N_BUCKETS)[None, :]).astype(F32)
    rows = CHUNK * KV_SPAN
    rb = jnp.pad(rel_bias.astype(F32), ((0, 0), (0, LANES - n_heads)))
    out = pl.pallas_call(
        _bias_kernel,
        out_shape=jax.ShapeDtypeStruct((rows, LANES), F32),
        name="l1_rel_bias",
    )(onehot, rb)
    return jnp.transpose(out[:, :n_heads].reshape(CHUNK, KV_SPAN, n_heads), (2, 0, 1))


PAIR = 2 * CHUNK
PAIR_KEYS = KV_SPAN + CHUNK


def _attn_kernel(qt_ref, kp_ref, km_ref, vtp_ref, vtm_ref, bias_ref, sink_ref, ot_ref, kbuf, vtbuf,
                 s_a, s_b, *, n_heads):
    i = pl.program_id(1)
    tq = qt_ref.shape[1]
    pad = WINDOW_CHUNKS * CHUNK
    per_kv = n_heads // N_KV_HEADS
    kbuf[0:pad, :] = kp_ref[...]
    kbuf[pad:pad + tq, :] = km_ref[...]
    vtbuf[:, 0:pad] = vtp_ref[...]
    vtbuf[:, pad:pad + tq] = vtm_ref[...]
    zeros = jnp.zeros((HEAD_DIM, PAIR), BF16)
    n_pairs = tq // PAIR

    def scores(jj, s_scr):
        c0 = jj * PAIR
        kc = kbuf[c0:c0 + PAIR_KEYS, :]
        for h in range(n_heads):
            pieces = [zeros] * N_KV_HEADS
            pieces[h // per_kv] = qt_ref[h * HEAD_DIM:(h + 1) * HEAD_DIM, c0:c0 + PAIR]
            rhs = jnp.concatenate(pieces, axis=0)
            s_scr[h] = jnp.dot(kc, rhs, preferred_element_type=F32)

    def softmax_values(jj, s_scr):
        c0 = jj * PAIR
        variant = jnp.where(i == 0, 1, 0) if jj == 0 else 0
        for h in range(n_heads):
            kv = h // per_kv
            s = s_scr[h] + bias_ref[variant, h]
            sk = sink_ref[h]
            m_ = jnp.maximum(jnp.max(s, axis=0, keepdims=True), sk)
            p = jnp.exp(s - m_)
            denom = jnp.sum(p, axis=0, keepdims=True) + jnp.exp(sk - m_)
            p = (p / denom).astype(BF16)
            vt = vtbuf[kv * HEAD_DIM:(kv + 1) * HEAD_DIM, c0:c0 + PAIR_KEYS]
            o = jnp.dot(vt, p, preferred_element_type=F32)
            ot_ref[h * HEAD_DIM:(h + 1) * HEAD_DIM, c0:c0 + PAIR] = o.astype(BF16)

    bufs = (s_a, s_b)
    scores(0, bufs[0])
    for jj in range(n_pairs):
        if jj + 1 < n_pairs:
            scores(jj + 1, bufs[(jj + 1) % 2])
        softmax_values(jj, bufs[jj % 2])


def _pair_bias(bias):
    bt = jnp.transpose(bias, (0, 2, 1)).astype(F32)
    neg = jnp.full((bias.shape[0], CHUNK, CHUNK), NEG_INF, F32)
    first = jnp.concatenate([bt, neg], axis=1)
    second = jnp.concatenate([neg, bt], axis=1)
    table = jnp.concatenate([first, second], axis=2)
    key = jnp.arange(PAIR_KEYS)[None, :, None]
    masked = jnp.where(key < WINDOW_CHUNKS * CHUNK, NEG_INF, table)
    return jnp.stack([table, masked])


def _attention(qt, k, vt, bias, sinks, tq):
    b, dq, s = qt.shape
    dk = k.shape[-1]
    n_heads = dq // HEAD_DIM
    pad = WINDOW_CHUNKS * CHUNK
    ratio = tq // pad
    table = _pair_bias(bias)
    prev = lambda i: jnp.maximum(i * ratio - 1, 0)
    return pl.pallas_call(
        functools.partial(_attn_kernel, n_heads=n_heads),
        out_shape=jax.ShapeDtypeStruct((b, dq, s), BF16),
        grid=(b, s // tq),
        in_specs=[
            pl.BlockSpec((None, dq, tq), lambda bi, i: (bi, 0, i)),
            pl.BlockSpec((None, pad, dk), lambda bi, i: (bi, prev(i), 0)),
            pl.BlockSpec((None, tq, dk), lambda bi, i: (bi, i, 0)),
            pl.BlockSpec((None, dk, pad), lambda bi, i: (bi, 0, prev(i))),
            pl.BlockSpec((None, dk, tq), lambda bi, i: (bi, 0, i)),
            _const_spec(table.shape),
            pl.BlockSpec(memory_space=pltpu.SMEM),
        ],
        out_specs=pl.BlockSpec((None, dq, tq), lambda bi, i: (bi, 0, i)),
        scratch_shapes=[pltpu.VMEM((pad + tq, dk), BF16), pltpu.VMEM((dk, pad + tq), BF16),
                        pltpu.VMEM((n_heads, PAIR_KEYS, PAIR), F32),
                        pltpu.VMEM((n_heads, PAIR_KEYS, PAIR), F32)],
        compiler_params=_params(2),
        name="l1_attention",
    )(qt, k, k, vt, vt, table, sinks.astype(F32))


INFO_E0, INFO_E1, INFO_R0, INFO_R1, INFO_W0, INFO_W1 = range(6)


def _oproj_router_kernel(ot_ref, x_ref, wo_ref, bo_ref, g_ref, wr_ref, x3_ref, info_ref, cnt_ref,
                         *, n_experts):
    tm = x_ref.shape[0]
    attn = lax.dot_general(ot_ref[...], wo_ref[...], _TN, preferred_element_type=F32)
    x3 = x_ref[...] + attn + bo_ref[...]
    x3_ref[...] = x3
    h = _rms(x3, g_ref[...])
    h_hi = h.astype(BF16)
    h_lo = (h - h_hi.astype(F32)).astype(BF16)
    wr = wr_ref[...]
    hw = jnp.dot(h_hi, wr, preferred_element_type=F32)
    logits = (hw[:, :LANES] + hw[:, LANES:]) + jnp.dot(h_lo, wr[:, :LANES], preferred_element_type=F32)
    lane_i = lax.broadcasted_iota(jnp.int32, (tm, LANES), 1)
    lane = lane_i.astype(F32)
    neg = jnp.float32(-jnp.inf)
    logits = jnp.where(lane_i < n_experts, logits, neg)
    m0 = jnp.max(logits, axis=-1, keepdims=True)
    e0 = jnp.min(jnp.where(logits == m0, lane, float(LANES)), axis=-1, keepdims=True)
    rest = jnp.where(lane == e0, neg, logits)
    m1 = jnp.max(rest, axis=-1, keepdims=True)
    e1 = jnp.min(jnp.where(rest == m1, lane, float(LANES)), axis=-1, keepdims=True)
    t = jnp.exp(m1 - m0)
    w0 = 1.0 / (1.0 + t)
    w1 = t / (1.0 + t)
    hot0 = lane == e0
    hot1 = lane == e1
    both = jnp.where(hot0 | hot1, 1.0, 0.0).astype(BF16)
    row = lax.broadcasted_iota(jnp.int32, (tm, tm), 0)
    col = lax.broadcasted_iota(jnp.int32, (tm, tm), 1)
    tri = (col < row).astype(BF16)
    before = jnp.dot(tri, both, preferred_element_type=F32)
    r0 = jnp.sum(jnp.where(hot0, before, 0.0), axis=-1, keepdims=True)
    r1 = jnp.sum(jnp.where(hot1, before, 0.0), axis=-1, keepdims=True)
    cnt_ref[...] = jnp.sum(both.astype(F32), axis=0, keepdims=True)
    info = jnp.zeros((tm, LANES), F32)
    for idx, val in ((INFO_E0, e0), (INFO_E1, e1), (INFO_R0, r0),
                     (INFO_R1, r1), (INFO_W0, w0), (INFO_W1, w1)):
        info = jnp.where(lane_i == idx, val, info)
    info_ref[...] = info


def _oproj_router(ot, x, w_o, b_o, g, w_router, tm):
    n, d = x.shape
    dq, seq = ot.shape[1:]
    per_seq = seq // tm
    n_experts = w_router.shape[1]
    wr = jnp.pad(w_router.astype(F32), ((0, 0), (0, LANES - n_experts)))
    wr_hi = wr.astype(BF16)
    wr_lo = (wr - wr_hi.astype(F32)).astype(BF16)
    wr = jnp.concatenate([wr_hi, wr_lo], axis=1)
    return pl.pallas_call(
        functools.partial(_oproj_router_kernel, n_experts=n_experts),
        out_shape=(jax.ShapeDtypeStruct((n, d), F32),
                   jax.ShapeDtypeStruct((n, LANES), F32),
                   jax.ShapeDtypeStruct((n // tm, 1, LANES), F32)),
        grid=(n // tm,),
        in_specs=[
            pl.BlockSpec((None, dq, tm), lambda i: (i // per_seq, 0, i % per_seq)),
            pl.BlockSpec((tm, d), lambda i: (i, 0)),
            _const_spec(w_o.shape),
            _const_spec((1, d)),
            _const_spec((1, d)),
            _const_spec(wr.shape),
        ],
        out_specs=(pl.BlockSpec((tm, d), lambda i: (i, 0)),
                   pl.BlockSpec((tm, LANES), lambda i: (i, 0)),
                   pl.BlockSpec((None, 1, LANES), lambda i: (i, 0, 0))),
        compiler_params=_params(1),
        name="l1_oproj_router",
    )(ot, x, w_o, b_o.reshape(1, d), g.reshape(1, d), wr)


def _pieces(n, n_bits, fn):
    for bit in reversed(range(n_bits)):
        size = 1 << bit
        done = n & ~((size << 1) - 1)
        pl.when((n & size) != 0)(functools.partial(fn, done, size))


def _rows(ref, first_row, n_rows):
    return ref.at[pl.ds(pl.multiple_of(first_row * SUBLANES, SUBLANES), n_rows * SUBLANES)]


def _segment_copies(tabs, i, n_experts, lbuf, hbm, sem, to_hbm, act):
    n_tab, o_tab, d_tab = tabs[:3]
    n_bits = (lbuf.shape[0] // SUBLANES // TOP_K).bit_length()
    for e in range(n_experts):
        local0 = o_tab[i * n_experts + e]
        sorted0 = d_tab[i * n_experts + e]

        def piece(done, size, local0=local0, sorted0=sorted0):
            local = _rows(lbuf, local0 + done, size)
            remote = _rows(hbm, sorted0 + done, size)
            act(pltpu.make_async_copy(local, remote, sem) if to_hbm
                else pltpu.make_async_copy(remote, local, sem))

        _pieces(n_tab[i * n_experts + e], n_bits, piece)


def _local_slots(info, o_tab, i, n_experts):
    slots = []
    for e_col, r_col in ((INFO_E0, INFO_R0), (INFO_E1, INFO_R1)):
        expert = info[:, e_col:e_col + 1]
        slot = info[:, r_col:r_col + 1]
        for e in range(n_experts):
            slot = slot + jnp.where(expert == float(e), o_tab[i * n_experts + e].astype(F32), 0.0)
        slots.append(slot.astype(jnp.int32))
    return slots


def _dispatch_kernel(n_tab, o_tab, d_tab, pad0_tab, padn_tab, nv_tab, x_ref, g_ref, info_ref, xs_hbm,
                     lbuf, sems, *, n_experts, tme):
    i = pl.program_id(0)
    tm = x_ref.shape[0]
    n_slots = TOP_K * tm
    h = _rms(x_ref[...], g_ref[...]).astype(BF16)
    slot0, slot1 = _local_slots(info_ref[...], o_tab, i, n_experts)
    lane = lax.broadcasted_iota(jnp.int32, (tm, n_slots), 1)
    onehot = jnp.where((lane == slot0) | (lane == slot1), 1.0, 0.0).astype(BF16)
    rows = lax.dot_general(onehot, h, _TN, preferred_element_type=F32)
    slot = i % 2
    for c in range(x_ref.shape[1] // LANES):
        lbuf[slot, pl.ds(c, n_slots, stride=SUBLANES), :] = rows[:, c * LANES:(c + 1) * LANES]
    tabs = (n_tab, o_tab, d_tab)

    def copies(tile, s, act):
        _segment_copies(tabs, tile, n_experts, lbuf.at[s], xs_hbm, sems.at[s], True, act)

    copies(i, slot, lambda cp: cp.start())

    @pl.when(i > 0)
    def _():
        copies(i - 1, 1 - slot, lambda cp: cp.wait())

    @pl.when(i == pl.num_programs(0) - 1)
    def _():
        copies(i, slot, lambda cp: cp.wait())
        zeros = lbuf.at[slot]
        zeros[0:tme * SUBLANES, :] = jnp.zeros((tme * SUBLANES, LANES), F32)

        def pads(act):
            for e in range(n_experts):
                def piece(done, size, e=e):
                    act(pltpu.make_async_copy(_rows(zeros, 0, size),
                                              _rows(xs_hbm, pad0_tab[e] + done, size), sems.at[slot]))
                _pieces(padn_tab[e], (tme - 1).bit_length(), piece)
            n_tiles = xs_hbm.shape[0] // (tme * SUBLANES)
            for t in range(n_experts):
                tile = nv_tab[0] + t
                pl.when(tile < n_tiles)(lambda tile=tile: act(pltpu.make_async_copy(
                    _rows(zeros, 0, tme), _rows(xs_hbm, tile * tme, tme), sems.at[slot])))

        pads(lambda cp: cp.start())
        pads(lambda cp: cp.wait())


def _dispatch(x, g, info, tabs, n_experts, n_rows, tm, tme):
    n, d = x.shape
    assert TOP_K * tm >= tme
    grid_spec = pltpu.PrefetchScalarGridSpec(
        num_scalar_prefetch=len(tabs),
        grid=(n // tm,),
        in_specs=[
            pl.BlockSpec((tm, d), lambda i, *_: (i, 0)),
            pl.BlockSpec((1, d), lambda i, *_: (0, 0)),
            pl.BlockSpec((tm, LANES), lambda i, *_: (i, 0)),
        ],
        out_specs=pl.BlockSpec(memory_space=pl.ANY),
        scratch_shapes=[pltpu.VMEM((2, TOP_K * tm * SUBLANES, LANES), F32),
                        pltpu.SemaphoreType.DMA((2,))],
    )
    return pl.pallas_call(
        functools.partial(_dispatch_kernel, n_experts=n_experts, tme=tme),
        out_shape=jax.ShapeDtypeStruct((n_rows * SUBLANES, LANES), F32),
        grid_spec=grid_spec,
        compiler_params=_params(1),
        name="l1_moe_dispatch",
    )(*tabs, x, g.reshape(1, d), info)


def _experts_kernel(te_ref, tv_ref, nv_ref, xs_ref, wg_ref, wu_ref, wd_ref, ys_ref, t_scr, *, sub):
    del te_ref, nv_ref
    j = pl.program_id(0)
    tm = t_scr.shape[0]
    n_slabs = wg_ref.shape[0] // LANES

    @pl.when(tv_ref[j] == 1)
    def _():
        slabs = [xs_ref[pl.ds(s, tm, stride=SUBLANES), :] for s in range(n_slabs)]
        x = jnp.concatenate(slabs, axis=1).astype(BF16)

        def body(q, carry):
            off = pl.multiple_of(q * sub, sub)
            a = jnp.dot(x, wg_ref[:, pl.ds(off, sub)], preferred_element_type=F32)
            b = jnp.dot(x, wu_ref[:, pl.ds(off, sub)], preferred_element_type=F32)
            t_scr[:, pl.ds(off, sub)] = (a * jax.nn.sigmoid(a) * b).astype(BF16)
            return carry
        lax.fori_loop(0, wg_ref.shape[1] // sub, body, 0, unroll=True)
        y = jnp.dot(t_scr[...], wd_ref[...], preferred_element_type=F32)
        for s in range(n_slabs):
            ys_ref[pl.ds(s, tm, stride=SUBLANES), :] = y[:, s * LANES:(s + 1) * LANES]

    @pl.when(tv_ref[j] != 1)
    def _():
        ys_ref[...] = jnp.zeros_like(ys_ref)


def _experts(xs, tile_expert, tile_valid, n_valid, w_gate, w_up, w_down, d, tm, sub):
    n_tiles = tile_expert.shape[0]
    ff = w_gate.shape[2]
    once = pl.Buffered(1)
    grid_spec = pltpu.PrefetchScalarGridSpec(
        num_scalar_prefetch=3,
        grid=(n_tiles,),
        in_specs=[
            pl.BlockSpec((tm * SUBLANES, LANES), lambda j, te, tv, nv: (jnp.minimum(j, nv[0] - 1), 0)),
            pl.BlockSpec((None, d, ff), lambda j, te, tv, nv: (te[j], 0, 0), pipeline_mode=once),
            pl.BlockSpec((None, d, ff), lambda j, te, tv, nv: (te[j], 0, 0), pipeline_mode=once),
            pl.BlockSpec((None, ff, d), lambda j, te, tv, nv: (te[j], 0, 0), pipeline_mode=once),
        ],
        out_specs=pl.BlockSpec((tm * SUBLANES, LANES), lambda j, te, tv, nv: (j, 0)),
        scratch_shapes=[pltpu.VMEM((tm, ff), BF16)],
    )
    return pl.pallas_call(
        functools.partial(_experts_kernel, sub=sub),
        out_shape=jax.ShapeDtypeStruct(xs.shape, F32),
        grid_spec=grid_spec,
        compiler_params=_params(1),
        name="l1_moe_experts",
    )(tile_expert, tile_valid, n_valid, xs, w_gate, w_up, w_down)


def _combine_kernel(n_tab, o_tab, d_tab, x_ref, info_ref, g_ref, ys_hbm, o_ref, lbuf, sems,
                    *, n_experts):
    i = pl.program_id(0)
    tm = x_ref.shape[0]
    n_slots = TOP_K * tm
    tabs = (n_tab, o_tab, d_tab)
    slot = i % 2

    def copies(tile, s, act):
        _segment_copies(tabs, tile, n_experts, lbuf.at[s], ys_hbm, sems.at[s], False, act)

    @pl.when(i == 0)
    def _():
        copies(i, slot, lambda cp: cp.start())

    @pl.when(i + 1 < pl.num_programs(0))
    def _():
        copies(i + 1, 1 - slot, lambda cp: cp.start())

    copies(i, slot, lambda cp: cp.wait())
    info = info_ref[...]
    slot0, slot1 = _local_slots(info, o_tab, i, n_experts)
    lane = lax.broadcasted_iota(jnp.int32, (tm, n_slots), 1)
    w = (jnp.where(lane == slot0, info[:, INFO_W0:INFO_W0 + 1], 0.0)
         + jnp.where(lane == slot1, info[:, INFO_W1:INFO_W1 + 1], 0.0))
    slabs = [lbuf[slot, pl.ds(c, n_slots, stride=SUBLANES), :] for c in range(x_ref.shape[1] // LANES)]
    y = jnp.concatenate(slabs, axis=1)
    w_hi = w.astype(BF16)
    w_lo = (w - w_hi.astype(F32)).astype(BF16)
    y_hi = y.astype(BF16)
    y_lo = (y - y_hi.astype(F32)).astype(BF16)
    moe = (jnp.dot(w_hi, y_hi, preferred_element_type=F32)
           + jnp.dot(w_hi, y_lo, preferred_element_type=F32)
           + jnp.dot(w_lo, y_hi, preferred_element_type=F32))
    o_ref[...] = _rms(x_ref[...] + moe, g_ref[...])


def _combine(x, info, tabs, n_experts, ys, g, tm):
    n, d = x.shape
    grid_spec = pltpu.PrefetchScalarGridSpec(
        num_scalar_prefetch=3,
        grid=(n // tm,),
        in_specs=[
            pl.BlockSpec((tm, d), lambda i, *_: (i, 0)),
            pl.BlockSpec((tm, LANES), lambda i, *_: (i, 0)),
            pl.BlockSpec((1, d), lambda i, *_: (0, 0)),
            pl.BlockSpec(memory_space=pl.ANY),
        ],
        out_specs=pl.BlockSpec((tm, d), lambda i, *_: (i, 0)),
        scratch_shapes=[pltpu.VMEM((2, TOP_K * tm * SUBLANES, LANES), F32),
                        pltpu.SemaphoreType.DMA((2,))],
    )
    return pl.pallas_call(
        functools.partial(_combine_kernel, n_experts=n_experts),
        out_shape=jax.ShapeDtypeStruct((n, d), F32),
        grid_spec=grid_spec,
        compiler_params=_params(1),
        name="l1_moe_combine",
    )(*tabs, x, info, g.reshape(1, d), ys)


def _forward(x, rel_bias, ev_norm_mix, ev_w_in, ev_pool_w, ev_pool_scale, ev_conv_w, ev_w_out,
             ev_norm_ffn, ev_ffn_gate, ev_ffn_up, ev_ffn_down, od_norm_mix, od_w_qkv, od_b_qkv,
             od_sinks, od_w_o, od_b_o, od_norm_ffn, od_router, od_exp_gate, od_exp_up,
             od_exp_down, final_norm, *, tm, tq, tme, fc_dense):
    b, s, d = x.shape
    n = b * s
    n_heads = od_sinks.shape[-1]
    dq = n_heads * HEAD_DIM
    dk = N_KV_HEADS * HEAD_DIM
    n_experts = od_router.shape[-1]

    x = _mixer(x, ev_norm_mix[0], ev_w_in[0].astype(BF16), ev_pool_w[0].astype(BF16),
               ev_pool_scale[0], ev_conv_w[0], ev_w_out[0].astype(BF16), tm)
    x = x.reshape(n, d)
    ff = od_exp_gate.shape[-1]
    expert_f32 = (od_exp_gate[0].reshape(n_experts * d, ff), od_exp_up[0].reshape(n_experts * d, ff),
                  od_exp_down[0].reshape(n_experts * ff, d))
    x, (exp_gate, exp_up, exp_down) = _ffn(
        x, ev_norm_ffn[0], ev_ffn_gate[0].astype(BF16), ev_ffn_up[0].astype(BF16),
        ev_ffn_down[0].astype(BF16), expert_f32, tm, fc_dense)

    qt, k, vt = _qkv(x.reshape(b, s, d), od_norm_mix[0], od_w_qkv[0], od_b_qkv[0], dq, dk, tm)
    bias = _rel_bias_table(rel_bias)
    attn_t = _attention(qt, k, vt, bias, od_sinks[0], tq)
    x, info, counts = _oproj_router(attn_t, x, od_w_o[0].astype(BF16), od_b_o[0], od_norm_ffn[0],
                                    od_router[0], tm)

    per_tile = counts[:, 0, :n_experts].astype(jnp.int32)
    local_off = jnp.cumsum(per_tile, axis=1) - per_tile
    earlier = jnp.cumsum(per_tile, axis=0) - per_tile
    totals = jnp.sum(per_tile, axis=0)
    tiles_per = (totals + tme - 1) // tme
    tile_end = jnp.cumsum(tiles_per)
    group_start = (tile_end - tiles_per) * tme
    n_tiles = (TOP_K * n) // tme + n_experts
    tile_ids = jnp.arange(n_tiles, dtype=jnp.int32)
    tile_valid = (tile_ids < tile_end[-1]).astype(jnp.int32)
    last_valid = jnp.minimum(tile_ids, tile_end[-1] - 1)
    tile_expert = jnp.sum(last_valid[:, None] >= tile_end[None, :], axis=1).astype(jnp.int32)
    tabs = (per_tile.reshape(-1), local_off.reshape(-1), (group_start[None, :] + earlier).reshape(-1))
    pad_tabs = (group_start + totals, tiles_per * tme - totals, tile_end[-1:])

    xs = _dispatch(x, od_norm_ffn[0], info, tabs + pad_tabs, n_experts, n_tiles * tme, tm, tme)
    ys = _experts(xs, tile_expert, tile_valid, tile_end[-1:], exp_gate.reshape(n_experts, d, ff),
                  exp_up.reshape(n_experts, d, ff), exp_down.reshape(n_experts, ff, d), d, tme, fc_dense)
    out = _combine(x, info, tabs, n_experts, ys, final_norm, tm)
    return out.reshape(b, s, d)


def kernel(x, rel_bias, ev_norm_mix, ev_w_in, ev_pool_w, ev_pool_scale, ev_conv_w, ev_w_out, ev_norm_ffn, ev_ffn_gate, ev_ffn_up, ev_ffn_down, od_norm_mix, od_w_qkv, od_b_qkv, od_sinks, od_w_o, od_b_o, od_norm_ffn, od_router, od_exp_gate, od_exp_up, od_exp_down, final_norm):
    return _forward(x, rel_bias, ev_norm_mix, ev_w_in, ev_pool_w, ev_pool_scale, ev_conv_w, ev_w_out,
                    ev_norm_ffn, ev_ffn_gate, ev_ffn_up, ev_ffn_down, od_norm_mix, od_w_qkv, od_b_qkv,
                    od_sinks, od_w_o, od_b_o, od_norm_ffn, od_router, od_exp_gate, od_exp_up,
                    od_exp_down, final_norm, tm=512, tq=512, tme=512, fc_dense=256)
```

```python
import functools
import math

import jax
import jax.numpy as jnp
from jax import lax
from jax.experimental import pallas as pl
from jax.experimental.pallas import tpu as pltpu

F32 = jnp.float32
BF16 = jnp.bfloat16

RMS_EPS = 1e-6
CHUNK = 64
POOL_WINDOWS = (2, 4, 8, 16)
CONV_K = 3
HEAD_DIM = 64
N_KV_HEADS = 2
WINDOW_CHUNKS = 2
KV_SPAN = (WINDOW_CHUNKS + 1) * CHUNK
N_BUCKETS = 32
MAX_DISTANCE = 128
TOP_K = 2
NEG_INF = -1e30

LANES = 128
SUBLANES = 8
VMEM_LIMIT = 56 * 1024 * 1024

POOL_HALO = 16
CONV_HALO = 8

def _rms(x, g):
    return x * lax.rsqrt(jnp.mean(x * x, axis=-1, keepdims=True) + RMS_EPS) * g


def _params(n_axes, vmem=VMEM_LIMIT):
    return pltpu.CompilerParams(dimension_semantics=("arbitrary",) * n_axes,
                                vmem_limit_bytes=vmem)


def _const_spec(shape):
    nd = len(shape)
    return pl.BlockSpec(shape, lambda *_: (0,) * nd, pipeline_mode=pl.Buffered(1))


def _mixer_kernel(x_ref, g_ref, win_ref, pw_ref, ps_ref, cw_ref, wout_ref, o_ref, ubuf, zbuf):
    s = pl.program_id(1)
    tm = x_ref.shape[0]
    pool_w = ubuf.shape[1]
    grp = pool_w // len(POOL_WINDOWS)
    conv_c = zbuf.shape[1]

    @pl.when(s == 0)
    def _():
        ubuf[0:POOL_HALO, :] = jnp.zeros((POOL_HALO, pool_w), F32)
        zbuf[0:CONV_HALO, :] = jnp.zeros((CONV_HALO, conv_c), F32)

    @pl.when(s > 0)
    def _():
        ubuf[0:POOL_HALO, :] = ubuf[tm:tm + POOL_HALO, :]
        zbuf[0:CONV_HALO, :] = zbuf[tm:tm + CONV_HALO, :]

    x = x_ref[...]
    h = _rms(x, g_ref[...]).astype(BF16)
    proj = jnp.dot(h, win_ref[...], preferred_element_type=F32)
    u = proj[:, :pool_w]
    gate_post = proj[:, pool_w:pool_w + conv_c]
    gate_pre = proj[:, pool_w + conv_c:pool_w + 2 * conv_c]
    v = proj[:, pool_w + 2 * conv_c:]
    ubuf[POOL_HALO:POOL_HALO + tm, :] = u
    z = gate_pre * v
    zbuf[CONV_HALO:CONV_HALO + tm, :] = z

    t = s * tm + lax.broadcasted_iota(jnp.int32, (tm, 1), 0)
    parts = []
    for g, w in enumerate(POOL_WINDOWS):
        lo, hi = g * grp, (g + 1) * grp
        ug = u[:, lo:hi]
        acc = ug
        for k in range(1, w):
            acc = acc + ubuf[POOL_HALO - k:POOL_HALO - k + tm, lo:hi]
        count = jnp.minimum(t + 1, w).astype(F32)
        pooled = acc / count - ug
        mixed = jnp.dot(pooled.astype(BF16), pw_ref[g], preferred_element_type=F32)
        parts.append(mixed * ps_ref[:, lo:hi])
    conv = cw_ref[CONV_K - 1:CONV_K, :] * z
    for k in range(CONV_K - 1):
        off = CONV_HALO - (CONV_K - 1) + k
        conv = conv + cw_ref[k:k + 1, :] * zbuf[off:off + tm, :]
    parts.append(gate_post * conv)
    y = jnp.concatenate(parts, axis=1).astype(BF16)
    o_ref[...] = x + jnp.dot(y, wout_ref[...], preferred_element_type=F32)


def _mixer(x, g, w_in, pool_w, pool_scale, conv_w, w_out, tm):
    b, s, d = x.shape
    pool_width = pool_scale.shape[-1]
    conv_c = conv_w.shape[-1]
    return pl.pallas_call(
        _mixer_kernel,
        out_shape=jax.ShapeDtypeStruct((b, s, d), F32),
        grid=(b, s // tm),
        in_specs=[
            pl.BlockSpec((None, tm, d), lambda i, j: (i, j, 0)),
            _const_spec((1, d)),
            _const_spec(w_in.shape),
            _const_spec(pool_w.shape),
            _const_spec((1, pool_width)),
            _const_spec(conv_w.shape),
            _const_spec(w_out.shape),
        ],
        out_specs=pl.BlockSpec((None, tm, d), lambda i, j: (i, j, 0)),
        scratch_shapes=[pltpu.VMEM((POOL_HALO + tm, pool_width), F32),
                        pltpu.VMEM((CONV_HALO + tm, conv_c), F32)],
        compiler_params=_params(2),
        name="l0_mixer",
    )(x, g.reshape(1, d), w_in, pool_w, pool_scale.reshape(1, pool_width), conv_w, w_out)


def _ffn_kernel(x_ref, g_ref, wg_ref, wu_ref, wd_ref, *rest, fc, n_cast):
    cast_in, (o_ref, *cast_out), (h_scr, t_scr) = rest[:n_cast], rest[n_cast:2 * n_cast + 1], rest[2 * n_cast + 1:]
    x = x_ref[...]
    h_scr[...] = _rms(x, g_ref[...]).astype(BF16)
    n_chunks = wg_ref.shape[1] // fc

    def body(c, carry):
        off = pl.multiple_of(c * fc, fc)
        h = h_scr[...]
        a = jnp.dot(h, wg_ref[:, pl.ds(off, fc)], preferred_element_type=F32)
        b = jnp.dot(h, wu_ref[:, pl.ds(off, fc)], preferred_element_type=F32)
        t_scr[:, pl.ds(off, fc)] = (a * jax.nn.sigmoid(a) * b).astype(BF16)
        return carry

    lax.fori_loop(0, n_chunks, body, 0, unroll=True)
    o_ref[...] = x + jnp.dot(t_scr[...], wd_ref[...], preferred_element_type=F32)
    for src, dst in zip(cast_in, cast_out):
        dst[...] = src[...].astype(BF16)


def _ffn(x, g, w_gate, w_up, w_down, to_cast, tm, fc):
    n, d = x.shape
    steps = n // tm
    slabs = []
    for w in to_cast:
        rows = w.shape[0] // steps
        assert rows * steps == w.shape[0] and rows % (2 * SUBLANES) == 0, w.shape
        slabs.append(pl.BlockSpec((rows, w.shape[1]), lambda i: (i, 0)))
    outs = pl.pallas_call(
        functools.partial(_ffn_kernel, fc=fc, n_cast=len(to_cast)),
        out_shape=(jax.ShapeDtypeStruct((n, d), F32),
                   *[jax.ShapeDtypeStruct(w.shape, BF16) for w in to_cast]),
        grid=(steps,),
        in_specs=[
            pl.BlockSpec((tm, d), lambda i: (i, 0)),
            _const_spec((1, d)),
            _const_spec(w_gate.shape),
            _const_spec(w_up.shape),
            _const_spec(w_down.shape),
            *slabs,
        ],
        out_specs=(pl.BlockSpec((tm, d), lambda i: (i, 0)), *slabs),
        scratch_shapes=[pltpu.VMEM((tm, d), BF16), pltpu.VMEM((tm, w_gate.shape[1]), BF16)],
        compiler_params=_params(1),
        name="l0_ffn",
    )(x, g.reshape(1, d), w_gate, w_up, w_down, *to_cast)
    return outs[0], outs[1:]


_NT = (((1,), (1,)), ((), ()))
_TN = (((0,), (0,)), ((), ()))


def _qkv_kernel(x_ref, g_ref, wqt_ref, bq_ref, wk_ref, bk_ref, wvt_ref, bv_ref, qt_ref, k_ref, vt_ref):
    h = _rms(x_ref[...], g_ref[...]).astype(BF16)
    qt = lax.dot_general(wqt_ref[...], h, _NT, preferred_element_type=F32) + bq_ref[...]
    qt_ref[...] = (qt * (HEAD_DIM ** -0.5)).astype(BF16)
    k_ref[...] = (jnp.dot(h, wk_ref[...], preferred_element_type=F32) + bk_ref[...]).astype(BF16)
    vt = lax.dot_general(wvt_ref[...], h, _NT, preferred_element_type=F32) + bv_ref[...]
    vt_ref[...] = vt.astype(BF16)


def _qkv(x, g, w_qkv, b_qkv, dq, dk, tm):
    b, s, d = x.shape
    wqt = w_qkv[:, :dq].T.astype(BF16)
    wk = w_qkv[:, dq:dq + dk].astype(BF16)
    wvt = w_qkv[:, dq + dk:].T.astype(BF16)
    bq = b_qkv[:dq].reshape(dq, 1)
    bk = b_qkv[dq:dq + dk].reshape(1, dk)
    bv = b_qkv[dq + dk:].reshape(dk, 1)
    return pl.pallas_call(
        _qkv_kernel,
        out_shape=(jax.ShapeDtypeStruct((b, dq, s), BF16),
                   jax.ShapeDtypeStruct((b, s, dk), BF16),
                   jax.ShapeDtypeStruct((b, dk, s), BF16)),
        grid=(b, s // tm),
        in_specs=[
            pl.BlockSpec((None, tm, d), lambda bi, i: (bi, i, 0)),
            _const_spec((1, d)),
            _const_spec(wqt.shape), _const_spec(bq.shape),
            _const_spec(wk.shape), _const_spec(bk.shape),
            _const_spec(wvt.shape), _const_spec(bv.shape),
        ],
        out_specs=(pl.BlockSpec((None, dq, tm), lambda bi, i: (bi, 0, i)),
                   pl.BlockSpec((None, tm, dk), lambda bi, i: (bi, i, 0)),
                   pl.BlockSpec((None, dk, tm), lambda bi, i: (bi, 0, i))),
        compiler_params=_params(2),
        name="l1_qkv",
    )(x, g.reshape(1, d), wqt, bq, wk, bk, wvt, bv)


def _bias_kernel(onehot_ref, rb_ref, o_ref):
    o_ref[...] = jnp.dot(onehot_ref[...], rb_ref[...], preferred_element_type=F32,
                         precision=lax.Precision.HIGHEST)


def _t5_bucket(rel):
    nb = N_BUCKETS // 2
    max_exact = nb // 2
    ret = jnp.where(rel > 0, nb, 0)
    n = jnp.abs(rel)
    nf = jnp.maximum(n, 1).astype(jnp.float32)
    large = max_exact + (jnp.log(nf / max_exact) / math.log(MAX_DISTANCE / max_exact)
                         * (nb - max_exact)).astype(jnp.int32)
    large = jnp.minimum(large, nb - 1)
    return ret + jnp.where(n < max_exact, n, large)


def _rel_bias_table(rel_bias):
    n_heads = rel_bias.shape[1]
    pad = WINDOW_CHUNKS * CHUNK
    rel = (jnp.arange(KV_SPAN) - pad)[None, :] - jnp.arange(CHUNK)[:, None]
    bucket = _t5_bucket(rel).reshape(-1)
    onehot = (bucket[:, None] == jnp.arange(N_BUCKETS)[None, :]).astype(F32)
    rows = CHUNK * KV_SPAN
    rb = jnp.pad(rel_bias.astype(F32), ((0, 0), (0, LANES - n_heads)))
    out = pl.pallas_call(
        _bias_kernel,
        out_shape=jax.ShapeDtypeStruct((rows, LANES), F32),
        name="l1_rel_bias",
    )(onehot, rb)
    return jnp.transpose(out[:, :n_heads].reshape(CHUNK, KV_SPAN, n_heads), (2, 0, 1))


PAIR = 2 * CHUNK
PAIR_KEYS = KV_SPAN + CHUNK


def _attn_kernel(qt_ref, kp_ref, km_ref, vtp_ref, vtm_ref, bias_ref, sink_ref, ot_ref, kbuf, vtbuf,
                 s_a, s_b, *, n_heads):
    i = pl.program_id(1)
    tq = qt_ref.shape[1]
    pad = WINDOW_CHUNKS * CHUNK
    per_kv = n_heads // N_KV_HEADS
    kbuf[0:pad, :] = kp_ref[...]
    kbuf[pad:pad + tq, :] = km_ref[...]
    vtbuf[:, 0:pad] = vtp_ref[...]
    vtbuf[:, pad:pad + tq] = vtm_ref[...]
    zeros = jnp.zeros((HEAD_DIM, PAIR), BF16)
    n_pairs = tq // PAIR

    def scores(jj, s_scr):
        c0 = jj * PAIR
        kc = kbuf[c0:c0 + PAIR_KEYS, :]
        for h in range(n_heads):
            pieces = [zeros] * N_KV_HEADS
            pieces[h // per_kv] = qt_ref[h * HEAD_DIM:(h + 1) * HEAD_DIM, c0:c0 + PAIR]
            rhs = jnp.concatenate(pieces, axis=0)
            s_scr[h] = jnp.dot(kc, rhs, preferred_element_type=F32)

    def softmax_values(jj, s_scr):
        c0 = jj * PAIR
        variant = jnp.where(i == 0, 1, 0) if jj == 0 else 0
        for h in range(n_heads):
            kv = h // per_kv
            s = s_scr[h] + bias_ref[variant, h]
            sk = sink_ref[h]
            m_ = jnp.maximum(jnp.max(s, axis=0, keepdims=True), sk)
            p = jnp.exp(s - m_)
            denom = jnp.sum(p, axis=0, keepdims=True) + jnp.exp(sk - m_)
            p = (p / denom).astype(BF16)
            vt = vtbuf[kv * HEAD_DIM:(kv + 1) * HEAD_DIM, c0:c0 + PAIR_KEYS]
            o = jnp.dot(vt, p, preferred_element_type=F32)
            ot_ref[h * HEAD_DIM:(h + 1) * HEAD_DIM, c0:c0 + PAIR] = o.astype(BF16)

    bufs = (s_a, s_b)
    scores(0, bufs[0])
    for jj in range(n_pairs):
        if jj + 1 < n_pairs:
            scores(jj + 1, bufs[(jj + 1) % 2])
        softmax_values(jj, bufs[jj % 2])


def _pair_bias(bias):
    bt = jnp.transpose(bias, (0, 2, 1)).astype(F32)
    neg = jnp.full((bias.shape[0], CHUNK, CHUNK), NEG_INF, F32)
    first = jnp.concatenate([bt, neg], axis=1)
    second = jnp.concatenate([neg, bt], axis=1)
    table = jnp.concatenate([first, second], axis=2)
    key = jnp.arange(PAIR_KEYS)[None, :, None]
    masked = jnp.where(key < WINDOW_CHUNKS * CHUNK, NEG_INF, table)
    return jnp.stack([table, masked])


def _attention(qt, k, vt, bias, sinks, tq):
    b, dq, s = qt.shape
    dk = k.shape[-1]
    n_heads = dq // HEAD_DIM
    pad = WINDOW_CHUNKS * CHUNK
    ratio = tq // pad
    table = _pair_bias(bias)
    prev = lambda i: jnp.maximum(i * ratio - 1, 0)
    return pl.pallas_call(
        functools.partial(_attn_kernel, n_heads=n_heads),
        out_shape=jax.ShapeDtypeStruct((b, dq, s), BF16),
        grid=(b, s // tq),
        in_specs=[
            pl.BlockSpec((None, dq, tq), lambda bi, i: (bi, 0, i)),
            pl.BlockSpec((None, pad, dk), lambda bi, i: (bi, prev(i), 0)),
            pl.BlockSpec((None, tq, dk), lambda bi, i: (bi, i, 0)),
            pl.BlockSpec((None, dk, pad), lambda bi, i: (bi, 0, prev(i))),
            pl.BlockSpec((None, dk, tq), lambda bi, i: (bi, 0, i)),
            _const_spec(table.shape),
            pl.BlockSpec(memory_space=pltpu.SMEM),
        ],
        out_specs=pl.BlockSpec((None, dq, tq), lambda bi, i: (bi, 0, i)),
        scratch_shapes=[pltpu.VMEM((pad + tq, dk), BF16), pltpu.VMEM((dk, pad + tq), BF16),
                        pltpu.VMEM((n_heads, PAIR_KEYS, PAIR), F32),
                        pltpu.VMEM((n_heads, PAIR_KEYS, PAIR), F32)],
        compiler_params=_params(2),
        name="l1_attention",
    )(qt, k, k, vt, vt, table, sinks.astype(F32))


INFO_E0, INFO_E1, INFO_R0, INFO_R1, INFO_W0, INFO_W1 = range(6)


def _oproj_router_kernel(ot_ref, x_ref, wo_ref, bo_ref, g_ref, wr_ref, x3_ref, info_ref, cnt_ref,
                         *, n_experts):
    tm = x_ref.shape[0]
    attn = lax.dot_general(ot_ref[...], wo_ref[...], _TN, preferred_element_type=F32)
    x3 = x_ref[...] + attn + bo_ref[...]
    x3_ref[...] = x3
    h = _rms(x3, g_ref[...])
    h_hi = h.astype(BF16)
    h_lo = (h - h_hi.astype(F32)).astype(BF16)
    wr = wr_ref[...]
    hw = jnp.dot(h_hi, wr, preferred_element_type=F32)
    logits = (hw[:, :LANES] + hw[:, LANES:]) + jnp.dot(h_lo, wr[:, :LANES], preferred_element_type=F32)
    lane_i = lax.broadcasted_iota(jnp.int32, (tm, LANES), 1)
    lane = lane_i.astype(F32)
    neg = jnp.float32(-jnp.inf)
    logits = jnp.where(lane_i < n_experts, logits, neg)
    m0 = jnp.max(logits, axis=-1, keepdims=True)
    e0 = jnp.min(jnp.where(logits == m0, lane, float(LANES)), axis=-1, keepdims=True)
    rest = jnp.where(lane == e0, neg, logits)
    m1 = jnp.max(rest, axis=-1, keepdims=True)
    e1 = jnp.min(jnp.where(rest == m1, lane, float(LANES)), axis=-1, keepdims=True)
    t = jnp.exp(m1 - m0)
    w0 = 1.0 / (1.0 + t)
    w1 = t / (1.0 + t)
    hot0 = lane == e0
    hot1 = lane == e1
    both = jnp.where(hot0 | hot1, 1.0, 0.0).astype(BF16)
    row = lax.broadcasted_iota(jnp.int32, (tm, tm), 0)
    col = lax.broadcasted_iota(jnp.int32, (tm, tm), 1)
    tri = (col < row).astype(BF16)
    before = jnp.dot(tri, both, preferred_element_type=F32)
    r0 = jnp.sum(jnp.where(hot0, before, 0.0), axis=-1, keepdims=True)
    r1 = jnp.sum(jnp.where(hot1, before, 0.0), axis=-1, keepdims=True)
    cnt_ref[...] = jnp.sum(both.astype(F32), axis=0, keepdims=True)
    info = jnp.zeros((tm, LANES), F32)
    for idx, val in ((INFO_E0, e0), (INFO_E1, e1), (INFO_R0, r0),
                     (INFO_R1, r1), (INFO_W0, w0), (INFO_W1, w1)):
        info = jnp.where(lane_i == idx, val, info)
    info_ref[...] = info


def _oproj_router(ot, x, w_o, b_o, g, w_router, tm):
    n, d = x.shape
    dq, seq = ot.shape[1:]
    per_seq = seq // tm
    n_experts = w_router.shape[1]
    wr = jnp.pad(w_router.astype(F32), ((0, 0), (0, LANES - n_experts)))
    wr_hi = wr.astype(BF16)
    wr_lo = (wr - wr_hi.astype(F32)).astype(BF16)
    wr = jnp.concatenate([wr_hi, wr_lo], axis=1)
    return pl.pallas_call(
        functools.partial(_oproj_router_kernel, n_experts=n_experts),
        out_shape=(jax.ShapeDtypeStruct((n, d), F32),
                   jax.ShapeDtypeStruct((n, LANES), F32),
                   jax.ShapeDtypeStruct((n // tm, 1, LANES), F32)),
        grid=(n // tm,),
        in_specs=[
            pl.BlockSpec((None, dq, tm), lambda i: (i // per_seq, 0, i % per_seq)),
            pl.BlockSpec((tm, d), lambda i: (i, 0)),
            _const_spec(w_o.shape),
            _const_spec((1, d)),
            _const_spec((1, d)),
            _const_spec(wr.shape),
        ],
        out_specs=(pl.BlockSpec((tm, d), lambda i: (i, 0)),
                   pl.BlockSpec((tm, LANES), lambda i: (i, 0)),
                   pl.BlockSpec((None, 1, LANES), lambda i: (i, 0, 0))),
        compiler_params=_params(1),
        name="l1_oproj_router",
    )(ot, x, w_o, b_o.reshape(1, d), g.reshape(1, d), wr)


def _pieces(n, n_bits, fn):
    for bit in reversed(range(n_bits)):
        size = 1 << bit
        done = n & ~((size << 1) - 1)
        pl.when((n & size) != 0)(functools.partial(fn, done, size))


def _rows(ref, first_row, n_rows):
    return ref.at[pl.ds(pl.multiple_of(first_row * SUBLANES, SUBLANES), n_rows * SUBLANES)]


def _segment_copies(tabs, i, n_experts, lbuf, hbm, sem, to_hbm, act):
    n_tab, o_tab, d_tab = tabs[:3]
    n_bits = (lbuf.shape[0] // SUBLANES // TOP_K).bit_length()
    for e in range(n_experts):
        local0 = o_tab[i * n_experts + e]
        sorted0 = d_tab[i * n_experts + e]

        def piece(done, size, local0=local0, sorted0=sorted0):
            local = _rows(lbuf, local0 + done, size)
            remote = _rows(hbm, sorted0 + done, size)
            act(pltpu.make_async_copy(local, remote, sem) if to_hbm
                else pltpu.make_async_copy(remote, local, sem))

        _pieces(n_tab[i * n_experts + e], n_bits, piece)


def _local_slots(info, o_tab, i, n_experts):
    slots = []
    for e_col, r_col in ((INFO_E0, INFO_R0), (INFO_E1, INFO_R1)):
        expert = info[:, e_col:e_col + 1]
        slot = info[:, r_col:r_col + 1]
        for e in range(n_experts):
            slot = slot + jnp.where(expert == float(e), o_tab[i * n_experts + e].astype(F32), 0.0)
        slots.append(slot.astype(jnp.int32))
    return slots


def _dispatch_kernel(n_tab, o_tab, d_tab, pad0_tab, padn_tab, nv_tab, x_ref, g_ref, info_ref, xs_hbm,
                     lbuf, sems, *, n_experts, tme):
    i = pl.program_id(0)
    tm = x_ref.shape[0]
    n_slots = TOP_K * tm
    h = _rms(x_ref[...], g_ref[...]).astype(BF16)
    slot0, slot1 = _local_slots(info_ref[...], o_tab, i, n_experts)
    lane = lax.broadcasted_iota(jnp.int32, (tm, n_slots), 1)
    onehot = jnp.where((lane == slot0) | (lane == slot1), 1.0, 0.0).astype(BF16)
    rows = lax.dot_general(onehot, h, _TN, preferred_element_type=F32)
    slot = i % 2
    for c in range(x_ref.shape[1] // LANES):
        lbuf[slot, pl.ds(c, n_slots, stride=SUBLANES), :] = rows[:, c * LANES:(c + 1) * LANES]
    tabs = (n_tab, o_tab, d_tab)

    def copies(tile, s, act):
        _segment_copies(tabs, tile, n_experts, lbuf.at[s], xs_hbm, sems.at[s], True, act)

    copies(i, slot, lambda cp: cp.start())

    def wait_all(s):
        pltpu.make_async_copy(lbuf.at[s], _rows(xs_hbm, 0, n_slots), sems.at[s]).wait()

    @pl.when(i > 0)
    def _():
        wait_all(1 - slot)

    @pl.when(i == pl.num_programs(0) - 1)
    def _():
        wait_all(slot)
        zeros = lbuf.at[slot]
        zeros[0:tme * SUBLANES, :] = jnp.zeros((tme * SUBLANES, LANES), F32)

        def pads(act):
            for e in range(n_experts):
                def piece(done, size, e=e):
                    act(pltpu.make_async_copy(_rows(zeros, 0, size),
                                              _rows(xs_hbm, pad0_tab[e] + done, size), sems.at[slot]))
                _pieces(padn_tab[e], (tme - 1).bit_length(), piece)
            n_tiles = xs_hbm.shape[0] // (tme * SUBLANES)
            for t in range(n_experts):
                tile = nv_tab[0] + t
                pl.when(tile < n_tiles)(lambda tile=tile: act(pltpu.make_async_copy(
                    _rows(zeros, 0, tme), _rows(xs_hbm, tile * tme, tme), sems.at[slot])))

        pads(lambda cp: cp.start())
        pads(lambda cp: cp.wait())


def _dispatch(x, g, info, tabs, n_experts, n_rows, tm, tme):
    n, d = x.shape
    assert TOP_K * tm >= tme
    grid_spec = pltpu.PrefetchScalarGridSpec(
        num_scalar_prefetch=len(tabs),
        grid=(n // tm,),
        in_specs=[
            pl.BlockSpec((tm, d), lambda i, *_: (i, 0)),
            pl.BlockSpec((1, d), lambda i, *_: (0, 0)),
            pl.BlockSpec((tm, LANES), lambda i, *_: (i, 0)),
        ],
        out_specs=pl.BlockSpec(memory_space=pl.ANY),
        scratch_shapes=[pltpu.VMEM((2, TOP_K * tm * SUBLANES, LANES), F32),
                        pltpu.SemaphoreType.DMA((2,))],
    )
    return pl.pallas_call(
        functools.partial(_dispatch_kernel, n_experts=n_experts, tme=tme),
        out_shape=jax.ShapeDtypeStruct((n_rows * SUBLANES, LANES), F32),
        grid_spec=grid_spec,
        compiler_params=_params(1),
        name="l1_moe_dispatch",
    )(*tabs, x, g.reshape(1, d), info)


def _experts_kernel(te_ref, tv_ref, nv_ref, xs_ref, wg_ref, wu_ref, wd_ref, ys_ref, t_scr, *, sub):
    del te_ref, nv_ref
    j = pl.program_id(0)
    tm = t_scr.shape[0]
    n_slabs = wg_ref.shape[0] // LANES

    @pl.when(tv_ref[j] == 1)
    def _():
        slabs = [xs_ref[pl.ds(s, tm, stride=SUBLANES), :] for s in range(n_slabs)]
        x = jnp.concatenate(slabs, axis=1).astype(BF16)

        def body(q, carry):
            off = pl.multiple_of(q * sub, sub)
            a = jnp.dot(x, wg_ref[:, pl.ds(off, sub)], preferred_element_type=F32)
            b = jnp.dot(x, wu_ref[:, pl.ds(off, sub)], preferred_element_type=F32)
            t_scr[:, pl.ds(off, sub)] = (a * jax.nn.sigmoid(a) * b).astype(BF16)
            return carry
        lax.fori_loop(0, wg_ref.shape[1] // sub, body, 0, unroll=True)
        y = jnp.dot(t_scr[...], wd_ref[...], preferred_element_type=F32)
        for s in range(n_slabs):
            ys_ref[pl.ds(s, tm, stride=SUBLANES), :] = y[:, s * LANES:(s + 1) * LANES]

    @pl.when(tv_ref[j] != 1)
    def _():
        ys_ref[...] = jnp.zeros_like(ys_ref)


def _experts(xs, tile_expert, tile_valid, n_valid, w_gate, w_up, w_down, d, tm, sub):
    n_tiles = tile_expert.shape[0]
    ff = w_gate.shape[2]
    once = pl.Buffered(1)
    grid_spec = pltpu.PrefetchScalarGridSpec(
        num_scalar_prefetch=3,
        grid=(n_tiles,),
        in_specs=[
            pl.BlockSpec((tm * SUBLANES, LANES), lambda j, te, tv, nv: (jnp.minimum(j, nv[0] - 1), 0)),
            pl.BlockSpec((None, d, ff), lambda j, te, tv, nv: (te[j], 0, 0), pipeline_mode=once),
            pl.BlockSpec((None, d, ff), lambda j, te, tv, nv: (te[j], 0, 0), pipeline_mode=once),
            pl.BlockSpec((None, ff, d), lambda j, te, tv, nv: (te[j], 0, 0)),
        ],
        out_specs=pl.BlockSpec((tm * SUBLANES, LANES), lambda j, te, tv, nv: (j, 0)),
        scratch_shapes=[pltpu.VMEM((tm, ff), BF16)],
    )
    return pl.pallas_call(
        functools.partial(_experts_kernel, sub=sub),
        out_shape=jax.ShapeDtypeStruct(xs.shape, F32),
        grid_spec=grid_spec,
        compiler_params=_params(1),
        name="l1_moe_experts",
    )(tile_expert, tile_valid, n_valid, xs, w_gate, w_up, w_down)


def _combine_kernel(n_tab, o_tab, d_tab, x_ref, info_ref, g_ref, ys_hbm, o_ref, lbuf, sems,
                    *, n_experts):
    i = pl.program_id(0)
    tm = x_ref.shape[0]
    n_slots = TOP_K * tm
    tabs = (n_tab, o_tab, d_tab)
    slot = i % 2

    def copies(tile, s, act):
        _segment_copies(tabs, tile, n_experts, lbuf.at[s], ys_hbm, sems.at[s], False, act)

    @pl.when(i == 0)
    def _():
        copies(i, slot, lambda cp: cp.start())

    @pl.when(i + 1 < pl.num_programs(0))
    def _():
        copies(i + 1, 1 - slot, lambda cp: cp.start())

    pltpu.make_async_copy(_rows(ys_hbm, 0, n_slots), lbuf.at[slot], sems.at[slot]).wait()
    info = info_ref[...]
    slot0, slot1 = _local_slots(info, o_tab, i, n_experts)
    lane = lax.broadcasted_iota(jnp.int32, (tm, n_slots), 1)
    w = (jnp.where(lane == slot0, info[:, INFO_W0:INFO_W0 + 1], 0.0)
         + jnp.where(lane == slot1, info[:, INFO_W1:INFO_W1 + 1], 0.0))
    slabs = [lbuf[slot, pl.ds(c, n_slots, stride=SUBLANES), :] for c in range(x_ref.shape[1] // LANES)]
    y = jnp.concatenate(slabs, axis=1)
    w_hi = w.astype(BF16)
    w_lo = (w - w_hi.astype(F32)).astype(BF16)
    y_hi = y.astype(BF16)
    y_lo = (y - y_hi.astype(F32)).astype(BF16)
    moe = (jnp.dot(w_hi, y_hi, preferred_element_type=F32)
           + jnp.dot(w_hi, y_lo, preferred_element_type=F32)
           + jnp.dot(w_lo, y_hi, preferred_element_type=F32))
    o_ref[...] = _rms(x_ref[...] + moe, g_ref[...])


def _combine(x, info, tabs, n_experts, ys, g, tm):
    n, d = x.shape
    grid_spec = pltpu.PrefetchScalarGridSpec(
        num_scalar_prefetch=3,
        grid=(n // tm,),
        in_specs=[
            pl.BlockSpec((tm, d), lambda i, *_: (i, 0)),
            pl.BlockSpec((tm, LANES), lambda i, *_: (i, 0)),
            pl.BlockSpec((1, d), lambda i, *_: (0, 0)),
            pl.BlockSpec(memory_space=pl.ANY),
        ],
        out_specs=pl.BlockSpec((tm, d), lambda i, *_: (i, 0)),
        scratch_shapes=[pltpu.VMEM((2, TOP_K * tm * SUBLANES, LANES), F32),
                        pltpu.SemaphoreType.DMA((2,))],
    )
    return pl.pallas_call(
        functools.partial(_combine_kernel, n_experts=n_experts),
        out_shape=jax.ShapeDtypeStruct((n, d), F32),
        grid_spec=grid_spec,
        compiler_params=_params(1),
        name="l1_moe_combine",
    )(*tabs, x, info, g.reshape(1, d), ys)


def _forward(x, rel_bias, ev_norm_mix, ev_w_in, ev_pool_w, ev_pool_scale, ev_conv_w, ev_w_out,
             ev_norm_ffn, ev_ffn_gate, ev_ffn_up, ev_ffn_down, od_norm_mix, od_w_qkv, od_b_qkv,
             od_sinks, od_w_o, od_b_o, od_norm_ffn, od_router, od_exp_gate, od_exp_up,
             od_exp_down, final_norm, *, tm, tq, tme, fc_dense):
    b, s, d = x.shape
    n = b * s
    n_heads = od_sinks.shape[-1]
    dq = n_heads * HEAD_DIM
    dk = N_KV_HEADS * HEAD_DIM
    n_experts = od_router.shape[-1]

    x = _mixer(x, ev_norm_mix[0], ev_w_in[0].astype(BF16), ev_pool_w[0].astype(BF16),
               ev_pool_scale[0], ev_conv_w[0], ev_w_out[0].astype(BF16), tm)
    x = x.reshape(n, d)
    ff = od_exp_gate.shape[-1]
    expert_f32 = (od_exp_gate[0].reshape(n_experts * d, ff), od_exp_up[0].reshape(n_experts * d, ff),
                  od_exp_down[0].reshape(n_experts * ff, d))
    x, (exp_gate, exp_up, exp_down) = _ffn(
        x, ev_norm_ffn[0], ev_ffn_gate[0].astype(BF16), ev_ffn_up[0].astype(BF16),
        ev_ffn_down[0].astype(BF16), expert_f32, tm, fc_dense)

    qt, k, vt = _qkv(x.reshape(b, s, d), od_norm_mix[0], od_w_qkv[0], od_b_qkv[0], dq, dk, tm)
    bias = _rel_bias_table(rel_bias)
    attn_t = _attention(qt, k, vt, bias, od_sinks[0], tq)
    x, info, counts = _oproj_router(attn_t, x, od_w_o[0].astype(BF16), od_b_o[0], od_norm_ffn[0],
                                    od_router[0], tm)

    per_tile = counts[:, 0, :n_experts].astype(jnp.int32)
    local_off = jnp.cumsum(per_tile, axis=1) - per_tile
    earlier = jnp.cumsum(per_tile, axis=0) - per_tile
    totals = jnp.sum(per_tile, axis=0)
    tiles_per = (totals + tme - 1) // tme
    tile_end = jnp.cumsum(tiles_per)
    group_start = (tile_end - tiles_per) * tme
    n_tiles = (TOP_K * n) // tme + n_experts
    tile_ids = jnp.arange(n_tiles, dtype=jnp.int32)
    tile_valid = (tile_ids < tile_end[-1]).astype(jnp.int32)
    last_valid = jnp.minimum(tile_ids, tile_end[-1] - 1)
    tile_expert = jnp.sum(last_valid[:, None] >= tile_end[None, :], axis=1).astype(jnp.int32)
    tabs = (per_tile.reshape(-1), local_off.reshape(-1), (group_start[None, :] + earlier).reshape(-1))
    pad_tabs = (group_start + totals, tiles_per * tme - totals, tile_end[-1:])

    xs = _dispatch(x, od_norm_ffn[0], info, tabs + pad_tabs, n_experts, n_tiles * tme, tm, tme)
    ys = _experts(xs, tile_expert, tile_valid, tile_end[-1:], exp_gate.reshape(n_experts, d, ff),
                  exp_up.reshape(n_experts, d, ff), exp_down.reshape(n_experts, ff, d), d, tme, fc_dense)
    out = _combine(x, info, tabs, n_experts, ys, final_norm, tm)
    return out.reshape(b, s, d)


def kernel(x, rel_bias, ev_norm_mix, ev_w_in, ev_pool_w, ev_pool_scale, ev_conv_w, ev_w_out, ev_norm_ffn, ev_ffn_gate, ev_ffn_up, ev_ffn_down, od_norm_mix, od_w_qkv, od_b_qkv, od_sinks, od_w_o, od_b_o, od_norm_ffn, od_router, od_exp_gate, od_exp_up, od_exp_down, final_norm):
    return _forward(x, rel_bias, ev_norm_mix, ev_w_in, ev_pool_w, ev_pool_scale, ev_conv_w, ev_w_out,
                    ev_norm_ffn, ev_ffn_gate, ev_ffn_up, ev_ffn_down, od_norm_mix, od_w_qkv, od_b_qkv,
                    od_sinks, od_w_o, od_b_o, od_norm_ffn, od_router, od_exp_gate, od_exp_up,
                    od_exp_down, final_norm, tm=512, tq=1024, tme=512, fc_dense=256)
```

```python
import functools
import math

import jax
import jax.numpy as jnp
from jax import lax
from jax.experimental import pallas as pl
from jax.experimental.pallas import tpu as pltpu

F32 = jnp.float32
BF16 = jnp.bfloat16

RMS_EPS = 1e-6
CHUNK = 64
POOL_WINDOWS = (2, 4, 8, 16)
CONV_K = 3
HEAD_DIM = 64
N_KV_HEADS = 2
WINDOW_CHUNKS = 2
KV_SPAN = (WINDOW_CHUNKS + 1) * CHUNK
N_BUCKETS = 32
MAX_DISTANCE = 128
TOP_K = 2
NEG_INF = -1e30

LANES = 128
SUBLANES = 8
VMEM_LIMIT = 56 * 1024 * 1024

POOL_HALO = 16
CONV_HALO = 8

def _rms(x, g):
    return x * lax.rsqrt(jnp.mean(x * x, axis=-1, keepdims=True) + RMS_EPS) * g


def _params(n_axes, vmem=VMEM_LIMIT):
    return pltpu.CompilerParams(dimension_semantics=("arbitrary",) * n_axes,
                                vmem_limit_bytes=vmem)


def _const_spec(shape):
    nd = len(shape)
    return pl.BlockSpec(shape, lambda *_: (0,) * nd, pipeline_mode=pl.Buffered(1))


def _mixer_kernel(x_ref, g_ref, win_ref, pw_ref, ps_ref, cw_ref, wout_ref, o_ref, ubuf, zbuf):
    s = pl.program_id(1)
    tm = x_ref.shape[0]
    pool_w = ubuf.shape[1]
    grp = pool_w // len(POOL_WINDOWS)
    conv_c = zbuf.shape[1]

    @pl.when(s == 0)
    def _():
        ubuf[0:POOL_HALO, :] = jnp.zeros((POOL_HALO, pool_w), F32)
        zbuf[0:CONV_HALO, :] = jnp.zeros((CONV_HALO, conv_c), F32)

    @pl.when(s > 0)
    def _():
        ubuf[0:POOL_HALO, :] = ubuf[tm:tm + POOL_HALO, :]
        zbuf[0:CONV_HALO, :] = zbuf[tm:tm + CONV_HALO, :]

    x = x_ref[...]
    h = _rms(x, g_ref[...]).astype(BF16)
    proj = jnp.dot(h, win_ref[...], preferred_element_type=F32)
    u = proj[:, :pool_w]
    gate_post = proj[:, pool_w:pool_w + conv_c]
    gate_pre = proj[:, pool_w + conv_c:pool_w + 2 * conv_c]
    v = proj[:, pool_w + 2 * conv_c:]
    ubuf[POOL_HALO:POOL_HALO + tm, :] = u
    z = gate_pre * v
    zbuf[CONV_HALO:CONV_HALO + tm, :] = z

    t = s * tm + lax.broadcasted_iota(jnp.int32, (tm, 1), 0)
    parts = []
    for g, w in enumerate(POOL_WINDOWS):
        lo, hi = g * grp, (g + 1) * grp
        ug = u[:, lo:hi]
        acc = ug
        for k in range(1, w):
            acc = acc + ubuf[POOL_HALO - k:POOL_HALO - k + tm, lo:hi]
        count = jnp.minimum(t + 1, w).astype(F32)
        pooled = acc / count - ug
        mixed = jnp.dot(pooled.astype(BF16), pw_ref[g], preferred_element_type=F32)
        parts.append(mixed * ps_ref[:, lo:hi])
    conv = cw_ref[CONV_K - 1:CONV_K, :] * z
    for k in range(CONV_K - 1):
        off = CONV_HALO - (CONV_K - 1) + k
        conv = conv + cw_ref[k:k + 1, :] * zbuf[off:off + tm, :]
    parts.append(gate_post * conv)
    y = jnp.concatenate(parts, axis=1).astype(BF16)
    o_ref[...] = x + jnp.dot(y, wout_ref[...], preferred_element_type=F32)


def _mixer(x, g, w_in, pool_w, pool_scale, conv_w, w_out, tm):
    b, s, d = x.shape
    pool_width = pool_scale.shape[-1]
    conv_c = conv_w.shape[-1]
    return pl.pallas_call(
        _mixer_kernel,
        out_shape=jax.ShapeDtypeStruct((b, s, d), F32),
        grid=(b, s // tm),
        in_specs=[
            pl.BlockSpec((None, tm, d), lambda i, j: (i, j, 0)),
            _const_spec((1, d)),
            _const_spec(w_in.shape),
            _const_spec(pool_w.shape),
            _const_spec((1, pool_width)),
            _const_spec(conv_w.shape),
            _const_spec(w_out.shape),
        ],
        out_specs=pl.BlockSpec((None, tm, d), lambda i, j: (i, j, 0)),
        scratch_shapes=[pltpu.VMEM((POOL_HALO + tm, pool_width), F32),
                        pltpu.VMEM((CONV_HALO + tm, conv_c), F32)],
        compiler_params=_params(2),
        name="l0_mixer",
    )(x, g.reshape(1, d), w_in, pool_w, pool_scale.reshape(1, pool_width), conv_w, w_out)


def _ffn_kernel(x_ref, g_ref, wg_ref, wu_ref, wd_ref, *rest, fc, n_cast):
    cast_in, (o_ref, *cast_out), (h_scr, t_scr) = rest[:n_cast], rest[n_cast:2 * n_cast + 1], rest[2 * n_cast + 1:]
    x = x_ref[...]
    h_scr[...] = _rms(x, g_ref[...]).astype(BF16)
    n_chunks = wg_ref.shape[1] // fc

    def body(c, carry):
        off = pl.multiple_of(c * fc, fc)
        h = h_scr[...]
        a = jnp.dot(h, wg_ref[:, pl.ds(off, fc)], preferred_element_type=F32)
        b = jnp.dot(h, wu_ref[:, pl.ds(off, fc)], preferred_element_type=F32)
        t_scr[:, pl.ds(off, fc)] = (a * jax.nn.sigmoid(a) * b).astype(BF16)
        return carry

    lax.fori_loop(0, n_chunks, body, 0, unroll=True)
    o_ref[...] = x + jnp.dot(t_scr[...], wd_ref[...], preferred_element_type=F32)
    for src, dst in zip(cast_in, cast_out):
        dst[...] = src[...].astype(BF16)


def _ffn(x, g, w_gate, w_up, w_down, to_cast, tm, fc):
    n, d = x.shape
    steps = n // tm
    slabs = []
    for w in to_cast:
        rows = w.shape[0] // steps
        assert rows * steps == w.shape[0] and rows % (2 * SUBLANES) == 0, w.shape
        slabs.append(pl.BlockSpec((rows, w.shape[1]), lambda i: (i, 0)))
    outs = pl.pallas_call(
        functools.partial(_ffn_kernel, fc=fc, n_cast=len(to_cast)),
        out_shape=(jax.ShapeDtypeStruct((n, d), F32),
                   *[jax.ShapeDtypeStruct(w.shape, BF16) for w in to_cast]),
        grid=(steps,),
        in_specs=[
            pl.BlockSpec((tm, d), lambda i: (i, 0)),
            _const_spec((1, d)),
            _const_spec(w_gate.shape),
            _const_spec(w_up.shape),
            _const_spec(w_down.shape),
            *slabs,
        ],
        out_specs=(pl.BlockSpec((tm, d), lambda i: (i, 0)), *slabs),
        scratch_shapes=[pltpu.VMEM((tm, d), BF16), pltpu.VMEM((tm, w_gate.shape[1]), BF16)],
        compiler_params=_params(1),
        name="l0_ffn",
    )(x, g.reshape(1, d), w_gate, w_up, w_down, *to_cast)
    return outs[0], outs[1:]


_NT = (((1,), (1,)), ((), ()))
_TN = (((0,), (0,)), ((), ()))


def _qkv_kernel(x_ref, g_ref, wqt_ref, bq_ref, wk_ref, bk_ref, wvt_ref, bv_ref, qt_ref, k_ref, vt_ref):
    h = _rms(x_ref[...], g_ref[...]).astype(BF16)
    qt = lax.dot_general(wqt_ref[...], h, _NT, preferred_element_type=F32) + bq_ref[...]
    qt_ref[...] = (qt * (HEAD_DIM ** -0.5)).astype(BF16)
    k_ref[...] = (jnp.dot(h, wk_ref[...], preferred_element_type=F32) + bk_ref[...]).astype(BF16)
    vt = lax.dot_general(wvt_ref[...], h, _NT, preferred_element_type=F32) + bv_ref[...]
    vt_ref[...] = vt.astype(BF16)


def _qkv(x, g, w_qkv, b_qkv, dq, dk, tm):
    b, s, d = x.shape
    wqt = w_qkv[:, :dq].T.astype(BF16)
    wk = w_qkv[:, dq:dq + dk].astype(BF16)
    wvt = w_qkv[:, dq + dk:].T.astype(BF16)
    bq = b_qkv[:dq].reshape(dq, 1)
    bk = b_qkv[dq:dq + dk].reshape(1, dk)
    bv = b_qkv[dq + dk:].reshape(dk, 1)
    return pl.pallas_call(
        _qkv_kernel,
        out_shape=(jax.ShapeDtypeStruct((b, dq, s), BF16),
                   jax.ShapeDtypeStruct((b, s, dk), BF16),
                   jax.ShapeDtypeStruct((b, dk, s), BF16)),
        grid=(b, s // tm),
        in_specs=[
            pl.BlockSpec((None, tm, d), lambda bi, i: (bi, i, 0)),
            _const_spec((1, d)),
            _const_spec(wqt.shape), _const_spec(bq.shape),
            _const_spec(wk.shape), _const_spec(bk.shape),
            _const_spec(wvt.shape), _const_spec(bv.shape),
        ],
        out_specs=(pl.BlockSpec((None, dq, tm), lambda bi, i: (bi, 0, i)),
                   pl.BlockSpec((None, tm, dk), lambda bi, i: (bi, i, 0)),
                   pl.BlockSpec((None, dk, tm), lambda bi, i: (bi, 0, i))),
        compiler_params=_params(2),
        name="l1_qkv",
    )(x, g.reshape(1, d), wqt, bq, wk, bk, wvt, bv)


def _bias_kernel(onehot_ref, rb_ref, o_ref):
    o_ref[...] = jnp.dot(onehot_ref[...], rb_ref[...], preferred_element_type=F32,
                         precision=lax.Precision.HIGHEST)


def _t5_bucket(rel):
    nb = N_BUCKETS // 2
    max_exact = nb // 2
    ret = jnp.where(rel > 0, nb, 0)
    n = jnp.abs(rel)
    nf = jnp.maximum(n, 1).astype(jnp.float32)
    large = max_exact + (jnp.log(nf / max_exact) / math.log(MAX_DISTANCE / max_exact)
                         * (nb - max_exact)).astype(jnp.int32)
    large = jnp.minimum(large, nb - 1)
    return ret + jnp.where(n < max_exact, n, large)


def _rel_bias_table(rel_bias):
    n_heads = rel_bias.shape[1]
    pad = WINDOW_CHUNKS * CHUNK
    rel = (jnp.arange(KV_SPAN) - pad)[None, :] - jnp.arange(CHUNK)[:, None]
    bucket = _t5_bucket(rel).reshape(-1)
    onehot = (bucket[:, None] == jnp.arange(N_BUCKETS)[None, :]).astype(F32)
    rows = CHUNK * KV_SPAN
    rb = jnp.pad(rel_bias.astype(F32), ((0, 0), (0, LANES - n_heads)))
    out = pl.pallas_call(
        _bias_kernel,
        out_shape=jax.ShapeDtypeStruct((rows, LANES), F32),
        name="l1_rel_bias",
    )(onehot, rb)
    return jnp.transpose(out[:, :n_heads].reshape(CHUNK, KV_SPAN, n_heads), (2, 0, 1))


PAIR = 2 * CHUNK
PAIR_KEYS = KV_SPAN + CHUNK


def _attn_kernel(qt_ref, kp_ref, km_ref, vtp_ref, vtm_ref, bias_ref, sink_ref, ot_ref, kbuf, vtbuf,
                 s_a, s_b, *, n_heads):
    i = pl.program_id(1)
    tq = qt_ref.shape[1]
    pad = WINDOW_CHUNKS * CHUNK
    per_kv = n_heads // N_KV_HEADS
    kbuf[0:pad, :] = kp_ref[...]
    kbuf[pad:pad + tq, :] = km_ref[...]
    vtbuf[:, 0:pad] = vtp_ref[...]
    vtbuf[:, pad:pad + tq] = vtm_ref[...]
    zeros = jnp.zeros((HEAD_DIM, PAIR), BF16)
    n_pairs = tq // PAIR

    def scores(jj, s_scr):
        c0 = jj * PAIR
        kc = kbuf[c0:c0 + PAIR_KEYS, :]
        for h in range(n_heads):
            pieces = [zeros] * N_KV_HEADS
            pieces[h // per_kv] = qt_ref[h * HEAD_DIM:(h + 1) * HEAD_DIM, c0:c0 + PAIR]
            rhs = jnp.concatenate(pieces, axis=0)
            s_scr[h] = jnp.dot(kc, rhs, preferred_element_type=F32)

    def softmax_values(jj, s_scr):
        c0 = jj * PAIR
        variant = jnp.where(i == 0, 1, 0) if jj == 0 else 0
        for h in range(n_heads):
            kv = h // per_kv
            s = s_scr[h] + bias_ref[variant, h]
            sk = sink_ref[h]
            m_ = jnp.maximum(jnp.max(s, axis=0, keepdims=True), sk)
            p = jnp.exp(s - m_)
            denom = jnp.sum(p, axis=0, keepdims=True) + jnp.exp(sk - m_)
            p = (p / denom).astype(BF16)
            vt = vtbuf[kv * HEAD_DIM:(kv + 1) * HEAD_DIM, c0:c0 + PAIR_KEYS]
            o = jnp.dot(vt, p, preferred_element_type=F32)
            ot_ref[h * HEAD_DIM:(h + 1) * HEAD_DIM, c0:c0 + PAIR] = o.astype(BF16)

    bufs = (s_a, s_b)
    scores(0, bufs[0])
    for jj in range(n_pairs):
        if jj + 1 < n_pairs:
            scores(jj + 1, bufs[(jj + 1) % 2])
        softmax_values(jj, bufs[jj % 2])


def _pair_bias(bias):
    bt = jnp.transpose(bias, (0, 2, 1)).astype(F32)
    neg = jnp.full((bias.shape[0], CHUNK, CHUNK), NEG_INF, F32)
    first = jnp.concatenate([bt, neg], axis=1)
    second = jnp.concatenate([neg, bt], axis=1)
    table = jnp.concatenate([first, second], axis=2)
    key = jnp.arange(PAIR_KEYS)[None, :, None]
    masked = jnp.where(key < WINDOW_CHUNKS * CHUNK, NEG_INF, table)
    return jnp.stack([table, masked])


def _attention(qt, k, vt, bias, sinks, tq):
    b, dq, s = qt.shape
    dk = k.shape[-1]
    n_heads = dq // HEAD_DIM
    pad = WINDOW_CHUNKS * CHUNK
    ratio = tq // pad
    table = _pair_bias(bias)
    prev = lambda i: jnp.maximum(i * ratio - 1, 0)
    return pl.pallas_call(
        functools.partial(_attn_kernel, n_heads=n_heads),
        out_shape=jax.ShapeDtypeStruct((b, dq, s), BF16),
        grid=(b, s // tq),
        in_specs=[
            pl.BlockSpec((None, dq, tq), lambda bi, i: (bi, 0, i)),
            pl.BlockSpec((None, pad, dk), lambda bi, i: (bi, prev(i), 0)),
            pl.BlockSpec((None, tq, dk), lambda bi, i: (bi, i, 0)),
            pl.BlockSpec((None, dk, pad), lambda bi, i: (bi, 0, prev(i))),
            pl.BlockSpec((None, dk, tq), lambda bi, i: (bi, 0, i)),
            _const_spec(table.shape),
            pl.BlockSpec(memory_space=pltpu.SMEM),
        ],
        out_specs=pl.BlockSpec((None, dq, tq), lambda bi, i: (bi, 0, i)),
        scratch_shapes=[pltpu.VMEM((pad + tq, dk), BF16), pltpu.VMEM((dk, pad + tq), BF16),
                        pltpu.VMEM((n_heads, PAIR_KEYS, PAIR), F32),
                        pltpu.VMEM((n_heads, PAIR_KEYS, PAIR), F32)],
        compiler_params=_params(2),
        name="l1_attention",
    )(qt, k, k, vt, vt, table, sinks.astype(F32))


INFO_E0, INFO_E1, INFO_R0, INFO_R1, INFO_W0, INFO_W1 = range(6)


def _oproj_router_kernel(ot_ref, x_ref, wo_ref, bo_ref, g_ref, wr_ref, x3_ref, info_ref, cnt_ref,
                         *, n_experts):
    tm = x_ref.shape[0]
    attn = lax.dot_general(ot_ref[...], wo_ref[...], _TN, preferred_element_type=F32)
    x3 = x_ref[...] + attn + bo_ref[...]
    x3_ref[...] = x3
    h = _rms(x3, g_ref[...])
    h_hi = h.astype(BF16)
    h_lo = (h - h_hi.astype(F32)).astype(BF16)
    wr = wr_ref[...]
    hw = jnp.dot(h_hi, wr, preferred_element_type=F32)
    logits = (hw[:, :LANES] + hw[:, LANES:]) + jnp.dot(h_lo, wr[:, :LANES], preferred_element_type=F32)
    lane_i = lax.broadcasted_iota(jnp.int32, (tm, LANES), 1)
    lane = lane_i.astype(F32)
    neg = jnp.float32(-jnp.inf)
    logits = jnp.where(lane_i < n_experts, logits, neg)
    m0 = jnp.max(logits, axis=-1, keepdims=True)
    e0 = jnp.min(jnp.where(logits == m0, lane, float(LANES)), axis=-1, keepdims=True)
    rest = jnp.where(lane == e0, neg, logits)
    m1 = jnp.max(rest, axis=-1, keepdims=True)
    e1 = jnp.min(jnp.where(rest == m1, lane, float(LANES)), axis=-1, keepdims=True)
    t = jnp.exp(m1 - m0)
    w0 = 1.0 / (1.0 + t)
    w1 = t / (1.0 + t)
    hot0 = lane == e0
    hot1 = lane == e1
    both = jnp.where(hot0 | hot1, 1.0, 0.0).astype(BF16)
    row = lax.broadcasted_iota(jnp.int32, (tm, tm), 0)
    col = lax.broadcasted_iota(jnp.int32, (tm, tm), 1)
    tri = (col < row).astype(BF16)
    before = jnp.dot(tri, both, preferred_element_type=F32)
    r0 = jnp.sum(jnp.where(hot0, before, 0.0), axis=-1, keepdims=True)
    r1 = jnp.sum(jnp.where(hot1, before, 0.0), axis=-1, keepdims=True)
    cnt_ref[...] = jnp.sum(both.astype(F32), axis=0, keepdims=True)
    info = jnp.zeros((tm, LANES), F32)
    for idx, val in ((INFO_E0, e0), (INFO_E1, e1), (INFO_R0, r0),
                     (INFO_R1, r1), (INFO_W0, w0), (INFO_W1, w1)):
        info = jnp.where(lane_i == idx, val, info)
    info_ref[...] = info


def _oproj_router(ot, x, w_o, b_o, g, w_router, tm):
    n, d = x.shape
    dq, seq = ot.shape[1:]
    per_seq = seq // tm
    n_experts = w_router.shape[1]
    wr = jnp.pad(w_router.astype(F32), ((0, 0), (0, LANES - n_experts)))
    wr_hi = wr.astype(BF16)
    wr_lo = (wr - wr_hi.astype(F32)).astype(BF16)
    wr = jnp.concatenate([wr_hi, wr_lo], axis=1)
    return pl.pallas_call(
        functools.partial(_oproj_router_kernel, n_experts=n_experts),
        out_shape=(jax.ShapeDtypeStruct((n, d), F32),
                   jax.ShapeDtypeStruct((n, LANES), F32),
                   jax.ShapeDtypeStruct((n // tm, 1, LANES), F32)),
        grid=(n // tm,),
        in_specs=[
            pl.BlockSpec((None, dq, tm), lambda i: (i // per_seq, 0, i % per_seq)),
            pl.BlockSpec((tm, d), lambda i: (i, 0)),
            _const_spec(w_o.shape),
            _const_spec((1, d)),
            _const_spec((1, d)),
            _const_spec(wr.shape),
        ],
        out_specs=(pl.BlockSpec((tm, d), lambda i: (i, 0)),
                   pl.BlockSpec((tm, LANES), lambda i: (i, 0)),
                   pl.BlockSpec((None, 1, LANES), lambda i: (i, 0, 0))),
        compiler_params=_params(1),
        name="l1_oproj_router",
    )(ot, x, w_o, b_o.reshape(1, d), g.reshape(1, d), wr)


def _pieces(n, n_bits, fn):
    for bit in reversed(range(n_bits)):
        size = 1 << bit
        done = n & ~((size << 1) - 1)
        pl.when((n & size) != 0)(functools.partial(fn, done, size))


def _rows(ref, first_row, n_rows):
    return ref.at[pl.ds(pl.multiple_of(first_row * SUBLANES, SUBLANES), n_rows * SUBLANES)]


def _segment_copies(tabs, i, n_experts, lbuf, hbm, sem, to_hbm, act):
    n_tab, o_tab, d_tab = tabs[:3]
    n_bits = (lbuf.shape[0] // SUBLANES // TOP_K).bit_length()
    for e in range(n_experts):
        local0 = o_tab[i * n_experts + e]
        sorted0 = d_tab[i * n_experts + e]

        def piece(done, size, local0=local0, sorted0=sorted0):
            local = _rows(lbuf, local0 + done, size)
            remote = _rows(hbm, sorted0 + done, size)
            act(pltpu.make_async_copy(local, remote, sem) if to_hbm
                else pltpu.make_async_copy(remote, local, sem))

        _pieces(n_tab[i * n_experts + e], n_bits, piece)


def _dispatch_kernel(n_tab, o_tab, d_tab, pad0_tab, padn_tab, nv_tab, slot_ref, x_ref, g_ref, xs_hbm,
                     hbuf, lbuf, sems, *, n_experts, tme):
    i = pl.program_id(0)
    tm = x_ref.shape[0]
    n_slots = TOP_K * tm
    h = _rms(x_ref[...], g_ref[...])
    for c in range(x_ref.shape[1] // LANES):
        hbuf[pl.ds(c, tm, stride=SUBLANES), :] = h[:, c * LANES:(c + 1) * LANES]
    slot = i % 2
    sorted_rows = lbuf.at[slot]

    def token(t, carry):
        row = hbuf[pl.ds(pl.multiple_of(t * SUBLANES, SUBLANES), SUBLANES), :]
        for k in range(TOP_K):
            dst = pl.multiple_of(slot_ref[TOP_K * t + k] * SUBLANES, SUBLANES)
            sorted_rows[pl.ds(dst, SUBLANES), :] = row
        return carry

    lax.fori_loop(0, tm, token, 0, unroll=8)
    tabs = (n_tab, o_tab, d_tab)

    def copies(tile, s, act):
        _segment_copies(tabs, tile, n_experts, lbuf.at[s], xs_hbm, sems.at[s], True, act)

    copies(i, slot, lambda cp: cp.start())

    def wait_all(s):
        pltpu.make_async_copy(lbuf.at[s], _rows(xs_hbm, 0, n_slots), sems.at[s]).wait()

    @pl.when(i > 0)
    def _():
        wait_all(1 - slot)

    @pl.when(i == pl.num_programs(0) - 1)
    def _():
        wait_all(slot)
        zeros = lbuf.at[slot]
        zeros[0:tme * SUBLANES, :] = jnp.zeros((tme * SUBLANES, LANES), F32)

        def pads(act):
            for e in range(n_experts):
                def piece(done, size, e=e):
                    act(pltpu.make_async_copy(_rows(zeros, 0, size),
                                              _rows(xs_hbm, pad0_tab[e] + done, size), sems.at[slot]))
                _pieces(padn_tab[e], (tme - 1).bit_length(), piece)
            n_tiles = xs_hbm.shape[0] // (tme * SUBLANES)
            for t in range(n_experts):
                tile = nv_tab[0] + t
                pl.when(tile < n_tiles)(lambda tile=tile: act(pltpu.make_async_copy(
                    _rows(zeros, 0, tme), _rows(xs_hbm, tile * tme, tme), sems.at[slot])))

        pads(lambda cp: cp.start())
        pads(lambda cp: cp.wait())


def _dispatch(x, g, slots, tabs, n_experts, n_rows, tm, tme):
    n, d = x.shape
    assert TOP_K * tm >= tme
    grid_spec = pltpu.PrefetchScalarGridSpec(
        num_scalar_prefetch=len(tabs),
        grid=(n // tm,),
        in_specs=[
            pl.BlockSpec((TOP_K * tm,), lambda i, *_: (i,), memory_space=pltpu.SMEM),
            pl.BlockSpec((tm, d), lambda i, *_: (i, 0)),
            pl.BlockSpec((1, d), lambda i, *_: (0, 0)),
        ],
        out_specs=pl.BlockSpec(memory_space=pl.ANY),
        scratch_shapes=[pltpu.VMEM((tm * SUBLANES, LANES), F32),
                        pltpu.VMEM((2, TOP_K * tm * SUBLANES, LANES), F32),
                        pltpu.SemaphoreType.DMA((2,))],
    )
    return pl.pallas_call(
        functools.partial(_dispatch_kernel, n_experts=n_experts, tme=tme),
        out_shape=jax.ShapeDtypeStruct((n_rows * SUBLANES, LANES), F32),
        grid_spec=grid_spec,
        compiler_params=_params(1),
        name="l1_moe_dispatch",
    )(*tabs, slots, x, g.reshape(1, d))


def _experts_kernel(te_ref, tv_ref, nv_ref, xs_ref, wg_ref, wu_ref, wd_ref, ys_ref, t_scr, *, sub):
    del te_ref, nv_ref
    j = pl.program_id(0)
    tm = t_scr.shape[0]
    n_slabs = wg_ref.shape[0] // LANES

    @pl.when(tv_ref[j] == 1)
    def _():
        slabs = [xs_ref[pl.ds(s, tm, stride=SUBLANES), :] for s in range(n_slabs)]
        x = jnp.concatenate(slabs, axis=1).astype(BF16)

        def body(q, carry):
            off = pl.multiple_of(q * sub, sub)
            a = jnp.dot(x, wg_ref[:, pl.ds(off, sub)], preferred_element_type=F32)
            b = jnp.dot(x, wu_ref[:, pl.ds(off, sub)], preferred_element_type=F32)
            t_scr[:, pl.ds(off, sub)] = (a * jax.nn.sigmoid(a) * b).astype(BF16)
            return carry
        lax.fori_loop(0, wg_ref.shape[1] // sub, body, 0, unroll=True)
        y = jnp.dot(t_scr[...], wd_ref[...], preferred_element_type=F32)
        for s in range(n_slabs):
            ys_ref[pl.ds(s, tm, stride=SUBLANES), :] = y[:, s * LANES:(s + 1) * LANES]

    @pl.when(tv_ref[j] != 1)
    def _():
        ys_ref[...] = jnp.zeros_like(ys_ref)


def _experts(xs, tile_expert, tile_valid, n_valid, w_gate, w_up, w_down, d, tm, sub):
    n_tiles = tile_expert.shape[0]
    ff = w_gate.shape[2]
    once = pl.Buffered(1)
    grid_spec = pltpu.PrefetchScalarGridSpec(
        num_scalar_prefetch=3,
        grid=(n_tiles,),
        in_specs=[
            pl.BlockSpec((tm * SUBLANES, LANES), lambda j, te, tv, nv: (jnp.minimum(j, nv[0] - 1), 0)),
            pl.BlockSpec((None, d, ff), lambda j, te, tv, nv: (te[j], 0, 0), pipeline_mode=once),
            pl.BlockSpec((None, d, ff), lambda j, te, tv, nv: (te[j], 0, 0), pipeline_mode=once),
            pl.BlockSpec((None, ff, d), lambda j, te, tv, nv: (te[j], 0, 0)),
        ],
        out_specs=pl.BlockSpec((tm * SUBLANES, LANES), lambda j, te, tv, nv: (j, 0)),
        scratch_shapes=[pltpu.VMEM((tm, ff), BF16)],
    )
    return pl.pallas_call(
        functools.partial(_experts_kernel, sub=sub),
        out_shape=jax.ShapeDtypeStruct(xs.shape, F32),
        grid_spec=grid_spec,
        compiler_params=_params(1),
        name="l1_moe_experts",
    )(tile_expert, tile_valid, n_valid, xs, w_gate, w_up, w_down)


def _combine_kernel(n_tab, o_tab, d_tab, slot_ref, w_ref, x_ref, g_ref, ys_hbm, o_ref, lbuf, mbuf, sems,
                    *, n_experts):
    i = pl.program_id(0)
    tm = x_ref.shape[0]
    n_slots = TOP_K * tm
    tabs = (n_tab, o_tab, d_tab)
    slot = i % 2

    def copies(tile, s, act):
        _segment_copies(tabs, tile, n_experts, lbuf.at[s], ys_hbm, sems.at[s], False, act)

    @pl.when(i == 0)
    def _():
        copies(i, slot, lambda cp: cp.start())

    @pl.when(i + 1 < pl.num_programs(0))
    def _():
        copies(i + 1, 1 - slot, lambda cp: cp.start())

    pltpu.make_async_copy(_rows(ys_hbm, 0, n_slots), lbuf.at[slot], sems.at[slot]).wait()
    y = lbuf.at[slot]

    def token(t, carry):
        acc = None
        for k in range(TOP_K):
            row = y[pl.ds(pl.multiple_of(slot_ref[TOP_K * t + k] * SUBLANES, SUBLANES), SUBLANES), :]
            term = w_ref[TOP_K * t + k] * row
            acc = term if acc is None else acc + term
        mbuf[pl.ds(pl.multiple_of(t * SUBLANES, SUBLANES), SUBLANES), :] = acc
        return carry

    lax.fori_loop(0, tm, token, 0, unroll=8)
    moe = jnp.concatenate([mbuf[pl.ds(c, tm, stride=SUBLANES), :]
                           for c in range(x_ref.shape[1] // LANES)], axis=1)
    o_ref[...] = _rms(x_ref[...] + moe, g_ref[...])


def _combine(x, slots, weights, tabs, n_experts, ys, g, tm):
    n, d = x.shape
    grid_spec = pltpu.PrefetchScalarGridSpec(
        num_scalar_prefetch=3,
        grid=(n // tm,),
        in_specs=[
            pl.BlockSpec((TOP_K * tm,), lambda i, *_: (i,), memory_space=pltpu.SMEM),
            pl.BlockSpec((TOP_K * tm,), lambda i, *_: (i,), memory_space=pltpu.SMEM),
            pl.BlockSpec((tm, d), lambda i, *_: (i, 0)),
            pl.BlockSpec((1, d), lambda i, *_: (0, 0)),
            pl.BlockSpec(memory_space=pl.ANY),
        ],
        out_specs=pl.BlockSpec((tm, d), lambda i, *_: (i, 0)),
        scratch_shapes=[pltpu.VMEM((2, TOP_K * tm * SUBLANES, LANES), F32),
                        pltpu.VMEM((tm * SUBLANES, LANES), F32),
                        pltpu.SemaphoreType.DMA((2,))],
    )
    return pl.pallas_call(
        functools.partial(_combine_kernel, n_experts=n_experts),
        out_shape=jax.ShapeDtypeStruct((n, d), F32),
        grid_spec=grid_spec,
        compiler_params=_params(1),
        name="l1_moe_combine",
    )(*tabs, slots, weights, x, g.reshape(1, d), ys)


def _forward(x, rel_bias, ev_norm_mix, ev_w_in, ev_pool_w, ev_pool_scale, ev_conv_w, ev_w_out,
             ev_norm_ffn, ev_ffn_gate, ev_ffn_up, ev_ffn_down, od_norm_mix, od_w_qkv, od_b_qkv,
             od_sinks, od_w_o, od_b_o, od_norm_ffn, od_router, od_exp_gate, od_exp_up,
             od_exp_down, final_norm, *, tm, tq, tme, fc_dense):
    b, s, d = x.shape
    n = b * s
    n_heads = od_sinks.shape[-1]
    dq = n_heads * HEAD_DIM
    dk = N_KV_HEADS * HEAD_DIM
    n_experts = od_router.shape[-1]

    x = _mixer(x, ev_norm_mix[0], ev_w_in[0].astype(BF16), ev_pool_w[0].astype(BF16),
               ev_pool_scale[0], ev_conv_w[0], ev_w_out[0].astype(BF16), tm)
    x = x.reshape(n, d)
    ff = od_exp_gate.shape[-1]
    expert_f32 = (od_exp_gate[0].reshape(n_experts * d, ff), od_exp_up[0].reshape(n_experts * d, ff),
                  od_exp_down[0].reshape(n_experts * ff, d))
    x, (exp_gate, exp_up, exp_down) = _ffn(
        x, ev_norm_ffn[0], ev_ffn_gate[0].astype(BF16), ev_ffn_up[0].astype(BF16),
        ev_ffn_down[0].astype(BF16), expert_f32, tm, fc_dense)

    qt, k, vt = _qkv(x.reshape(b, s, d), od_norm_mix[0], od_w_qkv[0], od_b_qkv[0], dq, dk, tm)
    bias = _rel_bias_table(rel_bias)
    attn_t = _attention(qt, k, vt, bias, od_sinks[0], tq)
    x, info, counts = _oproj_router(attn_t, x, od_w_o[0].astype(BF16), od_b_o[0], od_norm_ffn[0],
                                    od_router[0], tm)

    per_tile = counts[:, 0, :n_experts].astype(jnp.int32)
    local_off = jnp.cumsum(per_tile, axis=1) - per_tile
    earlier = jnp.cumsum(per_tile, axis=0) - per_tile
    totals = jnp.sum(per_tile, axis=0)
    tiles_per = (totals + tme - 1) // tme
    tile_end = jnp.cumsum(tiles_per)
    group_start = (tile_end - tiles_per) * tme
    n_tiles = (TOP_K * n) // tme + n_experts
    tile_ids = jnp.arange(n_tiles, dtype=jnp.int32)
    tile_valid = (tile_ids < tile_end[-1]).astype(jnp.int32)
    last_valid = jnp.minimum(tile_ids, tile_end[-1] - 1)
    tile_expert = jnp.sum(last_valid[:, None] >= tile_end[None, :], axis=1).astype(jnp.int32)
    tabs = (per_tile.reshape(-1), local_off.reshape(-1), (group_start[None, :] + earlier).reshape(-1))
    pad_tabs = (group_start + totals, tiles_per * tme - totals, tile_end[-1:])

    tile_of = jnp.arange(n, dtype=jnp.int32)[:, None] // tm
    expert = info[:, INFO_E0:INFO_E1 + 1].astype(jnp.int32)
    slots = (local_off[tile_of, expert] + info[:, INFO_R0:INFO_R1 + 1].astype(jnp.int32)).reshape(-1)
    weights = info[:, INFO_W0:INFO_W1 + 1].reshape(-1)

    xs = _dispatch(x, od_norm_ffn[0], slots, tabs + pad_tabs, n_experts, n_tiles * tme, tm, tme)
    ys = _experts(xs, tile_expert, tile_valid, tile_end[-1:], exp_gate.reshape(n_experts, d, ff),
                  exp_up.reshape(n_experts, d, ff), exp_down.reshape(n_experts, ff, d), d, tme, fc_dense)
    out = _combine(x, slots, weights, tabs, n_experts, ys, final_norm, tm)
    return out.reshape(b, s, d)


def kernel(x, rel_bias, ev_norm_mix, ev_w_in, ev_pool_w, ev_pool_scale, ev_conv_w, ev_w_out, ev_norm_ffn, ev_ffn_gate, ev_ffn_up, ev_ffn_down, od_norm_mix, od_w_qkv, od_b_qkv, od_sinks, od_w_o, od_b_o, od_norm_ffn, od_router, od_exp_gate, od_exp_up, od_exp_down, final_norm):
    return _forward(x, rel_bias, ev_norm_mix, ev_w_in, ev_pool_w, ev_pool_scale, ev_conv_w, ev_w_out,
                    ev_norm_ffn, ev_ffn_gate, ev_ffn_up, ev_ffn_down, od_norm_mix, od_w_qkv, od_b_qkv,
                    od_sinks, od_w_o, od_b_o, od_norm_ffn, od_router, od_exp_gate, od_exp_up,
                    od_exp_down, final_norm, tm=512, tq=512, tme=512, fc_dense=256)
```

```python
import functools
import math

import jax
import jax.numpy as jnp
from jax import lax
from jax.experimental import pallas as pl
from jax.experimental.pallas import tpu as pltpu

F32 = jnp.float32
BF16 = jnp.bfloat16

RMS_EPS = 1e-6
CHUNK = 64
POOL_WINDOWS = (2, 4, 8, 16)
CONV_K = 3
HEAD_DIM = 64
N_KV_HEADS = 2
WINDOW_CHUNKS = 2
KV_SPAN = (WINDOW_CHUNKS + 1) * CHUNK
N_BUCKETS = 32
MAX_DISTANCE = 128
TOP_K = 2
NEG_INF = -1e30

LANES = 128
SUBLANES = 8
VMEM_LIMIT = 56 * 1024 * 1024

POOL_HALO = 16
CONV_HALO = 8

def _rms(x, g):
    return x * lax.rsqrt(jnp.mean(x * x, axis=-1, keepdims=True) + RMS_EPS) * g


def _params(n_axes, vmem=VMEM_LIMIT):
    return pltpu.CompilerParams(dimension_semantics=("arbitrary",) * n_axes,
                                vmem_limit_bytes=vmem)


def _const_spec(shape):
    nd = len(shape)
    return pl.BlockSpec(shape, lambda *_: (0,) * nd, pipeline_mode=pl.Buffered(1))


def _mixer_kernel(x_ref, g_ref, win_ref, pw_ref, ps_ref, cw_ref, wout_ref, o_ref, ubuf, zbuf):
    s = pl.program_id(1)
    tm = x_ref.shape[0]
    pool_w = ubuf.shape[1]
    grp = pool_w // len(POOL_WINDOWS)
    conv_c = zbuf.shape[1]

    @pl.when(s == 0)
    def _():
        ubuf[0:POOL_HALO, :] = jnp.zeros((POOL_HALO, pool_w), F32)
        zbuf[0:CONV_HALO, :] = jnp.zeros((CONV_HALO, conv_c), F32)

    @pl.when(s > 0)
    def _():
        ubuf[0:POOL_HALO, :] = ubuf[tm:tm + POOL_HALO, :]
        zbuf[0:CONV_HALO, :] = zbuf[tm:tm + CONV_HALO, :]

    x = x_ref[...]
    h = _rms(x, g_ref[...]).astype(BF16)
    proj = jnp.dot(h, win_ref[...], preferred_element_type=F32)
    u = proj[:, :pool_w]
    gate_post = proj[:, pool_w:pool_w + conv_c]
    gate_pre = proj[:, pool_w + conv_c:pool_w + 2 * conv_c]
    v = proj[:, pool_w + 2 * conv_c:]
    ubuf[POOL_HALO:POOL_HALO + tm, :] = u
    z = gate_pre * v
    zbuf[CONV_HALO:CONV_HALO + tm, :] = z

    t = s * tm + lax.broadcasted_iota(jnp.int32, (tm, 1), 0)
    parts = []
    for g, w in enumerate(POOL_WINDOWS):
        lo, hi = g * grp, (g + 1) * grp
        ug = u[:, lo:hi]
        acc = ug
        for k in range(1, w):
            acc = acc + ubuf[POOL_HALO - k:POOL_HALO - k + tm, lo:hi]
        count = jnp.minimum(t + 1, w).astype(F32)
        pooled = acc / count - ug
        mixed = jnp.dot(pooled.astype(BF16), pw_ref[g], preferred_element_type=F32)
        parts.append(mixed * ps_ref[:, lo:hi])
    conv = cw_ref[CONV_K - 1:CONV_K, :] * z
    for k in range(CONV_K - 1):
        off = CONV_HALO - (CONV_K - 1) + k
        conv = conv + cw_ref[k:k + 1, :] * zbuf[off:off + tm, :]
    parts.append(gate_post * conv)
    y = jnp.concatenate(parts, axis=1).astype(BF16)
    o_ref[...] = x + jnp.dot(y, wout_ref[...], preferred_element_type=F32)


def _mixer(x, g, w_in, pool_w, pool_scale, conv_w, w_out, tm):
    b, s, d = x.shape
    pool_width = pool_scale.shape[-1]
    conv_c = conv_w.shape[-1]
    return pl.pallas_call(
        _mixer_kernel,
        out_shape=jax.ShapeDtypeStruct((b, s, d), F32),
        grid=(b, s // tm),
        in_specs=[
            pl.BlockSpec((None, tm, d), lambda i, j: (i, j, 0)),
            _const_spec((1, d)),
            _const_spec(w_in.shape),
            _const_spec(pool_w.shape),
            _const_spec((1, pool_width)),
            _const_spec(conv_w.shape),
            _const_spec(w_out.shape),
        ],
        out_specs=pl.BlockSpec((None, tm, d), lambda i, j: (i, j, 0)),
        scratch_shapes=[pltpu.VMEM((POOL_HALO + tm, pool_width), F32),
                        pltpu.VMEM((CONV_HALO + tm, conv_c), F32)],
        compiler_params=_params(2),
        name="l0_mixer",
    )(x, g.reshape(1, d), w_in, pool_w, pool_scale.reshape(1, pool_width), conv_w, w_out)


def _ffn_kernel(x_ref, g_ref, wg_ref, wu_ref, wd_ref, *rest, fc, n_cast):
    cast_in, (o_ref, *cast_out), (h_scr, t_scr) = rest[:n_cast], rest[n_cast:2 * n_cast + 1], rest[2 * n_cast + 1:]
    x = x_ref[...]
    h_scr[...] = _rms(x, g_ref[...]).astype(BF16)
    n_chunks = wg_ref.shape[1] // fc

    def body(c, carry):
        off = pl.multiple_of(c * fc, fc)
        h = h_scr[...]
        a = jnp.dot(h, wg_ref[:, pl.ds(off, fc)], preferred_element_type=F32)
        b = jnp.dot(h, wu_ref[:, pl.ds(off, fc)], preferred_element_type=F32)
        t_scr[:, pl.ds(off, fc)] = (a * jax.nn.sigmoid(a) * b).astype(BF16)
        return carry

    lax.fori_loop(0, n_chunks, body, 0, unroll=True)
    o_ref[...] = x + jnp.dot(t_scr[...], wd_ref[...], preferred_element_type=F32)
    for src, dst in zip(cast_in, cast_out):
        dst[...] = src[...].astype(BF16)


def _ffn(x, g, w_gate, w_up, w_down, to_cast, tm, fc):
    n, d = x.shape
    steps = n // tm
    slabs = []
    for w in to_cast:
        rows = w.shape[0] // steps
        assert rows * steps == w.shape[0] and rows % (2 * SUBLANES) == 0, w.shape
        slabs.append(pl.BlockSpec((rows, w.shape[1]), lambda i: (i, 0)))
    outs = pl.pallas_call(
        functools.partial(_ffn_kernel, fc=fc, n_cast=len(to_cast)),
        out_shape=(jax.ShapeDtypeStruct((n, d), F32),
                   *[jax.ShapeDtypeStruct(w.shape, BF16) for w in to_cast]),
        grid=(steps,),
        in_specs=[
            pl.BlockSpec((tm, d), lambda i: (i, 0)),
            _const_spec((1, d)),
            _const_spec(w_gate.shape),
            _const_spec(w_up.shape),
            _const_spec(w_down.shape),
            *slabs,
        ],
        out_specs=(pl.BlockSpec((tm, d), lambda i: (i, 0)), *slabs),
        scratch_shapes=[pltpu.VMEM((tm, d), BF16), pltpu.VMEM((tm, w_gate.shape[1]), BF16)],
        compiler_params=_params(1),
        name="l0_ffn",
    )(x, g.reshape(1, d), w_gate, w_up, w_down, *to_cast)
    return outs[0], outs[1:]


_NT = (((1,), (1,)), ((), ()))
_TN = (((0,), (0,)), ((), ()))


def _qkv_kernel(x_ref, g_ref, wqt_ref, bq_ref, wk_ref, bk_ref, wvt_ref, bv_ref, qt_ref, k_ref, vt_ref):
    h = _rms(x_ref[...], g_ref[...]).astype(BF16)
    qt = lax.dot_general(wqt_ref[...], h, _NT, preferred_element_type=F32) + bq_ref[...]
    qt_ref[...] = (qt * (HEAD_DIM ** -0.5)).astype(BF16)
    k_ref[...] = (jnp.dot(h, wk_ref[...], preferred_element_type=F32) + bk_ref[...]).astype(BF16)
    vt = lax.dot_general(wvt_ref[...], h, _NT, preferred_element_type=F32) + bv_ref[...]
    vt_ref[...] = vt.astype(BF16)


def _qkv(x, g, w_qkv, b_qkv, dq, dk, tm):
    b, s, d = x.shape
    wqt = w_qkv[:, :dq].T.astype(BF16)
    wk = w_qkv[:, dq:dq + dk].astype(BF16)
    wvt = w_qkv[:, dq + dk:].T.astype(BF16)
    bq = b_qkv[:dq].reshape(dq, 1)
    bk = b_qkv[dq:dq + dk].reshape(1, dk)
    bv = b_qkv[dq + dk:].reshape(dk, 1)
    return pl.pallas_call(
        _qkv_kernel,
        out_shape=(jax.ShapeDtypeStruct((b, dq, s), BF16),
                   jax.ShapeDtypeStruct((b, s, dk), BF16),
                   jax.ShapeDtypeStruct((b, dk, s), BF16)),
        grid=(b, s // tm),
        in_specs=[
            pl.BlockSpec((None, tm, d), lambda bi, i: (bi, i, 0)),
            _const_spec((1, d)),
            _const_spec(wqt.shape), _const_spec(bq.shape),
            _const_spec(wk.shape), _const_spec(bk.shape),
            _const_spec(wvt.shape), _const_spec(bv.shape),
        ],
        out_specs=(pl.BlockSpec((None, dq, tm), lambda bi, i: (bi, 0, i)),
                   pl.BlockSpec((None, tm, dk), lambda bi, i: (bi, i, 0)),
                   pl.BlockSpec((None, dk, tm), lambda bi, i: (bi, 0, i))),
        compiler_params=_params(2),
        name="l1_qkv",
    )(x, g.reshape(1, d), wqt, bq, wk, bk, wvt, bv)


def _bias_kernel(onehot_ref, rb_ref, o_ref):
    o_ref[...] = jnp.dot(onehot_ref[...], rb_ref[...], preferred_element_type=F32,
                         precision=lax.Precision.HIGHEST)


def _t5_bucket(rel):
    nb = N_BUCKETS // 2
    max_exact = nb // 2
    ret = jnp.where(rel > 0, nb, 0)
    n = jnp.abs(rel)
    nf = jnp.maximum(n, 1).astype(jnp.float32)
    large = max_exact + (jnp.log(nf / max_exact) / math.log(MAX_DISTANCE / max_exact)
                         * (nb - max_exact)).astype(jnp.int32)
    large = jnp.minimum(large, nb - 1)
    return ret + jnp.where(n < max_exact, n, large)


def _rel_bias_table(rel_bias):
    n_heads = rel_bias.shape[1]
    pad = WINDOW_CHUNKS * CHUNK
    rel = (jnp.arange(KV_SPAN) - pad)[None, :] - jnp.arange(CHUNK)[:, None]
    bucket = _t5_bucket(rel).reshape(-1)
    onehot = (bucket[:, None] == jnp.arange(N_BUCKETS)[None, :]).astype(F32)
    rows = CHUNK * KV_SPAN
    rb = jnp.pad(rel_bias.astype(F32), ((0, 0), (0, LANES - n_heads)))
    out = pl.pallas_call(
        _bias_kernel,
        out_shape=jax.ShapeDtypeStruct((rows, LANES), F32),
        name="l1_rel_bias",
    )(onehot, rb)
    return jnp.transpose(out[:, :n_heads].reshape(CHUNK, KV_SPAN, n_heads), (2, 0, 1))


PAIR = 2 * CHUNK
PAIR_KEYS = KV_SPAN + CHUNK


def _attn_kernel(qt_ref, kp_ref, km_ref, vtp_ref, vtm_ref, bias_ref, sink_ref, ot_ref, kbuf, vtbuf,
                 s_a, s_b, *, n_heads):
    i = pl.program_id(1)
    tq = qt_ref.shape[1]
    pad = WINDOW_CHUNKS * CHUNK
    per_kv = n_heads // N_KV_HEADS
    kbuf[0:pad, :] = kp_ref[...]
    kbuf[pad:pad + tq, :] = km_ref[...]
    vtbuf[:, 0:pad] = vtp_ref[...]
    vtbuf[:, pad:pad + tq] = vtm_ref[...]
    zeros = jnp.zeros((HEAD_DIM, PAIR), BF16)
    n_pairs = tq // PAIR

    def scores(jj, s_scr):
        c0 = jj * PAIR
        kc = kbuf[c0:c0 + PAIR_KEYS, :]
        for h in range(n_heads):
            pieces = [zeros] * N_KV_HEADS
            pieces[h // per_kv] = qt_ref[h * HEAD_DIM:(h + 1) * HEAD_DIM, c0:c0 + PAIR]
            rhs = jnp.concatenate(pieces, axis=0)
            s_scr[h] = jnp.dot(kc, rhs, preferred_element_type=F32)

    def softmax_values(jj, s_scr):
        c0 = jj * PAIR
        variant = jnp.where(i == 0, 1, 0) if jj == 0 else 0
        for h in range(n_heads):
            kv = h // per_kv
            s = s_scr[h] + bias_ref[variant, h]
            sk = sink_ref[h]
            m_ = jnp.maximum(jnp.max(s, axis=0, keepdims=True), sk)
            p = jnp.exp(s - m_)
            denom = jnp.sum(p, axis=0, keepdims=True) + jnp.exp(sk - m_)
            p = (p / denom).astype(BF16)
            vt = vtbuf[kv * HEAD_DIM:(kv + 1) * HEAD_DIM, c0:c0 + PAIR_KEYS]
            o = jnp.dot(vt, p, preferred_element_type=F32)
            ot_ref[h * HEAD_DIM:(h + 1) * HEAD_DIM, c0:c0 + PAIR] = o.astype(BF16)

    bufs = (s_a, s_b)
    scores(0, bufs[0])
    for jj in range(n_pairs):
        if jj + 1 < n_pairs:
            scores(jj + 1, bufs[(jj + 1) % 2])
        softmax_values(jj, bufs[jj % 2])


def _pair_bias(bias):
    bt = jnp.transpose(bias, (0, 2, 1)).astype(F32)
    neg = jnp.full((bias.shape[0], CHUNK, CHUNK), NEG_INF, F32)
    first = jnp.concatenate([bt, neg], axis=1)
    second = jnp.concatenate([neg, bt], axis=1)
    table = jnp.concatenate([first, second], axis=2)
    key = jnp.arange(PAIR_KEYS)[None, :, None]
    masked = jnp.where(key < WINDOW_CHUNKS * CHUNK, NEG_INF, table)
    return jnp.stack([table, masked])


def _attention(qt, k, vt, bias, sinks, tq):
    b, dq, s = qt.shape
    dk = k.shape[-1]
    n_heads = dq // HEAD_DIM
    pad = WINDOW_CHUNKS * CHUNK
    ratio = tq // pad
    table = _pair_bias(bias)
    prev = lambda i: jnp.maximum(i * ratio - 1, 0)
    return pl.pallas_call(
        functools.partial(_attn_kernel, n_heads=n_heads),
        out_shape=jax.ShapeDtypeStruct((b, dq, s), BF16),
        grid=(b, s // tq),
        in_specs=[
            pl.BlockSpec((None, dq, tq), lambda bi, i: (bi, 0, i)),
            pl.BlockSpec((None, pad, dk), lambda bi, i: (bi, prev(i), 0)),
            pl.BlockSpec((None, tq, dk), lambda bi, i: (bi, i, 0)),
            pl.BlockSpec((None, dk, pad), lambda bi, i: (bi, 0, prev(i))),
            pl.BlockSpec((None, dk, tq), lambda bi, i: (bi, 0, i)),
            _const_spec(table.shape),
            pl.BlockSpec(memory_space=pltpu.SMEM),
        ],
        out_specs=pl.BlockSpec((None, dq, tq), lambda bi, i: (bi, 0, i)),
        scratch_shapes=[pltpu.VMEM((pad + tq, dk), BF16), pltpu.VMEM((dk, pad + tq), BF16),
                        pltpu.VMEM((n_heads, PAIR_KEYS, PAIR), F32),
                        pltpu.VMEM((n_heads, PAIR_KEYS, PAIR), F32)],
        compiler_params=_params(2),
        name="l1_attention",
    )(qt, k, k, vt, vt, table, sinks.astype(F32))


INFO_E0, INFO_E1, INFO_S0, INFO_S1, INFO_W0, INFO_W1 = range(6)


def _oproj_router_kernel(ot_ref, x_ref, wo_ref, bo_ref, g_ref, wr_ref, x3_ref, info_ref, cnt_ref,
                         *, n_experts):
    tm = x_ref.shape[0]
    attn = lax.dot_general(ot_ref[...], wo_ref[...], _TN, preferred_element_type=F32)
    x3 = x_ref[...] + attn + bo_ref[...]
    x3_ref[...] = x3
    h = _rms(x3, g_ref[...])
    h_hi = h.astype(BF16)
    h_lo = (h - h_hi.astype(F32)).astype(BF16)
    wr = wr_ref[...]
    hw = jnp.dot(h_hi, wr, preferred_element_type=F32)
    logits = (hw[:, :LANES] + hw[:, LANES:]) + jnp.dot(h_lo, wr[:, :LANES], preferred_element_type=F32)
    lane_i = lax.broadcasted_iota(jnp.int32, (tm, LANES), 1)
    lane = lane_i.astype(F32)
    neg = jnp.float32(-jnp.inf)
    logits = jnp.where(lane_i < n_experts, logits, neg)
    m0 = jnp.max(logits, axis=-1, keepdims=True)
    e0 = jnp.min(jnp.where(logits == m0, lane, float(LANES)), axis=-1, keepdims=True)
    rest = jnp.where(lane == e0, neg, logits)
    m1 = jnp.max(rest, axis=-1, keepdims=True)
    e1 = jnp.min(jnp.where(rest == m1, lane, float(LANES)), axis=-1, keepdims=True)
    t = jnp.exp(m1 - m0)
    w0 = 1.0 / (1.0 + t)
    w1 = t / (1.0 + t)
    hot0 = lane == e0
    hot1 = lane == e1
    both = jnp.where(hot0 | hot1, 1.0, 0.0).astype(BF16)
    row = lax.broadcasted_iota(jnp.int32, (tm, tm), 0)
    col = lax.broadcasted_iota(jnp.int32, (tm, tm), 1)
    tri = (col < row).astype(BF16)
    cnt = jnp.sum(both.astype(F32), axis=0, keepdims=True)
    cnt_ref[...] = cnt
    upto = jnp.broadcast_to(cnt, (SUBLANES, LANES))
    lane8 = lax.broadcasted_iota(jnp.int32, (SUBLANES, LANES), 1)
    shift = 1
    while shift < n_experts:
        upto = upto + jnp.where(lane8 >= shift, pltpu.roll(upto, shift, axis=1), 0.0)
        shift *= 2
    before = jnp.dot(tri, both, preferred_element_type=F32) + (upto[0:1, :] - cnt)
    r0 = jnp.sum(jnp.where(hot0, before, 0.0), axis=-1, keepdims=True)
    r1 = jnp.sum(jnp.where(hot1, before, 0.0), axis=-1, keepdims=True)
    info = jnp.zeros((tm, LANES), F32)
    for idx, val in ((INFO_E0, e0), (INFO_E1, e1), (INFO_S0, r0),
                     (INFO_S1, r1), (INFO_W0, w0), (INFO_W1, w1)):
        info = jnp.where(lane_i == idx, val, info)
    info_ref[...] = info


def _oproj_router(ot, x, w_o, b_o, g, w_router, tm):
    n, d = x.shape
    dq, seq = ot.shape[1:]
    per_seq = seq // tm
    n_experts = w_router.shape[1]
    wr = jnp.pad(w_router.astype(F32), ((0, 0), (0, LANES - n_experts)))
    wr_hi = wr.astype(BF16)
    wr_lo = (wr - wr_hi.astype(F32)).astype(BF16)
    wr = jnp.concatenate([wr_hi, wr_lo], axis=1)
    return pl.pallas_call(
        functools.partial(_oproj_router_kernel, n_experts=n_experts),
        out_shape=(jax.ShapeDtypeStruct((n, d), F32),
                   jax.ShapeDtypeStruct((n, LANES), F32),
                   jax.ShapeDtypeStruct((n // tm, 1, LANES), F32)),
        grid=(n // tm,),
        in_specs=[
            pl.BlockSpec((None, dq, tm), lambda i: (i // per_seq, 0, i % per_seq)),
            pl.BlockSpec((tm, d), lambda i: (i, 0)),
            _const_spec(w_o.shape),
            _const_spec((1, d)),
            _const_spec((1, d)),
            _const_spec(wr.shape),
        ],
        out_specs=(pl.BlockSpec((tm, d), lambda i: (i, 0)),
                   pl.BlockSpec((tm, LANES), lambda i: (i, 0)),
                   pl.BlockSpec((None, 1, LANES), lambda i: (i, 0, 0))),
        compiler_params=_params(1),
        name="l1_oproj_router",
    )(ot, x, w_o, b_o.reshape(1, d), g.reshape(1, d), wr)


def _pieces(n, n_bits, fn):
    for bit in reversed(range(n_bits)):
        size = 1 << bit
        done = n & ~((size << 1) - 1)
        pl.when((n & size) != 0)(functools.partial(fn, done, size))


def _rows(ref, first_row, n_rows):
    return ref.at[pl.ds(pl.multiple_of(first_row * SUBLANES, SUBLANES), n_rows * SUBLANES)]


def _segment_copies(tabs, i, n_experts, lbuf, hbm, sem, to_hbm, act):
    n_tab, o_tab, d_tab = tabs[:3]
    n_bits = (lbuf.shape[0] // SUBLANES // TOP_K).bit_length()
    for e in range(n_experts):
        local0 = o_tab[i * n_experts + e]
        sorted0 = d_tab[i * n_experts + e]

        def piece(done, size, local0=local0, sorted0=sorted0):
            local = _rows(lbuf, local0 + done, size)
            remote = _rows(hbm, sorted0 + done, size)
            act(pltpu.make_async_copy(local, remote, sem) if to_hbm
                else pltpu.make_async_copy(remote, local, sem))

        _pieces(n_tab[i * n_experts + e], n_bits, piece)


def _dispatch_kernel(n_tab, o_tab, d_tab, pad0_tab, padn_tab, nv_tab, slot_ref, x_ref, g_ref, xs_hbm,
                     hbuf, lbuf, sems, *, n_experts, tme):
    i = pl.program_id(0)
    tm = x_ref.shape[0]
    n_slots = TOP_K * tm
    h = _rms(x_ref[...], g_ref[...])
    for c in range(x_ref.shape[1] // LANES):
        hbuf[pl.ds(c, tm, stride=SUBLANES), :] = h[:, c * LANES:(c + 1) * LANES]
    slot = i % 2
    sorted_rows = lbuf.at[slot]

    def token(t, carry):
        row = hbuf[pl.ds(pl.multiple_of(t * SUBLANES, SUBLANES), SUBLANES), :]
        for k in range(TOP_K):
            dst = pl.multiple_of(slot_ref[TOP_K * t + k] * SUBLANES, SUBLANES)
            sorted_rows[pl.ds(dst, SUBLANES), :] = row
        return carry

    lax.fori_loop(0, tm, token, 0, unroll=8)
    tabs = (n_tab, o_tab, d_tab)

    def copies(tile, s, act):
        _segment_copies(tabs, tile, n_experts, lbuf.at[s], xs_hbm, sems.at[s], True, act)

    copies(i, slot, lambda cp: cp.start())

    def wait_all(s):
        pltpu.make_async_copy(lbuf.at[s], _rows(xs_hbm, 0, n_slots), sems.at[s]).wait()

    @pl.when(i > 0)
    def _():
        wait_all(1 - slot)

    @pl.when(i == pl.num_programs(0) - 1)
    def _():
        wait_all(slot)
        zeros = lbuf.at[slot]
        zeros[0:tme * SUBLANES, :] = jnp.zeros((tme * SUBLANES, LANES), F32)

        def pads(act):
            for e in range(n_experts):
                def piece(done, size, e=e):
                    act(pltpu.make_async_copy(_rows(zeros, 0, size),
                                              _rows(xs_hbm, pad0_tab[e] + done, size), sems.at[slot]))
                _pieces(padn_tab[e], (tme - 1).bit_length(), piece)
            n_tiles = xs_hbm.shape[0] // (tme * SUBLANES)
            for t in range(n_experts):
                tile = nv_tab[0] + t
                pl.when(tile < n_tiles)(lambda tile=tile: act(pltpu.make_async_copy(
                    _rows(zeros, 0, tme), _rows(xs_hbm, tile * tme, tme), sems.at[slot])))

        pads(lambda cp: cp.start())
        pads(lambda cp: cp.wait())


def _dispatch(x, g, slots, tabs, n_experts, n_rows, tm, tme):
    n, d = x.shape
    assert TOP_K * tm >= tme
    grid_spec = pltpu.PrefetchScalarGridSpec(
        num_scalar_prefetch=len(tabs),
        grid=(n // tm,),
        in_specs=[
            pl.BlockSpec((TOP_K * tm,), lambda i, *_: (i,), memory_space=pltpu.SMEM),
            pl.BlockSpec((tm, d), lambda i, *_: (i, 0)),
            pl.BlockSpec((1, d), lambda i, *_: (0, 0)),
        ],
        out_specs=pl.BlockSpec(memory_space=pl.ANY),
        scratch_shapes=[pltpu.VMEM((tm * SUBLANES, LANES), F32),
                        pltpu.VMEM((2, TOP_K * tm * SUBLANES, LANES), F32),
                        pltpu.SemaphoreType.DMA((2,))],
    )
    return pl.pallas_call(
        functools.partial(_dispatch_kernel, n_experts=n_experts, tme=tme),
        out_shape=jax.ShapeDtypeStruct((n_rows * SUBLANES, LANES), F32),
        grid_spec=grid_spec,
        compiler_params=_params(1),
        name="l1_moe_dispatch",
    )(*tabs, slots, x, g.reshape(1, d))


def _experts_kernel(te_ref, tv_ref, nv_ref, xs_ref, wg_ref, wu_ref, wd_ref, ys_ref, t_scr, *, sub):
    del te_ref, nv_ref
    j = pl.program_id(0)
    tm = t_scr.shape[0]
    n_slabs = wg_ref.shape[0] // LANES

    @pl.when(tv_ref[j] == 1)
    def _():
        slabs = [xs_ref[pl.ds(s, tm, stride=SUBLANES), :] for s in range(n_slabs)]
        x = jnp.concatenate(slabs, axis=1).astype(BF16)

        def body(q, carry):
            off = pl.multiple_of(q * sub, sub)
            a = jnp.dot(x, wg_ref[:, pl.ds(off, sub)], preferred_element_type=F32)
            b = jnp.dot(x, wu_ref[:, pl.ds(off, sub)], preferred_element_type=F32)
            t_scr[:, pl.ds(off, sub)] = (a * jax.nn.sigmoid(a) * b).astype(BF16)
            return carry
        lax.fori_loop(0, wg_ref.shape[1] // sub, body, 0, unroll=True)
        y = jnp.dot(t_scr[...], wd_ref[...], preferred_element_type=F32)
        for s in range(n_slabs):
            ys_ref[pl.ds(s, tm, stride=SUBLANES), :] = y[:, s * LANES:(s + 1) * LANES]

    @pl.when(tv_ref[j] != 1)
    def _():
        ys_ref[...] = jnp.zeros_like(ys_ref)


def _experts(xs, tile_expert, tile_valid, n_valid, w_gate, w_up, w_down, d, tm, sub):
    n_tiles = tile_expert.shape[0]
    ff = w_gate.shape[2]
    once = pl.Buffered(1)
    grid_spec = pltpu.PrefetchScalarGridSpec(
        num_scalar_prefetch=3,
        grid=(n_tiles,),
        in_specs=[
            pl.BlockSpec((tm * SUBLANES, LANES), lambda j, te, tv, nv: (jnp.minimum(j, nv[0] - 1), 0)),
            pl.BlockSpec((None, d, ff), lambda j, te, tv, nv: (te[j], 0, 0), pipeline_mode=once),
            pl.BlockSpec((None, d, ff), lambda j, te, tv, nv: (te[j], 0, 0), pipeline_mode=once),
            pl.BlockSpec((None, ff, d), lambda j, te, tv, nv: (te[j], 0, 0)),
        ],
        out_specs=pl.BlockSpec((tm * SUBLANES, LANES), lambda j, te, tv, nv: (j, 0)),
        scratch_shapes=[pltpu.VMEM((tm, ff), BF16)],
    )
    return pl.pallas_call(
        functools.partial(_experts_kernel, sub=sub),
        out_shape=jax.ShapeDtypeStruct(xs.shape, F32),
        grid_spec=grid_spec,
        compiler_params=_params(1),
        name="l1_moe_experts",
    )(tile_expert, tile_valid, n_valid, xs, w_gate, w_up, w_down)


def _combine_kernel(n_tab, o_tab, d_tab, slot_ref, w_ref, x_ref, g_ref, ys_hbm, o_ref, lbuf, mbuf, sems,
                    *, n_experts):
    i = pl.program_id(0)
    tm = x_ref.shape[0]
    n_slots = TOP_K * tm
    tabs = (n_tab, o_tab, d_tab)
    slot = i % 2

    def copies(tile, s, act):
        _segment_copies(tabs, tile, n_experts, lbuf.at[s], ys_hbm, sems.at[s], False, act)

    @pl.when(i == 0)
    def _():
        copies(i, slot, lambda cp: cp.start())

    @pl.when(i + 1 < pl.num_programs(0))
    def _():
        copies(i + 1, 1 - slot, lambda cp: cp.start())

    pltpu.make_async_copy(_rows(ys_hbm, 0, n_slots), lbuf.at[slot], sems.at[slot]).wait()
    y = lbuf.at[slot]

    def token(t, carry):
        acc = None
        for k in range(TOP_K):
            row = y[pl.ds(pl.multiple_of(slot_ref[TOP_K * t + k] * SUBLANES, SUBLANES), SUBLANES), :]
            term = w_ref[TOP_K * t + k] * row
            acc = term if acc is None else acc + term
        mbuf[pl.ds(pl.multiple_of(t * SUBLANES, SUBLANES), SUBLANES), :] = acc
        return carry

    lax.fori_loop(0, tm, token, 0, unroll=8)
    moe = jnp.concatenate([mbuf[pl.ds(c, tm, stride=SUBLANES), :]
                           for c in range(x_ref.shape[1] // LANES)], axis=1)
    o_ref[...] = _rms(x_ref[...] + moe, g_ref[...])


def _combine(x, slots, weights, tabs, n_experts, ys, g, tm):
    n, d = x.shape
    grid_spec = pltpu.PrefetchScalarGridSpec(
        num_scalar_prefetch=3,
        grid=(n // tm,),
        in_specs=[
            pl.BlockSpec((TOP_K * tm,), lambda i, *_: (i,), memory_space=pltpu.SMEM),
            pl.BlockSpec((TOP_K * tm,), lambda i, *_: (i,), memory_space=pltpu.SMEM),
            pl.BlockSpec((tm, d), lambda i, *_: (i, 0)),
            pl.BlockSpec((1, d), lambda i, *_: (0, 0)),
            pl.BlockSpec(memory_space=pl.ANY),
        ],
        out_specs=pl.BlockSpec((tm, d), lambda i, *_: (i, 0)),
        scratch_shapes=[pltpu.VMEM((2, TOP_K * tm * SUBLANES, LANES), F32),
                        pltpu.VMEM((tm * SUBLANES, LANES), F32),
                        pltpu.SemaphoreType.DMA((2,))],
    )
    return pl.pallas_call(
        functools.partial(_combine_kernel, n_experts=n_experts),
        out_shape=jax.ShapeDtypeStruct((n, d), F32),
        grid_spec=grid_spec,
        compiler_params=_params(1),
        name="l1_moe_combine",
    )(*tabs, slots, weights, x, g.reshape(1, d), ys)


def _forward(x, rel_bias, ev_norm_mix, ev_w_in, ev_pool_w, ev_pool_scale, ev_conv_w, ev_w_out,
             ev_norm_ffn, ev_ffn_gate, ev_ffn_up, ev_ffn_down, od_norm_mix, od_w_qkv, od_b_qkv,
             od_sinks, od_w_o, od_b_o, od_norm_ffn, od_router, od_exp_gate, od_exp_up,
             od_exp_down, final_norm, *, tm, tq, tme, fc_dense):
    b, s, d = x.shape
    n = b * s
    n_heads = od_sinks.shape[-1]
    dq = n_heads * HEAD_DIM
    dk = N_KV_HEADS * HEAD_DIM
    n_experts = od_router.shape[-1]

    x = _mixer(x, ev_norm_mix[0], ev_w_in[0].astype(BF16), ev_pool_w[0].astype(BF16),
               ev_pool_scale[0], ev_conv_w[0], ev_w_out[0].astype(BF16), tm)
    x = x.reshape(n, d)
    ff = od_exp_gate.shape[-1]
    expert_f32 = (od_exp_gate[0].reshape(n_experts * d, ff), od_exp_up[0].reshape(n_experts * d, ff),
                  od_exp_down[0].reshape(n_experts * ff, d))
    x, (exp_gate, exp_up, exp_down) = _ffn(
        x, ev_norm_ffn[0], ev_ffn_gate[0].astype(BF16), ev_ffn_up[0].astype(BF16),
        ev_ffn_down[0].astype(BF16), expert_f32, tm, fc_dense)

    qt, k, vt = _qkv(x.reshape(b, s, d), od_norm_mix[0], od_w_qkv[0], od_b_qkv[0], dq, dk, tm)
    bias = _rel_bias_table(rel_bias)
    attn_t = _attention(qt, k, vt, bias, od_sinks[0], tq)
    x, info, counts = _oproj_router(attn_t, x, od_w_o[0].astype(BF16), od_b_o[0], od_norm_ffn[0],
                                    od_router[0], tm)

    per_tile = counts[:, 0, :n_experts].astype(jnp.int32)
    local_off = jnp.cumsum(per_tile, axis=1) - per_tile
    earlier = jnp.cumsum(per_tile, axis=0) - per_tile
    totals = jnp.sum(per_tile, axis=0)
    tiles_per = (totals + tme - 1) // tme
    tile_end = jnp.cumsum(tiles_per)
    group_start = (tile_end - tiles_per) * tme
    n_tiles = (TOP_K * n) // tme + n_experts
    tile_ids = jnp.arange(n_tiles, dtype=jnp.int32)
    tile_valid = (tile_ids < tile_end[-1]).astype(jnp.int32)
    last_valid = jnp.minimum(tile_ids, tile_end[-1] - 1)
    tile_expert = jnp.sum(last_valid[:, None] >= tile_end[None, :], axis=1).astype(jnp.int32)
    tabs = (per_tile.reshape(-1), local_off.reshape(-1), (group_start[None, :] + earlier).reshape(-1))
    pad_tabs = (group_start + totals, tiles_per * tme - totals, tile_end[-1:])

    slots = info[:, INFO_S0:INFO_S1 + 1].astype(jnp.int32).reshape(-1)
    weights = info[:, INFO_W0:INFO_W1 + 1].reshape(-1)

    xs = _dispatch(x, od_norm_ffn[0], slots, tabs + pad_tabs, n_experts, n_tiles * tme, tm, tme)
    ys = _experts(xs, tile_expert, tile_valid, tile_end[-1:], exp_gate.reshape(n_experts, d, ff),
                  exp_up.reshape(n_experts, d, ff), exp_down.reshape(n_experts, ff, d), d, tme, fc_dense)
    out = _combine(x, slots, weights, tabs, n_experts, ys, final_norm, tm)
    return out.reshape(b, s, d)


def kernel(x, rel_bias, ev_norm_mix, ev_w_in, ev_pool_w, ev_pool_scale, ev_conv_w, ev_w_out, ev_norm_ffn, ev_ffn_gate, ev_ffn_up, ev_ffn_down, od_norm_mix, od_w_qkv, od_b_qkv, od_sinks, od_w_o, od_b_o, od_norm_ffn, od_router, od_exp_gate, od_exp_up, od_exp_down, final_norm):
    return _forward(x, rel_bias, ev_norm_mix, ev_w_in, ev_pool_w, ev_pool_scale, ev_conv_w, ev_w_out,
                    ev_norm_ffn, ev_ffn_gate, ev_ffn_up, ev_ffn_down, od_norm_mix, od_w_qkv, od_b_qkv,
                    od_sinks, od_w_o, od_b_o, od_norm_ffn, od_router, od_exp_gate, od_exp_up,
                    od_exp_down, final_norm, tm=512, tq=512, tme=512, fc_dense=256)
```

```python
import functools
import math

import jax
import jax.numpy as jnp
from jax import lax
from jax.experimental import pallas as pl
from jax.experimental.pallas import tpu as pltpu

F32 = jnp.float32
BF16 = jnp.bfloat16

RMS_EPS = 1e-6
CHUNK = 64
POOL_WINDOWS = (2, 4, 8, 16)
CONV_K = 3
HEAD_DIM = 64
N_KV_HEADS = 2
WINDOW_CHUNKS = 2
KV_SPAN = (WINDOW_CHUNKS + 1) * CHUNK
N_BUCKETS = 32
MAX_DISTANCE = 128
TOP_K = 2
NEG_INF = -1e30

LANES = 128
SUBLANES = 8
VMEM_LIMIT = 56 * 1024 * 1024

POOL_HALO = 16
CONV_HALO = 8

def _rms(x, g):
    return x * lax.rsqrt(jnp.mean(x * x, axis=-1, keepdims=True) + RMS_EPS) * g


def _params(n_axes, vmem=VMEM_LIMIT):
    return pltpu.CompilerParams(dimension_semantics=("arbitrary",) * n_axes,
                                vmem_limit_bytes=vmem)


def _const_spec(shape):
    nd = len(shape)
    return pl.BlockSpec(shape, lambda *_: (0,) * nd, pipeline_mode=pl.Buffered(1))


def _mixer_kernel(x_ref, g_ref, win_ref, pw_ref, ps_ref, cw_ref, wout_ref, o_ref, ubuf, zbuf):
    s = pl.program_id(1)
    tm = x_ref.shape[0]
    pool_w = ubuf.shape[1]
    grp = pool_w // len(POOL_WINDOWS)
    conv_c = zbuf.shape[1]

    @pl.when(s == 0)
    def _():
        ubuf[0:POOL_HALO, :] = jnp.zeros((POOL_HALO, pool_w), F32)
        zbuf[0:CONV_HALO, :] = jnp.zeros((CONV_HALO, conv_c), F32)

    @pl.when(s > 0)
    def _():
        ubuf[0:POOL_HALO, :] = ubuf[tm:tm + POOL_HALO, :]
        zbuf[0:CONV_HALO, :] = zbuf[tm:tm + CONV_HALO, :]

    x = x_ref[...]
    h = _rms(x, g_ref[...]).astype(BF16)
    proj = jnp.dot(h, win_ref[...], preferred_element_type=F32)
    u = proj[:, :pool_w]
    gate_post = proj[:, pool_w:pool_w + conv_c]
    gate_pre = proj[:, pool_w + conv_c:pool_w + 2 * conv_c]
    v = proj[:, pool_w + 2 * conv_c:]
    ubuf[POOL_HALO:POOL_HALO + tm, :] = u
    z = gate_pre * v
    zbuf[CONV_HALO:CONV_HALO + tm, :] = z

    t = s * tm + lax.broadcasted_iota(jnp.int32, (tm, 1), 0)
    parts = []
    for g, w in enumerate(POOL_WINDOWS):
        lo, hi = g * grp, (g + 1) * grp
        ug = u[:, lo:hi]
        acc = ug
        for k in range(1, w):
            acc = acc + ubuf[POOL_HALO - k:POOL_HALO - k + tm, lo:hi]
        count = jnp.minimum(t + 1, w).astype(F32)
        pooled = acc / count - ug
        mixed = jnp.dot(pooled.astype(BF16), pw_ref[g], preferred_element_type=F32)
        parts.append(mixed * ps_ref[:, lo:hi])
    conv = cw_ref[CONV_K - 1:CONV_K, :] * z
    for k in range(CONV_K - 1):
        off = CONV_HALO - (CONV_K - 1) + k
        conv = conv + cw_ref[k:k + 1, :] * zbuf[off:off + tm, :]
    parts.append(gate_post * conv)
    y = jnp.concatenate(parts, axis=1).astype(BF16)
    o_ref[...] = x + jnp.dot(y, wout_ref[...], preferred_element_type=F32)


def _mixer(x, g, w_in, pool_w, pool_scale, conv_w, w_out, tm):
    b, s, d = x.shape
    pool_width = pool_scale.shape[-1]
    conv_c = conv_w.shape[-1]
    return pl.pallas_call(
        _mixer_kernel,
        out_shape=jax.ShapeDtypeStruct((b, s, d), F32),
        grid=(b, s // tm),
        in_specs=[
            pl.BlockSpec((None, tm, d), lambda i, j: (i, j, 0)),
            _const_spec((1, d)),
            _const_spec(w_in.shape),
            _const_spec(pool_w.shape),
            _const_spec((1, pool_width)),
            _const_spec(conv_w.shape),
            _const_spec(w_out.shape),
        ],
        out_specs=pl.BlockSpec((None, tm, d), lambda i, j: (i, j, 0)),
        scratch_shapes=[pltpu.VMEM((POOL_HALO + tm, pool_width), F32),
                        pltpu.VMEM((CONV_HALO + tm, conv_c), F32)],
        compiler_params=_params(2),
        name="l0_mixer",
    )(x, g.reshape(1, d), w_in, pool_w, pool_scale.reshape(1, pool_width), conv_w, w_out)


def _ffn_kernel(x_ref, g_ref, wg_ref, wu_ref, wd_ref, *rest, fc, n_cast):
    cast_in, (o_ref, *cast_out), (h_scr, t_scr) = rest[:n_cast], rest[n_cast:2 * n_cast + 1], rest[2 * n_cast + 1:]
    x = x_ref[...]
    h_scr[...] = _rms(x, g_ref[...]).astype(BF16)
    n_chunks = wg_ref.shape[1] // fc

    def body(c, carry):
        off = pl.multiple_of(c * fc, fc)
        h = h_scr[...]
        a = jnp.dot(h, wg_ref[:, pl.ds(off, fc)], preferred_element_type=F32)
        b = jnp.dot(h, wu_ref[:, pl.ds(off, fc)], preferred_element_type=F32)
        t_scr[:, pl.ds(off, fc)] = (a * jax.nn.sigmoid(a) * b).astype(BF16)
        return carry

    lax.fori_loop(0, n_chunks, body, 0, unroll=True)
    o_ref[...] = x + jnp.dot(t_scr[...], wd_ref[...], preferred_element_type=F32)
    for src, dst in zip(cast_in, cast_out):
        dst[...] = src[...].astype(BF16)


def _ffn(x, g, w_gate, w_up, w_down, to_cast, tm, fc):
    n, d = x.shape
    steps = n // tm
    slabs = []
    for w in to_cast:
        rows = w.shape[0] // steps
        assert rows * steps == w.shape[0] and rows % (2 * SUBLANES) == 0, w.shape
        slabs.append(pl.BlockSpec((rows, w.shape[1]), lambda i: (i, 0)))
    outs = pl.pallas_call(
        functools.partial(_ffn_kernel, fc=fc, n_cast=len(to_cast)),
        out_shape=(jax.ShapeDtypeStruct((n, d), F32),
                   *[jax.ShapeDtypeStruct(w.shape, BF16) for w in to_cast]),
        grid=(steps,),
        in_specs=[
            pl.BlockSpec((tm, d), lambda i: (i, 0)),
            _const_spec((1, d)),
            _const_spec(w_gate.shape),
            _const_spec(w_up.shape),
            _const_spec(w_down.shape),
            *slabs,
        ],
        out_specs=(pl.BlockSpec((tm, d), lambda i: (i, 0)), *slabs),
        scratch_shapes=[pltpu.VMEM((tm, d), BF16), pltpu.VMEM((tm, w_gate.shape[1]), BF16)],
        compiler_params=_params(1),
        name="l0_ffn",
    )(x, g.reshape(1, d), w_gate, w_up, w_down, *to_cast)
    return outs[0], outs[1:]


_NT = (((1,), (1,)), ((), ()))
_TN = (((0,), (0,)), ((), ()))


def _qkv_kernel(x_ref, g_ref, wqt_ref, bq_ref, wk_ref, bk_ref, wvt_ref, bv_ref, qt_ref, k_ref, vt_ref):
    h = _rms(x_ref[...], g_ref[...]).astype(BF16)
    qt = lax.dot_general(wqt_ref[...], h, _NT, preferred_element_type=F32) + bq_ref[...]
    qt_ref[...] = (qt * (HEAD_DIM ** -0.5)).astype(BF16)
    k_ref[...] = (jnp.dot(h, wk_ref[...], preferred_element_type=F32) + bk_ref[...]).astype(BF16)
    vt = lax.dot_general(wvt_ref[...], h, _NT, preferred_element_type=F32) + bv_ref[...]
    vt_ref[...] = vt.astype(BF16)


def _qkv(x, g, w_qkv, b_qkv, dq, dk, tm):
    b, s, d = x.shape
    wqt = w_qkv[:, :dq].T.astype(BF16)
    wk = w_qkv[:, dq:dq + dk].astype(BF16)
    wvt = w_qkv[:, dq + dk:].T.astype(BF16)
    bq = b_qkv[:dq].reshape(dq, 1)
    bk = b_qkv[dq:dq + dk].reshape(1, dk)
    bv = b_qkv[dq + dk:].reshape(dk, 1)
    return pl.pallas_call(
        _qkv_kernel,
        out_shape=(jax.ShapeDtypeStruct((b, dq, s), BF16),
                   jax.ShapeDtypeStruct((b, s, dk), BF16),
                   jax.ShapeDtypeStruct((b, dk, s), BF16)),
        grid=(b, s // tm),
        in_specs=[
            pl.BlockSpec((None, tm, d), lambda bi, i: (bi, i, 0)),
            _const_spec((1, d)),
            _const_spec(wqt.shape), _const_spec(bq.shape),
            _const_spec(wk.shape), _const_spec(bk.shape),
            _const_spec(wvt.shape), _const_spec(bv.shape),
        ],
        out_specs=(pl.BlockSpec((None, dq, tm), lambda bi, i: (bi, 0, i)),
                   pl.BlockSpec((None, tm, dk), lambda bi, i: (bi, i, 0)),
                   pl.BlockSpec((None, dk, tm), lambda bi, i: (bi, 0, i))),
        compiler_params=_params(2),
        name="l1_qkv",
    )(x, g.reshape(1, d), wqt, bq, wk, bk, wvt, bv)


def _bias_kernel(onehot_ref, rb_ref, o_ref):
    o_ref[...] = jnp.dot(onehot_ref[...], rb_ref[...], preferred_element_type=F32,
                         precision=lax.Precision.HIGHEST)


def _t5_bucket(rel):
    nb = N_BUCKETS // 2
    max_exact = nb // 2
    ret = jnp.where(rel > 0, nb, 0)
    n = jnp.abs(rel)
    nf = jnp.maximum(n, 1).astype(jnp.float32)
    large = max_exact + (jnp.log(nf / max_exact) / math.log(MAX_DISTANCE / max_exact)
                         * (nb - max_exact)).astype(jnp.int32)
    large = jnp.minimum(large, nb - 1)
    return ret + jnp.where(n < max_exact, n, large)


def _rel_bias_table(rel_bias):
    n_heads = rel_bias.shape[1]
    pad = WINDOW_CHUNKS * CHUNK
    rel = (jnp.arange(KV_SPAN) - pad)[None, :] - jnp.arange(CHUNK)[:, None]
    bucket = _t5_bucket(rel).reshape(-1)
    onehot = (bucket[:, None] == jnp.arange(N_BUCKETS)[None, :]).astype(F32)
    rows = CHUNK * KV_SPAN
    rb = jnp.pad(rel_bias.astype(F32), ((0, 0), (0, LANES - n_heads)))
    out = pl.pallas_call(
        _bias_kernel,
        out_shape=jax.ShapeDtypeStruct((rows, LANES), F32),
        name="l1_rel_bias",
    )(onehot, rb)
    return jnp.transpose(out[:, :n_heads].reshape(CHUNK, KV_SPAN, n_heads), (2, 0, 1))


PAIR = 2 * CHUNK
PAIR_KEYS = KV_SPAN + CHUNK


def _attn_kernel(qt_ref, kp_ref, km_ref, vtp_ref, vtm_ref, bias_ref, sink_ref, ot_ref, kbuf, vtbuf,
                 s_a, s_b, *, n_heads):
    i = pl.program_id(1)
    tq = qt_ref.shape[1]
    pad = WINDOW_CHUNKS * CHUNK
    per_kv = n_heads // N_KV_HEADS
    kbuf[0:pad, :] = kp_ref[...]
    kbuf[pad:pad + tq, :] = km_ref[...]
    vtbuf[:, 0:pad] = vtp_ref[...]
    vtbuf[:, pad:pad + tq] = vtm_ref[...]
    zeros = jnp.zeros((HEAD_DIM, PAIR), BF16)
    n_pairs = tq // PAIR

    def scores(jj, s_scr):
        c0 = jj * PAIR
        kc = kbuf[c0:c0 + PAIR_KEYS, :]
        for h in range(n_heads):
            pieces = [zeros] * N_KV_HEADS
            pieces[h // per_kv] = qt_ref[h * HEAD_DIM:(h + 1) * HEAD_DIM, c0:c0 + PAIR]
            rhs = jnp.concatenate(pieces, axis=0)
            s_scr[h] = jnp.dot(kc, rhs, preferred_element_type=F32)

    def softmax_values(jj, s_scr):
        c0 = jj * PAIR
        variant = jnp.where(i == 0, 1, 0) if jj == 0 else 0
        for h in range(n_heads):
            kv = h // per_kv
            s = s_scr[h] + bias_ref[variant, h]
            sk = sink_ref[h]
            m_ = jnp.maximum(jnp.max(s, axis=0, keepdims=True), sk)
            p = jnp.exp(s - m_)
            denom = jnp.sum(p, axis=0, keepdims=True) + jnp.exp(sk - m_)
            p = (p / denom).astype(BF16)
            vt = vtbuf[kv * HEAD_DIM:(kv + 1) * HEAD_DIM, c0:c0 + PAIR_KEYS]
            o = jnp.dot(vt, p, preferred_element_type=F32)
            ot_ref[h * HEAD_DIM:(h + 1) * HEAD_DIM, c0:c0 + PAIR] = o.astype(BF16)

    bufs = (s_a, s_b)
    scores(0, bufs[0])
    for jj in range(n_pairs):
        if jj + 1 < n_pairs:
            scores(jj + 1, bufs[(jj + 1) % 2])
        softmax_values(jj, bufs[jj % 2])


def _pair_bias(bias):
    bt = jnp.transpose(bias, (0, 2, 1)).astype(F32)
    neg = jnp.full((bias.shape[0], CHUNK, CHUNK), NEG_INF, F32)
    first = jnp.concatenate([bt, neg], axis=1)
    second = jnp.concatenate([neg, bt], axis=1)
    table = jnp.concatenate([first, second], axis=2)
    key = jnp.arange(PAIR_KEYS)[None, :, None]
    masked = jnp.where(key < WINDOW_CHUNKS * CHUNK, NEG_INF, table)
    return jnp.stack([table, masked])


def _attention(qt, k, vt, bias, sinks, tq):
    b, dq, s = qt.shape
    dk = k.shape[-1]
    n_heads = dq // HEAD_DIM
    pad = WINDOW_CHUNKS * CHUNK
    ratio = tq // pad
    table = _pair_bias(bias)
    prev = lambda i: jnp.maximum(i * ratio - 1, 0)
    return pl.pallas_call(
        functools.partial(_attn_kernel, n_heads=n_heads),
        out_shape=jax.ShapeDtypeStruct((b, dq, s), BF16),
        grid=(b, s // tq),
        in_specs=[
            pl.BlockSpec((None, dq, tq), lambda bi, i: (bi, 0, i)),
            pl.BlockSpec((None, pad, dk), lambda bi, i: (bi, prev(i), 0)),
            pl.BlockSpec((None, tq, dk), lambda bi, i: (bi, i, 0)),
            pl.BlockSpec((None, dk, pad), lambda bi, i: (bi, 0, prev(i))),
            pl.BlockSpec((None, dk, tq), lambda bi, i: (bi, 0, i)),
            _const_spec(table.shape),
            pl.BlockSpec(memory_space=pltpu.SMEM),
        ],
        out_specs=pl.BlockSpec((None, dq, tq), lambda bi, i: (bi, 0, i)),
        scratch_shapes=[pltpu.VMEM((pad + tq, dk), BF16), pltpu.VMEM((dk, pad + tq), BF16),
                        pltpu.VMEM((n_heads, PAIR_KEYS, PAIR), F32),
                        pltpu.VMEM((n_heads, PAIR_KEYS, PAIR), F32)],
        compiler_params=_params(2),
        name="l1_attention",
    )(qt, k, k, vt, vt, table, sinks.astype(F32))


INFO_S0, INFO_S1, INFO_W0, INFO_W1 = range(4)


def _oproj_router_kernel(ot_ref, x_ref, wo_ref, bo_ref, g_ref, wr_ref, x3_ref, info_ref, cnt_ref,
                         *, n_experts):
    tm = x_ref.shape[0]
    attn = lax.dot_general(ot_ref[...], wo_ref[...], _TN, preferred_element_type=F32)
    x3 = x_ref[...] + attn + bo_ref[...]
    x3_ref[...] = x3
    h = _rms(x3, g_ref[...])
    h_hi = h.astype(BF16)
    h_lo = (h - h_hi.astype(F32)).astype(BF16)
    wr = wr_ref[...]
    hw = jnp.dot(h_hi, wr, preferred_element_type=F32)
    logits = (hw[:, :LANES] + hw[:, LANES:]) + jnp.dot(h_lo, wr[:, :LANES], preferred_element_type=F32)
    lane_i = lax.broadcasted_iota(jnp.int32, (tm, LANES), 1)
    lane = lane_i.astype(F32)
    neg = jnp.float32(-jnp.inf)
    logits = jnp.where(lane_i < n_experts, logits, neg)
    m0 = jnp.max(logits, axis=-1, keepdims=True)
    e0 = jnp.min(jnp.where(logits == m0, lane, float(LANES)), axis=-1, keepdims=True)
    rest = jnp.where(lane == e0, neg, logits)
    m1 = jnp.max(rest, axis=-1, keepdims=True)
    e1 = jnp.min(jnp.where(rest == m1, lane, float(LANES)), axis=-1, keepdims=True)
    t = jnp.exp(m1 - m0)
    w0 = 1.0 / (1.0 + t)
    w1 = t / (1.0 + t)
    hot0 = lane == e0
    hot1 = lane == e1
    both = jnp.where(hot0 | hot1, 1.0, 0.0).astype(BF16)
    row = lax.broadcasted_iota(jnp.int32, (tm, tm), 0)
    col = lax.broadcasted_iota(jnp.int32, (tm, tm), 1)
    tri = (col < row).astype(BF16)
    cnt = jnp.sum(both.astype(F32), axis=0, keepdims=True)
    cnt_ref[...] = cnt
    upto = jnp.broadcast_to(cnt, (SUBLANES, LANES))
    lane8 = lax.broadcasted_iota(jnp.int32, (SUBLANES, LANES), 1)
    shift = 1
    while shift < n_experts:
        upto = upto + jnp.where(lane8 >= shift, pltpu.roll(upto, shift, axis=1), 0.0)
        shift *= 2
    before = jnp.dot(tri, both, preferred_element_type=F32) + (upto[0:1, :] - cnt)
    r0 = jnp.sum(jnp.where(hot0, before, 0.0), axis=-1, keepdims=True)
    r1 = jnp.sum(jnp.where(hot1, before, 0.0), axis=-1, keepdims=True)
    info = jnp.zeros((tm, LANES), F32)
    for idx, val in ((INFO_S0, r0), (INFO_S1, r1), (INFO_W0, w0), (INFO_W1, w1)):
        info = jnp.where(lane_i == idx, val, info)
    info_ref[...] = info.T[0:SUBLANES, :]


def _oproj_router(ot, x, w_o, b_o, g, w_router, tm):
    n, d = x.shape
    dq, seq = ot.shape[1:]
    per_seq = seq // tm
    n_experts = w_router.shape[1]
    wr = jnp.pad(w_router.astype(F32), ((0, 0), (0, LANES - n_experts)))
    wr_hi = wr.astype(BF16)
    wr_lo = (wr - wr_hi.astype(F32)).astype(BF16)
    wr = jnp.concatenate([wr_hi, wr_lo], axis=1)
    return pl.pallas_call(
        functools.partial(_oproj_router_kernel, n_experts=n_experts),
        out_shape=(jax.ShapeDtypeStruct((n, d), F32),
                   jax.ShapeDtypeStruct((n // tm, SUBLANES, tm), F32),
                   jax.ShapeDtypeStruct((n // tm, 1, LANES), F32)),
        grid=(n // tm,),
        in_specs=[
            pl.BlockSpec((None, dq, tm), lambda i: (i // per_seq, 0, i % per_seq)),
            pl.BlockSpec((tm, d), lambda i: (i, 0)),
            _const_spec(w_o.shape),
            _const_spec((1, d)),
            _const_spec((1, d)),
            _const_spec(wr.shape),
        ],
        out_specs=(pl.BlockSpec((tm, d), lambda i: (i, 0)),
                   pl.BlockSpec((None, SUBLANES, tm), lambda i: (i, 0, 0)),
                   pl.BlockSpec((None, 1, LANES), lambda i: (i, 0, 0))),
        compiler_params=_params(1),
        name="l1_oproj_router",
    )(ot, x, w_o, b_o.reshape(1, d), g.reshape(1, d), wr)


def _pieces(n, n_bits, fn):
    for bit in reversed(range(n_bits)):
        size = 1 << bit
        done = n & ~((size << 1) - 1)
        pl.when((n & size) != 0)(functools.partial(fn, done, size))


def _rows(ref, first_row, n_rows):
    return ref.at[pl.ds(pl.multiple_of(first_row * SUBLANES, SUBLANES), n_rows * SUBLANES)]


def _segment_copies(tabs, i, n_experts, lbuf, hbm, sem, to_hbm, act):
    n_tab, o_tab, d_tab = tabs[:3]
    n_bits = (lbuf.shape[0] // SUBLANES // TOP_K).bit_length()
    for e in range(n_experts):
        local0 = o_tab[i * n_experts + e]
        sorted0 = d_tab[i * n_experts + e]

        def piece(done, size, local0=local0, sorted0=sorted0):
            local = _rows(lbuf, local0 + done, size)
            remote = _rows(hbm, sorted0 + done, size)
            act(pltpu.make_async_copy(local, remote, sem) if to_hbm
                else pltpu.make_async_copy(remote, local, sem))

        _pieces(n_tab[i * n_experts + e], n_bits, piece)


def _dispatch_kernel(n_tab, o_tab, d_tab, pad0_tab, padn_tab, nv_tab, slot_ref, x_ref, g_ref, xs_hbm,
                     hbuf, lbuf, sems, *, n_experts, tme):
    i = pl.program_id(0)
    tm = x_ref.shape[0]
    n_slots = TOP_K * tm
    h = _rms(x_ref[...], g_ref[...])
    for c in range(x_ref.shape[1] // LANES):
        hbuf[pl.ds(c, tm, stride=SUBLANES), :] = h[:, c * LANES:(c + 1) * LANES]
    slot = i % 2
    sorted_rows = lbuf.at[slot]

    def token(t, carry):
        row = hbuf[pl.ds(pl.multiple_of(t * SUBLANES, SUBLANES), SUBLANES), :]
        for k in range(TOP_K):
            dst = pl.multiple_of(slot_ref[k * tm + t] * SUBLANES, SUBLANES)
            sorted_rows[pl.ds(dst, SUBLANES), :] = row
        return carry

    lax.fori_loop(0, tm, token, 0, unroll=8)
    tabs = (n_tab, o_tab, d_tab)

    def copies(tile, s, act):
        _segment_copies(tabs, tile, n_experts, lbuf.at[s], xs_hbm, sems.at[s], True, act)

    copies(i, slot, lambda cp: cp.start())

    def wait_all(s):
        pltpu.make_async_copy(lbuf.at[s], _rows(xs_hbm, 0, n_slots), sems.at[s]).wait()

    @pl.when(i > 0)
    def _():
        wait_all(1 - slot)

    @pl.when(i == pl.num_programs(0) - 1)
    def _():
        wait_all(slot)
        zeros = lbuf.at[slot]
        zeros[0:tme * SUBLANES, :] = jnp.zeros((tme * SUBLANES, LANES), F32)

        def pads(act):
            for e in range(n_experts):
                def piece(done, size, e=e):
                    act(pltpu.make_async_copy(_rows(zeros, 0, size),
                                              _rows(xs_hbm, pad0_tab[e] + done, size), sems.at[slot]))
                _pieces(padn_tab[e], (tme - 1).bit_length(), piece)
            n_tiles = xs_hbm.shape[0] // (tme * SUBLANES)
            for t in range(n_experts):
                tile = nv_tab[0] + t
                pl.when(tile < n_tiles)(lambda tile=tile: act(pltpu.make_async_copy(
                    _rows(zeros, 0, tme), _rows(xs_hbm, tile * tme, tme), sems.at[slot])))

        pads(lambda cp: cp.start())
        pads(lambda cp: cp.wait())


def _dispatch(x, g, slots, tabs, n_experts, n_rows, tm, tme):
    n, d = x.shape
    assert TOP_K * tm >= tme
    grid_spec = pltpu.PrefetchScalarGridSpec(
        num_scalar_prefetch=len(tabs),
        grid=(n // tm,),
        in_specs=[
            pl.BlockSpec((TOP_K * tm,), lambda i, *_: (i,), memory_space=pltpu.SMEM),
            pl.BlockSpec((tm, d), lambda i, *_: (i, 0)),
            pl.BlockSpec((1, d), lambda i, *_: (0, 0)),
        ],
        out_specs=pl.BlockSpec(memory_space=pl.ANY),
        scratch_shapes=[pltpu.VMEM((tm * SUBLANES, LANES), F32),
                        pltpu.VMEM((2, TOP_K * tm * SUBLANES, LANES), F32),
                        pltpu.SemaphoreType.DMA((2,))],
    )
    return pl.pallas_call(
        functools.partial(_dispatch_kernel, n_experts=n_experts, tme=tme),
        out_shape=jax.ShapeDtypeStruct((n_rows * SUBLANES, LANES), F32),
        grid_spec=grid_spec,
        compiler_params=_params(1),
        name="l1_moe_dispatch",
    )(*tabs, slots, x, g.reshape(1, d))


def _experts_kernel(te_ref, tv_ref, nv_ref, xs_ref, wg_ref, wu_ref, wd_ref, ys_ref, t_scr, *, sub):
    del te_ref, nv_ref
    j = pl.program_id(0)
    tm = t_scr.shape[0]
    n_slabs = wg_ref.shape[0] // LANES

    @pl.when(tv_ref[j] == 1)
    def _():
        slabs = [xs_ref[pl.ds(s, tm, stride=SUBLANES), :] for s in range(n_slabs)]
        x = jnp.concatenate(slabs, axis=1).astype(BF16)

        def body(q, carry):
            off = pl.multiple_of(q * sub, sub)
            a = jnp.dot(x, wg_ref[:, pl.ds(off, sub)], preferred_element_type=F32)
            b = jnp.dot(x, wu_ref[:, pl.ds(off, sub)], preferred_element_type=F32)
            t_scr[:, pl.ds(off, sub)] = (a * jax.nn.sigmoid(a) * b).astype(BF16)
            return carry
        lax.fori_loop(0, wg_ref.shape[1] // sub, body, 0, unroll=True)
        y = jnp.dot(t_scr[...], wd_ref[...], preferred_element_type=F32)
        for s in range(n_slabs):
            ys_ref[pl.ds(s, tm, stride=SUBLANES), :] = y[:, s * LANES:(s + 1) * LANES]

    @pl.when(tv_ref[j] != 1)
    def _():
        ys_ref[...] = jnp.zeros_like(ys_ref)


def _experts(xs, tile_expert, tile_valid, n_valid, w_gate, w_up, w_down, d, tm, sub):
    n_tiles = tile_expert.shape[0]
    ff = w_gate.shape[2]
    once = pl.Buffered(1)
    grid_spec = pltpu.PrefetchScalarGridSpec(
        num_scalar_prefetch=3,
        grid=(n_tiles,),
        in_specs=[
            pl.BlockSpec((tm * SUBLANES, LANES), lambda j, te, tv, nv: (jnp.minimum(j, nv[0] - 1), 0)),
            pl.BlockSpec((None, d, ff), lambda j, te, tv, nv: (te[j], 0, 0), pipeline_mode=once),
            pl.BlockSpec((None, d, ff), lambda j, te, tv, nv: (te[j], 0, 0), pipeline_mode=once),
            pl.BlockSpec((None, ff, d), lambda j, te, tv, nv: (te[j], 0, 0)),
        ],
        out_specs=pl.BlockSpec((tm * SUBLANES, LANES), lambda j, te, tv, nv: (j, 0)),
        scratch_shapes=[pltpu.VMEM((tm, ff), BF16)],
    )
    return pl.pallas_call(
        functools.partial(_experts_kernel, sub=sub),
        out_shape=jax.ShapeDtypeStruct(xs.shape, F32),
        grid_spec=grid_spec,
        compiler_params=_params(1),
        name="l1_moe_experts",
    )(tile_expert, tile_valid, n_valid, xs, w_gate, w_up, w_down)


def _combine_kernel(n_tab, o_tab, d_tab, slot_ref, w_ref, x_ref, g_ref, ys_hbm, o_ref, lbuf, mbuf, sems,
                    *, n_experts):
    i = pl.program_id(0)
    tm = x_ref.shape[0]
    n_slots = TOP_K * tm
    tabs = (n_tab, o_tab, d_tab)
    slot = i % 2

    def copies(tile, s, act):
        _segment_copies(tabs, tile, n_experts, lbuf.at[s], ys_hbm, sems.at[s], False, act)

    @pl.when(i == 0)
    def _():
        copies(i, slot, lambda cp: cp.start())

    @pl.when(i + 1 < pl.num_programs(0))
    def _():
        copies(i + 1, 1 - slot, lambda cp: cp.start())

    pltpu.make_async_copy(_rows(ys_hbm, 0, n_slots), lbuf.at[slot], sems.at[slot]).wait()
    y = lbuf.at[slot]

    def token(t, carry):
        acc = None
        for k in range(TOP_K):
            row = y[pl.ds(pl.multiple_of(slot_ref[k * tm + t] * SUBLANES, SUBLANES), SUBLANES), :]
            term = w_ref[k * tm + t] * row
            acc = term if acc is None else acc + term
        mbuf[pl.ds(pl.multiple_of(t * SUBLANES, SUBLANES), SUBLANES), :] = acc
        return carry

    lax.fori_loop(0, tm, token, 0, unroll=8)
    moe = jnp.concatenate([mbuf[pl.ds(c, tm, stride=SUBLANES), :]
                           for c in range(x_ref.shape[1] // LANES)], axis=1)
    o_ref[...] = _rms(x_ref[...] + moe, g_ref[...])


def _combine(x, slots, weights, tabs, n_experts, ys, g, tm):
    n, d = x.shape
    grid_spec = pltpu.PrefetchScalarGridSpec(
        num_scalar_prefetch=3,
        grid=(n // tm,),
        in_specs=[
            pl.BlockSpec((TOP_K * tm,), lambda i, *_: (i,), memory_space=pltpu.SMEM),
            pl.BlockSpec((TOP_K * tm,), lambda i, *_: (i,), memory_space=pltpu.SMEM),
            pl.BlockSpec((tm, d), lambda i, *_: (i, 0)),
            pl.BlockSpec((1, d), lambda i, *_: (0, 0)),
            pl.BlockSpec(memory_space=pl.ANY),
        ],
        out_specs=pl.BlockSpec((tm, d), lambda i, *_: (i, 0)),
        scratch_shapes=[pltpu.VMEM((2, TOP_K * tm * SUBLANES, LANES), F32),
                        pltpu.VMEM((tm * SUBLANES, LANES), F32),
                        pltpu.SemaphoreType.DMA((2,))],
    )
    return pl.pallas_call(
        functools.partial(_combine_kernel, n_experts=n_experts),
        out_shape=jax.ShapeDtypeStruct((n, d), F32),
        grid_spec=grid_spec,
        compiler_params=_params(1),
        name="l1_moe_combine",
    )(*tabs, slots, weights, x, g.reshape(1, d), ys)


def _forward(x, rel_bias, ev_norm_mix, ev_w_in, ev_pool_w, ev_pool_scale, ev_conv_w, ev_w_out,
             ev_norm_ffn, ev_ffn_gate, ev_ffn_up, ev_ffn_down, od_norm_mix, od_w_qkv, od_b_qkv,
             od_sinks, od_w_o, od_b_o, od_norm_ffn, od_router, od_exp_gate, od_exp_up,
             od_exp_down, final_norm, *, tm, tq, tme, fc_dense):
    b, s, d = x.shape
    n = b * s
    n_heads = od_sinks.shape[-1]
    dq = n_heads * HEAD_DIM
    dk = N_KV_HEADS * HEAD_DIM
    n_experts = od_router.shape[-1]

    x = _mixer(x, ev_norm_mix[0], ev_w_in[0].astype(BF16), ev_pool_w[0].astype(BF16),
               ev_pool_scale[0], ev_conv_w[0], ev_w_out[0].astype(BF16), tm)
    x = x.reshape(n, d)
    ff = od_exp_gate.shape[-1]
    expert_f32 = (od_exp_gate[0].reshape(n_experts * d, ff), od_exp_up[0].reshape(n_experts * d, ff),
                  od_exp_down[0].reshape(n_experts * ff, d))
    x, (exp_gate, exp_up, exp_down) = _ffn(
        x, ev_norm_ffn[0], ev_ffn_gate[0].astype(BF16), ev_ffn_up[0].astype(BF16),
        ev_ffn_down[0].astype(BF16), expert_f32, tm, fc_dense)

    qt, k, vt = _qkv(x.reshape(b, s, d), od_norm_mix[0], od_w_qkv[0], od_b_qkv[0], dq, dk, tm)
    bias = _rel_bias_table(rel_bias)
    attn_t = _attention(qt, k, vt, bias, od_sinks[0], tq)
    x, info, counts = _oproj_router(attn_t, x, od_w_o[0].astype(BF16), od_b_o[0], od_norm_ffn[0],
                                    od_router[0], tm)

    per_tile = counts[:, 0, :n_experts].astype(jnp.int32)
    local_off = jnp.cumsum(per_tile, axis=1) - per_tile
    earlier = jnp.cumsum(per_tile, axis=0) - per_tile
    totals = jnp.sum(per_tile, axis=0)
    tiles_per = (totals + tme - 1) // tme
    tile_end = jnp.cumsum(tiles_per)
    group_start = (tile_end - tiles_per) * tme
    n_tiles = (TOP_K * n) // tme + n_experts
    tile_ids = jnp.arange(n_tiles, dtype=jnp.int32)
    tile_valid = (tile_ids < tile_end[-1]).astype(jnp.int32)
    last_valid = jnp.minimum(tile_ids, tile_end[-1] - 1)
    tile_expert = jnp.sum(last_valid[:, None] >= tile_end[None, :], axis=1).astype(jnp.int32)
    tabs = (per_tile.reshape(-1), local_off.reshape(-1), (group_start[None, :] + earlier).reshape(-1))
    pad_tabs = (group_start + totals, tiles_per * tme - totals, tile_end[-1:])

    slots = info[:, INFO_S0:INFO_S1 + 1, :].astype(jnp.int32).reshape(-1)
    weights = info[:, INFO_W0:INFO_W1 + 1, :].reshape(-1)

    xs = _dispatch(x, od_norm_ffn[0], slots, tabs + pad_tabs, n_experts, n_tiles * tme, tm, tme)
    ys = _experts(xs, tile_expert, tile_valid, tile_end[-1:], exp_gate.reshape(n_experts, d, ff),
                  exp_up.reshape(n_experts, d, ff), exp_down.reshape(n_experts, ff, d), d, tme, fc_dense)
    out = _combine(x, slots, weights, tabs, n_experts, ys, final_norm, tm)
    return out.reshape(b, s, d)


def kernel(x, rel_bias, ev_norm_mix, ev_w_in, ev_pool_w, ev_pool_scale, ev_conv_w, ev_w_out, ev_norm_ffn, ev_ffn_gate, ev_ffn_up, ev_ffn_down, od_norm_mix, od_w_qkv, od_b_qkv, od_sinks, od_w_o, od_b_o, od_norm_ffn, od_router, od_exp_gate, od_exp_up, od_exp_down, final_norm):
    return _forward(x, rel_bias, ev_norm_mix, ev_w_in, ev_pool_w, ev_pool_scale, ev_conv_w, ev_w_out,
                    ev_norm_ffn, ev_ffn_gate, ev_ffn_up, ev_ffn_down, od_norm_mix, od_w_qkv, od_b_qkv,
                    od_sinks, od_w_o, od_b_o, od_norm_ffn, od_router, od_exp_gate, od_exp_up,
                    od_exp_down, final_norm, tm=512, tq=512, tme=512, fc_dense=256)
```

```python
import functools
import math

import jax
import jax.numpy as jnp
from jax import lax
from jax.experimental import pallas as pl
from jax.experimental.pallas import tpu as pltpu

F32 = jnp.float32
BF16 = jnp.bfloat16

RMS_EPS = 1e-6
CHUNK = 64
POOL_WINDOWS = (2, 4, 8, 16)
CONV_K = 3
HEAD_DIM = 64
N_KV_HEADS = 2
WINDOW_CHUNKS = 2
KV_SPAN = (WINDOW_CHUNKS + 1) * CHUNK
N_BUCKETS = 32
MAX_DISTANCE = 128
TOP_K = 2
NEG_INF = -1e30

LANES = 128
SUBLANES = 8
VMEM_LIMIT = 56 * 1024 * 1024
EXPERT_VMEM_LIMIT = 62 * 1024 * 1024

POOL_HALO = 16
CONV_HALO = 8

def _rms(x, g):
    return x * lax.rsqrt(jnp.mean(x * x, axis=-1, keepdims=True) + RMS_EPS) * g


def _params(n_axes, vmem=VMEM_LIMIT):
    return pltpu.CompilerParams(dimension_semantics=("arbitrary",) * n_axes,
                                vmem_limit_bytes=vmem)


def _const_spec(shape):
    nd = len(shape)
    return pl.BlockSpec(shape, lambda *_: (0,) * nd, pipeline_mode=pl.Buffered(1))


def _mixer_kernel(x_ref, g_ref, win_ref, pw_ref, ps_ref, cw_ref, wout_ref, o_ref, ubuf, zbuf):
    s = pl.program_id(1)
    tm = x_ref.shape[0]
    pool_w = ubuf.shape[1]
    grp = pool_w // len(POOL_WINDOWS)
    conv_c = zbuf.shape[1]

    @pl.when(s == 0)
    def _():
        ubuf[0:POOL_HALO, :] = jnp.zeros((POOL_HALO, pool_w), F32)
        zbuf[0:CONV_HALO, :] = jnp.zeros((CONV_HALO, conv_c), F32)

    @pl.when(s > 0)
    def _():
        ubuf[0:POOL_HALO, :] = ubuf[tm:tm + POOL_HALO, :]
        zbuf[0:CONV_HALO, :] = zbuf[tm:tm + CONV_HALO, :]

    x = x_ref[...]
    h = _rms(x, g_ref[...]).astype(BF16)
    proj = jnp.dot(h, win_ref[...], preferred_element_type=F32)
    u = proj[:, :pool_w]
    gate_post = proj[:, pool_w:pool_w + conv_c]
    gate_pre = proj[:, pool_w + conv_c:pool_w + 2 * conv_c]
    v = proj[:, pool_w + 2 * conv_c:]
    ubuf[POOL_HALO:POOL_HALO + tm, :] = u
    z = gate_pre * v
    zbuf[CONV_HALO:CONV_HALO + tm, :] = z

    t = s * tm + lax.broadcasted_iota(jnp.int32, (tm, 1), 0)
    parts = []
    for g, w in enumerate(POOL_WINDOWS):
        lo, hi = g * grp, (g + 1) * grp
        ug = u[:, lo:hi]
        acc = ug
        for k in range(1, w):
            acc = acc + ubuf[POOL_HALO - k:POOL_HALO - k + tm, lo:hi]
        count = jnp.minimum(t + 1, w).astype(F32)
        pooled = acc / count - ug
        mixed = jnp.dot(pooled.astype(BF16), pw_ref[g], preferred_element_type=F32)
        parts.append(mixed * ps_ref[:, lo:hi])
    conv = cw_ref[CONV_K - 1:CONV_K, :] * z
    for k in range(CONV_K - 1):
        off = CONV_HALO - (CONV_K - 1) + k
        conv = conv + cw_ref[k:k + 1, :] * zbuf[off:off + tm, :]
    parts.append(gate_post * conv)
    y = jnp.concatenate(parts, axis=1).astype(BF16)
    o_ref[...] = x + jnp.dot(y, wout_ref[...], preferred_element_type=F32)


def _mixer(x, g, w_in, pool_w, pool_scale, conv_w, w_out, tm):
    b, s, d = x.shape
    pool_width = pool_scale.shape[-1]
    conv_c = conv_w.shape[-1]
    return pl.pallas_call(
        _mixer_kernel,
        out_shape=jax.ShapeDtypeStruct((b, s, d), F32),
        grid=(b, s // tm),
        in_specs=[
            pl.BlockSpec((None, tm, d), lambda i, j: (i, j, 0)),
            _const_spec((1, d)),
            _const_spec(w_in.shape),
            _const_spec(pool_w.shape),
            _const_spec((1, pool_width)),
            _const_spec(conv_w.shape),
            _const_spec(w_out.shape),
        ],
        out_specs=pl.BlockSpec((None, tm, d), lambda i, j: (i, j, 0)),
        scratch_shapes=[pltpu.VMEM((POOL_HALO + tm, pool_width), F32),
                        pltpu.VMEM((CONV_HALO + tm, conv_c), F32)],
        compiler_params=_params(2),
        name="l0_mixer",
    )(x, g.reshape(1, d), w_in, pool_w, pool_scale.reshape(1, pool_width), conv_w, w_out)


def _ffn_kernel(x_ref, g_ref, wg_ref, wu_ref, wd_ref, *rest, fc, n_cast):
    cast_in, (o_ref, *cast_out), (h_scr, t_scr) = rest[:n_cast], rest[n_cast:2 * n_cast + 1], rest[2 * n_cast + 1:]
    x = x_ref[...]
    h_scr[...] = _rms(x, g_ref[...]).astype(BF16)
    n_chunks = wg_ref.shape[1] // fc

    def body(c, carry):
        off = pl.multiple_of(c * fc, fc)
        h = h_scr[...]
        a = jnp.dot(h, wg_ref[:, pl.ds(off, fc)], preferred_element_type=F32)
        b = jnp.dot(h, wu_ref[:, pl.ds(off, fc)], preferred_element_type=F32)
        t_scr[:, pl.ds(off, fc)] = (a * jax.nn.sigmoid(a) * b).astype(BF16)
        return carry

    lax.fori_loop(0, n_chunks, body, 0, unroll=True)
    o_ref[...] = x + jnp.dot(t_scr[...], wd_ref[...], preferred_element_type=F32)
    for src, dst in zip(cast_in, cast_out):
        dst[...] = src[...].astype(BF16)


def _ffn(x, g, w_gate, w_up, w_down, to_cast, tm, fc):
    n, d = x.shape
    steps = n // tm
    slabs = []
    for w in to_cast:
        rows = w.shape[0] // steps
        assert rows * steps == w.shape[0] and rows % (2 * SUBLANES) == 0, w.shape
        slabs.append(pl.BlockSpec((rows, w.shape[1]), lambda i: (i, 0)))
    outs = pl.pallas_call(
        functools.partial(_ffn_kernel, fc=fc, n_cast=len(to_cast)),
        out_shape=(jax.ShapeDtypeStruct((n, d), F32),
                   *[jax.ShapeDtypeStruct(w.shape, BF16) for w in to_cast]),
        grid=(steps,),
        in_specs=[
            pl.BlockSpec((tm, d), lambda i: (i, 0)),
            _const_spec((1, d)),
            _const_spec(w_gate.shape),
            _const_spec(w_up.shape),
            _const_spec(w_down.shape),
            *slabs,
        ],
        out_specs=(pl.BlockSpec((tm, d), lambda i: (i, 0)), *slabs),
        scratch_shapes=[pltpu.VMEM((tm, d), BF16), pltpu.VMEM((tm, w_gate.shape[1]), BF16)],
        compiler_params=_params(1),
        name="l0_ffn",
    )(x, g.reshape(1, d), w_gate, w_up, w_down, *to_cast)
    return outs[0], outs[1:]


_NT = (((1,), (1,)), ((), ()))
_TN = (((0,), (0,)), ((), ()))


def _qkv_kernel(x_ref, g_ref, wqt_ref, bq_ref, wk_ref, bk_ref, wvt_ref, bv_ref, qt_ref, k_ref, vt_ref):
    h = _rms(x_ref[...], g_ref[...]).astype(BF16)
    qt = lax.dot_general(wqt_ref[...], h, _NT, preferred_element_type=F32) + bq_ref[...]
    qt_ref[...] = (qt * (HEAD_DIM ** -0.5)).astype(BF16)
    k_ref[...] = (jnp.dot(h, wk_ref[...], preferred_element_type=F32) + bk_ref[...]).astype(BF16)
    vt = lax.dot_general(wvt_ref[...], h, _NT, preferred_element_type=F32) + bv_ref[...]
    vt_ref[...] = vt.astype(BF16)


def _qkv(x, g, w_qkv, b_qkv, dq, dk, tm):
    b, s, d = x.shape
    wqt = w_qkv[:, :dq].T.astype(BF16)
    wk = w_qkv[:, dq:dq + dk].astype(BF16)
    wvt = w_qkv[:, dq + dk:].T.astype(BF16)
    bq = b_qkv[:dq].reshape(dq, 1)
    bk = b_qkv[dq:dq + dk].reshape(1, dk)
    bv = b_qkv[dq + dk:].reshape(dk, 1)
    return pl.pallas_call(
        _qkv_kernel,
        out_shape=(jax.ShapeDtypeStruct((b, dq, s), BF16),
                   jax.ShapeDtypeStruct((b, s, dk), BF16),
                   jax.ShapeDtypeStruct((b, dk, s), BF16)),
        grid=(b, s // tm),
        in_specs=[
            pl.BlockSpec((None, tm, d), lambda bi, i: (bi, i, 0)),
            _const_spec((1, d)),
            _const_spec(wqt.shape), _const_spec(bq.shape),
            _const_spec(wk.shape), _const_spec(bk.shape),
            _const_spec(wvt.shape), _const_spec(bv.shape),
        ],
        out_specs=(pl.BlockSpec((None, dq, tm), lambda bi, i: (bi, 0, i)),
                   pl.BlockSpec((None, tm, dk), lambda bi, i: (bi, i, 0)),
                   pl.BlockSpec((None, dk, tm), lambda bi, i: (bi, 0, i))),
        compiler_params=_params(2),
        name="l1_qkv",
    )(x, g.reshape(1, d), wqt, bq, wk, bk, wvt, bv)


def _bias_kernel(onehot_ref, rb_ref, o_ref):
    o_ref[...] = jnp.dot(onehot_ref[...], rb_ref[...], preferred_element_type=F32,
                         precision=lax.Precision.HIGHEST)


def _t5_bucket(rel):
    nb = N_BUCKETS // 2
    max_exact = nb // 2
    ret = jnp.where(rel > 0, nb, 0)
    n = jnp.abs(rel)
    nf = jnp.maximum(n, 1).astype(jnp.float32)
    large = max_exact + (jnp.log(nf / max_exact) / math.log(MAX_DISTANCE / max_exact)
                         * (nb - max_exact)).astype(jnp.int32)
    large = jnp.minimum(large, nb - 1)
    return ret + jnp.where(n < max_exact, n, large)


def _rel_bias_table(rel_bias):
    n_heads = rel_bias.shape[1]
    pad = WINDOW_CHUNKS * CHUNK
    rel = (jnp.arange(KV_SPAN) - pad)[None, :] - jnp.arange(CHUNK)[:, None]
    bucket = _t5_bucket(rel).reshape(-1)
    onehot = (bucket[:, None] == jnp.arange(N_BUCKETS)[None, :]).astype(F32)
    rows = CHUNK * KV_SPAN
    rb = jnp.pad(rel_bias.astype(F32), ((0, 0), (0, LANES - n_heads)))
    out = pl.pallas_call(
        _bias_kernel,
        out_shape=jax.ShapeDtypeStruct((rows, LANES), F32),
        name="l1_rel_bias",
    )(onehot, rb)
    return jnp.transpose(out[:, :n_heads].reshape(CHUNK, KV_SPAN, n_heads), (2, 0, 1))


PAIR = 2 * CHUNK
PAIR_KEYS = KV_SPAN + CHUNK


def _attn_kernel(qt_ref, kp_ref, km_ref, vtp_ref, vtm_ref, bias_ref, sink_ref, ot_ref, kbuf, vtbuf,
                 s_a, s_b, *, n_heads):
    i = pl.program_id(1)
    tq = qt_ref.shape[1]
    pad = WINDOW_CHUNKS * CHUNK
    per_kv = n_heads // N_KV_HEADS
    kbuf[0:pad, :] = kp_ref[...]
    kbuf[pad:pad + tq, :] = km_ref[...]
    vtbuf[:, 0:pad] = vtp_ref[...]
    vtbuf[:, pad:pad + tq] = vtm_ref[...]
    zeros = jnp.zeros((HEAD_DIM, PAIR), BF16)
    n_pairs = tq // PAIR

    def scores(jj, s_scr):
        c0 = jj * PAIR
        kc = kbuf[c0:c0 + PAIR_KEYS, :]
        for h in range(n_heads):
            pieces = [zeros] * N_KV_HEADS
            pieces[h // per_kv] = qt_ref[h * HEAD_DIM:(h + 1) * HEAD_DIM, c0:c0 + PAIR]
            rhs = jnp.concatenate(pieces, axis=0)
            s_scr[h] = jnp.dot(kc, rhs, preferred_element_type=F32)

    def softmax_values(jj, s_scr):
        c0 = jj * PAIR
        variant = jnp.where(i == 0, 1, 0) if jj == 0 else 0
        for h in range(n_heads):
            kv = h // per_kv
            s = s_scr[h] + bias_ref[variant, h]
            sk = sink_ref[h]
            m_ = jnp.maximum(jnp.max(s, axis=0, keepdims=True), sk)
            p = jnp.exp(s - m_)
            denom = jnp.sum(p, axis=0, keepdims=True) + jnp.exp(sk - m_)
            p = (p / denom).astype(BF16)
            vt = vtbuf[kv * HEAD_DIM:(kv + 1) * HEAD_DIM, c0:c0 + PAIR_KEYS]
            o = jnp.dot(vt, p, preferred_element_type=F32)
            ot_ref[h * HEAD_DIM:(h + 1) * HEAD_DIM, c0:c0 + PAIR] = o.astype(BF16)

    bufs = (s_a, s_b)
    scores(0, bufs[0])
    for jj in range(n_pairs):
        if jj + 1 < n_pairs:
            scores(jj + 1, bufs[(jj + 1) % 2])
        softmax_values(jj, bufs[jj % 2])


def _pair_bias(bias):
    bt = jnp.transpose(bias, (0, 2, 1)).astype(F32)
    neg = jnp.full((bias.shape[0], CHUNK, CHUNK), NEG_INF, F32)
    first = jnp.concatenate([bt, neg], axis=1)
    second = jnp.concatenate([neg, bt], axis=1)
    table = jnp.concatenate([first, second], axis=2)
    key = jnp.arange(PAIR_KEYS)[None, :, None]
    masked = jnp.where(key < WINDOW_CHUNKS * CHUNK, NEG_INF, table)
    return jnp.stack([table, masked])


def _attention(qt, k, vt, bias, sinks, tq):
    b, dq, s = qt.shape
    dk = k.shape[-1]
    n_heads = dq // HEAD_DIM
    pad = WINDOW_CHUNKS * CHUNK
    ratio = tq // pad
    table = _pair_bias(bias)
    prev = lambda i: jnp.maximum(i * ratio - 1, 0)
    return pl.pallas_call(
        functools.partial(_attn_kernel, n_heads=n_heads),
        out_shape=jax.ShapeDtypeStruct((b, dq, s), BF16),
        grid=(b, s // tq),
        in_specs=[
            pl.BlockSpec((None, dq, tq), lambda bi, i: (bi, 0, i)),
            pl.BlockSpec((None, pad, dk), lambda bi, i: (bi, prev(i), 0)),
            pl.BlockSpec((None, tq, dk), lambda bi, i: (bi, i, 0)),
            pl.BlockSpec((None, dk, pad), lambda bi, i: (bi, 0, prev(i))),
            pl.BlockSpec((None, dk, tq), lambda bi, i: (bi, 0, i)),
            _const_spec(table.shape),
            pl.BlockSpec(memory_space=pltpu.SMEM),
        ],
        out_specs=pl.BlockSpec((None, dq, tq), lambda bi, i: (bi, 0, i)),
        scratch_shapes=[pltpu.VMEM((pad + tq, dk), BF16), pltpu.VMEM((dk, pad + tq), BF16),
                        pltpu.VMEM((n_heads, PAIR_KEYS, PAIR), F32),
                        pltpu.VMEM((n_heads, PAIR_KEYS, PAIR), F32)],
        compiler_params=_params(2),
        name="l1_attention",
    )(qt, k, k, vt, vt, table, sinks.astype(F32))


INFO_S0, INFO_S1, INFO_W0, INFO_W1 = range(4)


def _oproj_router_kernel(ot_ref, x_ref, wo_ref, bo_ref, g_ref, wr_ref, x3_ref, info_ref, cnt_ref,
                         *, n_experts):
    tm = x_ref.shape[0]
    attn = lax.dot_general(ot_ref[...], wo_ref[...], _TN, preferred_element_type=F32)
    x3 = x_ref[...] + attn + bo_ref[...]
    x3_ref[...] = x3
    h = _rms(x3, g_ref[...])
    h_hi = h.astype(BF16)
    h_lo = (h - h_hi.astype(F32)).astype(BF16)
    wr = wr_ref[...]
    hw = jnp.dot(h_hi, wr, preferred_element_type=F32)
    logits = (hw[:, :LANES] + hw[:, LANES:]) + jnp.dot(h_lo, wr[:, :LANES], preferred_element_type=F32)
    lane_i = lax.broadcasted_iota(jnp.int32, (tm, LANES), 1)
    lane = lane_i.astype(F32)
    neg = jnp.float32(-jnp.inf)
    logits = jnp.where(lane_i < n_experts, logits, neg)
    m0 = jnp.max(logits, axis=-1, keepdims=True)
    e0 = jnp.min(jnp.where(logits == m0, lane, float(LANES)), axis=-1, keepdims=True)
    rest = jnp.where(lane == e0, neg, logits)
    m1 = jnp.max(rest, axis=-1, keepdims=True)
    e1 = jnp.min(jnp.where(rest == m1, lane, float(LANES)), axis=-1, keepdims=True)
    t = jnp.exp(m1 - m0)
    w0 = 1.0 / (1.0 + t)
    w1 = t / (1.0 + t)
    hot0 = lane == e0
    hot1 = lane == e1
    both = jnp.where(hot0 | hot1, 1.0, 0.0).astype(BF16)
    row = lax.broadcasted_iota(jnp.int32, (tm, tm), 0)
    col = lax.broadcasted_iota(jnp.int32, (tm, tm), 1)
    tri = (col < row).astype(BF16)
    cnt = jnp.sum(both.astype(F32), axis=0, keepdims=True)
    cnt_ref[...] = cnt
    upto = jnp.broadcast_to(cnt, (SUBLANES, LANES))
    lane8 = lax.broadcasted_iota(jnp.int32, (SUBLANES, LANES), 1)
    shift = 1
    while shift < n_experts:
        upto = upto + jnp.where(lane8 >= shift, pltpu.roll(upto, shift, axis=1), 0.0)
        shift *= 2
    before = jnp.dot(tri, both, preferred_element_type=F32) + (upto[0:1, :] - cnt)
    r0 = jnp.sum(jnp.where(hot0, before, 0.0), axis=-1, keepdims=True)
    r1 = jnp.sum(jnp.where(hot1, before, 0.0), axis=-1, keepdims=True)
    info = jnp.zeros((tm, LANES), F32)
    for idx, val in ((INFO_S0, r0), (INFO_S1, r1), (INFO_W0, w0), (INFO_W1, w1)):
        info = jnp.where(lane_i == idx, val, info)
    info_ref[...] = info.T[0:SUBLANES, :]


def _oproj_router(ot, x, w_o, b_o, g, w_router, tm):
    n, d = x.shape
    dq, seq = ot.shape[1:]
    per_seq = seq // tm
    n_experts = w_router.shape[1]
    wr = jnp.pad(w_router.astype(F32), ((0, 0), (0, LANES - n_experts)))
    wr_hi = wr.astype(BF16)
    wr_lo = (wr - wr_hi.astype(F32)).astype(BF16)
    wr = jnp.concatenate([wr_hi, wr_lo], axis=1)
    return pl.pallas_call(
        functools.partial(_oproj_router_kernel, n_experts=n_experts),
        out_shape=(jax.ShapeDtypeStruct((n, d), F32),
                   jax.ShapeDtypeStruct((n // tm, SUBLANES, tm), F32),
                   jax.ShapeDtypeStruct((n // tm, 1, LANES), F32)),
        grid=(n // tm,),
        in_specs=[
            pl.BlockSpec((None, dq, tm), lambda i: (i // per_seq, 0, i % per_seq)),
            pl.BlockSpec((tm, d), lambda i: (i, 0)),
            _const_spec(w_o.shape),
            _const_spec((1, d)),
            _const_spec((1, d)),
            _const_spec(wr.shape),
        ],
        out_specs=(pl.BlockSpec((tm, d), lambda i: (i, 0)),
                   pl.BlockSpec((None, SUBLANES, tm), lambda i: (i, 0, 0)),
                   pl.BlockSpec((None, 1, LANES), lambda i: (i, 0, 0))),
        compiler_params=_params(1),
        name="l1_oproj_router",
    )(ot, x, w_o, b_o.reshape(1, d), g.reshape(1, d), wr)


def _pieces(n, n_bits, fn):
    for bit in reversed(range(n_bits)):
        size = 1 << bit
        done = n & ~((size << 1) - 1)
        pl.when((n & size) != 0)(functools.partial(fn, done, size))


def _rows(ref, first_row, n_rows):
    return ref.at[pl.ds(pl.multiple_of(first_row * SUBLANES, SUBLANES), n_rows * SUBLANES)]


def _segment_copies(tabs, i, n_experts, lbuf, hbm, sem, to_hbm, act):
    n_tab, o_tab, d_tab = tabs[:3]
    n_bits = (lbuf.shape[0] // SUBLANES // TOP_K).bit_length()
    for e in range(n_experts):
        local0 = o_tab[i * n_experts + e]
        sorted0 = d_tab[i * n_experts + e]

        def piece(done, size, local0=local0, sorted0=sorted0):
            local = _rows(lbuf, local0 + done, size)
            remote = _rows(hbm, sorted0 + done, size)
            act(pltpu.make_async_copy(local, remote, sem) if to_hbm
                else pltpu.make_async_copy(remote, local, sem))

        _pieces(n_tab[i * n_experts + e], n_bits, piece)


def _dispatch_kernel(n_tab, o_tab, d_tab, pad0_tab, padn_tab, nv_tab, slot_ref, x_ref, g_ref, xs_hbm,
                     hbuf, lbuf, sems, *, n_experts, tme):
    i = pl.program_id(0)
    tm = x_ref.shape[0]
    n_slots = TOP_K * tm
    h = _rms(x_ref[...], g_ref[...])
    for c in range(x_ref.shape[1] // LANES):
        hbuf[pl.ds(c, tm, stride=SUBLANES), :] = h[:, c * LANES:(c + 1) * LANES]
    slot = i % 2
    sorted_rows = lbuf.at[slot]

    def token(t, carry):
        row = hbuf[pl.ds(pl.multiple_of(t * SUBLANES, SUBLANES), SUBLANES), :]
        for k in range(TOP_K):
            dst = pl.multiple_of(slot_ref[k * tm + t] * SUBLANES, SUBLANES)
            sorted_rows[pl.ds(dst, SUBLANES), :] = row
        return carry

    lax.fori_loop(0, tm, token, 0, unroll=8)
    tabs = (n_tab, o_tab, d_tab)

    def copies(tile, s, act):
        _segment_copies(tabs, tile, n_experts, lbuf.at[s], xs_hbm, sems.at[s], True, act)

    copies(i, slot, lambda cp: cp.start())

    def wait_all(s):
        pltpu.make_async_copy(lbuf.at[s], _rows(xs_hbm, 0, n_slots), sems.at[s]).wait()

    @pl.when(i > 0)
    def _():
        wait_all(1 - slot)

    @pl.when(i == pl.num_programs(0) - 1)
    def _():
        wait_all(slot)
        zeros = lbuf.at[slot]
        zeros[0:tme * SUBLANES, :] = jnp.zeros((tme * SUBLANES, LANES), F32)

        def pads(act):
            for e in range(n_experts):
                def piece(done, size, e=e):
                    act(pltpu.make_async_copy(_rows(zeros, 0, size),
                                              _rows(xs_hbm, pad0_tab[e] + done, size), sems.at[slot]))
                _pieces(padn_tab[e], (tme - 1).bit_length(), piece)
            n_tiles = xs_hbm.shape[0] // (tme * SUBLANES)
            for t in range(n_experts):
                tile = nv_tab[0] + t
                pl.when(tile < n_tiles)(lambda tile=tile: act(pltpu.make_async_copy(
                    _rows(zeros, 0, tme), _rows(xs_hbm, tile * tme, tme), sems.at[slot])))

        pads(lambda cp: cp.start())
        pads(lambda cp: cp.wait())


def _dispatch(x, g, slots, tabs, n_experts, n_rows, tm, tme):
    n, d = x.shape
    assert TOP_K * tm >= tme
    grid_spec = pltpu.PrefetchScalarGridSpec(
        num_scalar_prefetch=len(tabs),
        grid=(n // tm,),
        in_specs=[
            pl.BlockSpec((TOP_K * tm,), lambda i, *_: (i,), memory_space=pltpu.SMEM),
            pl.BlockSpec((tm, d), lambda i, *_: (i, 0)),
            pl.BlockSpec((1, d), lambda i, *_: (0, 0)),
        ],
        out_specs=pl.BlockSpec(memory_space=pl.ANY),
        scratch_shapes=[pltpu.VMEM((tm * SUBLANES, LANES), F32),
                        pltpu.VMEM((2, TOP_K * tm * SUBLANES, LANES), F32),
                        pltpu.SemaphoreType.DMA((2,))],
    )
    return pl.pallas_call(
        functools.partial(_dispatch_kernel, n_experts=n_experts, tme=tme),
        out_shape=jax.ShapeDtypeStruct((n_rows * SUBLANES, LANES), F32),
        grid_spec=grid_spec,
        compiler_params=_params(1),
        name="l1_moe_dispatch",
    )(*tabs, slots, x, g.reshape(1, d))


def _experts_kernel(te_ref, tv_ref, nv_ref, xs_ref, wg_ref, wu_ref, wd_ref, ys_ref, t_scr, *, sub):
    del te_ref, nv_ref
    j = pl.program_id(0)
    tm = t_scr.shape[0]
    n_slabs = wg_ref.shape[0] // LANES

    @pl.when(tv_ref[j] == 1)
    def _():
        slabs = [xs_ref[pl.ds(s, tm, stride=SUBLANES), :] for s in range(n_slabs)]
        x = jnp.concatenate(slabs, axis=1).astype(BF16)

        def body(q, carry):
            off = pl.multiple_of(q * sub, sub)
            a = jnp.dot(x, wg_ref[:, pl.ds(off, sub)], preferred_element_type=F32)
            b = jnp.dot(x, wu_ref[:, pl.ds(off, sub)], preferred_element_type=F32)
            t_scr[:, pl.ds(off, sub)] = (a * jax.nn.sigmoid(a) * b).astype(BF16)
            return carry
        lax.fori_loop(0, wg_ref.shape[1] // sub, body, 0, unroll=True)
        y = jnp.dot(t_scr[...], wd_ref[...], preferred_element_type=F32)
        for s in range(n_slabs):
            ys_ref[pl.ds(s, tm, stride=SUBLANES), :] = y[:, s * LANES:(s + 1) * LANES]

    @pl.when(tv_ref[j] != 1)
    def _():
        ys_ref[...] = jnp.zeros_like(ys_ref)


def _experts(xs, tile_expert, tile_valid, n_valid, w_gate, w_up, w_down, d, tm, sub):
    n_tiles = tile_expert.shape[0]
    ff = w_gate.shape[2]
    grid_spec = pltpu.PrefetchScalarGridSpec(
        num_scalar_prefetch=3,
        grid=(n_tiles,),
        in_specs=[
            pl.BlockSpec((tm * SUBLANES, LANES), lambda j, te, tv, nv: (jnp.minimum(j, nv[0] - 1), 0)),
            pl.BlockSpec((None, d, ff), lambda j, te, tv, nv: (te[j], 0, 0)),
            pl.BlockSpec((None, d, ff), lambda j, te, tv, nv: (te[j], 0, 0)),
            pl.BlockSpec((None, ff, d), lambda j, te, tv, nv: (te[j], 0, 0)),
        ],
        out_specs=pl.BlockSpec((tm * SUBLANES, LANES), lambda j, te, tv, nv: (j, 0)),
        scratch_shapes=[pltpu.VMEM((tm, ff), BF16)],
    )
    return pl.pallas_call(
        functools.partial(_experts_kernel, sub=sub),
        out_shape=jax.ShapeDtypeStruct(xs.shape, F32),
        grid_spec=grid_spec,
        compiler_params=_params(1, vmem=EXPERT_VMEM_LIMIT),
        name="l1_moe_experts",
    )(tile_expert, tile_valid, n_valid, xs, w_gate, w_up, w_down)


def _combine_kernel(n_tab, o_tab, d_tab, slot_ref, w_ref, x_ref, g_ref, ys_hbm, o_ref, lbuf, mbuf, sems,
                    *, n_experts):
    i = pl.program_id(0)
    tm = x_ref.shape[0]
    n_slots = TOP_K * tm
    tabs = (n_tab, o_tab, d_tab)
    slot = i % 2

    def copies(tile, s, act):
        _segment_copies(tabs, tile, n_experts, lbuf.at[s], ys_hbm, sems.at[s], False, act)

    @pl.when(i == 0)
    def _():
        copies(i, slot, lambda cp: cp.start())

    @pl.when(i + 1 < pl.num_programs(0))
    def _():
        copies(i + 1, 1 - slot, lambda cp: cp.start())

    pltpu.make_async_copy(_rows(ys_hbm, 0, n_slots), lbuf.at[slot], sems.at[slot]).wait()
    y = lbuf.at[slot]

    def token(t, carry):
        acc = None
        for k in range(TOP_K):
            row = y[pl.ds(pl.multiple_of(slot_ref[k * tm + t] * SUBLANES, SUBLANES), SUBLANES), :]
            term = w_ref[k * tm + t] * row
            acc = term if acc is None else acc + term
        mbuf[pl.ds(pl.multiple_of(t * SUBLANES, SUBLANES), SUBLANES), :] = acc
        return carry

    lax.fori_loop(0, tm, token, 0, unroll=8)
    moe = jnp.concatenate([mbuf[pl.ds(c, tm, stride=SUBLANES), :]
                           for c in range(x_ref.shape[1] // LANES)], axis=1)
    o_ref[...] = _rms(x_ref[...] + moe, g_ref[...])


def _combine(x, slots, weights, tabs, n_experts, ys, g, tm):
    n, d = x.shape
    grid_spec = pltpu.PrefetchScalarGridSpec(
        num_scalar_prefetch=3,
        grid=(n // tm,),
        in_specs=[
            pl.BlockSpec((TOP_K * tm,), lambda i, *_: (i,), memory_space=pltpu.SMEM),
            pl.BlockSpec((TOP_K * tm,), lambda i, *_: (i,), memory_space=pltpu.SMEM),
            pl.BlockSpec((tm, d), lambda i, *_: (i, 0)),
            pl.BlockSpec((1, d), lambda i, *_: (0, 0)),
            pl.BlockSpec(memory_space=pl.ANY),
        ],
        out_specs=pl.BlockSpec((tm, d), lambda i, *_: (i, 0)),
        scratch_shapes=[pltpu.VMEM((2, TOP_K * tm * SUBLANES, LANES), F32),
                        pltpu.VMEM((tm * SUBLANES, LANES), F32),
                        pltpu.SemaphoreType.DMA((2,))],
    )
    return pl.pallas_call(
        functools.partial(_combine_kernel, n_experts=n_experts),
        out_shape=jax.ShapeDtypeStruct((n, d), F32),
        grid_spec=grid_spec,
        compiler_params=_params(1),
        name="l1_moe_combine",
    )(*tabs, slots, weights, x, g.reshape(1, d), ys)


def _forward(x, rel_bias, ev_norm_mix, ev_w_in, ev_pool_w, ev_pool_scale, ev_conv_w, ev_w_out,
             ev_norm_ffn, ev_ffn_gate, ev_ffn_up, ev_ffn_down, od_norm_mix, od_w_qkv, od_b_qkv,
             od_sinks, od_w_o, od_b_o, od_norm_ffn, od_router, od_exp_gate, od_exp_up,
             od_exp_down, final_norm, *, tm, tq, tme, fc_dense):
    b, s, d = x.shape
    n = b * s
    n_heads = od_sinks.shape[-1]
    dq = n_heads * HEAD_DIM
    dk = N_KV_HEADS * HEAD_DIM
    n_experts = od_router.shape[-1]

    x = _mixer(x, ev_norm_mix[0], ev_w_in[0].astype(BF16), ev_pool_w[0].astype(BF16),
               ev_pool_scale[0], ev_conv_w[0], ev_w_out[0].astype(BF16), tm)
    x = x.reshape(n, d)
    ff = od_exp_gate.shape[-1]
    expert_f32 = (od_exp_gate[0].reshape(n_experts * d, ff), od_exp_up[0].reshape(n_experts * d, ff),
                  od_exp_down[0].reshape(n_experts * ff, d))
    x, (exp_gate, exp_up, exp_down) = _ffn(
        x, ev_norm_ffn[0], ev_ffn_gate[0].astype(BF16), ev_ffn_up[0].astype(BF16),
        ev_ffn_down[0].astype(BF16), expert_f32, tm, fc_dense)

    qt, k, vt = _qkv(x.reshape(b, s, d), od_norm_mix[0], od_w_qkv[0], od_b_qkv[0], dq, dk, tm)
    bias = _rel_bias_table(rel_bias)
    attn_t = _attention(qt, k, vt, bias, od_sinks[0], tq)
    x, info, counts = _oproj_router(attn_t, x, od_w_o[0].astype(BF16), od_b_o[0], od_norm_ffn[0],
                                    od_router[0], tm)

    per_tile = counts[:, 0, :n_experts].astype(jnp.int32)
    local_off = jnp.cumsum(per_tile, axis=1) - per_tile
    earlier = jnp.cumsum(per_tile, axis=0) - per_tile
    totals = jnp.sum(per_tile, axis=0)
    tiles_per = (totals + tme - 1) // tme
    tile_end = jnp.cumsum(tiles_per)
    group_start = (tile_end - tiles_per) * tme
    n_tiles = (TOP_K * n) // tme + n_experts
    tile_ids = jnp.arange(n_tiles, dtype=jnp.int32)
    tile_valid = (tile_ids < tile_end[-1]).astype(jnp.int32)
    last_valid = jnp.minimum(tile_ids, tile_end[-1] - 1)
    tile_expert = jnp.sum(last_valid[:, None] >= tile_end[None, :], axis=1).astype(jnp.int32)
    tabs = (per_tile.reshape(-1), local_off.reshape(-1), (group_start[None, :] + earlier).reshape(-1))
    pad_tabs = (group_start + totals, tiles_per * tme - totals, tile_end[-1:])

    slots = info[:, INFO_S0:INFO_S1 + 1, :].astype(jnp.int32).reshape(-1)
    weights = info[:, INFO_W0:INFO_W1 + 1, :].reshape(-1)

    xs = _dispatch(x, od_norm_ffn[0], slots, tabs + pad_tabs, n_experts, n_tiles * tme, tm, tme)
    ys = _experts(xs, tile_expert, tile_valid, tile_end[-1:], exp_gate.reshape(n_experts, d, ff),
                  exp_up.reshape(n_experts, d, ff), exp_down.reshape(n_experts, ff, d), d, tme, fc_dense)
    out = _combine(x, slots, weights, tabs, n_experts, ys, final_norm, tm)
    return out.reshape(b, s, d)


def kernel(x, rel_bias, ev_norm_mix, ev_w_in, ev_pool_w, ev_pool_scale, ev_conv_w, ev_w_out, ev_norm_ffn, ev_ffn_gate, ev_ffn_up, ev_ffn_down, od_norm_mix, od_w_qkv, od_b_qkv, od_sinks, od_w_o, od_b_o, od_norm_ffn, od_router, od_exp_gate, od_exp_up, od_exp_down, final_norm):
    return _forward(x, rel_bias, ev_norm_mix, ev_w_in, ev_pool_w, ev_pool_scale, ev_conv_w, ev_w_out,
                    ev_norm_ffn, ev_ffn_gate, ev_ffn_up, ev_ffn_down, od_norm_mix, od_w_qkv, od_b_qkv,
                    od_sinks, od_w_o, od_b_o, od_norm_ffn, od_router, od_exp_gate, od_exp_up,
                    od_exp_down, final_norm, tm=512, tq=512, tme=512, fc_dense=256)
```

```python
import functools
import math

import jax
import jax.numpy as jnp
from jax import lax
from jax.experimental import pallas as pl
from jax.experimental.pallas import tpu as pltpu

F32 = jnp.float32
BF16 = jnp.bfloat16

RMS_EPS = 1e-6
CHUNK = 64
POOL_WINDOWS = (2, 4, 8, 16)
CONV_K = 3
HEAD_DIM = 64
N_KV_HEADS = 2
WINDOW_CHUNKS = 2
KV_SPAN = (WINDOW_CHUNKS + 1) * CHUNK
N_BUCKETS = 32
MAX_DISTANCE = 128
TOP_K = 2
NEG_INF = -1e30

LANES = 128
SUBLANES = 8
VMEM_LIMIT = 56 * 1024 * 1024
EXPERT_VMEM_LIMIT = 62 * 1024 * 1024

POOL_HALO = 16
CONV_HALO = 8
ROW_LOOP_UNROLL = 16

def _rms(x, g):
    return x * lax.rsqrt(jnp.mean(x * x, axis=-1, keepdims=True) + RMS_EPS) * g


def _params(n_axes, vmem=VMEM_LIMIT):
    return pltpu.CompilerParams(dimension_semantics=("arbitrary",) * n_axes,
                                vmem_limit_bytes=vmem)


def _const_spec(shape):
    nd = len(shape)
    return pl.BlockSpec(shape, lambda *_: (0,) * nd, pipeline_mode=pl.Buffered(1))


def _mixer_kernel(x_ref, g_ref, win_ref, pw_ref, ps_ref, cw_ref, wout_ref, o_ref, ubuf, zbuf):
    s = pl.program_id(1)
    tm = x_ref.shape[0]
    pool_w = ubuf.shape[1]
    grp = pool_w // len(POOL_WINDOWS)
    conv_c = zbuf.shape[1]

    @pl.when(s == 0)
    def _():
        ubuf[0:POOL_HALO, :] = jnp.zeros((POOL_HALO, pool_w), F32)
        zbuf[0:CONV_HALO, :] = jnp.zeros((CONV_HALO, conv_c), F32)

    @pl.when(s > 0)
    def _():
        ubuf[0:POOL_HALO, :] = ubuf[tm:tm + POOL_HALO, :]
        zbuf[0:CONV_HALO, :] = zbuf[tm:tm + CONV_HALO, :]

    x = x_ref[...]
    h = _rms(x, g_ref[...]).astype(BF16)
    proj = jnp.dot(h, win_ref[...], preferred_element_type=F32)
    u = proj[:, :pool_w]
    gate_post = proj[:, pool_w:pool_w + conv_c]
    gate_pre = proj[:, pool_w + conv_c:pool_w + 2 * conv_c]
    v = proj[:, pool_w + 2 * conv_c:]
    ubuf[POOL_HALO:POOL_HALO + tm, :] = u
    z = gate_pre * v
    zbuf[CONV_HALO:CONV_HALO + tm, :] = z

    t = s * tm + lax.broadcasted_iota(jnp.int32, (tm, 1), 0)
    parts = []
    for g, w in enumerate(POOL_WINDOWS):
        lo, hi = g * grp, (g + 1) * grp
        ug = u[:, lo:hi]
        acc = ug
        for k in range(1, w):
            acc = acc + ubuf[POOL_HALO - k:POOL_HALO - k + tm, lo:hi]
        count = jnp.minimum(t + 1, w).astype(F32)
        pooled = acc / count - ug
        mixed = jnp.dot(pooled.astype(BF16), pw_ref[g], preferred_element_type=F32)
        parts.append(mixed * ps_ref[:, lo:hi])
    conv = cw_ref[CONV_K - 1:CONV_K, :] * z
    for k in range(CONV_K - 1):
        off = CONV_HALO - (CONV_K - 1) + k
        conv = conv + cw_ref[k:k + 1, :] * zbuf[off:off + tm, :]
    parts.append(gate_post * conv)
    y = jnp.concatenate(parts, axis=1).astype(BF16)
    o_ref[...] = x + jnp.dot(y, wout_ref[...], preferred_element_type=F32)


def _mixer(x, g, w_in, pool_w, pool_scale, conv_w, w_out, tm):
    b, s, d = x.shape
    pool_width = pool_scale.shape[-1]
    conv_c = conv_w.shape[-1]
    return pl.pallas_call(
        _mixer_kernel,
        out_shape=jax.ShapeDtypeStruct((b, s, d), F32),
        grid=(b, s // tm),
        in_specs=[
            pl.BlockSpec((None, tm, d), lambda i, j: (i, j, 0)),
            _const_spec((1, d)),
            _const_spec(w_in.shape),
            _const_spec(pool_w.shape),
            _const_spec((1, pool_width)),
            _const_spec(conv_w.shape),
            _const_spec(w_out.shape),
        ],
        out_specs=pl.BlockSpec((None, tm, d), lambda i, j: (i, j, 0)),
        scratch_shapes=[pltpu.VMEM((POOL_HALO + tm, pool_width), F32),
                        pltpu.VMEM((CONV_HALO + tm, conv_c), F32)],
        compiler_params=_params(2),
        name="l0_mixer",
    )(x, g.reshape(1, d), w_in, pool_w, pool_scale.reshape(1, pool_width), conv_w, w_out)


def _ffn_kernel(x_ref, g_ref, wg_ref, wu_ref, wd_ref, *rest, fc, n_cast):
    cast_in, (o_ref, *cast_out), (h_scr, t_scr) = rest[:n_cast], rest[n_cast:2 * n_cast + 1], rest[2 * n_cast + 1:]
    x = x_ref[...]
    h_scr[...] = _rms(x, g_ref[...]).astype(BF16)
    n_chunks = wg_ref.shape[1] // fc

    def body(c, carry):
        off = pl.multiple_of(c * fc, fc)
        h = h_scr[...]
        a = jnp.dot(h, wg_ref[:, pl.ds(off, fc)], preferred_element_type=F32)
        b = jnp.dot(h, wu_ref[:, pl.ds(off, fc)], preferred_element_type=F32)
        t_scr[:, pl.ds(off, fc)] = (a * jax.nn.sigmoid(a) * b).astype(BF16)
        return carry

    lax.fori_loop(0, n_chunks, body, 0, unroll=True)
    o_ref[...] = x + jnp.dot(t_scr[...], wd_ref[...], preferred_element_type=F32)
    for src, dst in zip(cast_in, cast_out):
        dst[...] = src[...].astype(BF16)


def _ffn(x, g, w_gate, w_up, w_down, to_cast, tm, fc):
    n, d = x.shape
    steps = n // tm
    slabs = []
    for w in to_cast:
        rows = w.shape[0] // steps
        assert rows * steps == w.shape[0] and rows % (2 * SUBLANES) == 0, w.shape
        slabs.append(pl.BlockSpec((rows, w.shape[1]), lambda i: (i, 0)))
    outs = pl.pallas_call(
        functools.partial(_ffn_kernel, fc=fc, n_cast=len(to_cast)),
        out_shape=(jax.ShapeDtypeStruct((n, d), F32),
                   *[jax.ShapeDtypeStruct(w.shape, BF16) for w in to_cast]),
        grid=(steps,),
        in_specs=[
            pl.BlockSpec((tm, d), lambda i: (i, 0)),
            _const_spec((1, d)),
            _const_spec(w_gate.shape),
            _const_spec(w_up.shape),
            _const_spec(w_down.shape),
            *slabs,
        ],
        out_specs=(pl.BlockSpec((tm, d), lambda i: (i, 0)), *slabs),
        scratch_shapes=[pltpu.VMEM((tm, d), BF16), pltpu.VMEM((tm, w_gate.shape[1]), BF16)],
        compiler_params=_params(1),
        name="l0_ffn",
    )(x, g.reshape(1, d), w_gate, w_up, w_down, *to_cast)
    return outs[0], outs[1:]


_NT = (((1,), (1,)), ((), ()))
_TN = (((0,), (0,)), ((), ()))


def _qkv_kernel(x_ref, g_ref, wqt_ref, bq_ref, wk_ref, bk_ref, wvt_ref, bv_ref, qt_ref, k_ref, vt_ref):
    h = _rms(x_ref[...], g_ref[...]).astype(BF16)
    qt = lax.dot_general(wqt_ref[...], h, _NT, preferred_element_type=F32) + bq_ref[...]
    qt_ref[...] = (qt * (HEAD_DIM ** -0.5)).astype(BF16)
    k_ref[...] = (jnp.dot(h, wk_ref[...], preferred_element_type=F32) + bk_ref[...]).astype(BF16)
    vt = lax.dot_general(wvt_ref[...], h, _NT, preferred_element_type=F32) + bv_ref[...]
    vt_ref[...] = vt.astype(BF16)


def _qkv(x, g, w_qkv, b_qkv, dq, dk, tm):
    b, s, d = x.shape
    wqt = w_qkv[:, :dq].T.astype(BF16)
    wk = w_qkv[:, dq:dq + dk].astype(BF16)
    wvt = w_qkv[:, dq + dk:].T.astype(BF16)
    bq = b_qkv[:dq].reshape(dq, 1)
    bk = b_qkv[dq:dq + dk].reshape(1, dk)
    bv = b_qkv[dq + dk:].reshape(dk, 1)
    return pl.pallas_call(
        _qkv_kernel,
        out_shape=(jax.ShapeDtypeStruct((b, dq, s), BF16),
                   jax.ShapeDtypeStruct((b, s, dk), BF16),
                   jax.ShapeDtypeStruct((b, dk, s), BF16)),
        grid=(b, s // tm),
        in_specs=[
            pl.BlockSpec((None, tm, d), lambda bi, i: (bi, i, 0)),
            _const_spec((1, d)),
            _const_spec(wqt.shape), _const_spec(bq.shape),
            _const_spec(wk.shape), _const_spec(bk.shape),
            _const_spec(wvt.shape), _const_spec(bv.shape),
        ],
        out_specs=(pl.BlockSpec((None, dq, tm), lambda bi, i: (bi, 0, i)),
                   pl.BlockSpec((None, tm, dk), lambda bi, i: (bi, i, 0)),
                   pl.BlockSpec((None, dk, tm), lambda bi, i: (bi, 0, i))),
        compiler_params=_params(2),
        name="l1_qkv",
    )(x, g.reshape(1, d), wqt, bq, wk, bk, wvt, bv)


def _bias_kernel(onehot_ref, rb_ref, o_ref):
    o_ref[...] = jnp.dot(onehot_ref[...], rb_ref[...], preferred_element_type=F32,
                         precision=lax.Precision.HIGHEST)


def _t5_bucket(rel):
    nb = N_BUCKETS // 2
    max_exact = nb // 2
    ret = jnp.where(rel > 0, nb, 0)
    n = jnp.abs(rel)
    nf = jnp.maximum(n, 1).astype(jnp.float32)
    large = max_exact + (jnp.log(nf / max_exact) / math.log(MAX_DISTANCE / max_exact)
                         * (nb - max_exact)).astype(jnp.int32)
    large = jnp.minimum(large, nb - 1)
    return ret + jnp.where(n < max_exact, n, large)


def _rel_bias_table(rel_bias):
    n_heads = rel_bias.shape[1]
    pad = WINDOW_CHUNKS * CHUNK
    rel = (jnp.arange(KV_SPAN) - pad)[None, :] - jnp.arange(CHUNK)[:, None]
    bucket = _t5_bucket(rel).reshape(-1)
    onehot = (bucket[:, None] == jnp.arange(N_BUCKETS)[None, :]).astype(F32)
    rows = CHUNK * KV_SPAN
    rb = jnp.pad(rel_bias.astype(F32), ((0, 0), (0, LANES - n_heads)))
    out = pl.pallas_call(
        _bias_kernel,
        out_shape=jax.ShapeDtypeStruct((rows, LANES), F32),
        name="l1_rel_bias",
    )(onehot, rb)
    return jnp.transpose(out[:, :n_heads].reshape(CHUNK, KV_SPAN, n_heads), (2, 0, 1))


PAIR = 2 * CHUNK
PAIR_KEYS = KV_SPAN + CHUNK


def _attn_kernel(qt_ref, kp_ref, km_ref, vtp_ref, vtm_ref, bias_ref, sink_ref, ot_ref, kbuf, vtbuf,
                 s_a, s_b, *, n_heads):
    i = pl.program_id(1)
    tq = qt_ref.shape[1]
    pad = WINDOW_CHUNKS * CHUNK
    per_kv = n_heads // N_KV_HEADS
    kbuf[0:pad, :] = kp_ref[...]
    kbuf[pad:pad + tq, :] = km_ref[...]
    vtbuf[:, 0:pad] = vtp_ref[...]
    vtbuf[:, pad:pad + tq] = vtm_ref[...]
    zeros = jnp.zeros((HEAD_DIM, PAIR), BF16)
    n_pairs = tq // PAIR

    def scores(jj, s_scr):
        c0 = jj * PAIR
        kc = kbuf[c0:c0 + PAIR_KEYS, :]
        for h in range(n_heads):
            pieces = [zeros] * N_KV_HEADS
            pieces[h // per_kv] = qt_ref[h * HEAD_DIM:(h + 1) * HEAD_DIM, c0:c0 + PAIR]
            rhs = jnp.concatenate(pieces, axis=0)
            s_scr[h] = jnp.dot(kc, rhs, preferred_element_type=F32)

    def softmax_values(jj, s_scr):
        c0 = jj * PAIR
        variant = jnp.where(i == 0, 1, 0) if jj == 0 else 0
        for h in range(n_heads):
            kv = h // per_kv
            s = s_scr[h] + bias_ref[variant, h]
            sk = sink_ref[h]
            m_ = jnp.maximum(jnp.max(s, axis=0, keepdims=True), sk)
            p = jnp.exp(s - m_)
            denom = jnp.sum(p, axis=0, keepdims=True) + jnp.exp(sk - m_)
            p = (p / denom).astype(BF16)
            vt = vtbuf[kv * HEAD_DIM:(kv + 1) * HEAD_DIM, c0:c0 + PAIR_KEYS]
            o = jnp.dot(vt, p, preferred_element_type=F32)
            ot_ref[h * HEAD_DIM:(h + 1) * HEAD_DIM, c0:c0 + PAIR] = o.astype(BF16)

    bufs = (s_a, s_b)
    scores(0, bufs[0])
    for jj in range(n_pairs):
        if jj + 1 < n_pairs:
            scores(jj + 1, bufs[(jj + 1) % 2])
        softmax_values(jj, bufs[jj % 2])


def _pair_bias(bias):
    bt = jnp.transpose(bias, (0, 2, 1)).astype(F32)
    neg = jnp.full((bias.shape[0], CHUNK, CHUNK), NEG_INF, F32)
    first = jnp.concatenate([bt, neg], axis=1)
    second = jnp.concatenate([neg, bt], axis=1)
    table = jnp.concatenate([first, second], axis=2)
    key = jnp.arange(PAIR_KEYS)[None, :, None]
    masked = jnp.where(key < WINDOW_CHUNKS * CHUNK, NEG_INF, table)
    return jnp.stack([table, masked])


def _attention(qt, k, vt, bias, sinks, tq):
    b, dq, s = qt.shape
    dk = k.shape[-1]
    n_heads = dq // HEAD_DIM
    pad = WINDOW_CHUNKS * CHUNK
    ratio = tq // pad
    table = _pair_bias(bias)
    prev = lambda i: jnp.maximum(i * ratio - 1, 0)
    return pl.pallas_call(
        functools.partial(_attn_kernel, n_heads=n_heads),
        out_shape=jax.ShapeDtypeStruct((b, dq, s), BF16),
        grid=(b, s // tq),
        in_specs=[
            pl.BlockSpec((None, dq, tq), lambda bi, i: (bi, 0, i)),
            pl.BlockSpec((None, pad, dk), lambda bi, i: (bi, prev(i), 0)),
            pl.BlockSpec((None, tq, dk), lambda bi, i: (bi, i, 0)),
            pl.BlockSpec((None, dk, pad), lambda bi, i: (bi, 0, prev(i))),
            pl.BlockSpec((None, dk, tq), lambda bi, i: (bi, 0, i)),
            _const_spec(table.shape),
            pl.BlockSpec(memory_space=pltpu.SMEM),
        ],
        out_specs=pl.BlockSpec((None, dq, tq), lambda bi, i: (bi, 0, i)),
        scratch_shapes=[pltpu.VMEM((pad + tq, dk), BF16), pltpu.VMEM((dk, pad + tq), BF16),
                        pltpu.VMEM((n_heads, PAIR_KEYS, PAIR), F32),
                        pltpu.VMEM((n_heads, PAIR_KEYS, PAIR), F32)],
        compiler_params=_params(2),
        name="l1_attention",
    )(qt, k, k, vt, vt, table, sinks.astype(F32))


INFO_S0, INFO_S1, INFO_W0, INFO_W1 = range(4)


def _oproj_router_kernel(ot_ref, x_ref, wo_ref, bo_ref, g_ref, wr_ref, x3_ref, info_ref, cnt_ref,
                         *, n_experts):
    tm = x_ref.shape[0]
    attn = lax.dot_general(ot_ref[...], wo_ref[...], _TN, preferred_element_type=F32)
    x3 = x_ref[...] + attn + bo_ref[...]
    x3_ref[...] = x3
    h = _rms(x3, g_ref[...])
    h_hi = h.astype(BF16)
    h_lo = (h - h_hi.astype(F32)).astype(BF16)
    wr = wr_ref[...]
    hw = jnp.dot(h_hi, wr, preferred_element_type=F32)
    logits = (hw[:, :LANES] + hw[:, LANES:]) + jnp.dot(h_lo, wr[:, :LANES], preferred_element_type=F32)
    lane_i = lax.broadcasted_iota(jnp.int32, (tm, LANES), 1)
    lane = lane_i.astype(F32)
    neg = jnp.float32(-jnp.inf)
    logits = jnp.where(lane_i < n_experts, logits, neg)
    m0 = jnp.max(logits, axis=-1, keepdims=True)
    e0 = jnp.min(jnp.where(logits == m0, lane, float(LANES)), axis=-1, keepdims=True)
    rest = jnp.where(lane == e0, neg, logits)
    m1 = jnp.max(rest, axis=-1, keepdims=True)
    e1 = jnp.min(jnp.where(rest == m1, lane, float(LANES)), axis=-1, keepdims=True)
    t = jnp.exp(m1 - m0)
    w0 = 1.0 / (1.0 + t)
    w1 = t / (1.0 + t)
    hot0 = lane == e0
    hot1 = lane == e1
    both = jnp.where(hot0 | hot1, 1.0, 0.0).astype(BF16)
    row = lax.broadcasted_iota(jnp.int32, (tm, tm), 0)
    col = lax.broadcasted_iota(jnp.int32, (tm, tm), 1)
    tri = (col < row).astype(BF16)
    cnt = jnp.sum(both.astype(F32), axis=0, keepdims=True)
    cnt_ref[...] = cnt
    upto = jnp.broadcast_to(cnt, (SUBLANES, LANES))
    lane8 = lax.broadcasted_iota(jnp.int32, (SUBLANES, LANES), 1)
    shift = 1
    while shift < n_experts:
        upto = upto + jnp.where(lane8 >= shift, pltpu.roll(upto, shift, axis=1), 0.0)
        shift *= 2
    before = jnp.dot(tri, both, preferred_element_type=F32) + (upto[0:1, :] - cnt)
    r0 = jnp.sum(jnp.where(hot0, before, 0.0), axis=-1, keepdims=True)
    r1 = jnp.sum(jnp.where(hot1, before, 0.0), axis=-1, keepdims=True)
    info = jnp.zeros((tm, LANES), F32)
    for idx, val in ((INFO_S0, r0 * SUBLANES), (INFO_S1, r1 * SUBLANES), (INFO_W0, w0), (INFO_W1, w1)):
        info = jnp.where(lane_i == idx, val, info)
    info_ref[...] = info.T[0:SUBLANES, :]


def _oproj_router(ot, x, w_o, b_o, g, w_router, tm):
    n, d = x.shape
    dq, seq = ot.shape[1:]
    per_seq = seq // tm
    n_experts = w_router.shape[1]
    wr = jnp.pad(w_router.astype(F32), ((0, 0), (0, LANES - n_experts)))
    wr_hi = wr.astype(BF16)
    wr_lo = (wr - wr_hi.astype(F32)).astype(BF16)
    wr = jnp.concatenate([wr_hi, wr_lo], axis=1)
    return pl.pallas_call(
        functools.partial(_oproj_router_kernel, n_experts=n_experts),
        out_shape=(jax.ShapeDtypeStruct((n, d), F32),
                   jax.ShapeDtypeStruct((n // tm, SUBLANES, tm), F32),
                   jax.ShapeDtypeStruct((n // tm, 1, LANES), F32)),
        grid=(n // tm,),
        in_specs=[
            pl.BlockSpec((None, dq, tm), lambda i: (i // per_seq, 0, i % per_seq)),
            pl.BlockSpec((tm, d), lambda i: (i, 0)),
            _const_spec(w_o.shape),
            _const_spec((1, d)),
            _const_spec((1, d)),
            _const_spec(wr.shape),
        ],
        out_specs=(pl.BlockSpec((tm, d), lambda i: (i, 0)),
                   pl.BlockSpec((None, SUBLANES, tm), lambda i: (i, 0, 0)),
                   pl.BlockSpec((None, 1, LANES), lambda i: (i, 0, 0))),
        compiler_params=_params(1),
        name="l1_oproj_router",
    )(ot, x, w_o, b_o.reshape(1, d), g.reshape(1, d), wr)


def _pieces(n, n_bits, fn):
    for bit in reversed(range(n_bits)):
        size = 1 << bit
        done = n & ~((size << 1) - 1)
        pl.when((n & size) != 0)(functools.partial(fn, done, size))


def _rows(ref, first_row, n_rows):
    return ref.at[pl.ds(pl.multiple_of(first_row * SUBLANES, SUBLANES), n_rows * SUBLANES)]


def _segment_copies(tabs, i, n_experts, lbuf, hbm, sem, to_hbm, act):
    n_tab, o_tab, d_tab = tabs[:3]
    n_bits = (lbuf.shape[0] // SUBLANES // TOP_K).bit_length()
    for e in range(n_experts):
        local0 = o_tab[i * n_experts + e]
        sorted0 = d_tab[i * n_experts + e]

        def piece(done, size, local0=local0, sorted0=sorted0):
            local = _rows(lbuf, local0 + done, size)
            remote = _rows(hbm, sorted0 + done, size)
            act(pltpu.make_async_copy(local, remote, sem) if to_hbm
                else pltpu.make_async_copy(remote, local, sem))

        _pieces(n_tab[i * n_experts + e], n_bits, piece)


def _dispatch_kernel(n_tab, o_tab, d_tab, pad0_tab, padn_tab, nv_tab, slot_ref, x_ref, g_ref, xs_hbm,
                     hbuf, lbuf, sems, *, n_experts, tme):
    i = pl.program_id(0)
    tm = x_ref.shape[0]
    n_slots = TOP_K * tm
    h = _rms(x_ref[...], g_ref[...])
    for c in range(x_ref.shape[1] // LANES):
        hbuf[pl.ds(c, tm, stride=SUBLANES), :] = h[:, c * LANES:(c + 1) * LANES]
    slot = i % 2
    sorted_rows = lbuf.at[slot]

    def token(t, carry):
        row = hbuf[pl.ds(pl.multiple_of(t * SUBLANES, SUBLANES), SUBLANES), :]
        for k in range(TOP_K):
            dst = pl.multiple_of(slot_ref[k * tm + t], SUBLANES)
            sorted_rows[pl.ds(dst, SUBLANES), :] = row
        return carry

    lax.fori_loop(0, tm, token, 0, unroll=ROW_LOOP_UNROLL)
    tabs = (n_tab, o_tab, d_tab)

    def copies(tile, s, act):
        _segment_copies(tabs, tile, n_experts, lbuf.at[s], xs_hbm, sems.at[s], True, act)

    copies(i, slot, lambda cp: cp.start())

    def wait_all(s):
        pltpu.make_async_copy(lbuf.at[s], _rows(xs_hbm, 0, n_slots), sems.at[s]).wait()

    @pl.when(i > 0)
    def _():
        wait_all(1 - slot)

    @pl.when(i == pl.num_programs(0) - 1)
    def _():
        wait_all(slot)
        zeros = lbuf.at[slot]
        zeros[0:tme * SUBLANES, :] = jnp.zeros((tme * SUBLANES, LANES), F32)

        def pads(act):
            for e in range(n_experts):
                def piece(done, size, e=e):
                    act(pltpu.make_async_copy(_rows(zeros, 0, size),
                                              _rows(xs_hbm, pad0_tab[e] + done, size), sems.at[slot]))
                _pieces(padn_tab[e], (tme - 1).bit_length(), piece)
            n_tiles = xs_hbm.shape[0] // (tme * SUBLANES)
            for t in range(n_experts):
                tile = nv_tab[0] + t
                pl.when(tile < n_tiles)(lambda tile=tile: act(pltpu.make_async_copy(
                    _rows(zeros, 0, tme), _rows(xs_hbm, tile * tme, tme), sems.at[slot])))

        pads(lambda cp: cp.start())
        pads(lambda cp: cp.wait())


def _dispatch(x, g, slots, tabs, n_experts, n_rows, tm, tme):
    n, d = x.shape
    assert TOP_K * tm >= tme
    grid_spec = pltpu.PrefetchScalarGridSpec(
        num_scalar_prefetch=len(tabs),
        grid=(n // tm,),
        in_specs=[
            pl.BlockSpec((TOP_K * tm,), lambda i, *_: (i,), memory_space=pltpu.SMEM),
            pl.BlockSpec((tm, d), lambda i, *_: (i, 0)),
            pl.BlockSpec((1, d), lambda i, *_: (0, 0)),
        ],
        out_specs=pl.BlockSpec(memory_space=pl.ANY),
        scratch_shapes=[pltpu.VMEM((tm * SUBLANES, LANES), F32),
                        pltpu.VMEM((2, TOP_K * tm * SUBLANES, LANES), F32),
                        pltpu.SemaphoreType.DMA((2,))],
    )
    return pl.pallas_call(
        functools.partial(_dispatch_kernel, n_experts=n_experts, tme=tme),
        out_shape=jax.ShapeDtypeStruct((n_rows * SUBLANES, LANES), F32),
        grid_spec=grid_spec,
        compiler_params=_params(1),
        name="l1_moe_dispatch",
    )(*tabs, slots, x, g.reshape(1, d))


def _experts_kernel(te_ref, tv_ref, nv_ref, xs_ref, wg_ref, wu_ref, wd_ref, ys_ref, t_scr, *, sub):
    del te_ref, nv_ref
    j = pl.program_id(0)
    tm = t_scr.shape[0]
    n_slabs = wg_ref.shape[0] // LANES

    @pl.when(tv_ref[j] == 1)
    def _():
        slabs = [xs_ref[pl.ds(s, tm, stride=SUBLANES), :] for s in range(n_slabs)]
        x = jnp.concatenate(slabs, axis=1).astype(BF16)

        def body(q, carry):
            off = pl.multiple_of(q * sub, sub)
            a = jnp.dot(x, wg_ref[:, pl.ds(off, sub)], preferred_element_type=F32)
            b = jnp.dot(x, wu_ref[:, pl.ds(off, sub)], preferred_element_type=F32)
            t_scr[:, pl.ds(off, sub)] = (a * jax.nn.sigmoid(a) * b).astype(BF16)
            return carry
        lax.fori_loop(0, wg_ref.shape[1] // sub, body, 0, unroll=True)
        y = jnp.dot(t_scr[...], wd_ref[...], preferred_element_type=F32)
        for s in range(n_slabs):
            ys_ref[pl.ds(s, tm, stride=SUBLANES), :] = y[:, s * LANES:(s + 1) * LANES]

    @pl.when(tv_ref[j] != 1)
    def _():
        ys_ref[...] = jnp.zeros_like(ys_ref)


def _experts(xs, tile_expert, tile_valid, n_valid, w_gate, w_up, w_down, d, tm, sub):
    n_tiles = tile_expert.shape[0]
    ff = w_gate.shape[2]
    grid_spec = pltpu.PrefetchScalarGridSpec(
        num_scalar_prefetch=3,
        grid=(n_tiles,),
        in_specs=[
            pl.BlockSpec((tm * SUBLANES, LANES), lambda j, te, tv, nv: (jnp.minimum(j, nv[0] - 1), 0)),
            pl.BlockSpec((None, d, ff), lambda j, te, tv, nv: (te[j], 0, 0)),
            pl.BlockSpec((None, d, ff), lambda j, te, tv, nv: (te[j], 0, 0)),
            pl.BlockSpec((None, ff, d), lambda j, te, tv, nv: (te[j], 0, 0)),
        ],
        out_specs=pl.BlockSpec((tm * SUBLANES, LANES), lambda j, te, tv, nv: (j, 0)),
        scratch_shapes=[pltpu.VMEM((tm, ff), BF16)],
    )
    return pl.pallas_call(
        functools.partial(_experts_kernel, sub=sub),
        out_shape=jax.ShapeDtypeStruct(xs.shape, F32),
        grid_spec=grid_spec,
        compiler_params=_params(1, vmem=EXPERT_VMEM_LIMIT),
        name="l1_moe_experts",
    )(tile_expert, tile_valid, n_valid, xs, w_gate, w_up, w_down)


def _combine_kernel(n_tab, o_tab, d_tab, slot_ref, w_ref, x_ref, g_ref, ys_hbm, o_ref, lbuf, mbuf, sems,
                    *, n_experts):
    i = pl.program_id(0)
    tm = x_ref.shape[0]
    n_slots = TOP_K * tm
    tabs = (n_tab, o_tab, d_tab)
    slot = i % 2

    def copies(tile, s, act):
        _segment_copies(tabs, tile, n_experts, lbuf.at[s], ys_hbm, sems.at[s], False, act)

    @pl.when(i == 0)
    def _():
        copies(i, slot, lambda cp: cp.start())

    @pl.when(i + 1 < pl.num_programs(0))
    def _():
        copies(i + 1, 1 - slot, lambda cp: cp.start())

    pltpu.make_async_copy(_rows(ys_hbm, 0, n_slots), lbuf.at[slot], sems.at[slot]).wait()
    y = lbuf.at[slot]

    def token(t, carry):
        acc = None
        for k in range(TOP_K):
            row = y[pl.ds(pl.multiple_of(slot_ref[k * tm + t], SUBLANES), SUBLANES), :]
            term = w_ref[k * tm + t] * row
            acc = term if acc is None else acc + term
        mbuf[pl.ds(pl.multiple_of(t * SUBLANES, SUBLANES), SUBLANES), :] = acc
        return carry

    lax.fori_loop(0, tm, token, 0, unroll=ROW_LOOP_UNROLL)
    moe = jnp.concatenate([mbuf[pl.ds(c, tm, stride=SUBLANES), :]
                           for c in range(x_ref.shape[1] // LANES)], axis=1)
    o_ref[...] = _rms(x_ref[...] + moe, g_ref[...])


def _combine(x, slots, weights, tabs, n_experts, ys, g, tm):
    n, d = x.shape
    grid_spec = pltpu.PrefetchScalarGridSpec(
        num_scalar_prefetch=3,
        grid=(n // tm,),
        in_specs=[
            pl.BlockSpec((TOP_K * tm,), lambda i, *_: (i,), memory_space=pltpu.SMEM),
            pl.BlockSpec((TOP_K * tm,), lambda i, *_: (i,), memory_space=pltpu.SMEM),
            pl.BlockSpec((tm, d), lambda i, *_: (i, 0)),
            pl.BlockSpec((1, d), lambda i, *_: (0, 0)),
            pl.BlockSpec(memory_space=pl.ANY),
        ],
        out_specs=pl.BlockSpec((tm, d), lambda i, *_: (i, 0)),
        scratch_shapes=[pltpu.VMEM((2, TOP_K * tm * SUBLANES, LANES), F32),
                        pltpu.VMEM((tm * SUBLANES, LANES), F32),
                        pltpu.SemaphoreType.DMA((2,))],
    )
    return pl.pallas_call(
        functools.partial(_combine_kernel, n_experts=n_experts),
        out_shape=jax.ShapeDtypeStruct((n, d), F32),
        grid_spec=grid_spec,
        compiler_params=_params(1),
        name="l1_moe_combine",
    )(*tabs, slots, weights, x, g.reshape(1, d), ys)


def _forward(x, rel_bias, ev_norm_mix, ev_w_in, ev_pool_w, ev_pool_scale, ev_conv_w, ev_w_out,
             ev_norm_ffn, ev_ffn_gate, ev_ffn_up, ev_ffn_down, od_norm_mix, od_w_qkv, od_b_qkv,
             od_sinks, od_w_o, od_b_o, od_norm_ffn, od_router, od_exp_gate, od_exp_up,
             od_exp_down, final_norm, *, tm, tq, tme, fc_dense):
    b, s, d = x.shape
    n = b * s
    n_heads = od_sinks.shape[-1]
    dq = n_heads * HEAD_DIM
    dk = N_KV_HEADS * HEAD_DIM
    n_experts = od_router.shape[-1]

    x = _mixer(x, ev_norm_mix[0], ev_w_in[0].astype(BF16), ev_pool_w[0].astype(BF16),
               ev_pool_scale[0], ev_conv_w[0], ev_w_out[0].astype(BF16), tm)
    x = x.reshape(n, d)
    ff = od_exp_gate.shape[-1]
    expert_f32 = (od_exp_gate[0].reshape(n_experts * d, ff), od_exp_up[0].reshape(n_experts * d, ff),
                  od_exp_down[0].reshape(n_experts * ff, d))
    x, (exp_gate, exp_up, exp_down) = _ffn(
        x, ev_norm_ffn[0], ev_ffn_gate[0].astype(BF16), ev_ffn_up[0].astype(BF16),
        ev_ffn_down[0].astype(BF16), expert_f32, tm, fc_dense)

    qt, k, vt = _qkv(x.reshape(b, s, d), od_norm_mix[0], od_w_qkv[0], od_b_qkv[0], dq, dk, tm)
    bias = _rel_bias_table(rel_bias)
    attn_t = _attention(qt, k, vt, bias, od_sinks[0], tq)
    x, info, counts = _oproj_router(attn_t, x, od_w_o[0].astype(BF16), od_b_o[0], od_norm_ffn[0],
                                    od_router[0], tm)

    per_tile = counts[:, 0, :n_experts].astype(jnp.int32)
    local_off = jnp.cumsum(per_tile, axis=1) - per_tile
    earlier = jnp.cumsum(per_tile, axis=0) - per_tile
    totals = jnp.sum(per_tile, axis=0)
    tiles_per = (totals + tme - 1) // tme
    tile_end = jnp.cumsum(tiles_per)
    group_start = (tile_end - tiles_per) * tme
    n_tiles = (TOP_K * n) // tme + n_experts
    tile_ids = jnp.arange(n_tiles, dtype=jnp.int32)
    tile_valid = (tile_ids < tile_end[-1]).astype(jnp.int32)
    last_valid = jnp.minimum(tile_ids, tile_end[-1] - 1)
    tile_expert = jnp.sum(last_valid[:, None] >= tile_end[None, :], axis=1).astype(jnp.int32)
    tabs = (per_tile.reshape(-1), local_off.reshape(-1), (group_start[None, :] + earlier).reshape(-1))
    pad_tabs = (group_start + totals, tiles_per * tme - totals, tile_end[-1:])

    slots = info[:, INFO_S0:INFO_S1 + 1, :].astype(jnp.int32).reshape(-1)
    weights = info[:, INFO_W0:INFO_W1 + 1, :].reshape(-1)

    xs = _dispatch(x, od_norm_ffn[0], slots, tabs + pad_tabs, n_experts, n_tiles * tme, tm, tme)
    ys = _experts(xs, tile_expert, tile_valid, tile_end[-1:], exp_gate.reshape(n_experts, d, ff),
                  exp_up.reshape(n_experts, d, ff), exp_down.reshape(n_experts, ff, d), d, tme, fc_dense)
    out = _combine(x, slots, weights, tabs, n_experts, ys, final_norm, tm)
    return out.reshape(b, s, d)


def kernel(x, rel_bias, ev_norm_mix, ev_w_in, ev_pool_w, ev_pool_scale, ev_conv_w, ev_w_out, ev_norm_ffn, ev_ffn_gate, ev_ffn_up, ev_ffn_down, od_norm_mix, od_w_qkv, od_b_qkv, od_sinks, od_w_o, od_b_o, od_norm_ffn, od_router, od_exp_gate, od_exp_up, od_exp_down, final_norm):
    return _forward(x, rel_bias, ev_norm_mix, ev_w_in, ev_pool_w, ev_pool_scale, ev_conv_w, ev_w_out,
                    ev_norm_ffn, ev_ffn_gate, ev_ffn_up, ev_ffn_down, od_norm_mix, od_w_qkv, od_b_qkv,
                    od_sinks, od_w_o, od_b_o, od_norm_ffn, od_router, od_exp_gate, od_exp_up,
                    od_exp_down, final_norm, tm=512, tq=512, tme=512, fc_dense=256)
```

```python
import functools
import math

import jax
import jax.numpy as jnp
from jax import lax
from jax.experimental import pallas as pl
from jax.experimental.pallas import tpu as pltpu

F32 = jnp.float32
BF16 = jnp.bfloat16

RMS_EPS = 1e-6
CHUNK = 64
POOL_WINDOWS = (2, 4, 8, 16)
CONV_K = 3
HEAD_DIM = 64
N_KV_HEADS = 2
WINDOW_CHUNKS = 2
KV_SPAN = (WINDOW_CHUNKS + 1) * CHUNK
N_BUCKETS = 32
MAX_DISTANCE = 128
TOP_K = 2
NEG_INF = -1e30

LANES = 128
SUBLANES = 8
VMEM_LIMIT = 56 * 1024 * 1024
EXPERT_VMEM_LIMIT = 62 * 1024 * 1024

POOL_HALO = 16
CONV_HALO = 8
ROW_LOOP_UNROLL = 16

def _rms(x, g):
    return x * lax.rsqrt(jnp.mean(x * x, axis=-1, keepdims=True) + RMS_EPS) * g


def _params(n_axes, vmem=VMEM_LIMIT):
    return pltpu.CompilerParams(dimension_semantics=("arbitrary",) * n_axes,
                                vmem_limit_bytes=vmem)


def _const_spec(shape):
    nd = len(shape)
    return pl.BlockSpec(shape, lambda *_: (0,) * nd, pipeline_mode=pl.Buffered(1))


def _mixer_kernel(x_ref, g_ref, win_ref, pw_ref, ps_ref, cw_ref, wout_ref, o_ref, ubuf, zbuf):
    s = pl.program_id(1)
    tm = x_ref.shape[0]
    pool_w = ubuf.shape[1]
    grp = pool_w // len(POOL_WINDOWS)
    conv_c = zbuf.shape[1]

    @pl.when(s == 0)
    def _():
        ubuf[0:POOL_HALO, :] = jnp.zeros((POOL_HALO, pool_w), F32)
        zbuf[0:CONV_HALO, :] = jnp.zeros((CONV_HALO, conv_c), F32)

    @pl.when(s > 0)
    def _():
        ubuf[0:POOL_HALO, :] = ubuf[tm:tm + POOL_HALO, :]
        zbuf[0:CONV_HALO, :] = zbuf[tm:tm + CONV_HALO, :]

    x = x_ref[...]
    h = _rms(x, g_ref[...]).astype(BF16)
    proj = jnp.dot(h, win_ref[...], preferred_element_type=F32)
    u = proj[:, :pool_w]
    gate_post = proj[:, pool_w:pool_w + conv_c]
    gate_pre = proj[:, pool_w + conv_c:pool_w + 2 * conv_c]
    v = proj[:, pool_w + 2 * conv_c:]
    ubuf[POOL_HALO:POOL_HALO + tm, :] = u
    z = gate_pre * v
    zbuf[CONV_HALO:CONV_HALO + tm, :] = z

    t = s * tm + lax.broadcasted_iota(jnp.int32, (tm, 1), 0)
    parts = []
    for g, w in enumerate(POOL_WINDOWS):
        lo, hi = g * grp, (g + 1) * grp
        ug = u[:, lo:hi]
        acc = ug
        for k in range(1, w):
            acc = acc + ubuf[POOL_HALO - k:POOL_HALO - k + tm, lo:hi]
        count = jnp.minimum(t + 1, w).astype(F32)
        pooled = acc / count - ug
        mixed = jnp.dot(pooled.astype(BF16), pw_ref[g], preferred_element_type=F32)
        parts.append(mixed * ps_ref[:, lo:hi])
    conv = cw_ref[CONV_K - 1:CONV_K, :] * z
    for k in range(CONV_K - 1):
        off = CONV_HALO - (CONV_K - 1) + k
        conv = conv + cw_ref[k:k + 1, :] * zbuf[off:off + tm, :]
    parts.append(gate_post * conv)
    y = jnp.concatenate(parts, axis=1).astype(BF16)
    o_ref[...] = x + jnp.dot(y, wout_ref[...], preferred_element_type=F32)


def _mixer(x, g, w_in, pool_w, pool_scale, conv_w, w_out, tm):
    b, s, d = x.shape
    pool_width = pool_scale.shape[-1]
    conv_c = conv_w.shape[-1]
    return pl.pallas_call(
        _mixer_kernel,
        out_shape=jax.ShapeDtypeStruct((b, s, d), F32),
        grid=(b, s // tm),
        in_specs=[
            pl.BlockSpec((None, tm, d), lambda i, j: (i, j, 0)),
            _const_spec((1, d)),
            _const_spec(w_in.shape),
            _const_spec(pool_w.shape),
            _const_spec((1, pool_width)),
            _const_spec(conv_w.shape),
            _const_spec(w_out.shape),
        ],
        out_specs=pl.BlockSpec((None, tm, d), lambda i, j: (i, j, 0)),
        scratch_shapes=[pltpu.VMEM((POOL_HALO + tm, pool_width), F32),
                        pltpu.VMEM((CONV_HALO + tm, conv_c), F32)],
        compiler_params=_params(2),
        name="l0_mixer",
    )(x, g.reshape(1, d), w_in, pool_w, pool_scale.reshape(1, pool_width), conv_w, w_out)


def _ffn_kernel(x_ref, g_ref, wg_ref, wu_ref, wd_ref, *rest, fc, n_cast):
    cast_in, (o_ref, *cast_out), (h_scr, t_scr) = rest[:n_cast], rest[n_cast:2 * n_cast + 1], rest[2 * n_cast + 1:]
    x = x_ref[...]
    h_scr[...] = _rms(x, g_ref[...]).astype(BF16)
    n_chunks = wg_ref.shape[1] // fc

    def body(c, carry):
        off = pl.multiple_of(c * fc, fc)
        h = h_scr[...]
        a = jnp.dot(h, wg_ref[:, pl.ds(off, fc)], preferred_element_type=F32)
        b = jnp.dot(h, wu_ref[:, pl.ds(off, fc)], preferred_element_type=F32)
        t_scr[:, pl.ds(off, fc)] = (a * jax.nn.sigmoid(a) * b).astype(BF16)
        return carry

    lax.fori_loop(0, n_chunks, body, 0, unroll=True)
    o_ref[...] = x + jnp.dot(t_scr[...], wd_ref[...], preferred_element_type=F32)
    for src, dst in zip(cast_in, cast_out):
        dst[...] = src[...].astype(BF16)


def _ffn(x, g, w_gate, w_up, w_down, to_cast, tm, fc):
    n, d = x.shape
    steps = n // tm
    slabs = []
    for w in to_cast:
        rows = w.shape[0] // steps
        assert rows * steps == w.shape[0] and rows % (2 * SUBLANES) == 0, w.shape
        slabs.append(pl.BlockSpec((rows, w.shape[1]), lambda i: (i, 0)))
    outs = pl.pallas_call(
        functools.partial(_ffn_kernel, fc=fc, n_cast=len(to_cast)),
        out_shape=(jax.ShapeDtypeStruct((n, d), F32),
                   *[jax.ShapeDtypeStruct(w.shape, BF16) for w in to_cast]),
        grid=(steps,),
        in_specs=[
            pl.BlockSpec((tm, d), lambda i: (i, 0)),
            _const_spec((1, d)),
            _const_spec(w_gate.shape),
            _const_spec(w_up.shape),
            _const_spec(w_down.shape),
            *slabs,
        ],
        out_specs=(pl.BlockSpec((tm, d), lambda i: (i, 0)), *slabs),
        scratch_shapes=[pltpu.VMEM((tm, d), BF16), pltpu.VMEM((tm, w_gate.shape[1]), BF16)],
        compiler_params=_params(1),
        name="l0_ffn",
    )(x, g.reshape(1, d), w_gate, w_up, w_down, *to_cast)
    return outs[0], outs[1:]


_NT = (((1,), (1,)), ((), ()))
_TN = (((0,), (0,)), ((), ()))


def _qkv_kernel(x_ref, g_ref, wqt_ref, bq_ref, wk_ref, bk_ref, wvt_ref, bv_ref, qt_ref, k_ref, vt_ref):
    h = _rms(x_ref[...], g_ref[...]).astype(BF16)
    qt = lax.dot_general(wqt_ref[...], h, _NT, preferred_element_type=F32) + bq_ref[...]
    qt_ref[...] = (qt * (HEAD_DIM ** -0.5)).astype(BF16)
    k_ref[...] = (jnp.dot(h, wk_ref[...], preferred_element_type=F32) + bk_ref[...]).astype(BF16)
    vt = lax.dot_general(wvt_ref[...], h, _NT, preferred_element_type=F32) + bv_ref[...]
    vt_ref[...] = vt.astype(BF16)


def _qkv(x, g, w_qkv, b_qkv, dq, dk, tm):
    b, s, d = x.shape
    wqt = w_qkv[:, :dq].T.astype(BF16)
    wk = w_qkv[:, dq:dq + dk].astype(BF16)
    wvt = w_qkv[:, dq + dk:].T.astype(BF16)
    bq = b_qkv[:dq].reshape(dq, 1)
    bk = b_qkv[dq:dq + dk].reshape(1, dk)
    bv = b_qkv[dq + dk:].reshape(dk, 1)
    return pl.pallas_call(
        _qkv_kernel,
        out_shape=(jax.ShapeDtypeStruct((b, dq, s), BF16),
                   jax.ShapeDtypeStruct((b, s, dk), BF16),
                   jax.ShapeDtypeStruct((b, dk, s), BF16)),
        grid=(b, s // tm),
        in_specs=[
            pl.BlockSpec((None, tm, d), lambda bi, i: (bi, i, 0)),
            _const_spec((1, d)),
            _const_spec(wqt.shape), _const_spec(bq.shape),
            _const_spec(wk.shape), _const_spec(bk.shape),
            _const_spec(wvt.shape), _const_spec(bv.shape),
        ],
        out_specs=(pl.BlockSpec((None, dq, tm), lambda bi, i: (bi, 0, i)),
                   pl.BlockSpec((None, tm, dk), lambda bi, i: (bi, i, 0)),
                   pl.BlockSpec((None, dk, tm), lambda bi, i: (bi, 0, i))),
        compiler_params=_params(2),
        name="l1_qkv",
    )(x, g.reshape(1, d), wqt, bq, wk, bk, wvt, bv)


def _bias_kernel(onehot_ref, rb_ref, o_ref):
    o_ref[...] = jnp.dot(onehot_ref[...], rb_ref[...], preferred_element_type=F32,
                         precision=lax.Precision.HIGHEST)


def _t5_bucket(rel):
    nb = N_BUCKETS // 2
    max_exact = nb // 2
    ret = jnp.where(rel > 0, nb, 0)
    n = jnp.abs(rel)
    nf = jnp.maximum(n, 1).astype(jnp.float32)
    large = max_exact + (jnp.log(nf / max_exact) / math.log(MAX_DISTANCE / max_exact)
                         * (nb - max_exact)).astype(jnp.int32)
    large = jnp.minimum(large, nb - 1)
    return ret + jnp.where(n < max_exact, n, large)


def _rel_bias_table(rel_bias):
    n_heads = rel_bias.shape[1]
    pad = WINDOW_CHUNKS * CHUNK
    rel = (jnp.arange(KV_SPAN) - pad)[None, :] - jnp.arange(CHUNK)[:, None]
    bucket = _t5_bucket(rel).reshape(-1)
    onehot = (bucket[:, None] == jnp.arange(N_BUCKETS)[None, :]).astype(F32)
    rows = CHUNK * KV_SPAN
    rb = jnp.pad(rel_bias.astype(F32), ((0, 0), (0, LANES - n_heads)))
    out = pl.pallas_call(
        _bias_kernel,
        out_shape=jax.ShapeDtypeStruct((rows, LANES), F32),
        name="l1_rel_bias",
    )(onehot, rb)
    return jnp.transpose(out[:, :n_heads].reshape(CHUNK, KV_SPAN, n_heads), (2, 0, 1))


PAIR = 2 * CHUNK
PAIR_KEYS = KV_SPAN + CHUNK


def _attn_kernel(qt_ref, kp_ref, km_ref, vtp_ref, vtm_ref, bias_ref, sink_ref, ot_ref, kbuf, vtbuf,
                 s_a, s_b, *, n_heads):
    i = pl.program_id(1)
    tq = qt_ref.shape[1]
    pad = WINDOW_CHUNKS * CHUNK
    per_kv = n_heads // N_KV_HEADS
    kbuf[0:pad, :] = kp_ref[...]
    kbuf[pad:pad + tq, :] = km_ref[...]
    vtbuf[:, 0:pad] = vtp_ref[...]
    vtbuf[:, pad:pad + tq] = vtm_ref[...]
    zeros = jnp.zeros((HEAD_DIM, PAIR), BF16)
    n_pairs = tq // PAIR

    def scores(jj, s_scr):
        c0 = jj * PAIR
        kc = kbuf[c0:c0 + PAIR_KEYS, :]
        for h in range(n_heads):
            pieces = [zeros] * N_KV_HEADS
            pieces[h // per_kv] = qt_ref[h * HEAD_DIM:(h + 1) * HEAD_DIM, c0:c0 + PAIR]
            rhs = jnp.concatenate(pieces, axis=0)
            s_scr[h] = jnp.dot(kc, rhs, preferred_element_type=F32)

    def softmax_values(jj, s_scr):
        c0 = jj * PAIR
        variant = jnp.where(i == 0, 1, 0) if jj == 0 else 0
        for h in range(n_heads):
            kv = h // per_kv
            s = s_scr[h] + bias_ref[variant, h]
            sk = sink_ref[h]
            m_ = jnp.maximum(jnp.max(s, axis=0, keepdims=True), sk)
            p = jnp.exp(s - m_)
            denom = jnp.sum(p, axis=0, keepdims=True) + jnp.exp(sk - m_)
            p = (p / denom).astype(BF16)
            vt = vtbuf[kv * HEAD_DIM:(kv + 1) * HEAD_DIM, c0:c0 + PAIR_KEYS]
            o = jnp.dot(vt, p, preferred_element_type=F32)
            ot_ref[h * HEAD_DIM:(h + 1) * HEAD_DIM, c0:c0 + PAIR] = o.astype(BF16)

    bufs = (s_a, s_b)
    scores(0, bufs[0])
    for jj in range(n_pairs):
        if jj + 1 < n_pairs:
            scores(jj + 1, bufs[(jj + 1) % 2])
        softmax_values(jj, bufs[jj % 2])


def _pair_bias(bias):
    bt = jnp.transpose(bias, (0, 2, 1)).astype(F32)
    neg = jnp.full((bias.shape[0], CHUNK, CHUNK), NEG_INF, F32)
    first = jnp.concatenate([bt, neg], axis=1)
    second = jnp.concatenate([neg, bt], axis=1)
    table = jnp.concatenate([first, second], axis=2)
    key = jnp.arange(PAIR_KEYS)[None, :, None]
    masked = jnp.where(key < WINDOW_CHUNKS * CHUNK, NEG_INF, table)
    return jnp.stack([table, masked])


def _attention(qt, k, vt, bias, sinks, tq):
    b, dq, s = qt.shape
    dk = k.shape[-1]
    n_heads = dq // HEAD_DIM
    pad = WINDOW_CHUNKS * CHUNK
    ratio = tq // pad
    table = _pair_bias(bias)
    prev = lambda i: jnp.maximum(i * ratio - 1, 0)
    return pl.pallas_call(
        functools.partial(_attn_kernel, n_heads=n_heads),
        out_shape=jax.ShapeDtypeStruct((b, dq, s), BF16),
        grid=(b, s // tq),
        in_specs=[
            pl.BlockSpec((None, dq, tq), lambda bi, i: (bi, 0, i)),
            pl.BlockSpec((None, pad, dk), lambda bi, i: (bi, prev(i), 0)),
            pl.BlockSpec((None, tq, dk), lambda bi, i: (bi, i, 0)),
            pl.BlockSpec((None, dk, pad), lambda bi, i: (bi, 0, prev(i))),
            pl.BlockSpec((None, dk, tq), lambda bi, i: (bi, 0, i)),
            _const_spec(table.shape),
            pl.BlockSpec(memory_space=pltpu.SMEM),
        ],
        out_specs=pl.BlockSpec((None, dq, tq), lambda bi, i: (bi, 0, i)),
        scratch_shapes=[pltpu.VMEM((pad + tq, dk), BF16), pltpu.VMEM((dk, pad + tq), BF16),
                        pltpu.VMEM((n_heads, PAIR_KEYS, PAIR), F32),
                        pltpu.VMEM((n_heads, PAIR_KEYS, PAIR), F32)],
        compiler_params=_params(2),
        name="l1_attention",
    )(qt, k, k, vt, vt, table, sinks.astype(F32))


INFO_S0, INFO_S1, INFO_W0, INFO_W1 = range(4)


EXPERT_ROWS = 16


def _oproj_router_kernel(ot_ref, x_ref, wo_ref, bo_ref, g_ref, wrt_ref, x3_ref, info_ref, cnt_ref,
                         *, n_experts):
    tm = x_ref.shape[0]
    attn = lax.dot_general(ot_ref[...], wo_ref[...], _TN, preferred_element_type=F32)
    x3 = x_ref[...] + attn + bo_ref[...]
    x3_ref[...] = x3
    h = _rms(x3, g_ref[...])
    h_hi = h.astype(BF16)
    h_lo = (h - h_hi.astype(F32)).astype(BF16)
    wrt = wrt_ref[...]
    by_hi = lax.dot_general(wrt, h_hi, _NT, preferred_element_type=F32)
    by_lo = lax.dot_general(wrt[:EXPERT_ROWS], h_lo, _NT, preferred_element_type=F32)
    logits = (by_hi[:EXPERT_ROWS] + by_hi[EXPERT_ROWS:]) + by_lo
    row_i = lax.broadcasted_iota(jnp.int32, (EXPERT_ROWS, tm), 0)
    row = row_i.astype(F32)
    neg = jnp.float32(-jnp.inf)
    logits = jnp.where(row_i < n_experts, logits, neg)
    m0 = jnp.max(logits, axis=0, keepdims=True)
    e0 = jnp.min(jnp.where(logits == m0, row, float(EXPERT_ROWS)), axis=0, keepdims=True)
    rest = jnp.where(row == e0, neg, logits)
    m1 = jnp.max(rest, axis=0, keepdims=True)
    e1 = jnp.min(jnp.where(rest == m1, row, float(EXPERT_ROWS)), axis=0, keepdims=True)
    t = jnp.exp(m1 - m0)
    w0 = 1.0 / (1.0 + t)
    w1 = t / (1.0 + t)
    hot0 = row == e0
    hot1 = row == e1
    both = jnp.where(hot0 | hot1, 1.0, 0.0)
    u = lax.broadcasted_iota(jnp.int32, (tm, tm), 0)
    v = lax.broadcasted_iota(jnp.int32, (tm, tm), 1)
    earlier = (u < v).astype(BF16)
    cnt = jnp.broadcast_to(jnp.sum(both, axis=1, keepdims=True), (EXPERT_ROWS, LANES))
    cnt_ref[...] = cnt
    upto = cnt
    row_l = lax.broadcasted_iota(jnp.int32, (EXPERT_ROWS, LANES), 0)
    shift = 1
    while shift < n_experts:
        upto = upto + jnp.where(row_l >= shift, pltpu.roll(upto, shift, axis=0), 0.0)
        shift *= 2
    before = jnp.dot(both.astype(BF16), earlier, preferred_element_type=F32) + (upto - cnt)[:, 0:1]
    r0 = jnp.sum(jnp.where(hot0, before, 0.0), axis=0, keepdims=True)
    r1 = jnp.sum(jnp.where(hot1, before, 0.0), axis=0, keepdims=True)
    fields = {INFO_S0: r0 * SUBLANES, INFO_S1: r1 * SUBLANES, INFO_W0: w0, INFO_W1: w1}
    info_ref[...] = jnp.concatenate(
        [fields.get(k, jnp.zeros((1, tm), F32)) for k in range(SUBLANES)], axis=0)


def _oproj_router(ot, x, w_o, b_o, g, w_router, tm):
    n, d = x.shape
    dq, seq = ot.shape[1:]
    per_seq = seq // tm
    n_experts = w_router.shape[1]
    assert n_experts <= EXPERT_ROWS
    wrt = jnp.pad(w_router.astype(F32).T, ((0, EXPERT_ROWS - n_experts), (0, 0)))
    wrt_hi = wrt.astype(BF16)
    wrt_lo = (wrt - wrt_hi.astype(F32)).astype(BF16)
    wrt = jnp.concatenate([wrt_hi, wrt_lo], axis=0)
    return pl.pallas_call(
        functools.partial(_oproj_router_kernel, n_experts=n_experts),
        out_shape=(jax.ShapeDtypeStruct((n, d), F32),
                   jax.ShapeDtypeStruct((n // tm, SUBLANES, tm), F32),
                   jax.ShapeDtypeStruct((n // tm, EXPERT_ROWS, LANES), F32)),
        grid=(n // tm,),
        in_specs=[
            pl.BlockSpec((None, dq, tm), lambda i: (i // per_seq, 0, i % per_seq)),
            pl.BlockSpec((tm, d), lambda i: (i, 0)),
            _const_spec(w_o.shape),
            _const_spec((1, d)),
            _const_spec((1, d)),
            _const_spec(wrt.shape),
        ],
        out_specs=(pl.BlockSpec((tm, d), lambda i: (i, 0)),
                   pl.BlockSpec((None, SUBLANES, tm), lambda i: (i, 0, 0)),
                   pl.BlockSpec((None, EXPERT_ROWS, LANES), lambda i: (i, 0, 0))),
        compiler_params=_params(1),
        name="l1_oproj_router",
    )(ot, x, w_o, b_o.reshape(1, d), g.reshape(1, d), wrt)


def _pieces(n, n_bits, fn):
    for bit in reversed(range(n_bits)):
        size = 1 << bit
        done = n & ~((size << 1) - 1)
        pl.when((n & size) != 0)(functools.partial(fn, done, size))


def _rows(ref, first_row, n_rows):
    return ref.at[pl.ds(pl.multiple_of(first_row * SUBLANES, SUBLANES), n_rows * SUBLANES)]


def _segment_copies(tabs, i, n_experts, lbuf, hbm, sem, to_hbm, act):
    n_tab, o_tab, d_tab = tabs[:3]
    n_bits = (lbuf.shape[0] // SUBLANES // TOP_K).bit_length()
    for e in range(n_experts):
        local0 = o_tab[i * n_experts + e]
        sorted0 = d_tab[i * n_experts + e]

        def piece(done, size, local0=local0, sorted0=sorted0):
            local = _rows(lbuf, local0 + done, size)
            remote = _rows(hbm, sorted0 + done, size)
            act(pltpu.make_async_copy(local, remote, sem) if to_hbm
                else pltpu.make_async_copy(remote, local, sem))

        _pieces(n_tab[i * n_experts + e], n_bits, piece)


def _dispatch_kernel(n_tab, o_tab, d_tab, pad0_tab, padn_tab, nv_tab, slot_ref, x_ref, g_ref, xs_hbm,
                     hbuf, lbuf, sems, *, n_experts, tme):
    i = pl.program_id(0)
    tm = x_ref.shape[0]
    n_slots = TOP_K * tm
    h = _rms(x_ref[...], g_ref[...])
    for c in range(x_ref.shape[1] // LANES):
        hbuf[pl.ds(c, tm, stride=SUBLANES), :] = h[:, c * LANES:(c + 1) * LANES]
    slot = i % 2
    sorted_rows = lbuf.at[slot]

    def token(t, carry):
        row = hbuf[pl.ds(pl.multiple_of(t * SUBLANES, SUBLANES), SUBLANES), :]
        for k in range(TOP_K):
            dst = pl.multiple_of(slot_ref[k * tm + t], SUBLANES)
            sorted_rows[pl.ds(dst, SUBLANES), :] = row
        return carry

    lax.fori_loop(0, tm, token, 0, unroll=ROW_LOOP_UNROLL)
    tabs = (n_tab, o_tab, d_tab)

    def copies(tile, s, act):
        _segment_copies(tabs, tile, n_experts, lbuf.at[s], xs_hbm, sems.at[s], True, act)

    copies(i, slot, lambda cp: cp.start())

    def wait_all(s):
        pltpu.make_async_copy(lbuf.at[s], _rows(xs_hbm, 0, n_slots), sems.at[s]).wait()

    @pl.when(i > 0)
    def _():
        wait_all(1 - slot)

    @pl.when(i == pl.num_programs(0) - 1)
    def _():
        wait_all(slot)
        zeros = lbuf.at[slot]
        zeros[0:tme * SUBLANES, :] = jnp.zeros((tme * SUBLANES, LANES), F32)

        def pads(act):
            for e in range(n_experts):
                def piece(done, size, e=e):
                    act(pltpu.make_async_copy(_rows(zeros, 0, size),
                                              _rows(xs_hbm, pad0_tab[e] + done, size), sems.at[slot]))
                _pieces(padn_tab[e], (tme - 1).bit_length(), piece)
            n_tiles = xs_hbm.shape[0] // (tme * SUBLANES)
            for t in range(n_experts):
                tile = nv_tab[0] + t
                pl.when(tile < n_tiles)(lambda tile=tile: act(pltpu.make_async_copy(
                    _rows(zeros, 0, tme), _rows(xs_hbm, tile * tme, tme), sems.at[slot])))

        pads(lambda cp: cp.start())
        pads(lambda cp: cp.wait())


def _dispatch(x, g, slots, tabs, n_experts, n_rows, tm, tme):
    n, d = x.shape
    assert TOP_K * tm >= tme
    grid_spec = pltpu.PrefetchScalarGridSpec(
        num_scalar_prefetch=len(tabs),
        grid=(n // tm,),
        in_specs=[
            pl.BlockSpec((TOP_K * tm,), lambda i, *_: (i,), memory_space=pltpu.SMEM),
            pl.BlockSpec((tm, d), lambda i, *_: (i, 0)),
            pl.BlockSpec((1, d), lambda i, *_: (0, 0)),
        ],
        out_specs=pl.BlockSpec(memory_space=pl.ANY),
        scratch_shapes=[pltpu.VMEM((tm * SUBLANES, LANES), F32),
                        pltpu.VMEM((2, TOP_K * tm * SUBLANES, LANES), F32),
                        pltpu.SemaphoreType.DMA((2,))],
    )
    return pl.pallas_call(
        functools.partial(_dispatch_kernel, n_experts=n_experts, tme=tme),
        out_shape=jax.ShapeDtypeStruct((n_rows * SUBLANES, LANES), F32),
        grid_spec=grid_spec,
        compiler_params=_params(1),
        name="l1_moe_dispatch",
    )(*tabs, slots, x, g.reshape(1, d))


def _experts_kernel(te_ref, tv_ref, nv_ref, xs_ref, wg_ref, wu_ref, wd_ref, ys_ref, t_scr, *, sub):
    del te_ref, nv_ref
    j = pl.program_id(0)
    tm = t_scr.shape[0]
    n_slabs = wg_ref.shape[0] // LANES

    @pl.when(tv_ref[j] == 1)
    def _():
        slabs = [xs_ref[pl.ds(s, tm, stride=SUBLANES), :] for s in range(n_slabs)]
        x = jnp.concatenate(slabs, axis=1).astype(BF16)

        def body(q, carry):
            off = pl.multiple_of(q * sub, sub)
            a = jnp.dot(x, wg_ref[:, pl.ds(off, sub)], preferred_element_type=F32)
            b = jnp.dot(x, wu_ref[:, pl.ds(off, sub)], preferred_element_type=F32)
            t_scr[:, pl.ds(off, sub)] = (a * jax.nn.sigmoid(a) * b).astype(BF16)
            return carry
        lax.fori_loop(0, wg_ref.shape[1] // sub, body, 0, unroll=True)
        y = jnp.dot(t_scr[...], wd_ref[...], preferred_element_type=F32)
        for s in range(n_slabs):
            ys_ref[pl.ds(s, tm, stride=SUBLANES), :] = y[:, s * LANES:(s + 1) * LANES]

    @pl.when(tv_ref[j] != 1)
    def _():
        ys_ref[...] = jnp.zeros_like(ys_ref)


def _experts(xs, tile_expert, tile_valid, n_valid, w_gate, w_up, w_down, d, tm, sub):
    n_tiles = tile_expert.shape[0]
    ff = w_gate.shape[2]
    grid_spec = pltpu.PrefetchScalarGridSpec(
        num_scalar_prefetch=3,
        grid=(n_tiles,),
        in_specs=[
            pl.BlockSpec((tm * SUBLANES, LANES), lambda j, te, tv, nv: (jnp.minimum(j, nv[0] - 1), 0)),
            pl.BlockSpec((None, d, ff), lambda j, te, tv, nv: (te[j], 0, 0)),
            pl.BlockSpec((None, d, ff), lambda j, te, tv, nv: (te[j], 0, 0)),
            pl.BlockSpec((None, ff, d), lambda j, te, tv, nv: (te[j], 0, 0)),
        ],
        out_specs=pl.BlockSpec((tm * SUBLANES, LANES), lambda j, te, tv, nv: (j, 0)),
        scratch_shapes=[pltpu.VMEM((tm, ff), BF16)],
    )
    return pl.pallas_call(
        functools.partial(_experts_kernel, sub=sub),
        out_shape=jax.ShapeDtypeStruct(xs.shape, F32),
        grid_spec=grid_spec,
        compiler_params=_params(1, vmem=EXPERT_VMEM_LIMIT),
        name="l1_moe_experts",
    )(tile_expert, tile_valid, n_valid, xs, w_gate, w_up, w_down)


def _combine_kernel(n_tab, o_tab, d_tab, slot_ref, w_ref, x_ref, g_ref, ys_hbm, o_ref, lbuf, mbuf, sems,
                    *, n_experts):
    i = pl.program_id(0)
    tm = x_ref.shape[0]
    n_slots = TOP_K * tm
    tabs = (n_tab, o_tab, d_tab)
    slot = i % 2

    def copies(tile, s, act):
        _segment_copies(tabs, tile, n_experts, lbuf.at[s], ys_hbm, sems.at[s], False, act)

    @pl.when(i == 0)
    def _():
        copies(i, slot, lambda cp: cp.start())

    @pl.when(i + 1 < pl.num_programs(0))
    def _():
        copies(i + 1, 1 - slot, lambda cp: cp.start())

    pltpu.make_async_copy(_rows(ys_hbm, 0, n_slots), lbuf.at[slot], sems.at[slot]).wait()
    y = lbuf.at[slot]

    def token(t, carry):
        acc = None
        for k in range(TOP_K):
            row = y[pl.ds(pl.multiple_of(slot_ref[k * tm + t], SUBLANES), SUBLANES), :]
            term = w_ref[k * tm + t] * row
            acc = term if acc is None else acc + term
        mbuf[pl.ds(pl.multiple_of(t * SUBLANES, SUBLANES), SUBLANES), :] = acc
        return carry

    lax.fori_loop(0, tm, token, 0, unroll=ROW_LOOP_UNROLL)
    moe = jnp.concatenate([mbuf[pl.ds(c, tm, stride=SUBLANES), :]
                           for c in range(x_ref.shape[1] // LANES)], axis=1)
    o_ref[...] = _rms(x_ref[...] + moe, g_ref[...])


def _combine(x, slots, weights, tabs, n_experts, ys, g, tm):
    n, d = x.shape
    grid_spec = pltpu.PrefetchScalarGridSpec(
        num_scalar_prefetch=3,
        grid=(n // tm,),
        in_specs=[
            pl.BlockSpec((TOP_K * tm,), lambda i, *_: (i,), memory_space=pltpu.SMEM),
            pl.BlockSpec((TOP_K * tm,), lambda i, *_: (i,), memory_space=pltpu.SMEM),
            pl.BlockSpec((tm, d), lambda i, *_: (i, 0)),
            pl.BlockSpec((1, d), lambda i, *_: (0, 0)),
            pl.BlockSpec(memory_space=pl.ANY),
        ],
        out_specs=pl.BlockSpec((tm, d), lambda i, *_: (i, 0)),
        scratch_shapes=[pltpu.VMEM((2, TOP_K * tm * SUBLANES, LANES), F32),
                        pltpu.VMEM((tm * SUBLANES, LANES), F32),
                        pltpu.SemaphoreType.DMA((2,))],
    )
    return pl.pallas_call(
        functools.partial(_combine_kernel, n_experts=n_experts),
        out_shape=jax.ShapeDtypeStruct((n, d), F32),
        grid_spec=grid_spec,
        compiler_params=_params(1),
        name="l1_moe_combine",
    )(*tabs, slots, weights, x, g.reshape(1, d), ys)


def _forward(x, rel_bias, ev_norm_mix, ev_w_in, ev_pool_w, ev_pool_scale, ev_conv_w, ev_w_out,
             ev_norm_ffn, ev_ffn_gate, ev_ffn_up, ev_ffn_down, od_norm_mix, od_w_qkv, od_b_qkv,
             od_sinks, od_w_o, od_b_o, od_norm_ffn, od_router, od_exp_gate, od_exp_up,
             od_exp_down, final_norm, *, tm, tq, tme, fc_dense):
    b, s, d = x.shape
    n = b * s
    n_heads = od_sinks.shape[-1]
    dq = n_heads * HEAD_DIM
    dk = N_KV_HEADS * HEAD_DIM
    n_experts = od_router.shape[-1]

    x = _mixer(x, ev_norm_mix[0], ev_w_in[0].astype(BF16), ev_pool_w[0].astype(BF16),
               ev_pool_scale[0], ev_conv_w[0], ev_w_out[0].astype(BF16), tm)
    x = x.reshape(n, d)
    ff = od_exp_gate.shape[-1]
    expert_f32 = (od_exp_gate[0].reshape(n_experts * d, ff), od_exp_up[0].reshape(n_experts * d, ff),
                  od_exp_down[0].reshape(n_experts * ff, d))
    x, (exp_gate, exp_up, exp_down) = _ffn(
        x, ev_norm_ffn[0], ev_ffn_gate[0].astype(BF16), ev_ffn_up[0].astype(BF16),
        ev_ffn_down[0].astype(BF16), expert_f32, tm, fc_dense)

    qt, k, vt = _qkv(x.reshape(b, s, d), od_norm_mix[0], od_w_qkv[0], od_b_qkv[0], dq, dk, tm)
    bias = _rel_bias_table(rel_bias)
    attn_t = _attention(qt, k, vt, bias, od_sinks[0], tq)
    x, info, counts = _oproj_router(attn_t, x, od_w_o[0].astype(BF16), od_b_o[0], od_norm_ffn[0],
                                    od_router[0], tm)

    per_tile = counts[:, :n_experts, 0].astype(jnp.int32)
    local_off = jnp.cumsum(per_tile, axis=1) - per_tile
    earlier = jnp.cumsum(per_tile, axis=0) - per_tile
    totals = jnp.sum(per_tile, axis=0)
    tiles_per = (totals + tme - 1) // tme
    tile_end = jnp.cumsum(tiles_per)
    group_start = (tile_end - tiles_per) * tme
    n_tiles = (TOP_K * n) // tme + n_experts
    tile_ids = jnp.arange(n_tiles, dtype=jnp.int32)
    tile_valid = (tile_ids < tile_end[-1]).astype(jnp.int32)
    last_valid = jnp.minimum(tile_ids, tile_end[-1] - 1)
    tile_expert = jnp.sum(last_valid[:, None] >= tile_end[None, :], axis=1).astype(jnp.int32)
    tabs = (per_tile.reshape(-1), local_off.reshape(-1), (group_start[None, :] + earlier).reshape(-1))
    pad_tabs = (group_start + totals, tiles_per * tme - totals, tile_end[-1:])

    slots = info[:, INFO_S0:INFO_S1 + 1, :].astype(jnp.int32).reshape(-1)
    weights = info[:, INFO_W0:INFO_W1 + 1, :].reshape(-1)

    xs = _dispatch(x, od_norm_ffn[0], slots, tabs + pad_tabs, n_experts, n_tiles * tme, tm, tme)
    ys = _experts(xs, tile_expert, tile_valid, tile_end[-1:], exp_gate.reshape(n_experts, d, ff),
                  exp_up.reshape(n_experts, d, ff), exp_down.reshape(n_experts, ff, d), d, tme, fc_dense)
    out = _combine(x, slots, weights, tabs, n_experts, ys, final_norm, tm)
    return out.reshape(b, s, d)


def kernel(x, rel_bias, ev_norm_mix, ev_w_in, ev_pool_w, ev_pool_scale, ev_conv_w, ev_w_out, ev_norm_ffn, ev_ffn_gate, ev_ffn_up, ev_ffn_down, od_norm_mix, od_w_qkv, od_b_qkv, od_sinks, od_w_o, od_b_o, od_norm_ffn, od_router, od_exp_gate, od_exp_up, od_exp_down, final_norm):
    return _forward(x, rel_bias, ev_norm_mix, ev_w_in, ev_pool_w, ev_pool_scale, ev_conv_w, ev_w_out,
                    ev_norm_ffn, ev_ffn_gate, ev_ffn_up, ev_ffn_down, od_norm_mix, od_w_qkv, od_b_qkv,
                    od_sinks, od_w_o, od_b_o, od_norm_ffn, od_router, od_exp_gate, od_exp_up,
                    od_exp_down, final_norm, tm=512, tq=512, tme=512, fc_dense=256)
```

```python
import functools
import math

import jax
import jax.numpy as jnp
from jax import lax
from jax.experimental import pallas as pl
from jax.experimental.pallas import tpu as pltpu

F32 = jnp.float32
BF16 = jnp.bfloat16

RMS_EPS = 1e-6
CHUNK = 64
POOL_WINDOWS = (2, 4, 8, 16)
CONV_K = 3
HEAD_DIM = 64
N_KV_HEADS = 2
WINDOW_CHUNKS = 2
KV_SPAN = (WINDOW_CHUNKS + 1) * CHUNK
N_BUCKETS = 32
MAX_DISTANCE = 128
TOP_K = 2
NEG_INF = -1e30

LANES = 128
SUBLANES = 8
VMEM_LIMIT = 56 * 1024 * 1024
EXPERT_VMEM_LIMIT = 62 * 1024 * 1024

POOL_HALO = 16
CONV_HALO = 8
ROW_LOOP_UNROLL = 16

def _rms(x, g):
    return x * lax.rsqrt(jnp.mean(x * x, axis=-1, keepdims=True) + RMS_EPS) * g


def _params(n_axes, vmem=VMEM_LIMIT):
    return pltpu.CompilerParams(dimension_semantics=("arbitrary",) * n_axes,
                                vmem_limit_bytes=vmem)


def _const_spec(shape):
    nd = len(shape)
    return pl.BlockSpec(shape, lambda *_: (0,) * nd, pipeline_mode=pl.Buffered(1))


def _mixer_kernel(x_ref, g_ref, win_ref, pw_ref, ps_ref, cw_ref, wout_ref, o_ref, ubuf, zbuf):
    s = pl.program_id(1)
    tm = x_ref.shape[0]
    pool_w = ubuf.shape[1]
    grp = pool_w // len(POOL_WINDOWS)
    conv_c = zbuf.shape[1]

    @pl.when(s == 0)
    def _():
        ubuf[0:POOL_HALO, :] = jnp.zeros((POOL_HALO, pool_w), F32)
        zbuf[0:CONV_HALO, :] = jnp.zeros((CONV_HALO, conv_c), F32)

    @pl.when(s > 0)
    def _():
        ubuf[0:POOL_HALO, :] = ubuf[tm:tm + POOL_HALO, :]
        zbuf[0:CONV_HALO, :] = zbuf[tm:tm + CONV_HALO, :]

    x = x_ref[...]
    h = _rms(x, g_ref[...]).astype(BF16)
    proj = jnp.dot(h, win_ref[...], preferred_element_type=F32)
    u = proj[:, :pool_w]
    gate_post = proj[:, pool_w:pool_w + conv_c]
    gate_pre = proj[:, pool_w + conv_c:pool_w + 2 * conv_c]
    v = proj[:, pool_w + 2 * conv_c:]
    ubuf[POOL_HALO:POOL_HALO + tm, :] = u
    z = gate_pre * v
    zbuf[CONV_HALO:CONV_HALO + tm, :] = z

    t = s * tm + lax.broadcasted_iota(jnp.int32, (tm, 1), 0)
    parts = []
    for g, w in enumerate(POOL_WINDOWS):
        lo, hi = g * grp, (g + 1) * grp
        ug = u[:, lo:hi]
        acc = ug
        for k in range(1, w):
            acc = acc + ubuf[POOL_HALO - k:POOL_HALO - k + tm, lo:hi]
        count = jnp.minimum(t + 1, w).astype(F32)
        pooled = acc / count - ug
        mixed = jnp.dot(pooled.astype(BF16), pw_ref[g], preferred_element_type=F32)
        parts.append(mixed * ps_ref[:, lo:hi])
    conv = cw_ref[CONV_K - 1:CONV_K, :] * z
    for k in range(CONV_K - 1):
        off = CONV_HALO - (CONV_K - 1) + k
        conv = conv + cw_ref[k:k + 1, :] * zbuf[off:off + tm, :]
    parts.append(gate_post * conv)
    y = jnp.concatenate(parts, axis=1).astype(BF16)
    o_ref[...] = x + jnp.dot(y, wout_ref[...], preferred_element_type=F32)


def _mixer(x, g, w_in, pool_w, pool_scale, conv_w, w_out, tm):
    b, s, d = x.shape
    pool_width = pool_scale.shape[-1]
    conv_c = conv_w.shape[-1]
    return pl.pallas_call(
        _mixer_kernel,
        out_shape=jax.ShapeDtypeStruct((b, s, d), F32),
        grid=(b, s // tm),
        in_specs=[
            pl.BlockSpec((None, tm, d), lambda i, j: (i, j, 0)),
            _const_spec((1, d)),
            _const_spec(w_in.shape),
            _const_spec(pool_w.shape),
            _const_spec((1, pool_width)),
            _const_spec(conv_w.shape),
            _const_spec(w_out.shape),
        ],
        out_specs=pl.BlockSpec((None, tm, d), lambda i, j: (i, j, 0)),
        scratch_shapes=[pltpu.VMEM((POOL_HALO + tm, pool_width), F32),
                        pltpu.VMEM((CONV_HALO + tm, conv_c), F32)],
        compiler_params=_params(2),
        name="l0_mixer",
    )(x, g.reshape(1, d), w_in, pool_w, pool_scale.reshape(1, pool_width), conv_w, w_out)


def _ffn_kernel(x_ref, g_ref, wg_ref, wu_ref, wd_ref, *rest, fc, n_cast):
    cast_in, (o_ref, *cast_out), (h_scr, t_scr) = rest[:n_cast], rest[n_cast:2 * n_cast + 1], rest[2 * n_cast + 1:]
    x = x_ref[...]
    h_scr[...] = _rms(x, g_ref[...]).astype(BF16)
    n_chunks = wg_ref.shape[1] // fc

    def body(c, carry):
        off = pl.multiple_of(c * fc, fc)
        h = h_scr[...]
        a = jnp.dot(h, wg_ref[:, pl.ds(off, fc)], preferred_element_type=F32)
        b = jnp.dot(h, wu_ref[:, pl.ds(off, fc)], preferred_element_type=F32)
        t_scr[:, pl.ds(off, fc)] = (a * jax.nn.sigmoid(a) * b).astype(BF16)
        return carry

    lax.fori_loop(0, n_chunks, body, 0, unroll=True)
    o_ref[...] = x + jnp.dot(t_scr[...], wd_ref[...], preferred_element_type=F32)
    for src, dst in zip(cast_in, cast_out):
        dst[...] = src[...].astype(BF16)


def _ffn(x, g, w_gate, w_up, w_down, to_cast, tm, fc):
    n, d = x.shape
    steps = n // tm
    slabs = []
    for w in to_cast:
        rows = w.shape[0] // steps
        assert rows * steps == w.shape[0] and rows % (2 * SUBLANES) == 0, w.shape
        slabs.append(pl.BlockSpec((rows, w.shape[1]), lambda i: (i, 0)))
    outs = pl.pallas_call(
        functools.partial(_ffn_kernel, fc=fc, n_cast=len(to_cast)),
        out_shape=(jax.ShapeDtypeStruct((n, d), F32),
                   *[jax.ShapeDtypeStruct(w.shape, BF16) for w in to_cast]),
        grid=(steps,),
        in_specs=[
            pl.BlockSpec((tm, d), lambda i: (i, 0)),
            _const_spec((1, d)),
            _const_spec(w_gate.shape),
            _const_spec(w_up.shape),
            _const_spec(w_down.shape),
            *slabs,
        ],
        out_specs=(pl.BlockSpec((tm, d), lambda i: (i, 0)), *slabs),
        scratch_shapes=[pltpu.VMEM((tm, d), BF16), pltpu.VMEM((tm, w_gate.shape[1]), BF16)],
        compiler_params=_params(1),
        name="l0_ffn",
    )(x, g.reshape(1, d), w_gate, w_up, w_down, *to_cast)
    return outs[0], outs[1:]


_NT = (((1,), (1,)), ((), ()))
_TN = (((0,), (0,)), ((), ()))


def _qkv_kernel(x_ref, g_ref, wqt_ref, bq_ref, wk_ref, bk_ref, wvt_ref, bv_ref, qt_ref, k_ref, vt_ref):
    h = _rms(x_ref[...], g_ref[...]).astype(BF16)
    qt = lax.dot_general(wqt_ref[...], h, _NT, preferred_element_type=F32) + bq_ref[...]
    qt_ref[...] = (qt * (HEAD_DIM ** -0.5)).astype(BF16)
    k_ref[...] = (jnp.dot(h, wk_ref[...], preferred_element_type=F32) + bk_ref[...]).astype(BF16)
    vt = lax.dot_general(wvt_ref[...], h, _NT, preferred_element_type=F32) + bv_ref[...]
    vt_ref[...] = vt.astype(BF16)


def _qkv(x, g, w_qkv, b_qkv, dq, dk, tm):
    b, s, d = x.shape
    wqt = w_qkv[:, :dq].T.astype(BF16)
    wk = w_qkv[:, dq:dq + dk].astype(BF16)
    wvt = w_qkv[:, dq + dk:].T.astype(BF16)
    bq = b_qkv[:dq].reshape(dq, 1)
    bk = b_qkv[dq:dq + dk].reshape(1, dk)
    bv = b_qkv[dq + dk:].reshape(dk, 1)
    return pl.pallas_call(
        _qkv_kernel,
        out_shape=(jax.ShapeDtypeStruct((b, dq, s), BF16),
                   jax.ShapeDtypeStruct((b, s, dk), BF16),
                   jax.ShapeDtypeStruct((b, dk, s), BF16)),
        grid=(b, s // tm),
        in_specs=[
            pl.BlockSpec((None, tm, d), lambda bi, i: (bi, i, 0)),
            _const_spec((1, d)),
            _const_spec(wqt.shape), _const_spec(bq.shape),
            _const_spec(wk.shape), _const_spec(bk.shape),
            _const_spec(wvt.shape), _const_spec(bv.shape),
        ],
        out_specs=(pl.BlockSpec((None, dq, tm), lambda bi, i: (bi, 0, i)),
                   pl.BlockSpec((None, tm, dk), lambda bi, i: (bi, i, 0)),
                   pl.BlockSpec((None, dk, tm), lambda bi, i: (bi, 0, i))),
        compiler_params=_params(2),
        name="l1_qkv",
    )(x, g.reshape(1, d), wqt, bq, wk, bk, wvt, bv)


def _bias_kernel(onehot_ref, rb_ref, o_ref):
    o_ref[...] = jnp.dot(onehot_ref[...], rb_ref[...], preferred_element_type=F32,
                         precision=lax.Precision.HIGHEST)


def _t5_bucket(rel):
    nb = N_BUCKETS // 2
    max_exact = nb // 2
    ret = jnp.where(rel > 0, nb, 0)
    n = jnp.abs(rel)
    nf = jnp.maximum(n, 1).astype(jnp.float32)
    large = max_exact + (jnp.log(nf / max_exact) / math.log(MAX_DISTANCE / max_exact)
                         * (nb - max_exact)).astype(jnp.int32)
    large = jnp.minimum(large, nb - 1)
    return ret + jnp.where(n < max_exact, n, large)


def _rel_bias_table(rel_bias):
    n_heads = rel_bias.shape[1]
    pad = WINDOW_CHUNKS * CHUNK
    rel = (jnp.arange(KV_SPAN) - pad)[None, :] - jnp.arange(CHUNK)[:, None]
    bucket = _t5_bucket(rel).reshape(-1)
    onehot = (bucket[:, None] == jnp.arange(N_BUCKETS)[None, :]).astype(F32)
    rows = CHUNK * KV_SPAN
    rb = jnp.pad(rel_bias.astype(F32), ((0, 0), (0, LANES - n_heads)))
    out = pl.pallas_call(
        _bias_kernel,
        out_shape=jax.ShapeDtypeStruct((rows, LANES), F32),
        name="l1_rel_bias",
    )(onehot, rb)
    return jnp.transpose(out[:, :n_heads].reshape(CHUNK, KV_SPAN, n_heads), (2, 0, 1))


PAIR = 2 * CHUNK
PAIR_KEYS = KV_SPAN + CHUNK


def _attn_kernel(qt_ref, kp_ref, km_ref, vtp_ref, vtm_ref, bias_ref, sink_ref, ot_ref, kbuf, vtbuf,
                 s_a, s_b, *, n_heads):
    i = pl.program_id(1)
    tq = qt_ref.shape[1]
    pad = WINDOW_CHUNKS * CHUNK
    per_kv = n_heads // N_KV_HEADS
    kbuf[0:pad, :] = kp_ref[...]
    kbuf[pad:pad + tq, :] = km_ref[...]
    vtbuf[:, 0:pad] = vtp_ref[...]
    vtbuf[:, pad:pad + tq] = vtm_ref[...]
    zeros = jnp.zeros((HEAD_DIM, PAIR), BF16)
    first = lax.broadcasted_iota(jnp.int32, (CHUNK, PAIR), 1) < CHUNK
    n_pairs = tq // PAIR

    def scores(jj, s_scr):
        c0 = jj * PAIR
        kc = kbuf[c0:c0 + PAIR_KEYS, :]
        for h in range(n_heads):
            pieces = [zeros] * N_KV_HEADS
            pieces[h // per_kv] = qt_ref[h * HEAD_DIM:(h + 1) * HEAD_DIM, c0:c0 + PAIR]
            rhs = jnp.concatenate(pieces, axis=0)
            s_scr[h] = jnp.dot(kc, rhs, preferred_element_type=F32)

    def softmax_values(jj, s_scr):
        c0 = jj * PAIR
        variant = jnp.where(i == 0, 1, 0) if jj == 0 else 0
        for h in range(n_heads):
            kv = h // per_kv
            mid = s_scr[h, CHUNK:KV_SPAN, :] + bias_ref[variant, h, 0:KV_SPAN - CHUNK, :]
            edge = (jnp.where(first, s_scr[h, 0:CHUNK, :], s_scr[h, KV_SPAN:PAIR_KEYS, :])
                    + bias_ref[variant, h, KV_SPAN - CHUNK:KV_SPAN, :])
            sk = sink_ref[h]
            m_ = jnp.maximum(jnp.maximum(jnp.max(mid, axis=0, keepdims=True),
                                         jnp.max(edge, axis=0, keepdims=True)), sk)
            p_mid = jnp.exp(mid - m_)
            p_edge = jnp.exp(edge - m_)
            denom = (jnp.sum(p_mid, axis=0, keepdims=True) + jnp.sum(p_edge, axis=0, keepdims=True)
                     + jnp.exp(sk - m_))
            p_mid = (p_mid / denom).astype(BF16)
            p_edge = (p_edge / denom).astype(BF16)
            zero = jnp.zeros_like(p_edge)
            p = jnp.concatenate([jnp.where(first, p_edge, zero), p_mid, jnp.where(first, zero, p_edge)],
                                axis=0)
            vt = vtbuf[kv * HEAD_DIM:(kv + 1) * HEAD_DIM, c0:c0 + PAIR_KEYS]
            o = jnp.dot(vt, p, preferred_element_type=F32)
            ot_ref[h * HEAD_DIM:(h + 1) * HEAD_DIM, c0:c0 + PAIR] = o.astype(BF16)

    bufs = (s_a, s_b)
    scores(0, bufs[0])
    for jj in range(n_pairs):
        if jj + 1 < n_pairs:
            scores(jj + 1, bufs[(jj + 1) % 2])
        softmax_values(jj, bufs[jj % 2])


def _pair_bias(bias):
    bt = jnp.transpose(bias, (0, 2, 1)).astype(F32)
    mid = jnp.concatenate([bt[:, CHUNK:, :], bt[:, :KV_SPAN - CHUNK, :]], axis=2)
    edge = jnp.concatenate([bt[:, :CHUNK, :], bt[:, KV_SPAN - CHUNK:, :]], axis=2)
    table = jnp.concatenate([mid, edge], axis=1)
    row = jnp.arange(KV_SPAN)[:, None]
    lane = jnp.arange(PAIR)[None, :]
    key = jnp.where(row < KV_SPAN - CHUNK, row + CHUNK,
                    jnp.where(lane < CHUNK, row - (KV_SPAN - CHUNK), row + CHUNK))
    masked = jnp.where(key[None] < WINDOW_CHUNKS * CHUNK, NEG_INF, table)
    return jnp.stack([table, masked])


def _attention(qt, k, vt, bias, sinks, tq):
    b, dq, s = qt.shape
    dk = k.shape[-1]
    n_heads = dq // HEAD_DIM
    pad = WINDOW_CHUNKS * CHUNK
    ratio = tq // pad
    table = _pair_bias(bias)
    prev = lambda i: jnp.maximum(i * ratio - 1, 0)
    return pl.pallas_call(
        functools.partial(_attn_kernel, n_heads=n_heads),
        out_shape=jax.ShapeDtypeStruct((b, dq, s), BF16),
        grid=(b, s // tq),
        in_specs=[
            pl.BlockSpec((None, dq, tq), lambda bi, i: (bi, 0, i)),
            pl.BlockSpec((None, pad, dk), lambda bi, i: (bi, prev(i), 0)),
            pl.BlockSpec((None, tq, dk), lambda bi, i: (bi, i, 0)),
            pl.BlockSpec((None, dk, pad), lambda bi, i: (bi, 0, prev(i))),
            pl.BlockSpec((None, dk, tq), lambda bi, i: (bi, 0, i)),
            _const_spec(table.shape),
            pl.BlockSpec(memory_space=pltpu.SMEM),
        ],
        out_specs=pl.BlockSpec((None, dq, tq), lambda bi, i: (bi, 0, i)),
        scratch_shapes=[pltpu.VMEM((pad + tq, dk), BF16), pltpu.VMEM((dk, pad + tq), BF16),
                        pltpu.VMEM((n_heads, PAIR_KEYS, PAIR), F32),
                        pltpu.VMEM((n_heads, PAIR_KEYS, PAIR), F32)],
        compiler_params=_params(2),
        name="l1_attention",
    )(qt, k, k, vt, vt, table, sinks.astype(F32))


INFO_S0, INFO_S1, INFO_W0, INFO_W1 = range(4)


EXPERT_ROWS = 16


def _oproj_router_kernel(ot_ref, x_ref, wo_ref, bo_ref, g_ref, wrt_ref, x3_ref, info_ref, cnt_ref,
                         *, n_experts):
    tm = x_ref.shape[0]
    attn = lax.dot_general(ot_ref[...], wo_ref[...], _TN, preferred_element_type=F32)
    x3 = x_ref[...] + attn + bo_ref[...]
    x3_ref[...] = x3
    h = _rms(x3, g_ref[...])
    h_hi = h.astype(BF16)
    h_lo = (h - h_hi.astype(F32)).astype(BF16)
    wrt = wrt_ref[...]
    by_hi = lax.dot_general(wrt, h_hi, _NT, preferred_element_type=F32)
    by_lo = lax.dot_general(wrt[:EXPERT_ROWS], h_lo, _NT, preferred_element_type=F32)
    logits = (by_hi[:EXPERT_ROWS] + by_hi[EXPERT_ROWS:]) + by_lo
    row_i = lax.broadcasted_iota(jnp.int32, (EXPERT_ROWS, tm), 0)
    row = row_i.astype(F32)
    neg = jnp.float32(-jnp.inf)
    logits = jnp.where(row_i < n_experts, logits, neg)
    m0 = jnp.max(logits, axis=0, keepdims=True)
    e0 = jnp.min(jnp.where(logits == m0, row, float(EXPERT_ROWS)), axis=0, keepdims=True)
    rest = jnp.where(row == e0, neg, logits)
    m1 = jnp.max(rest, axis=0, keepdims=True)
    e1 = jnp.min(jnp.where(rest == m1, row, float(EXPERT_ROWS)), axis=0, keepdims=True)
    t = jnp.exp(m1 - m0)
    w0 = 1.0 / (1.0 + t)
    w1 = t / (1.0 + t)
    hot0 = row == e0
    hot1 = row == e1
    both = jnp.where(hot0 | hot1, 1.0, 0.0)
    u = lax.broadcasted_iota(jnp.int32, (tm, tm), 0)
    v = lax.broadcasted_iota(jnp.int32, (tm, tm), 1)
    earlier = (u < v).astype(BF16)
    cnt = jnp.broadcast_to(jnp.sum(both, axis=1, keepdims=True), (EXPERT_ROWS, LANES))
    cnt_ref[...] = cnt
    upto = cnt
    row_l = lax.broadcasted_iota(jnp.int32, (EXPERT_ROWS, LANES), 0)
    shift = 1
    while shift < n_experts:
        upto = upto + jnp.where(row_l >= shift, pltpu.roll(upto, shift, axis=0), 0.0)
        shift *= 2
    before = jnp.dot(both.astype(BF16), earlier, preferred_element_type=F32) + (upto - cnt)[:, 0:1]
    r0 = jnp.sum(jnp.where(hot0, before, 0.0), axis=0, keepdims=True)
    r1 = jnp.sum(jnp.where(hot1, before, 0.0), axis=0, keepdims=True)
    fields = {INFO_S0: r0 * SUBLANES, INFO_S1: r1 * SUBLANES, INFO_W0: w0, INFO_W1: w1}
    info_ref[...] = jnp.concatenate(
        [fields.get(k, jnp.zeros((1, tm), F32)) for k in range(SUBLANES)], axis=0)


def _oproj_router(ot, x, w_o, b_o, g, w_router, tm):
    n, d = x.shape
    dq, seq = ot.shape[1:]
    per_seq = seq // tm
    n_experts = w_router.shape[1]
    assert n_experts <= EXPERT_ROWS
    wrt = jnp.pad(w_router.astype(F32).T, ((0, EXPERT_ROWS - n_experts), (0, 0)))
    wrt_hi = wrt.astype(BF16)
    wrt_lo = (wrt - wrt_hi.astype(F32)).astype(BF16)
    wrt = jnp.concatenate([wrt_hi, wrt_lo], axis=0)
    return pl.pallas_call(
        functools.partial(_oproj_router_kernel, n_experts=n_experts),
        out_shape=(jax.ShapeDtypeStruct((n, d), F32),
                   jax.ShapeDtypeStruct((n // tm, SUBLANES, tm), F32),
                   jax.ShapeDtypeStruct((n // tm, EXPERT_ROWS, LANES), F32)),
        grid=(n // tm,),
        in_specs=[
            pl.BlockSpec((None, dq, tm), lambda i: (i // per_seq, 0, i % per_seq)),
            pl.BlockSpec((tm, d), lambda i: (i, 0)),
            _const_spec(w_o.shape),
            _const_spec((1, d)),
            _const_spec((1, d)),
            _const_spec(wrt.shape),
        ],
        out_specs=(pl.BlockSpec((tm, d), lambda i: (i, 0)),
                   pl.BlockSpec((None, SUBLANES, tm), lambda i: (i, 0, 0)),
                   pl.BlockSpec((None, EXPERT_ROWS, LANES), lambda i: (i, 0, 0))),
        compiler_params=_params(1),
        name="l1_oproj_router",
    )(ot, x, w_o, b_o.reshape(1, d), g.reshape(1, d), wrt)


def _pieces(n, n_bits, fn):
    for bit in reversed(range(n_bits)):
        size = 1 << bit
        done = n & ~((size << 1) - 1)
        pl.when((n & size) != 0)(functools.partial(fn, done, size))


def _rows(ref, first_row, n_rows):
    return ref.at[pl.ds(pl.multiple_of(first_row * SUBLANES, SUBLANES), n_rows * SUBLANES)]


def _segment_copies(tabs, i, n_experts, lbuf, hbm, sem, to_hbm, act):
    n_tab, o_tab, d_tab = tabs[:3]
    n_bits = (lbuf.shape[0] // SUBLANES // TOP_K).bit_length()
    for e in range(n_experts):
        local0 = o_tab[i * n_experts + e]
        sorted0 = d_tab[i * n_experts + e]

        def piece(done, size, local0=local0, sorted0=sorted0):
            local = _rows(lbuf, local0 + done, size)
            remote = _rows(hbm, sorted0 + done, size)
            act(pltpu.make_async_copy(local, remote, sem) if to_hbm
                else pltpu.make_async_copy(remote, local, sem))

        _pieces(n_tab[i * n_experts + e], n_bits, piece)


def _dispatch_kernel(n_tab, o_tab, d_tab, pad0_tab, padn_tab, nv_tab, slot_ref, x_ref, g_ref, xs_hbm,
                     hbuf, lbuf, sems, *, n_experts, tme):
    i = pl.program_id(0)
    tm = x_ref.shape[0]
    n_slots = TOP_K * tm
    h = _rms(x_ref[...], g_ref[...])
    for c in range(x_ref.shape[1] // LANES):
        hbuf[pl.ds(c, tm, stride=SUBLANES), :] = h[:, c * LANES:(c + 1) * LANES]
    slot = i % 2
    sorted_rows = lbuf.at[slot]

    def token(t, carry):
        row = hbuf[pl.ds(pl.multiple_of(t * SUBLANES, SUBLANES), SUBLANES), :]
        for k in range(TOP_K):
            dst = pl.multiple_of(slot_ref[k * tm + t], SUBLANES)
            sorted_rows[pl.ds(dst, SUBLANES), :] = row
        return carry

    lax.fori_loop(0, tm, token, 0, unroll=ROW_LOOP_UNROLL)
    tabs = (n_tab, o_tab, d_tab)

    def copies(tile, s, act):
        _segment_copies(tabs, tile, n_experts, lbuf.at[s], xs_hbm, sems.at[s], True, act)

    copies(i, slot, lambda cp: cp.start())

    def wait_all(s):
        pltpu.make_async_copy(lbuf.at[s], _rows(xs_hbm, 0, n_slots), sems.at[s]).wait()

    @pl.when(i > 0)
    def _():
        wait_all(1 - slot)

    @pl.when(i == pl.num_programs(0) - 1)
    def _():
        wait_all(slot)
        zeros = lbuf.at[slot]
        zeros[0:tme * SUBLANES, :] = jnp.zeros((tme * SUBLANES, LANES), F32)

        def pads(act):
            for e in range(n_experts):
                def piece(done, size, e=e):
                    act(pltpu.make_async_copy(_rows(zeros, 0, size),
                                              _rows(xs_hbm, pad0_tab[e] + done, size), sems.at[slot]))
                _pieces(padn_tab[e], (tme - 1).bit_length(), piece)
            n_tiles = xs_hbm.shape[0] // (tme * SUBLANES)
            for t in range(n_experts):
                tile = nv_tab[0] + t
                pl.when(tile < n_tiles)(lambda tile=tile: act(pltpu.make_async_copy(
                    _rows(zeros, 0, tme), _rows(xs_hbm, tile * tme, tme), sems.at[slot])))

        pads(lambda cp: cp.start())
        pads(lambda cp: cp.wait())


def _dispatch(x, g, slots, tabs, n_experts, n_rows, tm, tme):
    n, d = x.shape
    assert TOP_K * tm >= tme
    grid_spec = pltpu.PrefetchScalarGridSpec(
        num_scalar_prefetch=len(tabs),
        grid=(n // tm,),
        in_specs=[
            pl.BlockSpec((TOP_K * tm,), lambda i, *_: (i,), memory_space=pltpu.SMEM),
            pl.BlockSpec((tm, d), lambda i, *_: (i, 0)),
            pl.BlockSpec((1, d), lambda i, *_: (0, 0)),
        ],
        out_specs=pl.BlockSpec(memory_space=pl.ANY),
        scratch_shapes=[pltpu.VMEM((tm * SUBLANES, LANES), F32),
                        pltpu.VMEM((2, TOP_K * tm * SUBLANES, LANES), F32),
                        pltpu.SemaphoreType.DMA((2,))],
    )
    return pl.pallas_call(
        functools.partial(_dispatch_kernel, n_experts=n_experts, tme=tme),
        out_shape=jax.ShapeDtypeStruct((n_rows * SUBLANES, LANES), F32),
        grid_spec=grid_spec,
        compiler_params=_params(1),
        name="l1_moe_dispatch",
    )(*tabs, slots, x, g.reshape(1, d))


def _experts_kernel(te_ref, tv_ref, nv_ref, xs_ref, wg_ref, wu_ref, wd_ref, ys_ref, t_scr, *, sub):
    del te_ref, nv_ref
    j = pl.program_id(0)
    tm = t_scr.shape[0]
    n_slabs = wg_ref.shape[0] // LANES

    @pl.when(tv_ref[j] == 1)
    def _():
        slabs = [xs_ref[pl.ds(s, tm, stride=SUBLANES), :] for s in range(n_slabs)]
        x = jnp.concatenate(slabs, axis=1).astype(BF16)

        def body(q, carry):
            off = pl.multiple_of(q * sub, sub)
            a = jnp.dot(x, wg_ref[:, pl.ds(off, sub)], preferred_element_type=F32)
            b = jnp.dot(x, wu_ref[:, pl.ds(off, sub)], preferred_element_type=F32)
            t_scr[:, pl.ds(off, sub)] = (a * jax.nn.sigmoid(a) * b).astype(BF16)
            return carry
        lax.fori_loop(0, wg_ref.shape[1] // sub, body, 0, unroll=True)
        y = jnp.dot(t_scr[...], wd_ref[...], preferred_element_type=F32)
        for s in range(n_slabs):
            ys_ref[pl.ds(s, tm, stride=SUBLANES), :] = y[:, s * LANES:(s + 1) * LANES]

    @pl.when(tv_ref[j] != 1)
    def _():
        ys_ref[...] = jnp.zeros_like(ys_ref)


def _experts(xs, tile_expert, tile_valid, n_valid, w_gate, w_up, w_down, d, tm, sub):
    n_tiles = tile_expert.shape[0]
    ff = w_gate.shape[2]
    grid_spec = pltpu.PrefetchScalarGridSpec(
        num_scalar_prefetch=3,
        grid=(n_tiles,),
        in_specs=[
            pl.BlockSpec((tm * SUBLANES, LANES), lambda j, te, tv, nv: (jnp.minimum(j, nv[0] - 1), 0)),
            pl.BlockSpec((None, d, ff), lambda j, te, tv, nv: (te[j], 0, 0)),
            pl.BlockSpec((None, d, ff), lambda j, te, tv, nv: (te[j], 0, 0)),
            pl.BlockSpec((None, ff, d), lambda j, te, tv, nv: (te[j], 0, 0)),
        ],
        out_specs=pl.BlockSpec((tm * SUBLANES, LANES), lambda j, te, tv, nv: (j, 0)),
        scratch_shapes=[pltpu.VMEM((tm, ff), BF16)],
    )
    return pl.pallas_call(
        functools.partial(_experts_kernel, sub=sub),
        out_shape=jax.ShapeDtypeStruct(xs.shape, F32),
        grid_spec=grid_spec,
        compiler_params=_params(1, vmem=EXPERT_VMEM_LIMIT),
        name="l1_moe_experts",
    )(tile_expert, tile_valid, n_valid, xs, w_gate, w_up, w_down)


def _combine_kernel(n_tab, o_tab, d_tab, slot_ref, w_ref, x_ref, g_ref, ys_hbm, o_ref, lbuf, mbuf, sems,
                    *, n_experts):
    i = pl.program_id(0)
    tm = x_ref.shape[0]
    n_slots = TOP_K * tm
    tabs = (n_tab, o_tab, d_tab)
    slot = i % 2

    def copies(tile, s, act):
        _segment_copies(tabs, tile, n_experts, lbuf.at[s], ys_hbm, sems.at[s], False, act)

    @pl.when(i == 0)
    def _():
        copies(i, slot, lambda cp: cp.start())

    @pl.when(i + 1 < pl.num_programs(0))
    def _():
        copies(i + 1, 1 - slot, lambda cp: cp.start())

    pltpu.make_async_copy(_rows(ys_hbm, 0, n_slots), lbuf.at[slot], sems.at[slot]).wait()
    y = lbuf.at[slot]

    def token(t, carry):
        acc = None
        for k in range(TOP_K):
            row = y[pl.ds(pl.multiple_of(slot_ref[k * tm + t], SUBLANES), SUBLANES), :]
            term = w_ref[k * tm + t] * row
            acc = term if acc is None else acc + term
        mbuf[pl.ds(pl.multiple_of(t * SUBLANES, SUBLANES), SUBLANES), :] = acc
        return carry

    lax.fori_loop(0, tm, token, 0, unroll=ROW_LOOP_UNROLL)
    moe = jnp.concatenate([mbuf[pl.ds(c, tm, stride=SUBLANES), :]
                           for c in range(x_ref.shape[1] // LANES)], axis=1)
    o_ref[...] = _rms(x_ref[...] + moe, g_ref[...])


def _combine(x, slots, weights, tabs, n_experts, ys, g, tm):
    n, d = x.shape
    grid_spec = pltpu.PrefetchScalarGridSpec(
        num_scalar_prefetch=3,
        grid=(n // tm,),
        in_specs=[
            pl.BlockSpec((TOP_K * tm,), lambda i, *_: (i,), memory_space=pltpu.SMEM),
            pl.BlockSpec((TOP_K * tm,), lambda i, *_: (i,), memory_space=pltpu.SMEM),
            pl.BlockSpec((tm, d), lambda i, *_: (i, 0)),
            pl.BlockSpec((1, d), lambda i, *_: (0, 0)),
            pl.BlockSpec(memory_space=pl.ANY),
        ],
        out_specs=pl.BlockSpec((tm, d), lambda i, *_: (i, 0)),
        scratch_shapes=[pltpu.VMEM((2, TOP_K * tm * SUBLANES, LANES), F32),
                        pltpu.VMEM((tm * SUBLANES, LANES), F32),
                        pltpu.SemaphoreType.DMA((2,))],
    )
    return pl.pallas_call(
        functools.partial(_combine_kernel, n_experts=n_experts),
        out_shape=jax.ShapeDtypeStruct((n, d), F32),
        grid_spec=grid_spec,
        compiler_params=_params(1),
        name="l1_moe_combine",
    )(*tabs, slots, weights, x, g.reshape(1, d), ys)


def _forward(x, rel_bias, ev_norm_mix, ev_w_in, ev_pool_w, ev_pool_scale, ev_conv_w, ev_w_out,
             ev_norm_ffn, ev_ffn_gate, ev_ffn_up, ev_ffn_down, od_norm_mix, od_w_qkv, od_b_qkv,
             od_sinks, od_w_o, od_b_o, od_norm_ffn, od_router, od_exp_gate, od_exp_up,
             od_exp_down, final_norm, *, tm, tq, tme, fc_dense):
    b, s, d = x.shape
    n = b * s
    n_heads = od_sinks.shape[-1]
    dq = n_heads * HEAD_DIM
    dk = N_KV_HEADS * HEAD_DIM
    n_experts = od_router.shape[-1]

    x = _mixer(x, ev_norm_mix[0], ev_w_in[0].astype(BF16), ev_pool_w[0].astype(BF16),
               ev_pool_scale[0], ev_conv_w[0], ev_w_out[0].astype(BF16), tm)
    x = x.reshape(n, d)
    ff = od_exp_gate.shape[-1]
    expert_f32 = (od_exp_gate[0].reshape(n_experts * d, ff), od_exp_up[0].reshape(n_experts * d, ff),
                  od_exp_down[0].reshape(n_experts * ff, d))
    x, (exp_gate, exp_up, exp_down) = _ffn(
        x, ev_norm_ffn[0], ev_ffn_gate[0].astype(BF16), ev_ffn_up[0].astype(BF16),
        ev_ffn_down[0].astype(BF16), expert_f32, tm, fc_dense)

    qt, k, vt = _qkv(x.reshape(b, s, d), od_norm_mix[0], od_w_qkv[0], od_b_qkv[0], dq, dk, tm)
    bias = _rel_bias_table(rel_bias)
    attn_t = _attention(qt, k, vt, bias, od_sinks[0], tq)
    x, info, counts = _oproj_router(attn_t, x, od_w_o[0].astype(BF16), od_b_o[0], od_norm_ffn[0],
                                    od_router[0], tm)

    per_tile = counts[:, :n_experts, 0].astype(jnp.int32)
    local_off = jnp.cumsum(per_tile, axis=1) - per_tile
    earlier = jnp.cumsum(per_tile, axis=0) - per_tile
    totals = jnp.sum(per_tile, axis=0)
    tiles_per = (totals + tme - 1) // tme
    tile_end = jnp.cumsum(tiles_per)
    group_start = (tile_end - tiles_per) * tme
    n_tiles = (TOP_K * n) // tme + n_experts
    tile_ids = jnp.arange(n_tiles, dtype=jnp.int32)
    tile_valid = (tile_ids < tile_end[-1]).astype(jnp.int32)
    last_valid = jnp.minimum(tile_ids, tile_end[-1] - 1)
    tile_expert = jnp.sum(last_valid[:, None] >= tile_end[None, :], axis=1).astype(jnp.int32)
    tabs = (per_tile.reshape(-1), local_off.reshape(-1), (group_start[None, :] + earlier).reshape(-1))
    pad_tabs = (group_start + totals, tiles_per * tme - totals, tile_end[-1:])

    slots = info[:, INFO_S0:INFO_S1 + 1, :].astype(jnp.int32).reshape(-1)
    weights = info[:, INFO_W0:INFO_W1 + 1, :].reshape(-1)

    xs = _dispatch(x, od_norm_ffn[0], slots, tabs + pad_tabs, n_experts, n_tiles * tme, tm, tme)
    ys = _experts(xs, tile_expert, tile_valid, tile_end[-1:], exp_gate.reshape(n_experts, d, ff),
                  exp_up.reshape(n_experts, d, ff), exp_down.reshape(n_experts, ff, d), d, tme, fc_dense)
    out = _combine(x, slots, weights, tabs, n_experts, ys, final_norm, tm)
    return out.reshape(b, s, d)


def kernel(x, rel_bias, ev_norm_mix, ev_w_in, ev_pool_w, ev_pool_scale, ev_conv_w, ev_w_out, ev_norm_ffn, ev_ffn_gate, ev_ffn_up, ev_ffn_down, od_norm_mix, od_w_qkv, od_b_qkv, od_sinks, od_w_o, od_b_o, od_norm_ffn, od_router, od_exp_gate, od_exp_up, od_exp_down, final_norm):
    return _forward(x, rel_bias, ev_norm_mix, ev_w_in, ev_pool_w, ev_pool_scale, ev_conv_w, ev_w_out,
                    ev_norm_ffn, ev_ffn_gate, ev_ffn_up, ev_ffn_down, od_norm_mix, od_w_qkv, od_b_qkv,
                    od_sinks, od_w_o, od_b_o, od_norm_ffn, od_router, od_exp_gate, od_exp_up,
                    od_exp_down, final_norm, tm=512, tq=512, tme=512, fc_dense=256)
```

```python
import functools
import math

import jax
import jax.numpy as jnp
from jax import lax
from jax.experimental import pallas as pl
from jax.experimental.pallas import tpu as pltpu

F32 = jnp.float32
BF16 = jnp.bfloat16

RMS_EPS = 1e-6
CHUNK = 64
POOL_WINDOWS = (2, 4, 8, 16)
CONV_K = 3
HEAD_DIM = 64
N_KV_HEADS = 2
WINDOW_CHUNKS = 2
KV_SPAN = (WINDOW_CHUNKS + 1) * CHUNK
N_BUCKETS = 32
MAX_DISTANCE = 128
TOP_K = 2
NEG_INF = -1e30

LANES = 128
SUBLANES = 8
VMEM_LIMIT = 56 * 1024 * 1024
EXPERT_VMEM_LIMIT = 62 * 1024 * 1024

POOL_HALO = 16
CONV_HALO = 8
ROW_LOOP_UNROLL = 16

def _rms(x, g):
    return x * lax.rsqrt(jnp.mean(x * x, axis=-1, keepdims=True) + RMS_EPS) * g


def _params(n_axes, vmem=VMEM_LIMIT):
    return pltpu.CompilerParams(dimension_semantics=("arbitrary",) * n_axes,
                                vmem_limit_bytes=vmem)


def _const_spec(shape):
    nd = len(shape)
    return pl.BlockSpec(shape, lambda *_: (0,) * nd, pipeline_mode=pl.Buffered(1))


def _mixer_kernel(x_ref, g_ref, win_ref, pw_ref, ps_ref, cw_ref, wout_ref, o_ref, ubuf, zbuf):
    s = pl.program_id(1)
    tm = x_ref.shape[0]
    pool_w = ubuf.shape[1]
    grp = pool_w // len(POOL_WINDOWS)
    conv_c = zbuf.shape[1]

    @pl.when(s == 0)
    def _():
        ubuf[0:POOL_HALO, :] = jnp.zeros((POOL_HALO, pool_w), F32)
        zbuf[0:CONV_HALO, :] = jnp.zeros((CONV_HALO, conv_c), F32)

    @pl.when(s > 0)
    def _():
        ubuf[0:POOL_HALO, :] = ubuf[tm:tm + POOL_HALO, :]
        zbuf[0:CONV_HALO, :] = zbuf[tm:tm + CONV_HALO, :]

    x = x_ref[...]
    h = _rms(x, g_ref[...]).astype(BF16)
    proj = jnp.dot(h, win_ref[...], preferred_element_type=F32)
    u = proj[:, :pool_w]
    gate_post = proj[:, pool_w:pool_w + conv_c]
    gate_pre = proj[:, pool_w + conv_c:pool_w + 2 * conv_c]
    v = proj[:, pool_w + 2 * conv_c:]
    ubuf[POOL_HALO:POOL_HALO + tm, :] = u
    z = gate_pre * v
    zbuf[CONV_HALO:CONV_HALO + tm, :] = z

    t = s * tm + lax.broadcasted_iota(jnp.int32, (tm, 1), 0)
    parts = []
    for g, w in enumerate(POOL_WINDOWS):
        lo, hi = g * grp, (g + 1) * grp
        ug = u[:, lo:hi]
        acc = ug
        for k in range(1, w):
            acc = acc + ubuf[POOL_HALO - k:POOL_HALO - k + tm, lo:hi]
        count = jnp.minimum(t + 1, w).astype(F32)
        pooled = acc / count - ug
        mixed = jnp.dot(pooled.astype(BF16), pw_ref[g], preferred_element_type=F32)
        parts.append(mixed * ps_ref[:, lo:hi])
    conv = cw_ref[CONV_K - 1:CONV_K, :] * z
    for k in range(CONV_K - 1):
        off = CONV_HALO - (CONV_K - 1) + k
        conv = conv + cw_ref[k:k + 1, :] * zbuf[off:off + tm, :]
    parts.append(gate_post * conv)
    y = jnp.concatenate(parts, axis=1).astype(BF16)
    o_ref[...] = x + jnp.dot(y, wout_ref[...], preferred_element_type=F32)


def _mixer(x, g, w_in, pool_w, pool_scale, conv_w, w_out, tm):
    b, s, d = x.shape
    pool_width = pool_scale.shape[-1]
    conv_c = conv_w.shape[-1]
    return pl.pallas_call(
        _mixer_kernel,
        out_shape=jax.ShapeDtypeStruct((b, s, d), F32),
        grid=(b, s // tm),
        in_specs=[
            pl.BlockSpec((None, tm, d), lambda i, j: (i, j, 0)),
            _const_spec((1, d)),
            _const_spec(w_in.shape),
            _const_spec(pool_w.shape),
            _const_spec((1, pool_width)),
            _const_spec(conv_w.shape),
            _const_spec(w_out.shape),
        ],
        out_specs=pl.BlockSpec((None, tm, d), lambda i, j: (i, j, 0)),
        scratch_shapes=[pltpu.VMEM((POOL_HALO + tm, pool_width), F32),
                        pltpu.VMEM((CONV_HALO + tm, conv_c), F32)],
        compiler_params=_params(2),
        name="l0_mixer",
    )(x, g.reshape(1, d), w_in, pool_w, pool_scale.reshape(1, pool_width), conv_w, w_out)


def _ffn_kernel(x_ref, g_ref, wg_ref, wu_ref, wd_ref, *rest, fc, n_cast):
    cast_in, (o_ref, *cast_out), (h_scr, t_scr) = rest[:n_cast], rest[n_cast:2 * n_cast + 1], rest[2 * n_cast + 1:]
    x = x_ref[...]
    h_scr[...] = _rms(x, g_ref[...]).astype(BF16)
    n_chunks = wg_ref.shape[1] // fc

    def body(c, carry):
        off = pl.multiple_of(c * fc, fc)
        h = h_scr[...]
        a = jnp.dot(h, wg_ref[:, pl.ds(off, fc)], preferred_element_type=F32)
        b = jnp.dot(h, wu_ref[:, pl.ds(off, fc)], preferred_element_type=F32)
        t_scr[:, pl.ds(off, fc)] = (a * jax.nn.sigmoid(a) * b).astype(BF16)
        return carry

    lax.fori_loop(0, n_chunks, body, 0, unroll=True)
    o_ref[...] = x + jnp.dot(t_scr[...], wd_ref[...], preferred_element_type=F32)
    for src, dst in zip(cast_in, cast_out):
        dst[...] = src[...].astype(BF16)


def _ffn(x, g, w_gate, w_up, w_down, to_cast, tm, fc):
    n, d = x.shape
    steps = n // tm
    slabs = []
    for w in to_cast:
        rows = w.shape[0] // steps
        assert rows * steps == w.shape[0] and rows % (2 * SUBLANES) == 0, w.shape
        slabs.append(pl.BlockSpec((rows, w.shape[1]), lambda i: (i, 0)))
    outs = pl.pallas_call(
        functools.partial(_ffn_kernel, fc=fc, n_cast=len(to_cast)),
        out_shape=(jax.ShapeDtypeStruct((n, d), F32),
                   *[jax.ShapeDtypeStruct(w.shape, BF16) for w in to_cast]),
        grid=(steps,),
        in_specs=[
            pl.BlockSpec((tm, d), lambda i: (i, 0)),
            _const_spec((1, d)),
            _const_spec(w_gate.shape),
            _const_spec(w_up.shape),
            _const_spec(w_down.shape),
            *slabs,
        ],
        out_specs=(pl.BlockSpec((tm, d), lambda i: (i, 0)), *slabs),
        scratch_shapes=[pltpu.VMEM((tm, d), BF16), pltpu.VMEM((tm, w_gate.shape[1]), BF16)],
        compiler_params=_params(1),
        name="l0_ffn",
    )(x, g.reshape(1, d), w_gate, w_up, w_down, *to_cast)
    return outs[0], outs[1:]


_NT = (((1,), (1,)), ((), ()))
_TN = (((0,), (0,)), ((), ()))


def _qkv_kernel(x_ref, g_ref, wqt_ref, bq_ref, wk_ref, bk_ref, wvt_ref, bv_ref, qt_ref, k_ref, vt_ref):
    h = _rms(x_ref[...], g_ref[...]).astype(BF16)
    qt = lax.dot_general(wqt_ref[...], h, _NT, preferred_element_type=F32) + bq_ref[...]
    qt_ref[...] = (qt * (HEAD_DIM ** -0.5)).astype(BF16)
    k_ref[...] = (jnp.dot(h, wk_ref[...], preferred_element_type=F32) + bk_ref[...]).astype(BF16)
    vt = lax.dot_general(wvt_ref[...], h, _NT, preferred_element_type=F32) + bv_ref[...]
    vt_ref[...] = vt.astype(BF16)


def _qkv(x, g, w_qkv, b_qkv, dq, dk, tm):
    b, s, d = x.shape
    wqt = w_qkv[:, :dq].T.astype(BF16)
    wk = w_qkv[:, dq:dq + dk].astype(BF16)
    wvt = w_qkv[:, dq + dk:].T.astype(BF16)
    bq = b_qkv[:dq].reshape(dq, 1)
    bk = b_qkv[dq:dq + dk].reshape(1, dk)
    bv = b_qkv[dq + dk:].reshape(dk, 1)
    return pl.pallas_call(
        _qkv_kernel,
        out_shape=(jax.ShapeDtypeStruct((b, dq, s), BF16),
                   jax.ShapeDtypeStruct((b, s, dk), BF16),
                   jax.ShapeDtypeStruct((b, dk, s), BF16)),
        grid=(b, s // tm),
        in_specs=[
            pl.BlockSpec((None, tm, d), lambda bi, i: (bi, i, 0)),
            _const_spec((1, d)),
            _const_spec(wqt.shape), _const_spec(bq.shape),
            _const_spec(wk.shape), _const_spec(bk.shape),
            _const_spec(wvt.shape), _const_spec(bv.shape),
        ],
        out_specs=(pl.BlockSpec((None, dq, tm), lambda bi, i: (bi, 0, i)),
                   pl.BlockSpec((None, tm, dk), lambda bi, i: (bi, i, 0)),
                   pl.BlockSpec((None, dk, tm), lambda bi, i: (bi, 0, i))),
        compiler_params=_params(2),
        name="l1_qkv",
    )(x, g.reshape(1, d), wqt, bq, wk, bk, wvt, bv)


def _bias_kernel(onehot_ref, rb_ref, o_ref):
    o_ref[...] = jnp.dot(onehot_ref[...], rb_ref[...], preferred_element_type=F32,
                         precision=lax.Precision.HIGHEST)


def _t5_bucket(rel):
    nb = N_BUCKETS // 2
    max_exact = nb // 2
    ret = jnp.where(rel > 0, nb, 0)
    n = jnp.abs(rel)
    nf = jnp.maximum(n, 1).astype(jnp.float32)
    large = max_exact + (jnp.log(nf / max_exact) / math.log(MAX_DISTANCE / max_exact)
                         * (nb - max_exact)).astype(jnp.int32)
    large = jnp.minimum(large, nb - 1)
    return ret + jnp.where(n < max_exact, n, large)


def _rel_bias_table(rel_bias):
    n_heads = rel_bias.shape[1]
    pad = WINDOW_CHUNKS * CHUNK
    rel = (jnp.arange(KV_SPAN) - pad)[None, :] - jnp.arange(CHUNK)[:, None]
    bucket = _t5_bucket(rel).reshape(-1)
    onehot = (bucket[:, None] == jnp.arange(N_BUCKETS)[None, :]).astype(F32)
    rows = CHUNK * KV_SPAN
    rb = jnp.pad(rel_bias.astype(F32), ((0, 0), (0, LANES - n_heads)))
    out = pl.pallas_call(
        _bias_kernel,
        out_shape=jax.ShapeDtypeStruct((rows, LANES), F32),
        name="l1_rel_bias",
    )(onehot, rb)
    return jnp.transpose(out[:, :n_heads].reshape(CHUNK, KV_SPAN, n_heads), (2, 0, 1))


PAIR = 2 * CHUNK
PAIR_KEYS = KV_SPAN + CHUNK


def _attn_kernel(qt_ref, kp_ref, km_ref, vtp_ref, vtm_ref, bias_ref, sink_ref, ot_ref, kbuf, vtbuf,
                 s_a, s_b, *, n_heads):
    i = pl.program_id(1)
    tq = qt_ref.shape[1]
    pad = WINDOW_CHUNKS * CHUNK
    per_kv = n_heads // N_KV_HEADS
    kbuf[0:pad, :] = kp_ref[...]
    kbuf[pad:pad + tq, :] = km_ref[...]
    vtbuf[:, 0:pad] = vtp_ref[...]
    vtbuf[:, pad:pad + tq] = vtm_ref[...]
    zeros = jnp.zeros((HEAD_DIM, PAIR), BF16)
    first = lax.broadcasted_iota(jnp.int32, (CHUNK, PAIR), 1) < CHUNK
    n_pairs = tq // PAIR

    def scores(jj, s_scr):
        c0 = jj * PAIR
        kc = kbuf[c0:c0 + PAIR_KEYS, :]
        for h in range(n_heads):
            pieces = [zeros] * N_KV_HEADS
            pieces[h // per_kv] = qt_ref[h * HEAD_DIM:(h + 1) * HEAD_DIM, c0:c0 + PAIR]
            rhs = jnp.concatenate(pieces, axis=0)
            s_scr[h] = jnp.dot(kc, rhs, preferred_element_type=F32)

    def softmax_values(jj, s_scr):
        c0 = jj * PAIR
        variant = jnp.where(i == 0, 1, 0) if jj == 0 else 0
        for h in range(n_heads):
            kv = h // per_kv
            mid = s_scr[h, CHUNK:KV_SPAN, :] + bias_ref[variant, h, 0:KV_SPAN - CHUNK, :]
            edge = (jnp.where(first, s_scr[h, 0:CHUNK, :], s_scr[h, KV_SPAN:PAIR_KEYS, :])
                    + bias_ref[variant, h, KV_SPAN - CHUNK:KV_SPAN, :])
            sk = sink_ref[h]
            m_ = jnp.maximum(jnp.maximum(jnp.max(mid, axis=0, keepdims=True),
                                         jnp.max(edge, axis=0, keepdims=True)), sk)
            p_mid = jnp.exp(mid - m_)
            p_edge = jnp.exp(edge - m_)
            denom = (jnp.sum(p_mid, axis=0, keepdims=True) + jnp.sum(p_edge, axis=0, keepdims=True)
                     + jnp.exp(sk - m_))
            p_mid = (p_mid / denom).astype(BF16)
            p_edge = (p_edge / denom).astype(BF16)
            zero = jnp.zeros_like(p_edge)
            p = jnp.concatenate([jnp.where(first, p_edge, zero), p_mid, jnp.where(first, zero, p_edge)],
                                axis=0)
            vt = vtbuf[kv * HEAD_DIM:(kv + 1) * HEAD_DIM, c0:c0 + PAIR_KEYS]
            o = jnp.dot(vt, p, preferred_element_type=F32)
            ot_ref[h * HEAD_DIM:(h + 1) * HEAD_DIM, c0:c0 + PAIR] = o.astype(BF16)

    bufs = (s_a, s_b)
    scores(0, bufs[0])
    for jj in range(n_pairs):
        if jj + 1 < n_pairs:
            scores(jj + 1, bufs[(jj + 1) % 2])
        softmax_values(jj, bufs[jj % 2])


def _pair_bias(bias):
    bt = jnp.transpose(bias, (0, 2, 1)).astype(F32)
    mid = jnp.concatenate([bt[:, CHUNK:, :], bt[:, :KV_SPAN - CHUNK, :]], axis=2)
    edge = jnp.concatenate([bt[:, :CHUNK, :], bt[:, KV_SPAN - CHUNK:, :]], axis=2)
    table = jnp.concatenate([mid, edge], axis=1)
    row = jnp.arange(KV_SPAN)[:, None]
    lane = jnp.arange(PAIR)[None, :]
    key = jnp.where(row < KV_SPAN - CHUNK, row + CHUNK,
                    jnp.where(lane < CHUNK, row - (KV_SPAN - CHUNK), row + CHUNK))
    masked = jnp.where(key[None] < WINDOW_CHUNKS * CHUNK, NEG_INF, table)
    return jnp.stack([table, masked])


def _attention(qt, k, vt, bias, sinks, tq):
    b, dq, s = qt.shape
    dk = k.shape[-1]
    n_heads = dq // HEAD_DIM
    pad = WINDOW_CHUNKS * CHUNK
    ratio = tq // pad
    table = _pair_bias(bias)
    prev = lambda i: jnp.maximum(i * ratio - 1, 0)
    return pl.pallas_call(
        functools.partial(_attn_kernel, n_heads=n_heads),
        out_shape=jax.ShapeDtypeStruct((b, dq, s), BF16),
        grid=(b, s // tq),
        in_specs=[
            pl.BlockSpec((None, dq, tq), lambda bi, i: (bi, 0, i)),
            pl.BlockSpec((None, pad, dk), lambda bi, i: (bi, prev(i), 0)),
            pl.BlockSpec((None, tq, dk), lambda bi, i: (bi, i, 0)),
            pl.BlockSpec((None, dk, pad), lambda bi, i: (bi, 0, prev(i))),
            pl.BlockSpec((None, dk, tq), lambda bi, i: (bi, 0, i)),
            _const_spec(table.shape),
            pl.BlockSpec(memory_space=pltpu.SMEM),
        ],
        out_specs=pl.BlockSpec((None, dq, tq), lambda bi, i: (bi, 0, i)),
        scratch_shapes=[pltpu.VMEM((pad + tq, dk), BF16), pltpu.VMEM((dk, pad + tq), BF16),
                        pltpu.VMEM((n_heads, PAIR_KEYS, PAIR), F32),
                        pltpu.VMEM((n_heads, PAIR_KEYS, PAIR), F32)],
        compiler_params=_params(2),
        name="l1_attention",
    )(qt, k, k, vt, vt, table, sinks.astype(F32))


INFO_S0, INFO_S1, INFO_W0, INFO_W1 = range(4)


EXPERT_ROWS = 16


def _oproj_router_kernel(ot_ref, x_ref, wo_ref, bo_ref, g_ref, wrt_ref, x3_ref, info_ref, cnt_ref,
                         *, n_experts):
    tm = x_ref.shape[0]
    attn = lax.dot_general(ot_ref[...], wo_ref[...], _TN, preferred_element_type=F32)
    x3 = x_ref[...] + attn + bo_ref[...]
    x3_ref[...] = x3
    h = _rms(x3, g_ref[...])
    h_hi = h.astype(BF16)
    h_lo = (h - h_hi.astype(F32)).astype(BF16)
    wrt = wrt_ref[...]
    by_hi = lax.dot_general(wrt, h_hi, _NT, preferred_element_type=F32)
    by_lo = lax.dot_general(wrt[:EXPERT_ROWS], h_lo, _NT, preferred_element_type=F32)
    logits = (by_hi[:EXPERT_ROWS] + by_hi[EXPERT_ROWS:]) + by_lo
    row_i = lax.broadcasted_iota(jnp.int32, (EXPERT_ROWS, tm), 0)
    row = row_i.astype(F32)
    neg = jnp.float32(-jnp.inf)
    logits = jnp.where(row_i < n_experts, logits, neg)
    m0 = jnp.max(logits, axis=0, keepdims=True)
    e0 = jnp.min(jnp.where(logits == m0, row, float(EXPERT_ROWS)), axis=0, keepdims=True)
    rest = jnp.where(row == e0, neg, logits)
    m1 = jnp.max(rest, axis=0, keepdims=True)
    e1 = jnp.min(jnp.where(rest == m1, row, float(EXPERT_ROWS)), axis=0, keepdims=True)
    t = jnp.exp(m1 - m0)
    w0 = 1.0 / (1.0 + t)
    w1 = t / (1.0 + t)
    hot0 = row == e0
    hot1 = row == e1
    both = jnp.where(hot0 | hot1, 1.0, 0.0)
    u = lax.broadcasted_iota(jnp.int32, (tm, tm), 0)
    v = lax.broadcasted_iota(jnp.int32, (tm, tm), 1)
    earlier = (u < v).astype(BF16)
    cnt = jnp.broadcast_to(jnp.sum(both, axis=1, keepdims=True), (EXPERT_ROWS, LANES))
    cnt_ref[...] = cnt
    upto = cnt
    row_l = lax.broadcasted_iota(jnp.int32, (EXPERT_ROWS, LANES), 0)
    shift = 1
    while shift < n_experts:
        upto = upto + jnp.where(row_l >= shift, pltpu.roll(upto, shift, axis=0), 0.0)
        shift *= 2
    before = jnp.dot(both.astype(BF16), earlier, preferred_element_type=F32) + (upto - cnt)[:, 0:1]
    r0 = jnp.sum(jnp.where(hot0, before, 0.0), axis=0, keepdims=True)
    r1 = jnp.sum(jnp.where(hot1, before, 0.0), axis=0, keepdims=True)
    fields = {INFO_S0: r0 * SUBLANES, INFO_S1: r1 * SUBLANES, INFO_W0: w0, INFO_W1: w1}
    info_ref[...] = jnp.concatenate(
        [fields.get(k, jnp.zeros((1, tm), F32)) for k in range(SUBLANES)], axis=0)


def _oproj_router(ot, x, w_o, b_o, g, w_router, tm):
    n, d = x.shape
    dq, seq = ot.shape[1:]
    per_seq = seq // tm
    n_experts = w_router.shape[1]
    assert n_experts <= EXPERT_ROWS
    wrt = jnp.pad(w_router.astype(F32).T, ((0, EXPERT_ROWS - n_experts), (0, 0)))
    wrt_hi = wrt.astype(BF16)
    wrt_lo = (wrt - wrt_hi.astype(F32)).astype(BF16)
    wrt = jnp.concatenate([wrt_hi, wrt_lo], axis=0)
    return pl.pallas_call(
        functools.partial(_oproj_router_kernel, n_experts=n_experts),
        out_shape=(jax.ShapeDtypeStruct((n, d), F32),
                   jax.ShapeDtypeStruct((n // tm, SUBLANES, tm), F32),
                   jax.ShapeDtypeStruct((n // tm, EXPERT_ROWS, LANES), F32)),
        grid=(n // tm,),
        in_specs=[
            pl.BlockSpec((None, dq, tm), lambda i: (i // per_seq, 0, i % per_seq)),
            pl.BlockSpec((tm, d), lambda i: (i, 0)),
            _const_spec(w_o.shape),
            _const_spec((1, d)),
            _const_spec((1, d)),
            _const_spec(wrt.shape),
        ],
        out_specs=(pl.BlockSpec((tm, d), lambda i: (i, 0)),
                   pl.BlockSpec((None, SUBLANES, tm), lambda i: (i, 0, 0)),
                   pl.BlockSpec((None, EXPERT_ROWS, LANES), lambda i: (i, 0, 0))),
        compiler_params=_params(1),
        name="l1_oproj_router",
    )(ot, x, w_o, b_o.reshape(1, d), g.reshape(1, d), wrt)


def _pieces(n, n_bits, fn):
    for bit in reversed(range(n_bits)):
        size = 1 << bit
        done = n & ~((size << 1) - 1)
        pl.when((n & size) != 0)(functools.partial(fn, done, size))


def _rows(ref, first_row, n_rows):
    return ref.at[pl.ds(pl.multiple_of(first_row * SUBLANES, SUBLANES), n_rows * SUBLANES)]


def _segment_copies(tabs, i, n_experts, lbuf, hbm, sem, to_hbm, act):
    n_tab, o_tab, d_tab = tabs[:3]
    n_bits = (lbuf.shape[0] // SUBLANES // TOP_K).bit_length()
    for e in range(n_experts):
        local0 = o_tab[i * n_experts + e]
        sorted0 = d_tab[i * n_experts + e]

        def piece(done, size, local0=local0, sorted0=sorted0):
            local = _rows(lbuf, local0 + done, size)
            remote = _rows(hbm, sorted0 + done, size)
            act(pltpu.make_async_copy(local, remote, sem) if to_hbm
                else pltpu.make_async_copy(remote, local, sem))

        _pieces(n_tab[i * n_experts + e], n_bits, piece)


def _dispatch_kernel(n_tab, o_tab, d_tab, pad0_tab, padn_tab, nv_tab, slot_ref, x_ref, g_ref, xs_hbm,
                     hbuf, lbuf, sems, *, n_experts, tme):
    i = pl.program_id(0)
    tm = x_ref.shape[0]
    n_slots = TOP_K * tm
    h = _rms(x_ref[...], g_ref[...])
    for c in range(x_ref.shape[1] // LANES):
        hbuf[pl.ds(c, tm, stride=SUBLANES), :] = h[:, c * LANES:(c + 1) * LANES]
    slot = i % 2
    sorted_rows = lbuf.at[slot]

    def token(t, carry):
        row = hbuf[pl.ds(pl.multiple_of(t * SUBLANES, SUBLANES), SUBLANES), :]
        for k in range(TOP_K):
            dst = pl.multiple_of(slot_ref[k * tm + t], SUBLANES)
            sorted_rows[pl.ds(dst, SUBLANES), :] = row
        return carry

    lax.fori_loop(0, tm, token, 0, unroll=ROW_LOOP_UNROLL)
    tabs = (n_tab, o_tab, d_tab)

    def copies(tile, s, act):
        _segment_copies(tabs, tile, n_experts, lbuf.at[s], xs_hbm, sems.at[s], True, act)

    copies(i, slot, lambda cp: cp.start())

    def wait_all(s):
        pltpu.make_async_copy(lbuf.at[s], _rows(xs_hbm, 0, n_slots), sems.at[s]).wait()

    @pl.when(i > 0)
    def _():
        wait_all(1 - slot)

    @pl.when(i == pl.num_programs(0) - 1)
    def _():
        wait_all(slot)
        zeros = lbuf.at[slot]
        zeros[0:tme * SUBLANES, :] = jnp.zeros((tme * SUBLANES, LANES), F32)

        def pads(act):
            for e in range(n_experts):
                def piece(done, size, e=e):
                    act(pltpu.make_async_copy(_rows(zeros, 0, size),
                                              _rows(xs_hbm, pad0_tab[e] + done, size), sems.at[slot]))
                _pieces(padn_tab[e], (tme - 1).bit_length(), piece)
            n_tiles = xs_hbm.shape[0] // (tme * SUBLANES)
            for t in range(n_experts):
                tile = nv_tab[0] + t
                pl.when(tile < n_tiles)(lambda tile=tile: act(pltpu.make_async_copy(
                    _rows(zeros, 0, tme), _rows(xs_hbm, tile * tme, tme), sems.at[slot])))

        pads(lambda cp: cp.start())
        pads(lambda cp: cp.wait())


def _dispatch(x, g, slots, tabs, n_experts, n_rows, tm, tme):
    n, d = x.shape
    assert TOP_K * tm >= tme
    grid_spec = pltpu.PrefetchScalarGridSpec(
        num_scalar_prefetch=len(tabs),
        grid=(n // tm,),
        in_specs=[
            pl.BlockSpec((TOP_K * tm,), lambda i, *_: (i,), memory_space=pltpu.SMEM),
            pl.BlockSpec((tm, d), lambda i, *_: (i, 0)),
            pl.BlockSpec((1, d), lambda i, *_: (0, 0)),
        ],
        out_specs=pl.BlockSpec(memory_space=pl.ANY),
        scratch_shapes=[pltpu.VMEM((tm * SUBLANES, LANES), F32),
                        pltpu.VMEM((2, TOP_K * tm * SUBLANES, LANES), F32),
                        pltpu.SemaphoreType.DMA((2,))],
    )
    return pl.pallas_call(
        functools.partial(_dispatch_kernel, n_experts=n_experts, tme=tme),
        out_shape=jax.ShapeDtypeStruct((n_rows * SUBLANES, LANES), F32),
        grid_spec=grid_spec,
        compiler_params=_params(1),
        name="l1_moe_dispatch",
    )(*tabs, slots, x, g.reshape(1, d))


def _experts_kernel(te_ref, tv_ref, nv_ref, xs_ref, wg_ref, wu_ref, wd_ref, ys_ref, t_scr, *, sub):
    del te_ref, nv_ref
    j = pl.program_id(0)
    tm = t_scr.shape[0]
    n_slabs = wg_ref.shape[0] // LANES

    @pl.when(tv_ref[j] == 1)
    def _():
        slabs = [xs_ref[pl.ds(s, tm, stride=SUBLANES), :] for s in range(n_slabs)]
        x = jnp.concatenate(slabs, axis=1).astype(BF16)

        def body(q, carry):
            off = pl.multiple_of(q * sub, sub)
            a = jnp.dot(x, wg_ref[:, pl.ds(off, sub)], preferred_element_type=F32)
            b = jnp.dot(x, wu_ref[:, pl.ds(off, sub)], preferred_element_type=F32)
            t_scr[:, pl.ds(off, sub)] = (a * jax.nn.sigmoid(a) * b).astype(BF16)
            return carry
        lax.fori_loop(0, wg_ref.shape[1] // sub, body, 0, unroll=True)
        y = jnp.dot(t_scr[...], wd_ref[...], preferred_element_type=F32)
        for s in range(n_slabs):
            ys_ref[pl.ds(s, tm, stride=SUBLANES), :] = y[:, s * LANES:(s + 1) * LANES]

    @pl.when(tv_ref[j] != 1)
    def _():
        ys_ref[...] = jnp.zeros_like(ys_ref)


def _experts(xs, tile_expert, tile_valid, n_valid, w_gate, w_up, w_down, d, tm, sub):
    n_tiles = tile_expert.shape[0]
    ff = w_gate.shape[2]
    grid_spec = pltpu.PrefetchScalarGridSpec(
        num_scalar_prefetch=3,
        grid=(n_tiles,),
        in_specs=[
            pl.BlockSpec((tm * SUBLANES, LANES), lambda j, te, tv, nv: (jnp.minimum(j, nv[0] - 1), 0)),
            pl.BlockSpec((None, d, ff), lambda j, te, tv, nv: (te[j], 0, 0)),
            pl.BlockSpec((None, d, ff), lambda j, te, tv, nv: (te[j], 0, 0)),
            pl.BlockSpec((None, ff, d), lambda j, te, tv, nv: (te[j], 0, 0)),
        ],
        out_specs=pl.BlockSpec((tm * SUBLANES, LANES), lambda j, te, tv, nv: (j, 0)),
        scratch_shapes=[pltpu.VMEM((tm, ff), BF16)],
    )
    return pl.pallas_call(
        functools.partial(_experts_kernel, sub=sub),
        out_shape=jax.ShapeDtypeStruct(xs.shape, F32),
        grid_spec=grid_spec,
        compiler_params=_params(1, vmem=EXPERT_VMEM_LIMIT),
        name="l1_moe_experts",
    )(tile_expert, tile_valid, n_valid, xs, w_gate, w_up, w_down)


def _combine_kernel(n_tab, o_tab, d_tab, slot_ref, w_ref, x_ref, g_ref, ys_hbm, o_ref, lbuf, mbuf, sems,
                    *, n_experts):
    i = pl.program_id(0)
    tm = x_ref.shape[0]
    n_slots = TOP_K * tm
    tabs = (n_tab, o_tab, d_tab)
    slot = i % 2

    def copies(tile, s, act):
        _segment_copies(tabs, tile, n_experts, lbuf.at[s], ys_hbm, sems.at[s], False, act)

    @pl.when(i == 0)
    def _():
        copies(i, slot, lambda cp: cp.start())

    @pl.when(i + 1 < pl.num_programs(0))
    def _():
        copies(i + 1, 1 - slot, lambda cp: cp.start())

    pltpu.make_async_copy(_rows(ys_hbm, 0, n_slots), lbuf.at[slot], sems.at[slot]).wait()
    y = lbuf.at[slot]

    def token(t, carry):
        acc = None
        for k in range(TOP_K):
            row = y[pl.ds(pl.multiple_of(slot_ref[k * tm + t], SUBLANES), SUBLANES), :]
            term = w_ref[k * tm + t] * row
            acc = term if acc is None else acc + term
        mbuf[pl.ds(pl.multiple_of(t * SUBLANES, SUBLANES), SUBLANES), :] = acc
        return carry

    lax.fori_loop(0, tm, token, 0, unroll=ROW_LOOP_UNROLL)
    moe = jnp.concatenate([mbuf[pl.ds(c, tm, stride=SUBLANES), :]
                           for c in range(x_ref.shape[1] // LANES)], axis=1)
    o_ref[...] = _rms(x_ref[...] + moe, g_ref[...])


def _combine(x, slots, weights, tabs, n_experts, ys, g, tm):
    n, d = x.shape
    grid_spec = pltpu.PrefetchScalarGridSpec(
        num_scalar_prefetch=3,
        grid=(n // tm,),
        in_specs=[
            pl.BlockSpec((TOP_K * tm,), lambda i, *_: (i,), memory_space=pltpu.SMEM),
            pl.BlockSpec((TOP_K * tm,), lambda i, *_: (i,), memory_space=pltpu.SMEM),
            pl.BlockSpec((tm, d), lambda i, *_: (i, 0)),
            pl.BlockSpec((1, d), lambda i, *_: (0, 0)),
            pl.BlockSpec(memory_space=pl.ANY),
        ],
        out_specs=pl.BlockSpec((tm, d), lambda i, *_: (i, 0)),
        scratch_shapes=[pltpu.VMEM((2, TOP_K * tm * SUBLANES, LANES), F32),
                        pltpu.VMEM((tm * SUBLANES, LANES), F32),
                        pltpu.SemaphoreType.DMA((2,))],
    )
    return pl.pallas_call(
        functools.partial(_combine_kernel, n_experts=n_experts),
        out_shape=jax.ShapeDtypeStruct((n, d), F32),
        grid_spec=grid_spec,
        compiler_params=_params(1),
        name="l1_moe_combine",
    )(*tabs, slots, weights, x, g.reshape(1, d), ys)


def _forward(x, rel_bias, ev_norm_mix, ev_w_in, ev_pool_w, ev_pool_scale, ev_conv_w, ev_w_out,
             ev_norm_ffn, ev_ffn_gate, ev_ffn_up, ev_ffn_down, od_norm_mix, od_w_qkv, od_b_qkv,
             od_sinks, od_w_o, od_b_o, od_norm_ffn, od_router, od_exp_gate, od_exp_up,
             od_exp_down, final_norm, *, tm, tq, tme, fc_dense, tr):
    b, s, d = x.shape
    n = b * s
    n_heads = od_sinks.shape[-1]
    dq = n_heads * HEAD_DIM
    dk = N_KV_HEADS * HEAD_DIM
    n_experts = od_router.shape[-1]

    x = _mixer(x, ev_norm_mix[0], ev_w_in[0].astype(BF16), ev_pool_w[0].astype(BF16),
               ev_pool_scale[0], ev_conv_w[0], ev_w_out[0].astype(BF16), tm)
    x = x.reshape(n, d)
    ff = od_exp_gate.shape[-1]
    expert_f32 = (od_exp_gate[0].reshape(n_experts * d, ff), od_exp_up[0].reshape(n_experts * d, ff),
                  od_exp_down[0].reshape(n_experts * ff, d))
    x, (exp_gate, exp_up, exp_down) = _ffn(
        x, ev_norm_ffn[0], ev_ffn_gate[0].astype(BF16), ev_ffn_up[0].astype(BF16),
        ev_ffn_down[0].astype(BF16), expert_f32, tm, fc_dense)

    qt, k, vt = _qkv(x.reshape(b, s, d), od_norm_mix[0], od_w_qkv[0], od_b_qkv[0], dq, dk, tm)
    bias = _rel_bias_table(rel_bias)
    attn_t = _attention(qt, k, vt, bias, od_sinks[0], tq)
    x, info, counts = _oproj_router(attn_t, x, od_w_o[0].astype(BF16), od_b_o[0], od_norm_ffn[0],
                                    od_router[0], tr)

    per_tile = counts[:, :n_experts, 0].astype(jnp.int32)
    local_off = jnp.cumsum(per_tile, axis=1) - per_tile
    earlier = jnp.cumsum(per_tile, axis=0) - per_tile
    totals = jnp.sum(per_tile, axis=0)
    tiles_per = (totals + tme - 1) // tme
    tile_end = jnp.cumsum(tiles_per)
    group_start = (tile_end - tiles_per) * tme
    n_tiles = (TOP_K * n) // tme + n_experts
    tile_ids = jnp.arange(n_tiles, dtype=jnp.int32)
    tile_valid = (tile_ids < tile_end[-1]).astype(jnp.int32)
    last_valid = jnp.minimum(tile_ids, tile_end[-1] - 1)
    tile_expert = jnp.sum(last_valid[:, None] >= tile_end[None, :], axis=1).astype(jnp.int32)
    tabs = (per_tile.reshape(-1), local_off.reshape(-1), (group_start[None, :] + earlier).reshape(-1))
    pad_tabs = (group_start + totals, tiles_per * tme - totals, tile_end[-1:])

    slots = info[:, INFO_S0:INFO_S1 + 1, :].astype(jnp.int32).reshape(-1)
    weights = info[:, INFO_W0:INFO_W1 + 1, :].reshape(-1)

    xs = _dispatch(x, od_norm_ffn[0], slots, tabs + pad_tabs, n_experts, n_tiles * tme, tr, tme)
    ys = _experts(xs, tile_expert, tile_valid, tile_end[-1:], exp_gate.reshape(n_experts, d, ff),
                  exp_up.reshape(n_experts, d, ff), exp_down.reshape(n_experts, ff, d), d, tme, fc_dense)
    out = _combine(x, slots, weights, tabs, n_experts, ys, final_norm, tr)
    return out.reshape(b, s, d)


def kernel(x, rel_bias, ev_norm_mix, ev_w_in, ev_pool_w, ev_pool_scale, ev_conv_w, ev_w_out, ev_norm_ffn, ev_ffn_gate, ev_ffn_up, ev_ffn_down, od_norm_mix, od_w_qkv, od_b_qkv, od_sinks, od_w_o, od_b_o, od_norm_ffn, od_router, od_exp_gate, od_exp_up, od_exp_down, final_norm):
    return _forward(x, rel_bias, ev_norm_mix, ev_w_in, ev_pool_w, ev_pool_scale, ev_conv_w, ev_w_out,
                    ev_norm_ffn, ev_ffn_gate, ev_ffn_up, ev_ffn_down, od_norm_mix, od_w_qkv, od_b_qkv,
                    od_sinks, od_w_o, od_b_o, od_norm_ffn, od_router, od_exp_gate, od_exp_up,
                    od_exp_down, final_norm, tm=512, tq=512, tme=512, fc_dense=256, tr=1024)
```

```python
import functools
import math

import jax
import jax.numpy as jnp
from jax import lax
from jax.experimental import pallas as pl
from jax.experimental.pallas import tpu as pltpu

F32 = jnp.float32
BF16 = jnp.bfloat16

RMS_EPS = 1e-6
CHUNK = 64
POOL_WINDOWS = (2, 4, 8, 16)
CONV_K = 3
HEAD_DIM = 64
N_KV_HEADS = 2
WINDOW_CHUNKS = 2
KV_SPAN = (WINDOW_CHUNKS + 1) * CHUNK
N_BUCKETS = 32
MAX_DISTANCE = 128
TOP_K = 2
NEG_INF = -1e30

LANES = 128
SUBLANES = 8
VMEM_LIMIT = 56 * 1024 * 1024
EXPERT_VMEM_LIMIT = 62 * 1024 * 1024

POOL_HALO = 16
CONV_HALO = 8
ROW_LOOP_UNROLL = 16

def _rms(x, g):
    return x * lax.rsqrt(jnp.mean(x * x, axis=-1, keepdims=True) + RMS_EPS) * g


def _params(n_axes, vmem=VMEM_LIMIT):
    return pltpu.CompilerParams(dimension_semantics=("arbitrary",) * n_axes,
                                vmem_limit_bytes=vmem)


def _const_spec(shape):
    nd = len(shape)
    return pl.BlockSpec(shape, lambda *_: (0,) * nd, pipeline_mode=pl.Buffered(1))


def _mixer_kernel(x_ref, g_ref, win_ref, pw_ref, ps_ref, cw_ref, wout_ref, o_ref, ubuf, zbuf):
    s = pl.program_id(1)
    tm = x_ref.shape[0]
    pool_w = ubuf.shape[1]
    grp = pool_w // len(POOL_WINDOWS)
    conv_c = zbuf.shape[1]

    @pl.when(s == 0)
    def _():
        ubuf[0:POOL_HALO, :] = jnp.zeros((POOL_HALO, pool_w), F32)
        zbuf[0:CONV_HALO, :] = jnp.zeros((CONV_HALO, conv_c), F32)

    @pl.when(s > 0)
    def _():
        ubuf[0:POOL_HALO, :] = ubuf[tm:tm + POOL_HALO, :]
        zbuf[0:CONV_HALO, :] = zbuf[tm:tm + CONV_HALO, :]

    x = x_ref[...]
    h = _rms(x, g_ref[...]).astype(BF16)
    proj = jnp.dot(h, win_ref[...], preferred_element_type=F32)
    u = proj[:, :pool_w]
    gate_post = proj[:, pool_w:pool_w + conv_c]
    gate_pre = proj[:, pool_w + conv_c:pool_w + 2 * conv_c]
    v = proj[:, pool_w + 2 * conv_c:]
    ubuf[POOL_HALO:POOL_HALO + tm, :] = u
    z = gate_pre * v
    zbuf[CONV_HALO:CONV_HALO + tm, :] = z

    t = s * tm + lax.broadcasted_iota(jnp.int32, (tm, 1), 0)
    parts = []
    for g, w in enumerate(POOL_WINDOWS):
        lo, hi = g * grp, (g + 1) * grp
        ug = u[:, lo:hi]
        acc = ug
        for k in range(1, w):
            acc = acc + ubuf[POOL_HALO - k:POOL_HALO - k + tm, lo:hi]
        count = jnp.minimum(t + 1, w).astype(F32)
        pooled = acc / count - ug
        mixed = jnp.dot(pooled.astype(BF16), pw_ref[g], preferred_element_type=F32)
        parts.append(mixed * ps_ref[:, lo:hi])
    conv = cw_ref[CONV_K - 1:CONV_K, :] * z
    for k in range(CONV_K - 1):
        off = CONV_HALO - (CONV_K - 1) + k
        conv = conv + cw_ref[k:k + 1, :] * zbuf[off:off + tm, :]
    parts.append(gate_post * conv)
    y = jnp.concatenate(parts, axis=1).astype(BF16)
    o_ref[...] = x + jnp.dot(y, wout_ref[...], preferred_element_type=F32)


def _mixer(x, g, w_in, pool_w, pool_scale, conv_w, w_out, tm):
    b, s, d = x.shape
    pool_width = pool_scale.shape[-1]
    conv_c = conv_w.shape[-1]
    return pl.pallas_call(
        _mixer_kernel,
        out_shape=jax.ShapeDtypeStruct((b, s, d), F32),
        grid=(b, s // tm),
        in_specs=[
            pl.BlockSpec((None, tm, d), lambda i, j: (i, j, 0)),
            _const_spec((1, d)),
            _const_spec(w_in.shape),
            _const_spec(pool_w.shape),
            _const_spec((1, pool_width)),
            _const_spec(conv_w.shape),
            _const_spec(w_out.shape),
        ],
        out_specs=pl.BlockSpec((None, tm, d), lambda i, j: (i, j, 0)),
        scratch_shapes=[pltpu.VMEM((POOL_HALO + tm, pool_width), F32),
                        pltpu.VMEM((CONV_HALO + tm, conv_c), F32)],
        compiler_params=_params(2),
        name="l0_mixer",
    )(x, g.reshape(1, d), w_in, pool_w, pool_scale.reshape(1, pool_width), conv_w, w_out)


def _ffn_kernel(x_ref, g_ref, wg_ref, wu_ref, wd_ref, *rest, fc, n_cast):
    cast_in, (o_ref, *cast_out), (h_scr, t_scr) = rest[:n_cast], rest[n_cast:2 * n_cast + 1], rest[2 * n_cast + 1:]
    x = x_ref[...]
    h_scr[...] = _rms(x, g_ref[...]).astype(BF16)
    n_chunks = wg_ref.shape[1] // fc

    def body(c, carry):
        off = pl.multiple_of(c * fc, fc)
        h = h_scr[...]
        a = jnp.dot(h, wg_ref[:, pl.ds(off, fc)], preferred_element_type=F32)
        b = jnp.dot(h, wu_ref[:, pl.ds(off, fc)], preferred_element_type=F32)
        t_scr[:, pl.ds(off, fc)] = (a * jax.nn.sigmoid(a) * b).astype(BF16)
        return carry

    lax.fori_loop(0, n_chunks, body, 0, unroll=True)
    o_ref[...] = x + jnp.dot(t_scr[...], wd_ref[...], preferred_element_type=F32)
    for src, dst in zip(cast_in, cast_out):
        dst[...] = src[...].astype(BF16)


def _ffn(x, g, w_gate, w_up, w_down, to_cast, tm, fc):
    n, d = x.shape
    steps = n // tm
    slabs = []
    for w in to_cast:
        rows = w.shape[0] // steps
        assert rows * steps == w.shape[0] and rows % (2 * SUBLANES) == 0, w.shape
        slabs.append(pl.BlockSpec((rows, w.shape[1]), lambda i: (i, 0)))
    outs = pl.pallas_call(
        functools.partial(_ffn_kernel, fc=fc, n_cast=len(to_cast)),
        out_shape=(jax.ShapeDtypeStruct((n, d), F32),
                   *[jax.ShapeDtypeStruct(w.shape, BF16) for w in to_cast]),
        grid=(steps,),
        in_specs=[
            pl.BlockSpec((tm, d), lambda i: (i, 0)),
            _const_spec((1, d)),
            _const_spec(w_gate.shape),
            _const_spec(w_up.shape),
            _const_spec(w_down.shape),
            *slabs,
        ],
        out_specs=(pl.BlockSpec((tm, d), lambda i: (i, 0)), *slabs),
        scratch_shapes=[pltpu.VMEM((tm, d), BF16), pltpu.VMEM((tm, w_gate.shape[1]), BF16)],
        compiler_params=_params(1),
        name="l0_ffn",
    )(x, g.reshape(1, d), w_gate, w_up, w_down, *to_cast)
    return outs[0], outs[1:]


_NT = (((1,), (1,)), ((), ()))
_TN = (((0,), (0,)), ((), ()))


def _qkv_kernel(x_ref, g_ref, wqt_ref, bq_ref, wk_ref, bk_ref, wvt_ref, bv_ref, qt_ref, k_ref, vt_ref):
    h = _rms(x_ref[...], g_ref[...]).astype(BF16)
    qt = lax.dot_general(wqt_ref[...], h, _NT, preferred_element_type=F32) + bq_ref[...]
    qt_ref[...] = (qt * (HEAD_DIM ** -0.5)).astype(BF16)
    k_ref[...] = (jnp.dot(h, wk_ref[...], preferred_element_type=F32) + bk_ref[...]).astype(BF16)
    vt = lax.dot_general(wvt_ref[...], h, _NT, preferred_element_type=F32) + bv_ref[...]
    vt_ref[...] = vt.astype(BF16)


def _qkv(x, g, w_qkv, b_qkv, dq, dk, tm):
    b, s, d = x.shape
    wqt = w_qkv[:, :dq].T.astype(BF16)
    wk = w_qkv[:, dq:dq + dk].astype(BF16)
    wvt = w_qkv[:, dq + dk:].T.astype(BF16)
    bq = b_qkv[:dq].reshape(dq, 1)
    bk = b_qkv[dq:dq + dk].reshape(1, dk)
    bv = b_qkv[dq + dk:].reshape(dk, 1)
    return pl.pallas_call(
        _qkv_kernel,
        out_shape=(jax.ShapeDtypeStruct((b, dq, s), BF16),
                   jax.ShapeDtypeStruct((b, s, dk), BF16),
                   jax.ShapeDtypeStruct((b, dk, s), BF16)),
        grid=(b, s // tm),
        in_specs=[
            pl.BlockSpec((None, tm, d), lambda bi, i: (bi, i, 0)),
            _const_spec((1, d)),
            _const_spec(wqt.shape), _const_spec(bq.shape),
            _const_spec(wk.shape), _const_spec(bk.shape),
            _const_spec(wvt.shape), _const_spec(bv.shape),
        ],
        out_specs=(pl.BlockSpec((None, dq, tm), lambda bi, i: (bi, 0, i)),
                   pl.BlockSpec((None, tm, dk), lambda bi, i: (bi, i, 0)),
                   pl.BlockSpec((None, dk, tm), lambda bi, i: (bi, 0, i))),
        compiler_params=_params(2),
        name="l1_qkv",
    )(x, g.reshape(1, d), wqt, bq, wk, bk, wvt, bv)


def _bias_kernel(onehot_ref, rb_ref, o_ref):
    o_ref[...] = jnp.dot(onehot_ref[...], rb_ref[...], preferred_element_type=F32,
                         precision=lax.Precision.HIGHEST)


def _t5_bucket(rel):
    nb = N_BUCKETS // 2
    max_exact = nb // 2
    ret = jnp.where(rel > 0, nb, 0)
    n = jnp.abs(rel)
    nf = jnp.maximum(n, 1).astype(jnp.float32)
    large = max_exact + (jnp.log(nf / max_exact) / math.log(MAX_DISTANCE / max_exact)
                         * (nb - max_exact)).astype(jnp.int32)
    large = jnp.minimum(large, nb - 1)
    return ret + jnp.where(n < max_exact, n, large)


def _rel_bias_table(rel_bias):
    n_heads = rel_bias.shape[1]
    pad = WINDOW_CHUNKS * CHUNK
    rel = (jnp.arange(KV_SPAN) - pad)[None, :] - jnp.arange(CHUNK)[:, None]
    bucket = _t5_bucket(rel).reshape(-1)
    onehot = (bucket[:, None] == jnp.arange(N_BUCKETS)[None, :]).astype(F32)
    rows = CHUNK * KV_SPAN
    rb = jnp.pad(rel_bias.astype(F32), ((0, 0), (0, LANES - n_heads)))
    out = pl.pallas_call(
        _bias_kernel,
        out_shape=jax.ShapeDtypeStruct((rows, LANES), F32),
        name="l1_rel_bias",
    )(onehot, rb)
    return jnp.transpose(out[:, :n_heads].reshape(CHUNK, KV_SPAN, n_heads), (2, 0, 1))


PAIR = 2 * CHUNK
PAIR_KEYS = KV_SPAN + CHUNK


def _attn_kernel(qt_ref, kp_ref, km_ref, vtp_ref, vtm_ref, bias_ref, sink_ref, ot_ref, kbuf, vtbuf,
                 s_a, s_b, *, n_heads):
    i = pl.program_id(1)
    tq = qt_ref.shape[1]
    pad = WINDOW_CHUNKS * CHUNK
    per_kv = n_heads // N_KV_HEADS
    kbuf[0:pad, :] = kp_ref[...]
    kbuf[pad:pad + tq, :] = km_ref[...]
    vtbuf[:, 0:pad] = vtp_ref[...]
    vtbuf[:, pad:pad + tq] = vtm_ref[...]
    zeros = jnp.zeros((HEAD_DIM, PAIR), BF16)
    first = lax.broadcasted_iota(jnp.int32, (CHUNK, PAIR), 1) < CHUNK
    n_pairs = tq // PAIR

    def scores(jj, s_scr):
        c0 = jj * PAIR
        kc = kbuf[c0:c0 + PAIR_KEYS, :]
        for h in range(n_heads):
            pieces = [zeros] * N_KV_HEADS
            pieces[h // per_kv] = qt_ref[h * HEAD_DIM:(h + 1) * HEAD_DIM, c0:c0 + PAIR]
            rhs = jnp.concatenate(pieces, axis=0)
            s_scr[h] = jnp.dot(kc, rhs, preferred_element_type=F32)

    def softmax_values(jj, s_scr):
        c0 = jj * PAIR
        variant = jnp.where(i == 0, 1, 0) if jj == 0 else 0
        for h in range(n_heads):
            kv = h // per_kv
            mid = s_scr[h, CHUNK:KV_SPAN, :] + bias_ref[variant, h, 0:KV_SPAN - CHUNK, :]
            edge = (jnp.where(first, s_scr[h, 0:CHUNK, :], s_scr[h, KV_SPAN:PAIR_KEYS, :])
                    + bias_ref[variant, h, KV_SPAN - CHUNK:KV_SPAN, :])
            sk = sink_ref[h]
            m_ = jnp.maximum(jnp.maximum(jnp.max(mid, axis=0, keepdims=True),
                                         jnp.max(edge, axis=0, keepdims=True)), sk)
            p_mid = jnp.exp(mid - m_)
            p_edge = jnp.exp(edge - m_)
            denom = (jnp.sum(p_mid, axis=0, keepdims=True) + jnp.sum(p_edge, axis=0, keepdims=True)
                     + jnp.exp(sk - m_))
            p_mid = (p_mid / denom).astype(BF16)
            p_edge = (p_edge / denom).astype(BF16)
            zero = jnp.zeros_like(p_edge)
            p = jnp.concatenate([jnp.where(first, p_edge, zero), p_mid, jnp.where(first, zero, p_edge)],
                                axis=0)
            vt = vtbuf[kv * HEAD_DIM:(kv + 1) * HEAD_DIM, c0:c0 + PAIR_KEYS]
            o = jnp.dot(vt, p, preferred_element_type=F32)
            ot_ref[h * HEAD_DIM:(h + 1) * HEAD_DIM, c0:c0 + PAIR] = o.astype(BF16)

    bufs = (s_a, s_b)
    scores(0, bufs[0])
    for jj in range(n_pairs):
        if jj + 1 < n_pairs:
            scores(jj + 1, bufs[(jj + 1) % 2])
        softmax_values(jj, bufs[jj % 2])


def _pair_bias(bias):
    bt = jnp.transpose(bias, (0, 2, 1)).astype(F32)
    mid = jnp.concatenate([bt[:, CHUNK:, :], bt[:, :KV_SPAN - CHUNK, :]], axis=2)
    edge = jnp.concatenate([bt[:, :CHUNK, :], bt[:, KV_SPAN - CHUNK:, :]], axis=2)
    table = jnp.concatenate([mid, edge], axis=1)
    row = jnp.arange(KV_SPAN)[:, None]
    lane = jnp.arange(PAIR)[None, :]
    key = jnp.where(row < KV_SPAN - CHUNK, row + CHUNK,
                    jnp.where(lane < CHUNK, row - (KV_SPAN - CHUNK), row + CHUNK))
    masked = jnp.where(key[None] < WINDOW_CHUNKS * CHUNK, NEG_INF, table)
    return jnp.stack([table, masked])


def _attention(qt, k, vt, bias, sinks, tq):
    b, dq, s = qt.shape
    dk = k.shape[-1]
    n_heads = dq // HEAD_DIM
    pad = WINDOW_CHUNKS * CHUNK
    ratio = tq // pad
    table = _pair_bias(bias)
    prev = lambda i: jnp.maximum(i * ratio - 1, 0)
    return pl.pallas_call(
        functools.partial(_attn_kernel, n_heads=n_heads),
        out_shape=jax.ShapeDtypeStruct((b, dq, s), BF16),
        grid=(b, s // tq),
        in_specs=[
            pl.BlockSpec((None, dq, tq), lambda bi, i: (bi, 0, i)),
            pl.BlockSpec((None, pad, dk), lambda bi, i: (bi, prev(i), 0)),
            pl.BlockSpec((None, tq, dk), lambda bi, i: (bi, i, 0)),
            pl.BlockSpec((None, dk, pad), lambda bi, i: (bi, 0, prev(i))),
            pl.BlockSpec((None, dk, tq), lambda bi, i: (bi, 0, i)),
            _const_spec(table.shape),
            pl.BlockSpec(memory_space=pltpu.SMEM),
        ],
        out_specs=pl.BlockSpec((None, dq, tq), lambda bi, i: (bi, 0, i)),
        scratch_shapes=[pltpu.VMEM((pad + tq, dk), BF16), pltpu.VMEM((dk, pad + tq), BF16),
                        pltpu.VMEM((n_heads, PAIR_KEYS, PAIR), F32),
                        pltpu.VMEM((n_heads, PAIR_KEYS, PAIR), F32)],
        compiler_params=_params(2),
        name="l1_attention",
    )(qt, k, k, vt, vt, table, sinks.astype(F32))


INFO_S0, INFO_S1, INFO_W0, INFO_W1 = range(4)


EXPERT_ROWS = 16


def _oproj_router_kernel(ot_ref, x_ref, wo_ref, bo_ref, g_ref, wrt_ref, x3_ref, info_ref, cnt_ref,
                         *, n_experts):
    tm = x_ref.shape[0]
    attn = lax.dot_general(ot_ref[...], wo_ref[...], _TN, preferred_element_type=F32)
    x3 = x_ref[...] + attn + bo_ref[...]
    x3_ref[...] = x3
    h = _rms(x3, g_ref[...])
    h_hi = h.astype(BF16)
    h_lo = (h - h_hi.astype(F32)).astype(BF16)
    wrt = wrt_ref[...]
    by_hi = lax.dot_general(wrt, h_hi, _NT, preferred_element_type=F32)
    by_lo = lax.dot_general(wrt[:EXPERT_ROWS], h_lo, _NT, preferred_element_type=F32)
    logits = (by_hi[:EXPERT_ROWS] + by_hi[EXPERT_ROWS:]) + by_lo
    row_i = lax.broadcasted_iota(jnp.int32, (EXPERT_ROWS, tm), 0)
    row = row_i.astype(F32)
    neg = jnp.float32(-jnp.inf)
    logits = jnp.where(row_i < n_experts, logits, neg)
    m0 = jnp.max(logits, axis=0, keepdims=True)
    e0 = jnp.min(jnp.where(logits == m0, row, float(EXPERT_ROWS)), axis=0, keepdims=True)
    rest = jnp.where(row == e0, neg, logits)
    m1 = jnp.max(rest, axis=0, keepdims=True)
    e1 = jnp.min(jnp.where(rest == m1, row, float(EXPERT_ROWS)), axis=0, keepdims=True)
    t = jnp.exp(m1 - m0)
    w0 = 1.0 / (1.0 + t)
    w1 = t / (1.0 + t)
    hot0 = row == e0
    hot1 = row == e1
    both = jnp.where(hot0 | hot1, 1.0, 0.0)
    u = lax.broadcasted_iota(jnp.int32, (tm, tm), 0)
    v = lax.broadcasted_iota(jnp.int32, (tm, tm), 1)
    earlier = (u < v).astype(BF16)
    cnt = jnp.broadcast_to(jnp.sum(both, axis=1, keepdims=True), (EXPERT_ROWS, LANES))
    cnt_ref[...] = cnt
    upto = cnt
    row_l = lax.broadcasted_iota(jnp.int32, (EXPERT_ROWS, LANES), 0)
    shift = 1
    while shift < n_experts:
        upto = upto + jnp.where(row_l >= shift, pltpu.roll(upto, shift, axis=0), 0.0)
        shift *= 2
    before = jnp.dot(both.astype(BF16), earlier, preferred_element_type=F32) + (upto - cnt)[:, 0:1]
    r0 = jnp.sum(jnp.where(hot0, before, 0.0), axis=0, keepdims=True)
    r1 = jnp.sum(jnp.where(hot1, before, 0.0), axis=0, keepdims=True)
    fields = {INFO_S0: r0 * SUBLANES, INFO_S1: r1 * SUBLANES, INFO_W0: w0, INFO_W1: w1}
    info_ref[...] = jnp.concatenate(
        [fields.get(k, jnp.zeros((1, tm), F32)) for k in range(SUBLANES)], axis=0)


def _oproj_router(ot, x, w_o, b_o, g, w_router, tm):
    n, d = x.shape
    dq, seq = ot.shape[1:]
    per_seq = seq // tm
    n_experts = w_router.shape[1]
    assert n_experts <= EXPERT_ROWS
    wrt = jnp.pad(w_router.astype(F32).T, ((0, EXPERT_ROWS - n_experts), (0, 0)))
    wrt_hi = wrt.astype(BF16)
    wrt_lo = (wrt - wrt_hi.astype(F32)).astype(BF16)
    wrt = jnp.concatenate([wrt_hi, wrt_lo], axis=0)
    return pl.pallas_call(
        functools.partial(_oproj_router_kernel, n_experts=n_experts),
        out_shape=(jax.ShapeDtypeStruct((n, d), F32),
                   jax.ShapeDtypeStruct((n // tm, SUBLANES, tm), F32),
                   jax.ShapeDtypeStruct((n // tm, EXPERT_ROWS, LANES), F32)),
        grid=(n // tm,),
        in_specs=[
            pl.BlockSpec((None, dq, tm), lambda i: (i // per_seq, 0, i % per_seq)),
            pl.BlockSpec((tm, d), lambda i: (i, 0)),
            _const_spec(w_o.shape),
            _const_spec((1, d)),
            _const_spec((1, d)),
            _const_spec(wrt.shape),
        ],
        out_specs=(pl.BlockSpec((tm, d), lambda i: (i, 0)),
                   pl.BlockSpec((None, SUBLANES, tm), lambda i: (i, 0, 0)),
                   pl.BlockSpec((None, EXPERT_ROWS, LANES), lambda i: (i, 0, 0))),
        compiler_params=_params(1),
        name="l1_oproj_router",
    )(ot, x, w_o, b_o.reshape(1, d), g.reshape(1, d), wrt)


def _pieces(n, n_bits, fn):
    for bit in reversed(range(n_bits)):
        size = 1 << bit
        done = n & ~((size << 1) - 1)
        pl.when((n & size) != 0)(functools.partial(fn, done, size))


def _rows(ref, first_row, n_rows):
    return ref.at[pl.ds(pl.multiple_of(first_row * SUBLANES, SUBLANES), n_rows * SUBLANES)]


def _segment_copies(tabs, i, n_experts, lbuf, hbm, sem, to_hbm, act):
    n_tab, o_tab, d_tab = tabs[:3]
    n_bits = (lbuf.shape[0] // SUBLANES // TOP_K).bit_length()
    for e in range(n_experts):
        local0 = o_tab[i * n_experts + e]
        sorted0 = d_tab[i * n_experts + e]

        def piece(done, size, local0=local0, sorted0=sorted0):
            local = _rows(lbuf, local0 + done, size)
            remote = _rows(hbm, sorted0 + done, size)
            act(pltpu.make_async_copy(local, remote, sem) if to_hbm
                else pltpu.make_async_copy(remote, local, sem))

        _pieces(n_tab[i * n_experts + e], n_bits, piece)


def _dispatch_kernel(n_tab, o_tab, d_tab, pad0_tab, padn_tab, nv_tab, slot_ref, x_ref, g_ref, xs_hbm,
                     hbuf, lbuf, sems, *, n_experts, tme):
    i = pl.program_id(0)
    tm = x_ref.shape[0]
    n_slots = TOP_K * tm
    h = _rms(x_ref[...], g_ref[...])
    for c in range(x_ref.shape[1] // LANES):
        hbuf[pl.ds(c, tm, stride=SUBLANES), :] = h[:, c * LANES:(c + 1) * LANES]
    slot = i % 2
    sorted_rows = lbuf.at[slot]

    def token(t, carry):
        row = hbuf[pl.ds(pl.multiple_of(t * SUBLANES, SUBLANES), SUBLANES), :]
        for k in range(TOP_K):
            dst = pl.multiple_of(slot_ref[k * tm + t], SUBLANES)
            sorted_rows[pl.ds(dst, SUBLANES), :] = row
        return carry

    lax.fori_loop(0, tm, token, 0, unroll=ROW_LOOP_UNROLL)
    tabs = (n_tab, o_tab, d_tab)

    def copies(tile, s, act):
        _segment_copies(tabs, tile, n_experts, lbuf.at[s], xs_hbm, sems.at[s], True, act)

    copies(i, slot, lambda cp: cp.start())

    def wait_all(s):
        pltpu.make_async_copy(lbuf.at[s], _rows(xs_hbm, 0, n_slots), sems.at[s]).wait()

    @pl.when(i > 0)
    def _():
        wait_all(1 - slot)

    @pl.when(i == pl.num_programs(0) - 1)
    def _():
        wait_all(slot)
        zeros = lbuf.at[slot]
        zeros[0:tme * SUBLANES, :] = jnp.zeros((tme * SUBLANES, LANES), F32)

        def pads(act):
            for e in range(n_experts):
                def piece(done, size, e=e):
                    act(pltpu.make_async_copy(_rows(zeros, 0, size),
                                              _rows(xs_hbm, pad0_tab[e] + done, size), sems.at[slot]))
                _pieces(padn_tab[e], (tme - 1).bit_length(), piece)
            n_tiles = xs_hbm.shape[0] // (tme * SUBLANES)
            for t in range(n_experts):
                tile = nv_tab[0] + t
                pl.when(tile < n_tiles)(lambda tile=tile: act(pltpu.make_async_copy(
                    _rows(zeros, 0, tme), _rows(xs_hbm, tile * tme, tme), sems.at[slot])))

        pads(lambda cp: cp.start())
        pads(lambda cp: cp.wait())


def _dispatch(x, g, slots, tabs, n_experts, n_rows, tm, tme):
    n, d = x.shape
    assert TOP_K * tm >= tme
    grid_spec = pltpu.PrefetchScalarGridSpec(
        num_scalar_prefetch=len(tabs),
        grid=(n // tm,),
        in_specs=[
            pl.BlockSpec((TOP_K * tm,), lambda i, *_: (i,), memory_space=pltpu.SMEM),
            pl.BlockSpec((tm, d), lambda i, *_: (i, 0)),
            pl.BlockSpec((1, d), lambda i, *_: (0, 0)),
        ],
        out_specs=pl.BlockSpec(memory_space=pl.ANY),
        scratch_shapes=[pltpu.VMEM((tm * SUBLANES, LANES), F32),
                        pltpu.VMEM((2, TOP_K * tm * SUBLANES, LANES), F32),
                        pltpu.SemaphoreType.DMA((2,))],
    )
    return pl.pallas_call(
        functools.partial(_dispatch_kernel, n_experts=n_experts, tme=tme),
        out_shape=jax.ShapeDtypeStruct((n_rows * SUBLANES, LANES), F32),
        grid_spec=grid_spec,
        compiler_params=_params(1),
        name="l1_moe_dispatch",
    )(*tabs, slots, x, g.reshape(1, d))


def _experts_kernel(te_ref, tv_ref, nv_ref, xs_ref, wg_ref, wu_ref, wd_ref, ys_ref, t_scr, *, sub):
    del te_ref, nv_ref
    j = pl.program_id(0)
    tm = t_scr.shape[0]
    n_slabs = wg_ref.shape[0] // LANES

    @pl.when(tv_ref[j] == 1)
    def _():
        slabs = [xs_ref[pl.ds(s, tm, stride=SUBLANES), :] for s in range(n_slabs)]
        x = jnp.concatenate(slabs, axis=1).astype(BF16)

        def body(q, carry):
            off = pl.multiple_of(q * sub, sub)
            a = jnp.dot(x, wg_ref[:, pl.ds(off, sub)], preferred_element_type=F32)
            b = jnp.dot(x, wu_ref[:, pl.ds(off, sub)], preferred_element_type=F32)
            t_scr[:, pl.ds(off, sub)] = (a * jax.nn.sigmoid(a) * b).astype(BF16)
            return carry
        lax.fori_loop(0, wg_ref.shape[1] // sub, body, 0, unroll=True)
        y = jnp.dot(t_scr[...], wd_ref[...], preferred_element_type=F32)
        for s in range(n_slabs):
            ys_ref[pl.ds(s, tm, stride=SUBLANES), :] = y[:, s * LANES:(s + 1) * LANES]

    @pl.when(tv_ref[j] != 1)
    def _():
        ys_ref[...] = jnp.zeros_like(ys_ref)


def _experts(xs, tile_expert, tile_valid, n_valid, w_gate, w_up, w_down, d, tm, sub):
    n_tiles = tile_expert.shape[0]
    ff = w_gate.shape[2]
    grid_spec = pltpu.PrefetchScalarGridSpec(
        num_scalar_prefetch=3,
        grid=(n_tiles,),
        in_specs=[
            pl.BlockSpec((tm * SUBLANES, LANES), lambda j, te, tv, nv: (jnp.minimum(j, nv[0] - 1), 0)),
            pl.BlockSpec((None, d, ff), lambda j, te, tv, nv: (te[j], 0, 0)),
            pl.BlockSpec((None, d, ff), lambda j, te, tv, nv: (te[j], 0, 0)),
            pl.BlockSpec((None, ff, d), lambda j, te, tv, nv: (te[j], 0, 0)),
        ],
        out_specs=pl.BlockSpec((tm * SUBLANES, LANES), lambda j, te, tv, nv: (j, 0)),
        scratch_shapes=[pltpu.VMEM((tm, ff), BF16)],
    )
    return pl.pallas_call(
        functools.partial(_experts_kernel, sub=sub),
        out_shape=jax.ShapeDtypeStruct(xs.shape, F32),
        grid_spec=grid_spec,
        compiler_params=_params(1, vmem=EXPERT_VMEM_LIMIT),
        name="l1_moe_experts",
    )(tile_expert, tile_valid, n_valid, xs, w_gate, w_up, w_down)


def _combine_kernel(n_tab, o_tab, d_tab, slot_ref, w_ref, x_ref, g_ref, ys_hbm, o_ref, lbuf, mbuf, sems,
                    *, n_experts):
    i = pl.program_id(0)
    tm = x_ref.shape[0]
    n_slots = TOP_K * tm
    tabs = (n_tab, o_tab, d_tab)
    slot = i % 2

    def copies(tile, s, act):
        _segment_copies(tabs, tile, n_experts, lbuf.at[s], ys_hbm, sems.at[s], False, act)

    @pl.when(i == 0)
    def _():
        copies(i, slot, lambda cp: cp.start())

    @pl.when(i + 1 < pl.num_programs(0))
    def _():
        copies(i + 1, 1 - slot, lambda cp: cp.start())

    pltpu.make_async_copy(_rows(ys_hbm, 0, n_slots), lbuf.at[slot], sems.at[slot]).wait()
    y = lbuf.at[slot]

    def token(t, carry):
        acc = None
        for k in range(TOP_K):
            row = y[pl.ds(pl.multiple_of(slot_ref[k * tm + t], SUBLANES), SUBLANES), :]
            term = w_ref[k * tm + t] * row
            acc = term if acc is None else acc + term
        mbuf[pl.ds(pl.multiple_of(t * SUBLANES, SUBLANES), SUBLANES), :] = acc
        return carry

    lax.fori_loop(0, tm, token, 0, unroll=ROW_LOOP_UNROLL)
    moe = jnp.concatenate([mbuf[pl.ds(c, tm, stride=SUBLANES), :]
                           for c in range(x_ref.shape[1] // LANES)], axis=1)
    o_ref[...] = _rms(x_ref[...] + moe, g_ref[...])


def _combine(x, slots, weights, tabs, n_experts, ys, g, tm):
    n, d = x.shape
    grid_spec = pltpu.PrefetchScalarGridSpec(
        num_scalar_prefetch=3,
        grid=(n // tm,),
        in_specs=[
            pl.BlockSpec((TOP_K * tm,), lambda i, *_: (i,), memory_space=pltpu.SMEM),
            pl.BlockSpec((TOP_K * tm,), lambda i, *_: (i,), memory_space=pltpu.SMEM),
            pl.BlockSpec((tm, d), lambda i, *_: (i, 0)),
            pl.BlockSpec((1, d), lambda i, *_: (0, 0)),
            pl.BlockSpec(memory_space=pl.ANY),
        ],
        out_specs=pl.BlockSpec((tm, d), lambda i, *_: (i, 0)),
        scratch_shapes=[pltpu.VMEM((2, TOP_K * tm * SUBLANES, LANES), F32),
                        pltpu.VMEM((tm * SUBLANES, LANES), F32),
                        pltpu.SemaphoreType.DMA((2,))],
    )
    return pl.pallas_call(
        functools.partial(_combine_kernel, n_experts=n_experts),
        out_shape=jax.ShapeDtypeStruct((n, d), F32),
        grid_spec=grid_spec,
        compiler_params=_params(1),
        name="l1_moe_combine",
    )(*tabs, slots, weights, x, g.reshape(1, d), ys)


def _forward(x, rel_bias, ev_norm_mix, ev_w_in, ev_pool_w, ev_pool_scale, ev_conv_w, ev_w_out,
             ev_norm_ffn, ev_ffn_gate, ev_ffn_up, ev_ffn_down, od_norm_mix, od_w_qkv, od_b_qkv,
             od_sinks, od_w_o, od_b_o, od_norm_ffn, od_router, od_exp_gate, od_exp_up,
             od_exp_down, final_norm, *, tm, tq, tme, fc_dense, tr, tx):
    b, s, d = x.shape
    n = b * s
    n_heads = od_sinks.shape[-1]
    dq = n_heads * HEAD_DIM
    dk = N_KV_HEADS * HEAD_DIM
    n_experts = od_router.shape[-1]

    x = _mixer(x, ev_norm_mix[0], ev_w_in[0].astype(BF16), ev_pool_w[0].astype(BF16),
               ev_pool_scale[0], ev_conv_w[0], ev_w_out[0].astype(BF16), tx)
    x = x.reshape(n, d)
    ff = od_exp_gate.shape[-1]
    expert_f32 = (od_exp_gate[0].reshape(n_experts * d, ff), od_exp_up[0].reshape(n_experts * d, ff),
                  od_exp_down[0].reshape(n_experts * ff, d))
    x, (exp_gate, exp_up, exp_down) = _ffn(
        x, ev_norm_ffn[0], ev_ffn_gate[0].astype(BF16), ev_ffn_up[0].astype(BF16),
        ev_ffn_down[0].astype(BF16), expert_f32, tm, fc_dense)

    qt, k, vt = _qkv(x.reshape(b, s, d), od_norm_mix[0], od_w_qkv[0], od_b_qkv[0], dq, dk, tx)
    bias = _rel_bias_table(rel_bias)
    attn_t = _attention(qt, k, vt, bias, od_sinks[0], tq)
    x, info, counts = _oproj_router(attn_t, x, od_w_o[0].astype(BF16), od_b_o[0], od_norm_ffn[0],
                                    od_router[0], tr)

    per_tile = counts[:, :n_experts, 0].astype(jnp.int32)
    local_off = jnp.cumsum(per_tile, axis=1) - per_tile
    earlier = jnp.cumsum(per_tile, axis=0) - per_tile
    totals = jnp.sum(per_tile, axis=0)
    tiles_per = (totals + tme - 1) // tme
    tile_end = jnp.cumsum(tiles_per)
    group_start = (tile_end - tiles_per) * tme
    n_tiles = (TOP_K * n) // tme + n_experts
    tile_ids = jnp.arange(n_tiles, dtype=jnp.int32)
    tile_valid = (tile_ids < tile_end[-1]).astype(jnp.int32)
    last_valid = jnp.minimum(tile_ids, tile_end[-1] - 1)
    tile_expert = jnp.sum(last_valid[:, None] >= tile_end[None, :], axis=1).astype(jnp.int32)
    tabs = (per_tile.reshape(-1), local_off.reshape(-1), (group_start[None, :] + earlier).reshape(-1))
    pad_tabs = (group_start + totals, tiles_per * tme - totals, tile_end[-1:])

    slots = info[:, INFO_S0:INFO_S1 + 1, :].astype(jnp.int32).reshape(-1)
    weights = info[:, INFO_W0:INFO_W1 + 1, :].reshape(-1)

    xs = _dispatch(x, od_norm_ffn[0], slots, tabs + pad_tabs, n_experts, n_tiles * tme, tr, tme)
    ys = _experts(xs, tile_expert, tile_valid, tile_end[-1:], exp_gate.reshape(n_experts, d, ff),
                  exp_up.reshape(n_experts, d, ff), exp_down.reshape(n_experts, ff, d), d, tme, fc_dense)
    out = _combine(x, slots, weights, tabs, n_experts, ys, final_norm, tr)
    return out.reshape(b, s, d)


def kernel(x, rel_bias, ev_norm_mix, ev_w_in, ev_pool_w, ev_pool_scale, ev_conv_w, ev_w_out, ev_norm_ffn, ev_ffn_gate, ev_ffn_up, ev_ffn_down, od_norm_mix, od_w_qkv, od_b_qkv, od_sinks, od_w_o, od_b_o, od_norm_ffn, od_router, od_exp_gate, od_exp_up, od_exp_down, final_norm):
    return _forward(x, rel_bias, ev_norm_mix, ev_w_in, ev_pool_w, ev_pool_scale, ev_conv_w, ev_w_out,
                    ev_norm_ffn, ev_ffn_gate, ev_ffn_up, ev_ffn_down, od_norm_mix, od_w_qkv, od_b_qkv,
                    od_sinks, od_w_o, od_b_o, od_norm_ffn, od_router, od_exp_gate, od_exp_up,
                    od_exp_down, final_norm, tm=512, tq=512, tme=512, fc_dense=256, tr=1024, tx=1024)
```

```python
import functools
import math

import jax
import jax.numpy as jnp
from jax import lax
from jax.experimental import pallas as pl
from jax.experimental.pallas import tpu as pltpu

F32 = jnp.float32
BF16 = jnp.bfloat16

RMS_EPS = 1e-6
CHUNK = 64
POOL_WINDOWS = (2, 4, 8, 16)
CONV_K = 3
HEAD_DIM = 64
N_KV_HEADS = 2
WINDOW_CHUNKS = 2
KV_SPAN = (WINDOW_CHUNKS + 1) * CHUNK
N_BUCKETS = 32
MAX_DISTANCE = 128
TOP_K = 2
NEG_INF = -1e30

LANES = 128
SUBLANES = 8
VMEM_LIMIT = 56 * 1024 * 1024
EXPERT_VMEM_LIMIT = 62 * 1024 * 1024

POOL_HALO = 16
CONV_HALO = 8
ROW_LOOP_UNROLL = 16
DMA_QUEUES = 2

def _rms(x, g):
    return x * lax.rsqrt(jnp.mean(x * x, axis=-1, keepdims=True) + RMS_EPS) * g


def _params(n_axes, vmem=VMEM_LIMIT):
    return pltpu.CompilerParams(dimension_semantics=("arbitrary",) * n_axes,
                                vmem_limit_bytes=vmem)


def _const_spec(shape):
    nd = len(shape)
    return pl.BlockSpec(shape, lambda *_: (0,) * nd, pipeline_mode=pl.Buffered(1))


def _mixer_kernel(x_ref, g_ref, win_ref, pw_ref, ps_ref, cw_ref, wout_ref, o_ref, ubuf, zbuf):
    s = pl.program_id(1)
    tm = x_ref.shape[0]
    pool_w = ubuf.shape[1]
    grp = pool_w // len(POOL_WINDOWS)
    conv_c = zbuf.shape[1]

    @pl.when(s == 0)
    def _():
        ubuf[0:POOL_HALO, :] = jnp.zeros((POOL_HALO, pool_w), F32)
        zbuf[0:CONV_HALO, :] = jnp.zeros((CONV_HALO, conv_c), F32)

    @pl.when(s > 0)
    def _():
        ubuf[0:POOL_HALO, :] = ubuf[tm:tm + POOL_HALO, :]
        zbuf[0:CONV_HALO, :] = zbuf[tm:tm + CONV_HALO, :]

    x = x_ref[...]
    h = _rms(x, g_ref[...]).astype(BF16)
    proj = jnp.dot(h, win_ref[...], preferred_element_type=F32)
    u = proj[:, :pool_w]
    gate_post = proj[:, pool_w:pool_w + conv_c]
    gate_pre = proj[:, pool_w + conv_c:pool_w + 2 * conv_c]
    v = proj[:, pool_w + 2 * conv_c:]
    ubuf[POOL_HALO:POOL_HALO + tm, :] = u
    z = gate_pre * v
    zbuf[CONV_HALO:CONV_HALO + tm, :] = z

    t = s * tm + lax.broadcasted_iota(jnp.int32, (tm, 1), 0)
    parts = []
    for g, w in enumerate(POOL_WINDOWS):
        lo, hi = g * grp, (g + 1) * grp
        ug = u[:, lo:hi]
        acc = ug
        for k in range(1, w):
            acc = acc + ubuf[POOL_HALO - k:POOL_HALO - k + tm, lo:hi]
        count = jnp.minimum(t + 1, w).astype(F32)
        pooled = acc / count - ug
        mixed = jnp.dot(pooled.astype(BF16), pw_ref[g], preferred_element_type=F32)
        parts.append(mixed * ps_ref[:, lo:hi])
    conv = cw_ref[CONV_K - 1:CONV_K, :] * z
    for k in range(CONV_K - 1):
        off = CONV_HALO - (CONV_K - 1) + k
        conv = conv + cw_ref[k:k + 1, :] * zbuf[off:off + tm, :]
    parts.append(gate_post * conv)
    y = jnp.concatenate(parts, axis=1).astype(BF16)
    o_ref[...] = x + jnp.dot(y, wout_ref[...], preferred_element_type=F32)


def _mixer(x, g, w_in, pool_w, pool_scale, conv_w, w_out, tm):
    b, s, d = x.shape
    pool_width = pool_scale.shape[-1]
    conv_c = conv_w.shape[-1]
    return pl.pallas_call(
        _mixer_kernel,
        out_shape=jax.ShapeDtypeStruct((b, s, d), F32),
        grid=(b, s // tm),
        in_specs=[
            pl.BlockSpec((None, tm, d), lambda i, j: (i, j, 0)),
            _const_spec((1, d)),
            _const_spec(w_in.shape),
            _const_spec(pool_w.shape),
            _const_spec((1, pool_width)),
            _const_spec(conv_w.shape),
            _const_spec(w_out.shape),
        ],
        out_specs=pl.BlockSpec((None, tm, d), lambda i, j: (i, j, 0)),
        scratch_shapes=[pltpu.VMEM((POOL_HALO + tm, pool_width), F32),
                        pltpu.VMEM((CONV_HALO + tm, conv_c), F32)],
        compiler_params=_params(2),
        name="l0_mixer",
    )(x, g.reshape(1, d), w_in, pool_w, pool_scale.reshape(1, pool_width), conv_w, w_out)


def _ffn_kernel(x_ref, g_ref, wg_ref, wu_ref, wd_ref, *rest, fc, n_cast):
    cast_in, (o_ref, *cast_out), (h_scr, t_scr) = rest[:n_cast], rest[n_cast:2 * n_cast + 1], rest[2 * n_cast + 1:]
    x = x_ref[...]
    h_scr[...] = _rms(x, g_ref[...]).astype(BF16)
    n_chunks = wg_ref.shape[1] // fc

    def body(c, carry):
        off = pl.multiple_of(c * fc, fc)
        h = h_scr[...]
        a = jnp.dot(h, wg_ref[:, pl.ds(off, fc)], preferred_element_type=F32)
        b = jnp.dot(h, wu_ref[:, pl.ds(off, fc)], preferred_element_type=F32)
        t_scr[:, pl.ds(off, fc)] = (a * jax.nn.sigmoid(a) * b).astype(BF16)
        return carry

    lax.fori_loop(0, n_chunks, body, 0, unroll=True)
    o_ref[...] = x + jnp.dot(t_scr[...], wd_ref[...], preferred_element_type=F32)
    for src, dst in zip(cast_in, cast_out):
        dst[...] = src[...].astype(BF16)


def _ffn(x, g, w_gate, w_up, w_down, to_cast, tm, fc):
    n, d = x.shape
    steps = n // tm
    slabs = []
    for w in to_cast:
        rows = w.shape[0] // steps
        assert rows * steps == w.shape[0] and rows % (2 * SUBLANES) == 0, w.shape
        slabs.append(pl.BlockSpec((rows, w.shape[1]), lambda i: (i, 0)))
    outs = pl.pallas_call(
        functools.partial(_ffn_kernel, fc=fc, n_cast=len(to_cast)),
        out_shape=(jax.ShapeDtypeStruct((n, d), F32),
                   *[jax.ShapeDtypeStruct(w.shape, BF16) for w in to_cast]),
        grid=(steps,),
        in_specs=[
            pl.BlockSpec((tm, d), lambda i: (i, 0)),
            _const_spec((1, d)),
            _const_spec(w_gate.shape),
            _const_spec(w_up.shape),
            _const_spec(w_down.shape),
            *slabs,
        ],
        out_specs=(pl.BlockSpec((tm, d), lambda i: (i, 0)), *slabs),
        scratch_shapes=[pltpu.VMEM((tm, d), BF16), pltpu.VMEM((tm, w_gate.shape[1]), BF16)],
        compiler_params=_params(1),
        name="l0_ffn",
    )(x, g.reshape(1, d), w_gate, w_up, w_down, *to_cast)
    return outs[0], outs[1:]


_NT = (((1,), (1,)), ((), ()))
_TN = (((0,), (0,)), ((), ()))


def _qkv_kernel(x_ref, g_ref, wqt_ref, bq_ref, wk_ref, bk_ref, wvt_ref, bv_ref, qt_ref, k_ref, vt_ref):
    h = _rms(x_ref[...], g_ref[...]).astype(BF16)
    qt = lax.dot_general(wqt_ref[...], h, _NT, preferred_element_type=F32) + bq_ref[...]
    qt_ref[...] = (qt * (HEAD_DIM ** -0.5)).astype(BF16)
    k_ref[...] = (jnp.dot(h, wk_ref[...], preferred_element_type=F32) + bk_ref[...]).astype(BF16)
    vt = lax.dot_general(wvt_ref[...], h, _NT, preferred_element_type=F32) + bv_ref[...]
    vt_ref[...] = vt.astype(BF16)


def _qkv(x, g, w_qkv, b_qkv, dq, dk, tm):
    b, s, d = x.shape
    wqt = w_qkv[:, :dq].T.astype(BF16)
    wk = w_qkv[:, dq:dq + dk].astype(BF16)
    wvt = w_qkv[:, dq + dk:].T.astype(BF16)
    bq = b_qkv[:dq].reshape(dq, 1)
    bk = b_qkv[dq:dq + dk].reshape(1, dk)
    bv = b_qkv[dq + dk:].reshape(dk, 1)
    return pl.pallas_call(
        _qkv_kernel,
        out_shape=(jax.ShapeDtypeStruct((b, dq, s), BF16),
                   jax.ShapeDtypeStruct((b, s, dk), BF16),
                   jax.ShapeDtypeStruct((b, dk, s), BF16)),
        grid=(b, s // tm),
        in_specs=[
            pl.BlockSpec((None, tm, d), lambda bi, i: (bi, i, 0)),
            _const_spec((1, d)),
            _const_spec(wqt.shape), _const_spec(bq.shape),
            _const_spec(wk.shape), _const_spec(bk.shape),
            _const_spec(wvt.shape), _const_spec(bv.shape),
        ],
        out_specs=(pl.BlockSpec((None, dq, tm), lambda bi, i: (bi, 0, i)),
                   pl.BlockSpec((None, tm, dk), lambda bi, i: (bi, i, 0)),
                   pl.BlockSpec((None, dk, tm), lambda bi, i: (bi, 0, i))),
        compiler_params=_params(2),
        name="l1_qkv",
    )(x, g.reshape(1, d), wqt, bq, wk, bk, wvt, bv)


def _bias_kernel(onehot_ref, rb_ref, o_ref):
    o_ref[...] = jnp.dot(onehot_ref[...], rb_ref[...], preferred_element_type=F32,
                         precision=lax.Precision.HIGHEST)


def _t5_bucket(rel):
    nb = N_BUCKETS // 2
    max_exact = nb // 2
    ret = jnp.where(rel > 0, nb, 0)
    n = jnp.abs(rel)
    nf = jnp.maximum(n, 1).astype(jnp.float32)
    large = max_exact + (jnp.log(nf / max_exact) / math.log(MAX_DISTANCE / max_exact)
                         * (nb - max_exact)).astype(jnp.int32)
    large = jnp.minimum(large, nb - 1)
    return ret + jnp.where(n < max_exact, n, large)


def _rel_bias_table(rel_bias):
    n_heads = rel_bias.shape[1]
    pad = WINDOW_CHUNKS * CHUNK
    rel = (jnp.arange(KV_SPAN) - pad)[None, :] - jnp.arange(CHUNK)[:, None]
    bucket = _t5_bucket(rel).reshape(-1)
    onehot = (bucket[:, None] == jnp.arange(N_BUCKETS)[None, :]).astype(F32)
    rows = CHUNK * KV_SPAN
    rb = jnp.pad(rel_bias.astype(F32), ((0, 0), (0, LANES - n_heads)))
    out = pl.pallas_call(
        _bias_kernel,
        out_shape=jax.ShapeDtypeStruct((rows, LANES), F32),
        name="l1_rel_bias",
    )(onehot, rb)
    return jnp.transpose(out[:, :n_heads].reshape(CHUNK, KV_SPAN, n_heads), (2, 0, 1))


PAIR = 2 * CHUNK
PAIR_KEYS = KV_SPAN + CHUNK


def _attn_kernel(qt_ref, kp_ref, km_ref, vtp_ref, vtm_ref, bias_ref, sink_ref, ot_ref, kbuf, vtbuf,
                 s_a, s_b, *, n_heads):
    i = pl.program_id(1)
    tq = qt_ref.shape[1]
    pad = WINDOW_CHUNKS * CHUNK
    per_kv = n_heads // N_KV_HEADS
    kbuf[0:pad, :] = kp_ref[...]
    kbuf[pad:pad + tq, :] = km_ref[...]
    vtbuf[:, 0:pad] = vtp_ref[...]
    vtbuf[:, pad:pad + tq] = vtm_ref[...]
    zeros = jnp.zeros((HEAD_DIM, PAIR), BF16)
    first = lax.broadcasted_iota(jnp.int32, (CHUNK, PAIR), 1) < CHUNK
    n_pairs = tq // PAIR

    def scores(jj, s_scr):
        c0 = jj * PAIR
        kc = kbuf[c0:c0 + PAIR_KEYS, :]
        for h in range(n_heads):
            pieces = [zeros] * N_KV_HEADS
            pieces[h // per_kv] = qt_ref[h * HEAD_DIM:(h + 1) * HEAD_DIM, c0:c0 + PAIR]
            rhs = jnp.concatenate(pieces, axis=0)
            s_scr[h] = jnp.dot(kc, rhs, preferred_element_type=F32)

    def softmax_values(jj, s_scr):
        c0 = jj * PAIR
        variant = jnp.where(i == 0, 1, 0) if jj == 0 else 0
        for h in range(n_heads):
            kv = h // per_kv
            mid = s_scr[h, CHUNK:KV_SPAN, :] + bias_ref[variant, h, 0:KV_SPAN - CHUNK, :]
            edge = (jnp.where(first, s_scr[h, 0:CHUNK, :], s_scr[h, KV_SPAN:PAIR_KEYS, :])
                    + bias_ref[variant, h, KV_SPAN - CHUNK:KV_SPAN, :])
            sk = sink_ref[h]
            m_ = jnp.maximum(jnp.maximum(jnp.max(mid, axis=0, keepdims=True),
                                         jnp.max(edge, axis=0, keepdims=True)), sk)
            p_mid = jnp.exp(mid - m_)
            p_edge = jnp.exp(edge - m_)
            denom = (jnp.sum(p_mid, axis=0, keepdims=True) + jnp.sum(p_edge, axis=0, keepdims=True)
                     + jnp.exp(sk - m_))
            p_mid = (p_mid / denom).astype(BF16)
            p_edge = (p_edge / denom).astype(BF16)
            zero = jnp.zeros_like(p_edge)
            p = jnp.concatenate([jnp.where(first, p_edge, zero), p_mid, jnp.where(first, zero, p_edge)],
                                axis=0)
            vt = vtbuf[kv * HEAD_DIM:(kv + 1) * HEAD_DIM, c0:c0 + PAIR_KEYS]
            o = jnp.dot(vt, p, preferred_element_type=F32)
            ot_ref[h * HEAD_DIM:(h + 1) * HEAD_DIM, c0:c0 + PAIR] = o.astype(BF16)

    bufs = (s_a, s_b)
    scores(0, bufs[0])
    for jj in range(n_pairs):
        if jj + 1 < n_pairs:
            scores(jj + 1, bufs[(jj + 1) % 2])
        softmax_values(jj, bufs[jj % 2])


def _pair_bias(bias):
    bt = jnp.transpose(bias, (0, 2, 1)).astype(F32)
    mid = jnp.concatenate([bt[:, CHUNK:, :], bt[:, :KV_SPAN - CHUNK, :]], axis=2)
    edge = jnp.concatenate([bt[:, :CHUNK, :], bt[:, KV_SPAN - CHUNK:, :]], axis=2)
    table = jnp.concatenate([mid, edge], axis=1)
    row = jnp.arange(KV_SPAN)[:, None]
    lane = jnp.arange(PAIR)[None, :]
    key = jnp.where(row < KV_SPAN - CHUNK, row + CHUNK,
                    jnp.where(lane < CHUNK, row - (KV_SPAN - CHUNK), row + CHUNK))
    masked = jnp.where(key[None] < WINDOW_CHUNKS * CHUNK, NEG_INF, table)
    return jnp.stack([table, masked])


def _attention(qt, k, vt, bias, sinks, tq):
    b, dq, s = qt.shape
    dk = k.shape[-1]
    n_heads = dq // HEAD_DIM
    pad = WINDOW_CHUNKS * CHUNK
    ratio = tq // pad
    table = _pair_bias(bias)
    prev = lambda i: jnp.maximum(i * ratio - 1, 0)
    return pl.pallas_call(
        functools.partial(_attn_kernel, n_heads=n_heads),
        out_shape=jax.ShapeDtypeStruct((b, dq, s), BF16),
        grid=(b, s // tq),
        in_specs=[
            pl.BlockSpec((None, dq, tq), lambda bi, i: (bi, 0, i)),
            pl.BlockSpec((None, pad, dk), lambda bi, i: (bi, prev(i), 0)),
            pl.BlockSpec((None, tq, dk), lambda bi, i: (bi, i, 0)),
            pl.BlockSpec((None, dk, pad), lambda bi, i: (bi, 0, prev(i))),
            pl.BlockSpec((None, dk, tq), lambda bi, i: (bi, 0, i)),
            _const_spec(table.shape),
            pl.BlockSpec(memory_space=pltpu.SMEM),
        ],
        out_specs=pl.BlockSpec((None, dq, tq), lambda bi, i: (bi, 0, i)),
        scratch_shapes=[pltpu.VMEM((pad + tq, dk), BF16), pltpu.VMEM((dk, pad + tq), BF16),
                        pltpu.VMEM((n_heads, PAIR_KEYS, PAIR), F32),
                        pltpu.VMEM((n_heads, PAIR_KEYS, PAIR), F32)],
        compiler_params=_params(2),
        name="l1_attention",
    )(qt, k, k, vt, vt, table, sinks.astype(F32))


INFO_S0, INFO_S1, INFO_W0, INFO_W1 = range(4)


EXPERT_ROWS = 16


def _oproj_router_kernel(ot_ref, x_ref, wo_ref, bo_ref, g_ref, wrt_ref, x3_ref, info_ref, cnt_ref,
                         *, n_experts):
    tm = x_ref.shape[0]
    attn = lax.dot_general(ot_ref[...], wo_ref[...], _TN, preferred_element_type=F32)
    x3 = x_ref[...] + attn + bo_ref[...]
    x3_ref[...] = x3
    h = _rms(x3, g_ref[...])
    h_hi = h.astype(BF16)
    h_lo = (h - h_hi.astype(F32)).astype(BF16)
    wrt = wrt_ref[...]
    by_hi = lax.dot_general(wrt, h_hi, _NT, preferred_element_type=F32)
    by_lo = lax.dot_general(wrt[:EXPERT_ROWS], h_lo, _NT, preferred_element_type=F32)
    logits = (by_hi[:EXPERT_ROWS] + by_hi[EXPERT_ROWS:]) + by_lo
    row_i = lax.broadcasted_iota(jnp.int32, (EXPERT_ROWS, tm), 0)
    row = row_i.astype(F32)
    neg = jnp.float32(-jnp.inf)
    logits = jnp.where(row_i < n_experts, logits, neg)
    m0 = jnp.max(logits, axis=0, keepdims=True)
    e0 = jnp.min(jnp.where(logits == m0, row, float(EXPERT_ROWS)), axis=0, keepdims=True)
    rest = jnp.where(row == e0, neg, logits)
    m1 = jnp.max(rest, axis=0, keepdims=True)
    e1 = jnp.min(jnp.where(rest == m1, row, float(EXPERT_ROWS)), axis=0, keepdims=True)
    t = jnp.exp(m1 - m0)
    w0 = 1.0 / (1.0 + t)
    w1 = t / (1.0 + t)
    hot0 = row == e0
    hot1 = row == e1
    both = jnp.where(hot0 | hot1, 1.0, 0.0)
    u = lax.broadcasted_iota(jnp.int32, (tm, tm), 0)
    v = lax.broadcasted_iota(jnp.int32, (tm, tm), 1)
    earlier = (u < v).astype(BF16)
    cnt = jnp.broadcast_to(jnp.sum(both, axis=1, keepdims=True), (EXPERT_ROWS, LANES))
    cnt_ref[...] = cnt
    upto = cnt
    row_l = lax.broadcasted_iota(jnp.int32, (EXPERT_ROWS, LANES), 0)
    shift = 1
    while shift < n_experts:
        upto = upto + jnp.where(row_l >= shift, pltpu.roll(upto, shift, axis=0), 0.0)
        shift *= 2
    before = jnp.dot(both.astype(BF16), earlier, preferred_element_type=F32) + (upto - cnt)[:, 0:1]
    r0 = jnp.sum(jnp.where(hot0, before, 0.0), axis=0, keepdims=True)
    r1 = jnp.sum(jnp.where(hot1, before, 0.0), axis=0, keepdims=True)
    fields = {INFO_S0: r0 * SUBLANES, INFO_S1: r1 * SUBLANES, INFO_W0: w0, INFO_W1: w1}
    info_ref[...] = jnp.concatenate(
        [fields.get(k, jnp.zeros((1, tm), F32)) for k in range(SUBLANES)], axis=0)


def _oproj_router(ot, x, w_o, b_o, g, w_router, tm):
    n, d = x.shape
    dq, seq = ot.shape[1:]
    per_seq = seq // tm
    n_experts = w_router.shape[1]
    assert n_experts <= EXPERT_ROWS
    wrt = jnp.pad(w_router.astype(F32).T, ((0, EXPERT_ROWS - n_experts), (0, 0)))
    wrt_hi = wrt.astype(BF16)
    wrt_lo = (wrt - wrt_hi.astype(F32)).astype(BF16)
    wrt = jnp.concatenate([wrt_hi, wrt_lo], axis=0)
    return pl.pallas_call(
        functools.partial(_oproj_router_kernel, n_experts=n_experts),
        out_shape=(jax.ShapeDtypeStruct((n, d), F32),
                   jax.ShapeDtypeStruct((n // tm, SUBLANES, tm), F32),
                   jax.ShapeDtypeStruct((n // tm, EXPERT_ROWS, LANES), F32)),
        grid=(n // tm,),
        in_specs=[
            pl.BlockSpec((None, dq, tm), lambda i: (i // per_seq, 0, i % per_seq)),
            pl.BlockSpec((tm, d), lambda i: (i, 0)),
            _const_spec(w_o.shape),
            _const_spec((1, d)),
            _const_spec((1, d)),
            _const_spec(wrt.shape),
        ],
        out_specs=(pl.BlockSpec((tm, d), lambda i: (i, 0)),
                   pl.BlockSpec((None, SUBLANES, tm), lambda i: (i, 0, 0)),
                   pl.BlockSpec((None, EXPERT_ROWS, LANES), lambda i: (i, 0, 0))),
        compiler_params=_params(1),
        name="l1_oproj_router",
    )(ot, x, w_o, b_o.reshape(1, d), g.reshape(1, d), wrt)


def _pieces(n, n_bits, fn):
    for bit in reversed(range(n_bits)):
        size = 1 << bit
        done = n & ~((size << 1) - 1)
        pl.when((n & size) != 0)(functools.partial(fn, done, size))


def _rows(ref, first_row, n_rows):
    return ref.at[pl.ds(pl.multiple_of(first_row * SUBLANES, SUBLANES), n_rows * SUBLANES)]


def _segment_copies(tabs, i, n_experts, lbuf, hbm, sem, to_hbm, act):
    n_tab, o_tab, d_tab = tabs[:3]
    n_bits = (lbuf.shape[0] // SUBLANES // TOP_K).bit_length()
    for e in range(n_experts):
        local0 = o_tab[i * n_experts + e]
        sorted0 = d_tab[i * n_experts + e]

        def piece(done, size, local0=local0, sorted0=sorted0, queue=e % DMA_QUEUES):
            local = _rows(lbuf, local0 + done, size)
            remote = _rows(hbm, sorted0 + done, size)
            act(pltpu.make_async_copy(local, remote, sem) if to_hbm
                else pltpu.make_async_copy(remote, local, sem), queue)

        _pieces(n_tab[i * n_experts + e], n_bits, piece)


def _dispatch_kernel(n_tab, o_tab, d_tab, pad0_tab, padn_tab, nv_tab, slot_ref, x_ref, g_ref, xs_hbm,
                     hbuf, lbuf, sems, *, n_experts, tme):
    i = pl.program_id(0)
    tm = x_ref.shape[0]
    n_slots = TOP_K * tm
    h = _rms(x_ref[...], g_ref[...])
    for c in range(x_ref.shape[1] // LANES):
        hbuf[pl.ds(c, tm, stride=SUBLANES), :] = h[:, c * LANES:(c + 1) * LANES]
    slot = i % 2
    sorted_rows = lbuf.at[slot]

    def token(t, carry):
        row = hbuf[pl.ds(pl.multiple_of(t * SUBLANES, SUBLANES), SUBLANES), :]
        for k in range(TOP_K):
            dst = pl.multiple_of(slot_ref[k * tm + t], SUBLANES)
            sorted_rows[pl.ds(dst, SUBLANES), :] = row
        return carry

    lax.fori_loop(0, tm, token, 0, unroll=ROW_LOOP_UNROLL)
    tabs = (n_tab, o_tab, d_tab)

    def copies(tile, s, act):
        _segment_copies(tabs, tile, n_experts, lbuf.at[s], xs_hbm, sems.at[s], True, act)

    copies(i, slot, lambda cp, queue: cp.start(priority=queue))

    def wait_all(s):
        pltpu.make_async_copy(lbuf.at[s], _rows(xs_hbm, 0, n_slots), sems.at[s]).wait()

    @pl.when(i > 0)
    def _():
        wait_all(1 - slot)

    @pl.when(i == pl.num_programs(0) - 1)
    def _():
        wait_all(slot)
        zeros = lbuf.at[slot]
        zeros[0:tme * SUBLANES, :] = jnp.zeros((tme * SUBLANES, LANES), F32)

        def pads(act):
            for e in range(n_experts):
                def piece(done, size, e=e):
                    act(pltpu.make_async_copy(_rows(zeros, 0, size),
                                              _rows(xs_hbm, pad0_tab[e] + done, size), sems.at[slot]))
                _pieces(padn_tab[e], (tme - 1).bit_length(), piece)
            n_tiles = xs_hbm.shape[0] // (tme * SUBLANES)
            for t in range(n_experts):
                tile = nv_tab[0] + t
                pl.when(tile < n_tiles)(lambda tile=tile: act(pltpu.make_async_copy(
                    _rows(zeros, 0, tme), _rows(xs_hbm, tile * tme, tme), sems.at[slot])))

        pads(lambda cp: cp.start())
        pads(lambda cp: cp.wait())


def _dispatch(x, g, slots, tabs, n_experts, n_rows, tm, tme):
    n, d = x.shape
    assert TOP_K * tm >= tme
    grid_spec = pltpu.PrefetchScalarGridSpec(
        num_scalar_prefetch=len(tabs),
        grid=(n // tm,),
        in_specs=[
            pl.BlockSpec((TOP_K * tm,), lambda i, *_: (i,), memory_space=pltpu.SMEM),
            pl.BlockSpec((tm, d), lambda i, *_: (i, 0)),
            pl.BlockSpec((1, d), lambda i, *_: (0, 0)),
        ],
        out_specs=pl.BlockSpec(memory_space=pl.ANY),
        scratch_shapes=[pltpu.VMEM((tm * SUBLANES, LANES), F32),
                        pltpu.VMEM((2, TOP_K * tm * SUBLANES, LANES), F32),
                        pltpu.SemaphoreType.DMA((2,))],
    )
    return pl.pallas_call(
        functools.partial(_dispatch_kernel, n_experts=n_experts, tme=tme),
        out_shape=jax.ShapeDtypeStruct((n_rows * SUBLANES, LANES), F32),
        grid_spec=grid_spec,
        compiler_params=_params(1),
        name="l1_moe_dispatch",
    )(*tabs, slots, x, g.reshape(1, d))


def _experts_kernel(te_ref, tv_ref, nv_ref, xs_ref, wg_ref, wu_ref, wd_ref, ys_ref, t_scr, *, sub):
    del te_ref, nv_ref
    j = pl.program_id(0)
    tm = t_scr.shape[0]
    n_slabs = wg_ref.shape[0] // LANES

    @pl.when(tv_ref[j] == 1)
    def _():
        slabs = [xs_ref[pl.ds(s, tm, stride=SUBLANES), :] for s in range(n_slabs)]
        x = jnp.concatenate(slabs, axis=1).astype(BF16)

        def body(q, carry):
            off = pl.multiple_of(q * sub, sub)
            a = jnp.dot(x, wg_ref[:, pl.ds(off, sub)], preferred_element_type=F32)
            b = jnp.dot(x, wu_ref[:, pl.ds(off, sub)], preferred_element_type=F32)
            t_scr[:, pl.ds(off, sub)] = (a * jax.nn.sigmoid(a) * b).astype(BF16)
            return carry
        lax.fori_loop(0, wg_ref.shape[1] // sub, body, 0, unroll=True)
        y = jnp.dot(t_scr[...], wd_ref[...], preferred_element_type=F32)
        for s in range(n_slabs):
            ys_ref[pl.ds(s, tm, stride=SUBLANES), :] = y[:, s * LANES:(s + 1) * LANES]

    @pl.when(tv_ref[j] != 1)
    def _():
        ys_ref[...] = jnp.zeros_like(ys_ref)


def _experts(xs, tile_expert, tile_valid, n_valid, w_gate, w_up, w_down, d, tm, sub):
    n_tiles = tile_expert.shape[0]
    ff = w_gate.shape[2]
    grid_spec = pltpu.PrefetchScalarGridSpec(
        num_scalar_prefetch=3,
        grid=(n_tiles,),
        in_specs=[
            pl.BlockSpec((tm * SUBLANES, LANES), lambda j, te, tv, nv: (jnp.minimum(j, nv[0] - 1), 0)),
            pl.BlockSpec((None, d, ff), lambda j, te, tv, nv: (te[j], 0, 0)),
            pl.BlockSpec((None, d, ff), lambda j, te, tv, nv: (te[j], 0, 0)),
            pl.BlockSpec((None, ff, d), lambda j, te, tv, nv: (te[j], 0, 0)),
        ],
        out_specs=pl.BlockSpec((tm * SUBLANES, LANES), lambda j, te, tv, nv: (j, 0)),
        scratch_shapes=[pltpu.VMEM((tm, ff), BF16)],
    )
    return pl.pallas_call(
        functools.partial(_experts_kernel, sub=sub),
        out_shape=jax.ShapeDtypeStruct(xs.shape, F32),
        grid_spec=grid_spec,
        compiler_params=_params(1, vmem=EXPERT_VMEM_LIMIT),
        name="l1_moe_experts",
    )(tile_expert, tile_valid, n_valid, xs, w_gate, w_up, w_down)


def _combine_kernel(n_tab, o_tab, d_tab, slot_ref, w_ref, x_ref, g_ref, ys_hbm, o_ref, lbuf, mbuf, sems,
                    *, n_experts):
    i = pl.program_id(0)
    tm = x_ref.shape[0]
    n_slots = TOP_K * tm
    tabs = (n_tab, o_tab, d_tab)
    slot = i % 2

    def copies(tile, s, act):
        _segment_copies(tabs, tile, n_experts, lbuf.at[s], ys_hbm, sems.at[s], False, act)

    @pl.when(i == 0)
    def _():
        copies(i, slot, lambda cp, queue: cp.start(priority=queue))

    @pl.when(i + 1 < pl.num_programs(0))
    def _():
        copies(i + 1, 1 - slot, lambda cp, queue: cp.start(priority=queue))

    pltpu.make_async_copy(_rows(ys_hbm, 0, n_slots), lbuf.at[slot], sems.at[slot]).wait()
    y = lbuf.at[slot]

    def token(t, carry):
        acc = None
        for k in range(TOP_K):
            row = y[pl.ds(pl.multiple_of(slot_ref[k * tm + t], SUBLANES), SUBLANES), :]
            term = w_ref[k * tm + t] * row
            acc = term if acc is None else acc + term
        mbuf[pl.ds(pl.multiple_of(t * SUBLANES, SUBLANES), SUBLANES), :] = acc
        return carry

    lax.fori_loop(0, tm, token, 0, unroll=ROW_LOOP_UNROLL)
    moe = jnp.concatenate([mbuf[pl.ds(c, tm, stride=SUBLANES), :]
                           for c in range(x_ref.shape[1] // LANES)], axis=1)
    o_ref[...] = _rms(x_ref[...] + moe, g_ref[...])


def _combine(x, slots, weights, tabs, n_experts, ys, g, tm):
    n, d = x.shape
    grid_spec = pltpu.PrefetchScalarGridSpec(
        num_scalar_prefetch=3,
        grid=(n // tm,),
        in_specs=[
            pl.BlockSpec((TOP_K * tm,), lambda i, *_: (i,), memory_space=pltpu.SMEM),
            pl.BlockSpec((TOP_K * tm,), lambda i, *_: (i,), memory_space=pltpu.SMEM),
            pl.BlockSpec((tm, d), lambda i, *_: (i, 0)),
            pl.BlockSpec((1, d), lambda i, *_: (0, 0)),
            pl.BlockSpec(memory_space=pl.ANY),
        ],
        out_specs=pl.BlockSpec((tm, d), lambda i, *_: (i, 0)),
        scratch_shapes=[pltpu.VMEM((2, TOP_K * tm * SUBLANES, LANES), F32),
                        pltpu.VMEM((tm * SUBLANES, LANES), F32),
                        pltpu.SemaphoreType.DMA((2,))],
    )
    return pl.pallas_call(
        functools.partial(_combine_kernel, n_experts=n_experts),
        out_shape=jax.ShapeDtypeStruct((n, d), F32),
        grid_spec=grid_spec,
        compiler_params=_params(1),
        name="l1_moe_combine",
    )(*tabs, slots, weights, x, g.reshape(1, d), ys)


def _forward(x, rel_bias, ev_norm_mix, ev_w_in, ev_pool_w, ev_pool_scale, ev_conv_w, ev_w_out,
             ev_norm_ffn, ev_ffn_gate, ev_ffn_up, ev_ffn_down, od_norm_mix, od_w_qkv, od_b_qkv,
             od_sinks, od_w_o, od_b_o, od_norm_ffn, od_router, od_exp_gate, od_exp_up,
             od_exp_down, final_norm, *, tm, tq, tme, fc_dense, tr, tx):
    b, s, d = x.shape
    n = b * s
    n_heads = od_sinks.shape[-1]
    dq = n_heads * HEAD_DIM
    dk = N_KV_HEADS * HEAD_DIM
    n_experts = od_router.shape[-1]

    x = _mixer(x, ev_norm_mix[0], ev_w_in[0].astype(BF16), ev_pool_w[0].astype(BF16),
               ev_pool_scale[0], ev_conv_w[0], ev_w_out[0].astype(BF16), tx)
    x = x.reshape(n, d)
    ff = od_exp_gate.shape[-1]
    expert_f32 = (od_exp_gate[0].reshape(n_experts * d, ff), od_exp_up[0].reshape(n_experts * d, ff),
                  od_exp_down[0].reshape(n_experts * ff, d))
    x, (exp_gate, exp_up, exp_down) = _ffn(
        x, ev_norm_ffn[0], ev_ffn_gate[0].astype(BF16), ev_ffn_up[0].astype(BF16),
        ev_ffn_down[0].astype(BF16), expert_f32, tm, fc_dense)

    qt, k, vt = _qkv(x.reshape(b, s, d), od_norm_mix[0], od_w_qkv[0], od_b_qkv[0], dq, dk, tx)
    bias = _rel_bias_table(rel_bias)
    attn_t = _attention(qt, k, vt, bias, od_sinks[0], tq)
    x, info, counts = _oproj_router(attn_t, x, od_w_o[0].astype(BF16), od_b_o[0], od_norm_ffn[0],
                                    od_router[0], tr)

    per_tile = counts[:, :n_experts, 0].astype(jnp.int32)
    local_off = jnp.cumsum(per_tile, axis=1) - per_tile
    earlier = jnp.cumsum(per_tile, axis=0) - per_tile
    totals = jnp.sum(per_tile, axis=0)
    tiles_per = (totals + tme - 1) // tme
    tile_end = jnp.cumsum(tiles_per)
    group_start = (tile_end - tiles_per) * tme
    n_tiles = (TOP_K * n) // tme + n_experts
    tile_ids = jnp.arange(n_tiles, dtype=jnp.int32)
    tile_valid = (tile_ids < tile_end[-1]).astype(jnp.int32)
    last_valid = jnp.minimum(tile_ids, tile_end[-1] - 1)
    tile_expert = jnp.sum(last_valid[:, None] >= tile_end[None, :], axis=1).astype(jnp.int32)
    tabs = (per_tile.reshape(-1), local_off.reshape(-1), (group_start[None, :] + earlier).reshape(-1))
    pad_tabs = (group_start + totals, tiles_per * tme - totals, tile_end[-1:])

    slots = info[:, INFO_S0:INFO_S1 + 1, :].astype(jnp.int32).reshape(-1)
    weights = info[:, INFO_W0:INFO_W1 + 1, :].reshape(-1)

    xs = _dispatch(x, od_norm_ffn[0], slots, tabs + pad_tabs, n_experts, n_tiles * tme, tr, tme)
    ys = _experts(xs, tile_expert, tile_valid, tile_end[-1:], exp_gate.reshape(n_experts, d, ff),
                  exp_up.reshape(n_experts, d, ff), exp_down.reshape(n_experts, ff, d), d, tme, fc_dense)
    out = _combine(x, slots, weights, tabs, n_experts, ys, final_norm, tr)
    return out.reshape(b, s, d)


def kernel(x, rel_bias, ev_norm_mix, ev_w_in, ev_pool_w, ev_pool_scale, ev_conv_w, ev_w_out, ev_norm_ffn, ev_ffn_gate, ev_ffn_up, ev_ffn_down, od_norm_mix, od_w_qkv, od_b_qkv, od_sinks, od_w_o, od_b_o, od_norm_ffn, od_router, od_exp_gate, od_exp_up, od_exp_down, final_norm):
    return _forward(x, rel_bias, ev_norm_mix, ev_w_in, ev_pool_w, ev_pool_scale, ev_conv_w, ev_w_out,
                    ev_norm_ffn, ev_ffn_gate, ev_ffn_up, ev_ffn_down, od_norm_mix, od_w_qkv, od_b_qkv,
                    od_sinks, od_w_o, od_b_o, od_norm_ffn, od_router, od_exp_gate, od_exp_up,
                    od_exp_down, final_norm, tm=512, tq=512, tme=512, fc_dense=256, tr=1024, tx=1024)
```
